```python
import jax
import jax.numpy as jnp
from jax import lax
import numpy as np

D_MODEL = 1024
BATCH = 2
SEQ = 8192
DEPTH = 4

GRID_W = 64
CTX_LEN = 256

POOL_WINDOWS = (2, 4, 8, 16)
POOL_GROUPS = len(POOL_WINDOWS)
POOL_DIM = D_MODEL // 4
POOL_GROUP_DIM = POOL_DIM // POOL_GROUPS

HG_HEADS = 4
HG_DK = 128
HG_DV = (D_MODEL // 4) // HG_HEADS
HG_QK = HG_HEADS * HG_DK
HG_V = HG_HEADS * HG_DV
HG_CHUNK = 64

MLA_HEADS = 8
MLA_NOPE = 64
MLA_ROPE = 32
MLA_V = (D_MODEL // 2) // MLA_HEADS
MLA_Q_LORA = D_MODEL // 4
MLA_KV_LORA = D_MODEL // 8
MLA_QK = MLA_NOPE + MLA_ROPE
MLA_SCALE = MLA_QK ** -0.5
ROPE_BASE = 10000.0
Q_BLOCK = 128

MIX_DIM = POOL_DIM + HG_V + MLA_HEADS * MLA_V
COL_WIDTHS = (POOL_DIM, HG_QK, HG_QK, HG_QK, HG_V, HG_V, MLA_Q_LORA, MLA_KV_LORA, MLA_ROPE)
IN_DIM = sum(COL_WIDTHS)
COL_SPLITS = tuple(int(s) for s in np.cumsum(COL_WIDTHS)[:-1])

N_GROUPS = 4
EXPERTS_PER_GROUP = 8
N_EXPERTS = N_GROUPS * EXPERTS_PER_GROUP
TOP_K = 2
D_EXPERT = D_MODEL // 2
MOE_BLOCK = 128

ALPHA = (2 * DEPTH) ** 0.25
BETA = (8 * DEPTH) ** -0.25
EPS = 1e-6

kernel_name = "hybrid_pool_hgrn2_mla_hmoe_diffusion"

F32 = jnp.float32


def layer_norm(x):
    xf = x.astype(F32)
    mu = jnp.mean(xf, axis=-1, keepdims=True)
    var = jnp.mean(jnp.square(xf - mu), axis=-1, keepdims=True)
    return (xf - mu) * lax.rsqrt(var + EPS)


def modulate(x, shift, scale):
    return (layer_norm(x) * (1.0 + scale.astype(F32)) + shift.astype(F32)).astype(x.dtype)


def post_norm(x, g, b):
    return (layer_norm(x) * g.astype(F32) + b.astype(F32)).astype(x.dtype)


def rms_norm(x, g):
    xf = x.astype(F32)
    return (xf * lax.rsqrt(jnp.mean(xf * xf, axis=-1, keepdims=True) + EPS) * g.astype(F32)).astype(x.dtype)


def axial_rope_tables(n_tokens):
    rows = n_tokens // GRID_W
    row = jnp.repeat(jnp.arange(rows, dtype=F32), GRID_W)
    col = jnp.tile(jnp.arange(GRID_W, dtype=F32), rows)
    n_freq = MLA_ROPE // 4
    inv_freq = ROPE_BASE ** (-jnp.arange(n_freq, dtype=F32) / n_freq)
    ang = jnp.stack([row[:, None] * inv_freq, col[:, None] * inv_freq], axis=1)
    return jnp.cos(ang), jnp.sin(ang)


def apply_axial_rope(x, cos, sin):
    xr = x.astype(F32).reshape(x.shape[:-1] + (2, 2, MLA_ROPE // 4))
    x1, x2 = xr[..., 0, :], xr[..., 1, :]
    out = jnp.stack([x1 * cos - x2 * sin, x2 * cos + x1 * sin], axis=-2)
    return out.reshape(x.shape).astype(x.dtype)


def multi_scale_pool(u, w_pool, scale):
    n = u.shape[1]
    uf = u.astype(F32)
    cs = jnp.concatenate([jnp.zeros_like(uf[:, :1]), jnp.cumsum(uf, axis=1)], axis=1)
    t = jnp.arange(n)
    outs = []
    for gi, w in enumerate(POOL_WINDOWS):
        lo = jnp.clip(t - w // 2, 0, n)
        hi = jnp.clip(t + w // 2, 0, n)
        sl = slice(gi * POOL_GROUP_DIM, (gi + 1) * POOL_GROUP_DIM)
        csg = cs[:, :, sl]
        mean = (csg[:, hi] - csg[:, lo]) / (hi - lo).astype(F32)[None, :, None]
        outs.append(mean - uf[:, :, sl])
    pooled = jnp.stack(outs, axis=2)
    y = jnp.einsum("bngc,gcd->bngd", pooled, w_pool.astype(F32))
    return (y.reshape(u.shape) * scale.astype(F32)).astype(u.dtype)


def log_forget(z, lb):
    return jnp.logaddexp(jnp.log(lb), jnp.log1p(-lb) + jax.nn.log_sigmoid(z.astype(F32)))


def gla_chunk_scan(q, k, v, logf, s0):
    bsz, n, h, dk = q.shape
    dv = v.shape[-1]
    nc = n // HG_CHUNK

    def to_chunks(a):
        return a.reshape(bsz, nc, HG_CHUNK, h, a.shape[-1]).transpose(1, 0, 3, 2, 4)

    mask = jnp.tril(jnp.ones((HG_CHUNK, HG_CHUNK), dtype=bool))

    def step(S, inp):
        qc, kc, vc, gc = inp
        b = jnp.cumsum(gc, axis=2)
        o_inter = jnp.einsum("bhtd,bhde->bhte", qc * jnp.exp(b), S)
        diff = b[:, :, :, None, :] - b[:, :, None, :, :]
        decay = jnp.exp(jnp.where(mask[:, :, None], diff, -jnp.inf))
        attn = jnp.einsum("bhtd,bhsd,bhtsd->bhts", qc, kc, decay)
        o = o_inter + jnp.einsum("bhts,bhse->bhte", attn, vc)
        b_last = b[:, :, -1, :]
        S = jnp.exp(b_last)[..., None] * S + jnp.einsum(
            "bhsd,bhse->bhde", kc * jnp.exp(b_last[:, :, None, :] - b), vc)
        return S, o

    S, o = lax.scan(step, s0, (to_chunks(q), to_chunks(k), to_chunks(v), to_chunks(logf)))
    o = o.transpose(1, 0, 3, 2, 4).reshape(bsz, n, h, dv)
    return o, S


def hgrn2_prepare(q_pre, z_f, z_b, i_in, lb_f, lb_b):
    bsz, n = q_pre.shape[:2]

    def heads(a, d):
        return a.astype(F32).reshape(bsz, n, HG_HEADS, d)

    q = heads(jax.nn.silu(q_pre.astype(F32)), HG_DK)
    g_f = heads(log_forget(z_f, lb_f), HG_DK)
    g_b = heads(log_forget(z_b, lb_b), HG_DK)
    return q, -jnp.expm1(g_f), g_f, -jnp.expm1(g_b), g_b, heads(i_in, HG_DV)


def hgrn2_bidirectional(q, kf, gf, kb, gb, v, s_fwd, s_bwd):
    def flip(a):
        return a[:, ::-1]

    o_f, s_f = gla_chunk_scan(q, kf, v, gf, s_fwd)
    o_b, s_b = gla_chunk_scan(flip(q), flip(kb), flip(v), flip(gb), s_bwd)
    return o_f + flip(o_b), s_f, s_b


def hgrn2_readout(o, gate, norm_g, dtype):
    o = o * lax.rsqrt(jnp.mean(o * o, axis=-1, keepdims=True) + EPS)
    o = o.reshape(o.shape[0], o.shape[1], HG_V) * norm_g.astype(F32) * jax.nn.silu(gate.astype(F32))
    return o.astype(dtype)


def mla_queries(cq, q_norm_g, w_uq, cos, sin):
    bsz, n = cq.shape[:2]
    q = (rms_norm(cq, q_norm_g) @ w_uq).reshape(bsz, n, MLA_HEADS, MLA_QK)
    q_nope, q_rope = q[..., :MLA_NOPE], q[..., MLA_NOPE:]
    if cos is not None:
        q_rope = apply_axial_rope(q_rope, cos[:, None], sin[:, None])
    return jnp.concatenate([q_nope, q_rope], axis=-1).transpose(0, 2, 1, 3)


def mla_keys_values(ckv, k_rope, kv_norm_g, w_ukv, cos, sin):
    bsz, n = ckv.shape[:2]
    kv = (rms_norm(ckv, kv_norm_g) @ w_ukv).reshape(bsz, n, MLA_HEADS, MLA_NOPE + MLA_V)
    k_nope, v = kv[..., :MLA_NOPE], kv[..., MLA_NOPE:]
    if cos is not None:
        k_rope = apply_axial_rope(k_rope, cos, sin)
    k_rope = jnp.broadcast_to(k_rope[:, :, None, :], (bsz, n, MLA_HEADS, MLA_ROPE))
    k = jnp.concatenate([k_nope, k_rope], axis=-1)
    return k.transpose(0, 2, 1, 3), v.transpose(0, 2, 1, 3)


def softmax_attend(q, k, v):
    s = jnp.einsum("bhqd,bhkd->bhqk", q, k, preferred_element_type=F32) * MLA_SCALE
    p = jax.nn.softmax(s, axis=-1)
    return jnp.einsum("bhqk,bhkd->bhqd", p.astype(v.dtype), v)


def block_attention(q, k, v):
    bsz, h, n, d = q.shape
    nb = n // Q_BLOCK
    qb = jnp.moveaxis(q.reshape(bsz, h, nb, Q_BLOCK, d), 2, 0)
    o = lax.map(lambda blk: softmax_attend(blk, k, v), qb)
    return jnp.moveaxis(o, 0, 2).reshape(bsz, h, n, v.shape[-1])


def merge_heads(o):
    bsz, h, n, d = o.shape
    return o.transpose(0, 2, 1, 3).reshape(bsz, n, h * d)


def token_mixer(h, hc, w_in, pool_w, pool_scale, lb_f, lb_b, hg_norm_g,
                q_norm_g, w_uq, kv_norm_g, w_ukv, w_out, cos, sin, with_ctx):
    bsz = h.shape[0]
    (p_in, hq, hzf, hzb, hin, hg, cq, ckv, kr) = jnp.split(h @ w_in, COL_SPLITS, axis=-1)
    (p_in_c, hq_c, hzf_c, hzb_c, hin_c, hg_c, cq_c, ckv_c, kr_c) = jnp.split(hc @ w_in, COL_SPLITS, axis=-1)

    a_lat = multi_scale_pool(p_in, pool_w, pool_scale)

    lat_in = hgrn2_prepare(hq, hzf, hzb, hin, lb_f, lb_b)
    ctx_in = hgrn2_prepare(hq_c, hzf_c, hzb_c, hin_c, lb_f, lb_b)
    zero = jnp.zeros((bsz, HG_HEADS, HG_DK, HG_DV), F32)
    o_c, s_f, s_b = hgrn2_bidirectional(*ctx_in, zero, zero)
    o_l, _, _ = hgrn2_bidirectional(*lat_in, s_f, s_b)
    b_lat = hgrn2_readout(o_l, hg, hg_norm_g, h.dtype)

    q_l = mla_queries(cq, q_norm_g, w_uq, cos, sin)
    k_l, v_l = mla_keys_values(ckv, kr, kv_norm_g, w_ukv, cos, sin)
    k_c, v_c = mla_keys_values(ckv_c, kr_c, kv_norm_g, w_ukv, None, None)
    k_all = jnp.concatenate([k_l, k_c], axis=2)
    v_all = jnp.concatenate([v_l, v_c], axis=2)
    c_lat = merge_heads(block_attention(q_l, k_all, v_all))

    out_lat = jnp.concatenate([a_lat, b_lat, c_lat], axis=-1) @ w_out
    if not with_ctx:
        return out_lat, None

    a_ctx = multi_scale_pool(p_in_c, pool_w, pool_scale)
    b_ctx = hgrn2_readout(o_c, hg_c, hg_norm_g, hc.dtype)
    q_c = mla_queries(cq_c, q_norm_g, w_uq, None, None)
    c_ctx_out = merge_heads(softmax_attend(q_c, k_c, v_c))
    out_ctx = jnp.concatenate([a_ctx, b_ctx, c_ctx_out], axis=-1) @ w_out
    return out_lat, out_ctx


def hier_moe(h, w_rg, b_rg, w_re, b_re, w1, w3, w2):
    n, d = h.shape
    lg = (h @ w_rg).astype(F32) + b_rg.astype(F32)
    pg = jax.nn.softmax(lg, axis=-1)
    g_sel = jnp.argmax(lg, axis=-1)
    w_g = jnp.take_along_axis(pg, g_sel[:, None], axis=-1)[:, 0]
    le = ((h @ w_re).astype(F32) + b_re.astype(F32)).reshape(n, N_GROUPS, EXPERTS_PER_GROUP)
    le_sel = jnp.take_along_axis(le, g_sel[:, None, None], axis=1)[:, 0]
    top_v, top_i = lax.top_k(le_sel, TOP_K)
    gate = jax.nn.softmax(top_v, axis=-1) * w_g[:, None]
    expert = g_sel[:, None] * EXPERTS_PER_GROUP + top_i

    n_assign = n * TOP_K
    e_flat = expert.reshape(-1).astype(jnp.int32)
    tok = jnp.repeat(jnp.arange(n, dtype=jnp.int32), TOP_K)
    wt = gate.reshape(-1)
    order = jnp.argsort(e_flat)
    e_s, tok_s, w_s = e_flat[order], tok[order], wt[order]
    counts = jnp.bincount(e_flat, length=N_EXPERTS)
    starts = jnp.cumsum(counts) - counts
    pcounts = (counts + MOE_BLOCK - 1) // MOE_BLOCK * MOE_BLOCK
    pends = jnp.cumsum(pcounts)
    pstarts = pends - pcounts
    dest = pstarts[e_s] + jnp.arange(n_assign) - starts[e_s]
    n_rows = (n_assign + N_EXPERTS * (MOE_BLOCK - 1) + MOE_BLOCK - 1) // MOE_BLOCK * MOE_BLOCK
    n_blocks = n_rows // MOE_BLOCK
    row_tok = jnp.full((n_rows,), n, dtype=jnp.int32).at[dest].set(tok_s)
    row_w = jnp.zeros((n_rows,), F32).at[dest].set(w_s)
    block_e = jnp.clip(jnp.searchsorted(pends, jnp.arange(n_blocks) * MOE_BLOCK, side="right"),
                       0, N_EXPERTS - 1)
    hp = jnp.concatenate([h, jnp.zeros((1, d), h.dtype)], axis=0)
    xr = hp[row_tok].reshape(n_blocks, MOE_BLOCK, d)

    def expert_block(args):
        xb, e = args
        return (jax.nn.silu(xb @ w1[e]) * (xb @ w3[e])) @ w2[e]

    y = lax.map(expert_block, (xr, block_e)).reshape(n_rows, d)
    out = jnp.zeros((n + 1, d), F32).at[row_tok].add(y.astype(F32) * row_w[:, None])[:n]
    return out.astype(h.dtype)


def setup_inputs(seed: int = 0) -> dict:
    key = jax.random.key(seed)
    ks = jax.random.split(key, 28)
    D = D_MODEL

    def nrm(k, shape, s):
        return jax.random.normal(k, shape, F32) * s

    return {
        "x": nrm(ks[0], (BATCH, SEQ, D), 1.0),
        "c": nrm(ks[1], (BATCH, D), 1.0),
        "ctx": nrm(ks[2], (BATCH, CTX_LEN, D), 1.0),
        "c_ctx": nrm(ks[3], (D,), 1.0),
        "w_mod": nrm(ks[4], (DEPTH, D, 6 * D), 0.5 * D ** -0.5),
        "b_mod": nrm(ks[5], (DEPTH, 6 * D), 0.01),
        "w_in": nrm(ks[6], (DEPTH, D, IN_DIM), D ** -0.5),
        "pool_w": nrm(ks[7], (DEPTH, POOL_GROUPS, POOL_GROUP_DIM, POOL_GROUP_DIM), POOL_GROUP_DIM ** -0.5),
        "pool_scale": 1.0 + nrm(ks[8], (DEPTH, POOL_DIM), 0.05),
        "hg_lb_logits": nrm(ks[9], (DEPTH, 2, HG_QK), 0.5),
        "hg_norm_g": 1.0 + nrm(ks[10], (DEPTH, HG_V), 0.05),
        "mla_q_norm_g": 1.0 + nrm(ks[11], (DEPTH, MLA_Q_LORA), 0.05),
        "mla_w_uq": nrm(ks[12], (DEPTH, MLA_Q_LORA, MLA_HEADS * MLA_QK), MLA_Q_LORA ** -0.5),
        "mla_kv_norm_g": 1.0 + nrm(ks[13], (DEPTH, MLA_KV_LORA), 0.05),
        "mla_w_ukv": nrm(ks[14], (DEPTH, MLA_KV_LORA, MLA_HEADS * (MLA_NOPE + MLA_V)), MLA_KV_LORA ** -0.5),
        "w_out": nrm(ks[15], (DEPTH, MIX_DIM, D), MIX_DIM ** -0.5 * BETA),
        "ln1_g": 1.0 + nrm(ks[16], (DEPTH, D), 0.05),
        "ln1_b": nrm(ks[17], (DEPTH, D), 0.01),
        "ln2_g": 1.0 + nrm(ks[18], (DEPTH, D), 0.05),
        "ln2_b": nrm(ks[19], (DEPTH, D), 0.01),
        "router_group_w": nrm(ks[20], (DEPTH, D, N_GROUPS), D ** -0.5),
        "router_group_b": nrm(ks[21], (DEPTH, N_GROUPS), 0.01),
        "router_expert_w": nrm(ks[22], (DEPTH, D, N_EXPERTS), D ** -0.5),
        "router_expert_b": nrm(ks[23], (DEPTH, N_EXPERTS), 0.01),
        "expert_w1": nrm(ks[24], (DEPTH, N_EXPERTS, D, D_EXPERT), D ** -0.5),
        "expert_w3": nrm(ks[25], (DEPTH, N_EXPERTS, D, D_EXPERT), D ** -0.5),
        "expert_w2": nrm(ks[26], (DEPTH, N_EXPERTS, D_EXPERT, D), D_EXPERT ** -0.5 * BETA),
    }


def reference(x, c, ctx, c_ctx, w_mod, b_mod, w_in, pool_w, pool_scale, hg_lb_logits, hg_norm_g,
              mla_q_norm_g, mla_w_uq, mla_kv_norm_g, mla_w_ukv, w_out, ln1_g, ln1_b, ln2_g, ln2_b,
              router_group_w, router_group_b, router_expert_w, router_expert_b,
              expert_w1, expert_w3, expert_w2):
    bsz, n_lat, d = x.shape
    n_ctx = ctx.shape[1]
    cos, sin = axial_rope_tables(n_lat)
    lb_all = jnp.cumsum(jax.nn.softmax(hg_lb_logits.astype(F32), axis=0), axis=0)
    lb_all = lb_all - lb_all[0:1]
    c_act = jax.nn.silu(c)
    cc_act = jax.nn.silu(c_ctx)
    xc = ctx
    for l in range(DEPTH):
        with_ctx = l < DEPTH - 1
        mod = c_act @ w_mod[l] + b_mod[l]
        mod_c = cc_act @ w_mod[l] + b_mod[l]
        sh1, sc1, g1, sh2, sc2, g2 = jnp.split(mod[:, None, :], 6, axis=-1)
        sh1c, sc1c, g1c, sh2c, sc2c, g2c = jnp.split(mod_c, 6, axis=-1)

        h = modulate(x, sh1, sc1)
        hc = modulate(xc, sh1c, sc1c)
        m_lat, m_ctx = token_mixer(h, hc, w_in[l], pool_w[l], pool_scale[l], lb_all[l, 0], lb_all[l, 1],
                                   hg_norm_g[l], mla_q_norm_g[l], mla_w_uq[l], mla_kv_norm_g[l],
                                   mla_w_ukv[l], w_out[l], cos, sin, with_ctx)
        x = post_norm(ALPHA * x + g1 * m_lat, ln1_g[l], ln1_b[l])
        h = modulate(x, sh2, sc2)
        if with_ctx:
            xc = post_norm(ALPHA * xc + g1c * m_ctx, ln1_g[l], ln1_b[l])
            hc = modulate(xc, sh2c, sc2c)
            tokens = jnp.concatenate([h.reshape(-1, d), hc.reshape(-1, d)], axis=0)
            y = hier_moe(tokens, router_group_w[l], router_group_b[l], router_expert_w[l],
                         router_expert_b[l], expert_w1[l], expert_w3[l], expert_w2[l])
            y_lat = y[:bsz * n_lat].reshape(bsz, n_lat, d)
            y_ctx = y[bsz * n_lat:].reshape(bsz, n_ctx, d)
            xc = post_norm(ALPHA * xc + g2c * y_ctx, ln2_g[l], ln2_b[l])
        else:
            y_lat = hier_moe(h.reshape(-1, d), router_group_w[l], router_group_b[l], router_expert_w[l],
                             router_expert_b[l], expert_w1[l], expert_w3[l],
                             expert_w2[l]).reshape(bsz, n_lat, d)
        x = post_norm(ALPHA * x + g2 * y_lat, ln2_g[l], ln2_b[l])
    return x
```

```python
import functools

import numpy as np
import jax
import jax.numpy as jnp
from jax import lax
from jax.experimental import pallas as pl
from jax.experimental.pallas import tpu as pltpu

F32 = jnp.float32
BF16 = jnp.bfloat16

D_MODEL = 1024
GRID_W = 64
POOL_WINDOWS = (2, 4, 8, 16)
POOL_DIM = 256
POOL_GROUP_DIM = 64
HG_HEADS = 4
HG_DK = 128
HG_DV = 64
HG_QK = HG_HEADS * HG_DK
HG_V = HG_HEADS * HG_DV
HG_CHUNK = 64
HG_SUB = 16
MLA_HEADS = 8
MLA_NOPE = 64
MLA_ROPE = 32
MLA_V = 64
MLA_Q_LORA = 256
MLA_KV_LORA = 128
MLA_QK = MLA_NOPE + MLA_ROPE
MLA_SCALE = MLA_QK ** -0.5
ROPE_BASE = 10000.0
HEAD_LANES = 128
N_GROUPS = 4
EXPERTS_PER_GROUP = 8
N_EXPERTS = N_GROUPS * EXPERTS_PER_GROUP
TOP_K = 2
D_EXPERT = 512
EPS = 1e-6
ROUTER_LANES = 128

IN_SEGS = (("p_in", 256), ("hq", 512), ("hzf", 512), ("hzb", 512), ("hin", 256), ("hg", 256),
           ("cq", 256), ("ckv", 128), ("kr", 128), ("krs", 128))
IN_AUG = sum(w for _, w in IN_SEGS)
W_IN_MAIN = 2688

VMEM_LIMIT = 56 * 1024 * 1024


def _cparams(sem):
    return pltpu.CompilerParams(dimension_semantics=sem, vmem_limit_bytes=VMEM_LIMIT)


def _split_bf16(a):
    hi = a.astype(BF16)
    lo = (a - hi.astype(F32)).astype(BF16)
    return hi, lo


def _dot(a, b):
    return jnp.dot(a, b, preferred_element_type=F32)


def _dot_nt(a, b):
    return lax.dot_general(a, b, (((1,), (1,)), ((), ())), preferred_element_type=F32)


def _dot_tn(a, b):
    return lax.dot_general(a, b, (((0,), (0,)), ((), ())), preferred_element_type=F32)


def _dot3(a, b):
    a_hi, a_lo = _split_bf16(a)
    b_hi, b_lo = _split_bf16(b)
    return _dot(a_hi, b_hi) + _dot(a_hi, b_lo) + _dot(a_lo, b_hi)


def _dot_exact_rhs(a, b_bf16):
    a_hi = a.astype(BF16)
    r1 = a - a_hi.astype(F32)
    a_mid = r1.astype(BF16)
    a_lo = (r1 - a_mid.astype(F32)).astype(BF16)
    return _dot(a_hi, b_bf16) + _dot(a_mid, b_bf16) + _dot(a_lo, b_bf16)


def _layer_norm(x):
    mu = jnp.mean(x, axis=-1, keepdims=True)
    xc = x - mu
    var = jnp.mean(xc * xc, axis=-1, keepdims=True)
    return xc * lax.rsqrt(var + EPS)


def _sigmoid(x):
    return 1.0 / (1.0 + jnp.exp(-x))


def _silu(x):
    return x * _sigmoid(x)


def _mod_kernel(a_ref, w_ref, b_ref, o_ref):
    o_ref[0] = _dot3(a_ref[...], w_ref[0]) + b_ref[0]


def _mod_all(c_rows, w_mod, b_mod):
    depth, d, n6 = w_mod.shape
    tn = 1536
    rows = c_rows.shape[0]
    return pl.pallas_call(
        _mod_kernel,
        grid=(depth, n6 // tn),
        in_specs=[pl.BlockSpec((rows, d), lambda l, j: (0, 0)),
                  pl.BlockSpec((1, d, tn), lambda l, j: (l, 0, j)),
                  pl.BlockSpec((1, 1, tn), lambda l, j: (l, 0, j))],
        out_specs=pl.BlockSpec((1, rows, tn), lambda l, j: (l, 0, j)),
        out_shape=jax.ShapeDtypeStruct((depth, rows, n6), F32),
        compiler_params=_cparams(("arbitrary", "arbitrary")),
    )(c_rows, w_mod, b_mod.reshape(depth, 1, n6))


def _inproj_kernel(x_ref, sh_ref, sc_ref, w_ref, *out_refs):
    h = _layer_norm(x_ref[...]) * (1.0 + sc_ref[0]) + sh_ref[0]
    hb = h.astype(BF16)
    start = 0
    for (_, width), o_ref in zip(IN_SEGS, out_refs):
        o_ref[...] = _dot(hb, w_ref[:, start:start + width])
        start += width


def _in_proj(x, sh, sc, w_aug, tm, mod_index):
    n, d = x.shape
    row = lambda i: (i, 0)
    mod = lambda i: (mod_index(i), 0, 0)
    return pl.pallas_call(
        _inproj_kernel,
        grid=(n // tm,),
        in_specs=[pl.BlockSpec((tm, d), row),
                  pl.BlockSpec((1, 1, d), mod),
                  pl.BlockSpec((1, 1, d), mod),
                  pl.BlockSpec((d, IN_AUG), lambda i: (0, 0))],
        out_specs=[pl.BlockSpec((tm, w), row) for _, w in IN_SEGS],
        out_shape=[jax.ShapeDtypeStruct((n, w), F32) for _, w in IN_SEGS],
        compiler_params=_cparams(("parallel",)),
    )(x, sh, sc, w_aug)


POOL_HALO = 8


def _pool_kernel(x_ref, prev_ref, next_ref, w_ref, scale_ref, o_ref, *, tp, lat_tiles, tpb_lat, tpb_ctx,
                 n_lat, n_ctx):
    i = pl.program_id(0)
    is_lat = i < lat_tiles
    tile_in_seq = jnp.where(is_lat, i % tpb_lat, (i - lat_tiles) % tpb_ctx)
    tiles_in_seq = jnp.where(is_lat, tpb_lat, tpb_ctx)
    n_seq = jnp.where(is_lat, n_lat, n_ctx)
    first = tile_in_seq == 0
    last = tile_in_seq == tiles_in_seq - 1
    x = x_ref[...]
    prev = jnp.where(first, 0.0, prev_ref[...])
    nxt = jnp.where(last, 0.0, next_ref[...])
    ext = jnp.concatenate([prev, x, nxt], axis=0)
    n_ext = tp + 2 * POOL_HALO

    def back(a, k):
        return pltpu.roll(a, k, 0)

    def fwd(a, k):
        return pltpu.roll(a, n_ext - k, 0)

    e2 = ext + back(ext, 1)
    e4 = back(e2, 1) + fwd(e2, 1)
    e8 = back(e4, 2) + fwd(e4, 2)
    e16 = back(e8, 4) + fwd(e8, 4)
    sl = slice(POOL_HALO, POOL_HALO + tp)
    lane = lax.broadcasted_iota(jnp.int32, (tp, POOL_DIM), 1)
    grp = lane >> 6
    wsum = jnp.where(grp == 0, e2[sl], jnp.where(grp == 1, e4[sl], jnp.where(grp == 2, e8[sl], e16[sl])))
    half = jnp.where(grp == 0, 1, jnp.where(grp == 1, 2, jnp.where(grp == 2, 4, 8)))
    t = tile_in_seq * tp + lax.broadcasted_iota(jnp.int32, (tp, POOL_DIM), 0)
    cnt = jnp.minimum(t + half, n_seq) - jnp.maximum(t - half, 0)
    pooled = wsum / cnt.astype(F32) - x
    o_ref[...] = _dot(pooled.astype(BF16), w_ref[...]) * scale_ref[...]


def _pool(p_in, w_bd, scale, tp, n_lat_rows, n_lat, n_ctx):
    n, c = p_in.shape
    hb = tp // POOL_HALO
    last_halo_block = n // POOL_HALO - 1
    kern = functools.partial(_pool_kernel, tp=tp, lat_tiles=n_lat_rows // tp, tpb_lat=n_lat // tp,
                             tpb_ctx=n_ctx // tp, n_lat=n_lat, n_ctx=n_ctx)
    return pl.pallas_call(
        kern,
        grid=(n // tp,),
        in_specs=[pl.BlockSpec((tp, c), lambda i: (i, 0)),
                  pl.BlockSpec((POOL_HALO, c), lambda i: (jnp.maximum(i * hb - 1, 0), 0)),
                  pl.BlockSpec((POOL_HALO, c), lambda i: (jnp.minimum((i + 1) * hb, last_halo_block), 0)),
                  pl.BlockSpec((c, c), lambda i: (0, 0)),
                  pl.BlockSpec((1, c), lambda i: (0, 0))],
        out_specs=pl.BlockSpec((tp, c), lambda i: (i, 0)),
        out_shape=jax.ShapeDtypeStruct((n, c), F32),
        compiler_params=_cparams(("parallel",)),
    )(p_in, p_in, p_in, w_bd, scale)


def _hgrn_kernel(hq_ref, hz_ref, hin_ref, la_ref, lc_ref, s0_ref, o_ref, sT_ref, q_s, g_s, k_s, st_s,
                 *, reverse, nchunk):
    j = pl.program_id(1)

    @pl.when(j == 0)
    def _():
        st_s[...] = s0_ref[0]

    z = hz_ref[...]
    ls = jnp.minimum(z, 0.0) - jnp.log1p(jnp.exp(-jnp.abs(z)))
    lc = lc_ref[...]
    x2 = lc + ls
    la = la_ref[...]
    g_s[...] = jnp.maximum(la, x2) + jnp.log1p(jnp.exp(-jnp.abs(la - x2)))
    k_s[...] = jnp.exp(x2 - z)
    q_s[...] = _silu(hq_ref[...])

    C = HG_CHUNK
    row = lax.broadcasted_iota(jnp.int32, (C, C), 0)
    col = lax.broadcasted_iota(jnp.int32, (C, C), 1)
    tri = ((row <= col) if reverse else (row >= col)).astype(BF16)
    rmod = lax.broadcasted_iota(jnp.int32, (C, 1), 0) & (HG_SUB - 1)
    nsub = C // HG_SUB
    row_sub = row >> 4
    col_sub = col >> 4

    def chunk(c, carry):
        cc = (nchunk - 1 - c) if reverse else c
        r0 = pl.multiple_of(cc * C, C)
        b_all = _dot_exact_rhs_lhs(tri, g_s[pl.ds(r0, C), :])
        q_all = q_s[pl.ds(r0, C), :]
        k_all = k_s[pl.ds(r0, C), :]
        v_all = hin_ref[pl.ds(r0, C), :]
        outs = []
        for h in range(HG_HEADS):
            ks = slice(h * HG_DK, (h + 1) * HG_DK)
            b, q, k = b_all[:, ks], q_all[:, ks], k_all[:, ks]
            v = v_all[:, h * HG_DV:(h + 1) * HG_DV]
            st = st_s[h]
            b_last = b[0:1] if reverse else b[C - 1:C]
            o = _dot_nt((q * jnp.exp(b)).astype(BF16), st.astype(BF16))
            a_rows = []
            for i in range(nsub):
                rs = slice(i * HG_SUB, (i + 1) * HG_SUB)
                if (reverse and i == nsub - 1) or (not reverse and i == 0):
                    a_rows.append(jnp.zeros((HG_SUB, C), F32))
                    continue
                m = b[(i + 1) * HG_SUB - 1:(i + 1) * HG_SUB] if reverse else b[i * HG_SUB:i * HG_SUB + 1]
                qi = q[rs] * jnp.exp(b[rs] - m)
                ksc = k * jnp.exp(jnp.minimum(m - b, 0.0))
                a_rows.append(_dot_nt(qi.astype(BF16), ksc.astype(BF16)))
            a = jnp.concatenate(a_rows, axis=0)
            if reverse:
                a = jnp.where(col_sub > row_sub, a, 0.0)
            else:
                a = jnp.where(col_sub < row_sub, a, 0.0)
            o = o + _dot(a.astype(BF16), v.astype(BF16))
            for delta in range(HG_SUB):
                if delta == 0:
                    w = jnp.sum(q * k, axis=-1, keepdims=True)
                    o = o + w * v
                    continue
                shift = (C - delta) if reverse else delta
                kd = pltpu.roll(k, shift, 0)
                bd = pltpu.roll(b, shift, 0)
                vd = pltpu.roll(v, shift, 0)
                e = jnp.exp(jnp.minimum(b - bd, 0.0))
                w = jnp.sum(q * kd * e, axis=-1, keepdims=True)
                valid = (rmod + delta < HG_SUB) if reverse else (rmod >= delta)
                o = o + jnp.where(valid, w, 0.0) * vd
            outs.append(o)
            kdec = k * jnp.exp(b_last - b)
            st_s[h] = st * jnp.exp(b_last) + _dot_tn(v.astype(BF16), kdec.astype(BF16))
        o_ref[pl.ds(r0, C), :] = jnp.concatenate(outs, axis=1)
        return carry

    lax.fori_loop(0, nchunk, chunk, 0)

    @pl.when(j == pl.num_programs(1) - 1)
    def _():
        sT_ref[0] = st_s[...]


def _dot_exact_rhs_lhs(tri_bf16, g):
    g_hi = g.astype(BF16)
    r1 = g - g_hi.astype(F32)
    g_mid = r1.astype(BF16)
    g_lo = (r1 - g_mid.astype(F32)).astype(BF16)
    return _dot(tri_bf16, g_hi) + _dot(tri_bf16, g_mid) + _dot(tri_bf16, g_lo)


def _hgrn_scan(hq, hz, hin, la, lc, s0, *, reverse, base_row, n_seq, tb):
    bsz = s0.shape[0]
    nblk = n_seq // tb
    base_blk = base_row // tb

    def rows(b, j):
        jj = (nblk - 1 - j) if reverse else j
        return (base_blk + b * nblk + jj, 0)

    def orow(b, j):
        jj = (nblk - 1 - j) if reverse else j
        return (b * nblk + jj, 0)

    kern = functools.partial(_hgrn_kernel, reverse=reverse, nchunk=tb // HG_CHUNK)
    return pl.pallas_call(
        kern,
        grid=(bsz, nblk),
        in_specs=[pl.BlockSpec((tb, HG_QK), rows),
                  pl.BlockSpec((tb, HG_QK), rows),
                  pl.BlockSpec((tb, HG_V), rows),
                  pl.BlockSpec((1, HG_QK), lambda b, j: (0, 0)),
                  pl.BlockSpec((1, HG_QK), lambda b, j: (0, 0)),
                  pl.BlockSpec((1, HG_HEADS, HG_DV, HG_DK), lambda b, j: (b, 0, 0, 0))],
        out_specs=[pl.BlockSpec((tb, HG_V), orow),
                   pl.BlockSpec((1, HG_HEADS, HG_DV, HG_DK), lambda b, j: (b, 0, 0, 0))],
        out_shape=[jax.ShapeDtypeStruct((bsz * n_seq, HG_V), F32),
                   jax.ShapeDtypeStruct((bsz, HG_HEADS, HG_DV, HG_DK), F32)],
        scratch_shapes=[pltpu.VMEM((tb, HG_QK), F32), pltpu.VMEM((tb, HG_QK), F32),
                        pltpu.VMEM((tb, HG_QK), F32), pltpu.VMEM((HG_HEADS, HG_DV, HG_DK), F32)],
        compiler_params=_cparams(("parallel", "arbitrary")),
    )(hq, hz, hin, la, lc, s0)


def _rms(x, g):
    return x * lax.rsqrt(jnp.mean(x * x, axis=-1, keepdims=True) + EPS) * g


def _mla_proj_kernel(cq_ref, ckv_ref, kr_ref, krs_ref, cos_ref, sin_ref, qg_ref, kvg_ref,
                     wq_ref, wqs_ref, wk_ref, wv_ref, q_out, k_out, v_out):
    cos = cos_ref[...]
    sin = sin_ref[...]
    cos_h = jnp.concatenate([cos] * MLA_HEADS, axis=1)
    sin_h = jnp.concatenate([sin] * MLA_HEADS, axis=1)
    xq = _rms(cq_ref[...], qg_ref[...]).astype(BF16)
    q = _dot(xq, wq_ref[...]) * cos_h + _dot(xq, wqs_ref[...]) * sin_h
    q_out[...] = (q * MLA_SCALE).astype(BF16)
    xkv = _rms(ckv_ref[...], kvg_ref[...]).astype(BF16)
    k_rope = kr_ref[...] * cos + krs_ref[...] * sin
    k = _dot(xkv, wk_ref[...]) + jnp.concatenate([k_rope] * MLA_HEADS, axis=1)
    k_out[...] = k.astype(BF16)
    v_out[...] = _dot(xkv, wv_ref[...]).astype(BF16)


def _mla_proj(cq, ckv, kr, krs, cos_t, sin_t, qg, kvg, wq, wqs, wk, wv, tm):
    n = cq.shape[0]
    row = lambda i: (i, 0)
    full = lambda i: (0, 0)
    hq = MLA_HEADS * HEAD_LANES
    hv = MLA_HEADS * MLA_V
    return pl.pallas_call(
        _mla_proj_kernel,
        grid=(n // tm,),
        in_specs=[pl.BlockSpec((tm, MLA_Q_LORA), row), pl.BlockSpec((tm, MLA_KV_LORA), row),
                  pl.BlockSpec((tm, HEAD_LANES), row), pl.BlockSpec((tm, HEAD_LANES), row),
                  pl.BlockSpec((tm, HEAD_LANES), row), pl.BlockSpec((tm, HEAD_LANES), row),
                  pl.BlockSpec((1, MLA_Q_LORA), full), pl.BlockSpec((1, MLA_KV_LORA), full),
                  pl.BlockSpec((MLA_Q_LORA, hq), full), pl.BlockSpec((MLA_Q_LORA, hq), full),
                  pl.BlockSpec((MLA_KV_LORA, hq), full), pl.BlockSpec((MLA_KV_LORA, hv), full)],
        out_specs=[pl.BlockSpec((tm, hq), row), pl.BlockSpec((tm, hq), row), pl.BlockSpec((tm, hv), row)],
        out_shape=[jax.ShapeDtypeStruct((n, hq), BF16), jax.ShapeDtypeStruct((n, hq), BF16),
                   jax.ShapeDtypeStruct((n, hv), BF16)],
        compiler_params=_cparams(("parallel",)),
    )(cq, ckv, kr, krs, cos_t, sin_t, qg, kvg, wq, wqs, wk, wv)


def _attn_kernel(*refs, tk, n_lat_chunks, with_lat):
    if with_lat:
        q_ref, kl_ref, vl_ref, kc_ref, vc_ref, o_ref, m_s, l_s, acc_s = refs
    else:
        q_ref, kc_ref, vc_ref, o_ref, m_s, l_s, acc_s = refs
        kl_ref = vl_ref = None
    tq = q_ref.shape[0]
    m_s[...] = jnp.full(m_s.shape, -jnp.inf, F32)
    l_s[...] = jnp.zeros(l_s.shape, F32)
    acc_s[...] = jnp.zeros(acc_s.shape, F32)

    def update(a, k, v):
        hs = slice(a * HEAD_LANES, (a + 1) * HEAD_LANES)
        s = _dot_nt(q_ref[:, hs], k[:, hs])
        m_old = m_s[a]
        m_new = jnp.maximum(m_old, jnp.max(s, axis=-1, keepdims=True))
        alpha = jnp.exp(m_old - m_new)
        p = jnp.exp(s - m_new)
        l_s[a] = alpha * l_s[a] + jnp.sum(p, axis=-1, keepdims=True)
        acc_s[a] = alpha * acc_s[a] + _dot(p.astype(BF16), v)
        m_s[a] = m_new

    if with_lat:
        def body(c, carry):
            r0 = pl.multiple_of(c * tk, tk)
            k = kl_ref[pl.ds(r0, tk), :]
            v = vl_ref[pl.ds(r0, tk), :]
            for a in range(2):
                update(a, k, v)
            return carry

        lax.fori_loop(0, n_lat_chunks, body, 0)
    kc = kc_ref[...]
    vc = vc_ref[...]
    for a in range(2):
        update(a, kc, vc)
    lane = lax.broadcasted_iota(jnp.int32, (tq, 2 * MLA_V), 1)
    o = jnp.where(lane < MLA_V, acc_s[0] / l_s[0], acc_s[1] / l_s[1])
    o_ref[...] = o.astype(o_ref.dtype)


def _attention(q, k, v, *, bsz, n_q, q_base, n_lat, n_ctx, ctx_base, tq, tk, with_lat):
    pairs = MLA_HEADS // 2
    qblocks = n_q // tq
    qb0 = q_base // tq
    cb0 = ctx_base // n_ctx
    kw = 2 * HEAD_LANES
    vw = 2 * MLA_V
    in_specs = [pl.BlockSpec((tq, kw), lambda b, p, i: (qb0 + b * qblocks + i, p))]
    args = [q]
    if with_lat:
        in_specs += [pl.BlockSpec((n_lat, kw), lambda b, p, i: (b, p)),
                     pl.BlockSpec((n_lat, vw), lambda b, p, i: (b, p))]
        args += [k, v]
    in_specs += [pl.BlockSpec((n_ctx, kw), lambda b, p, i: (cb0 + b, p)),
                 pl.BlockSpec((n_ctx, vw), lambda b, p, i: (cb0 + b, p))]
    args += [k, v]
    kern = functools.partial(_attn_kernel, tk=tk, n_lat_chunks=(n_lat // tk) if with_lat else 0,
                             with_lat=with_lat)
    return pl.pallas_call(
        kern,
        grid=(bsz, pairs, qblocks),
        in_specs=in_specs,
        out_specs=pl.BlockSpec((tq, vw), lambda b, p, i: (b * qblocks + i, p)),
        out_shape=jax.ShapeDtypeStruct((bsz * n_q, MLA_HEADS * MLA_V), BF16),
        scratch_shapes=[pltpu.VMEM((2, tq, 1), F32), pltpu.VMEM((2, tq, 1), F32),
                        pltpu.VMEM((2, tq, vw), F32)],
        compiler_params=_cparams(("parallel", "parallel", "arbitrary")),
    )(*args)


def _outproj_kernel(a_ref, of_ref, ob_ref, hg_ref, c_ref, x_ref, g1_ref, sh2_ref, sc2_ref,
                    lng_ref, lnb_ref, ng_ref, gavg_ref, w_ref, wr_ref, x1_out, h2_out, lg_out, *, alpha):
    o = of_ref[...] + ob_ref[...]
    o2_hi, o2_lo = _split_bf16(o * o)
    ms = _dot(o2_hi, gavg_ref[...]) + _dot(o2_lo, gavg_ref[...])
    b_lat = o * lax.rsqrt(ms + EPS) * ng_ref[...] * _silu(hg_ref[...])
    m = (_dot(a_ref[...].astype(BF16), w_ref[0:POOL_DIM, :])
         + _dot(b_lat.astype(BF16), w_ref[POOL_DIM:POOL_DIM + HG_V, :])
         + _dot(c_ref[...], w_ref[POOL_DIM + HG_V:, :]))
    x1 = _layer_norm(alpha * x_ref[...] + g1_ref[0] * m) * lng_ref[...] + lnb_ref[...]
    x1_out[...] = x1
    h2 = _layer_norm(x1) * (1.0 + sc2_ref[0]) + sh2_ref[0]
    h2_out[...] = h2.astype(BF16)
    lg_out[...] = _dot3(h2, wr_ref[...])


def _out_proj(a, o_f, o_b, hg, c_att, x, g1, sh2, sc2, ln_g, ln_b, norm_g, gavg, w_out, w_router, tm,
              mod_index, alpha):
    n, d = x.shape
    row = lambda i: (i, 0)
    full = lambda i: (0, 0)
    mod = lambda i: (mod_index(i), 0, 0)
    return pl.pallas_call(
        functools.partial(_outproj_kernel, alpha=alpha),
        grid=(n // tm,),
        in_specs=[pl.BlockSpec((tm, POOL_DIM), row), pl.BlockSpec((tm, HG_V), row), pl.BlockSpec((tm, HG_V), row),
                  pl.BlockSpec((tm, HG_V), row), pl.BlockSpec((tm, MLA_HEADS * MLA_V), row),
                  pl.BlockSpec((tm, d), row),
                  pl.BlockSpec((1, 1, d), mod), pl.BlockSpec((1, 1, d), mod), pl.BlockSpec((1, 1, d), mod),
                  pl.BlockSpec((1, d), full), pl.BlockSpec((1, d), full), pl.BlockSpec((1, HG_V), full),
                  pl.BlockSpec((HG_V, HG_V), full), pl.BlockSpec((d, d), full),
                  pl.BlockSpec((d, ROUTER_LANES), full)],
        out_specs=[pl.BlockSpec((tm, d), row), pl.BlockSpec((tm, d), row), pl.BlockSpec((tm, ROUTER_LANES), row)],
        out_shape=[jax.ShapeDtypeStruct((n, d), F32), jax.ShapeDtypeStruct((n, d), BF16),
                   jax.ShapeDtypeStruct((n, ROUTER_LANES), F32)],
        compiler_params=_cparams(("parallel",)),
    )(a, o_f, o_b, hg, c_att, x, g1, sh2, sc2, ln_g, ln_b, norm_g, gavg, w_out, w_router)


def _moe_kernel(be_ref, nv_ref, x_ref, w1_ref, w3_ref, w2_ref, y_ref):
    i = pl.program_id(0)

    @pl.when(i < nv_ref[0])
    def _():
        x = x_ref[...]
        h1 = _dot(x, w1_ref[0].astype(BF16))
        h3 = _dot(x, w3_ref[0].astype(BF16))
        h = (_silu(h1) * h3).astype(BF16)
        y_ref[...] = _dot(h, w2_ref[0].astype(BF16)).astype(y_ref.dtype)

    @pl.when(i >= nv_ref[0])
    def _():
        y_ref[...] = jnp.zeros(y_ref.shape, y_ref.dtype)


def _moe_ffn(block_e, n_valid, xr, w1, w3, w2, tm):
    n_rows, d = xr.shape
    n_blocks = n_rows // tm
    grid_spec = pltpu.PrefetchScalarGridSpec(
        num_scalar_prefetch=2,
        grid=(n_blocks,),
        in_specs=[pl.BlockSpec((tm, d), lambda i, be, nv: (i, 0)),
                  pl.BlockSpec((1, d, D_EXPERT), lambda i, be, nv: (be[i], 0, 0)),
                  pl.BlockSpec((1, d, D_EXPERT), lambda i, be, nv: (be[i], 0, 0)),
                  pl.BlockSpec((1, D_EXPERT, d), lambda i, be, nv: (be[i], 0, 0))],
        out_specs=pl.BlockSpec((tm, d), lambda i, be, nv: (i, 0)),
    )
    return pl.pallas_call(
        _moe_kernel,
        grid_spec=grid_spec,
        out_shape=jax.ShapeDtypeStruct((n_rows, d), BF16),
        compiler_params=_cparams(("arbitrary",)),
    )(block_e, n_valid, xr, w1, w3, w2)


def _combine_kernel(x_ref, y0_ref, y1_ref, gw_ref, g2_ref, lng_ref, lnb_ref, o_ref, *, alpha):
    gw = gw_ref[...]
    y = gw[:, 0:1] * y0_ref[...].astype(F32) + gw[:, 1:2] * y1_ref[...].astype(F32)
    o_ref[...] = _layer_norm(alpha * x_ref[...] + g2_ref[0] * y) * lng_ref[...] + lnb_ref[...]


def _combine(x, y0, y1, gw, g2, ln_g, ln_b, tm, mod_index, alpha):
    n, d = x.shape
    row = lambda i: (i, 0)
    full = lambda i: (0, 0)
    return pl.pallas_call(
        functools.partial(_combine_kernel, alpha=alpha),
        grid=(n // tm,),
        in_specs=[pl.BlockSpec((tm, d), row), pl.BlockSpec((tm, d), row), pl.BlockSpec((tm, d), row),
                  pl.BlockSpec((tm, ROUTER_LANES), row),
                  pl.BlockSpec((1, 1, d), lambda i: (mod_index(i), 0, 0)),
                  pl.BlockSpec((1, d), full), pl.BlockSpec((1, d), full)],
        out_specs=pl.BlockSpec((tm, d), row),
        out_shape=jax.ShapeDtypeStruct((n, d), F32),
        compiler_params=_cparams(("parallel",)),
    )(x, y0, y1, gw, g2, ln_g, ln_b)


def _rot_half_cols(w):
    lead = w.shape[:-1]
    wr = w.reshape(lead + (2, 2, MLA_ROPE // 4))
    return jnp.stack([-wr[..., 1, :], wr[..., 0, :]], axis=-2).reshape(w.shape)


def _rope_tables(n_lat, n_ctx, bsz):
    rows = n_lat // GRID_W
    row = jnp.repeat(jnp.arange(rows, dtype=F32), GRID_W)
    col = jnp.tile(jnp.arange(GRID_W, dtype=F32), rows)
    n_freq = MLA_ROPE // 4
    inv_freq = ROPE_BASE ** (-jnp.arange(n_freq, dtype=F32) / n_freq)
    ang = jnp.stack([row[:, None] * inv_freq, col[:, None] * inv_freq], axis=1)
    cos = jnp.broadcast_to(jnp.cos(ang)[:, :, None, :], (n_lat, 2, 2, n_freq)).reshape(n_lat, MLA_ROPE)
    sin = jnp.broadcast_to(jnp.sin(ang)[:, :, None, :], (n_lat, 2, 2, n_freq)).reshape(n_lat, MLA_ROPE)
    pad_l = MLA_NOPE
    pad_r = HEAD_LANES - MLA_QK
    cos_t = jnp.concatenate([jnp.ones((n_lat, pad_l), F32), cos, jnp.ones((n_lat, pad_r), F32)], axis=1)
    sin_t = jnp.concatenate([jnp.zeros((n_lat, pad_l), F32), sin, jnp.zeros((n_lat, pad_r), F32)], axis=1)
    cos_all = jnp.concatenate([jnp.tile(cos_t, (bsz, 1)), jnp.ones((bsz * n_ctx, HEAD_LANES), F32)], axis=0)
    sin_all = jnp.concatenate([jnp.tile(sin_t, (bsz, 1)), jnp.zeros((bsz * n_ctx, HEAD_LANES), F32)], axis=0)
    return cos_all, sin_all


def _prep_w_in(w_in):
    depth, d, _ = w_in.shape
    kr = w_in[:, :, W_IN_MAIN:]
    z_l = jnp.zeros((depth, d, MLA_NOPE), w_in.dtype)
    z_r = jnp.zeros((depth, d, HEAD_LANES - MLA_QK), w_in.dtype)
    return jnp.concatenate([w_in[:, :, :W_IN_MAIN], z_l, kr, z_r, z_l, _rot_half_cols(kr), z_r],
                           axis=-1).astype(BF16)


def _prep_mla(w_uq, w_ukv):
    depth = w_uq.shape[0]
    wq = w_uq.reshape(depth, MLA_Q_LORA, MLA_HEADS, MLA_QK)
    pad = jnp.zeros((depth, MLA_Q_LORA, MLA_HEADS, HEAD_LANES - MLA_QK), w_uq.dtype)
    wq_p = jnp.concatenate([wq, pad], axis=-1).reshape(depth, MLA_Q_LORA, MLA_HEADS * HEAD_LANES)
    wq_s = jnp.concatenate([jnp.zeros_like(wq[..., :MLA_NOPE]), _rot_half_cols(wq[..., MLA_NOPE:]), pad],
                           axis=-1).reshape(depth, MLA_Q_LORA, MLA_HEADS * HEAD_LANES)
    wkv = w_ukv.reshape(depth, MLA_KV_LORA, MLA_HEADS, MLA_NOPE + MLA_V)
    kpad = jnp.zeros((depth, MLA_KV_LORA, MLA_HEADS, HEAD_LANES - MLA_NOPE), w_ukv.dtype)
    wk = jnp.concatenate([wkv[..., :MLA_NOPE], kpad], axis=-1).reshape(depth, MLA_KV_LORA, MLA_HEADS * HEAD_LANES)
    wv = wkv[..., MLA_NOPE:].reshape(depth, MLA_KV_LORA, MLA_HEADS * MLA_V)
    return wq_p.astype(BF16), wq_s.astype(BF16), wk.astype(BF16), wv.astype(BF16)


def _block_diag(pool_w):
    depth, g, c, _ = pool_w.shape
    eye = jnp.eye(g, dtype=pool_w.dtype)
    return jnp.einsum("lgcd,gh->lgchd", pool_w, eye).reshape(depth, g * c, g * c).astype(BF16)


def _route(logits, b_rg, b_re, tm_moe):
    n = logits.shape[0]
    lg = logits[:, :N_GROUPS] + b_rg
    le = (logits[:, N_GROUPS:N_GROUPS + N_EXPERTS] + b_re).reshape(n, N_GROUPS, EXPERTS_PER_GROUP)
    pg = jax.nn.softmax(lg, axis=-1)
    g_sel = jnp.argmax(lg, axis=-1)
    w_g = jnp.take_along_axis(pg, g_sel[:, None], axis=-1)[:, 0]
    le_sel = jnp.take_along_axis(le, g_sel[:, None, None], axis=1)[:, 0]
    top_v, top_i = lax.top_k(le_sel, TOP_K)
    gate = jax.nn.softmax(top_v, axis=-1) * w_g[:, None]
    expert = (g_sel[:, None] * EXPERTS_PER_GROUP + top_i).astype(jnp.int32)

    n_assign = n * TOP_K
    e_flat = expert.reshape(-1)
    onehot = (e_flat[:, None] == jnp.arange(N_EXPERTS, dtype=jnp.int32)[None, :]).astype(jnp.int32)
    csum = jnp.cumsum(onehot, axis=0)
    counts = csum[-1]
    rank = jnp.take_along_axis(csum, e_flat[:, None], axis=1)[:, 0] - 1
    pcounts = (counts + tm_moe - 1) // tm_moe * tm_moe
    pends = jnp.cumsum(pcounts)
    pstarts = pends - pcounts
    dest = pstarts[e_flat] + rank
    n_rows = (n_assign + N_EXPERTS * (tm_moe - 1) + tm_moe - 1) // tm_moe * tm_moe
    n_blocks = n_rows // tm_moe
    tok = jnp.repeat(jnp.arange(n, dtype=jnp.int32), TOP_K)
    row_tok = jnp.zeros((n_rows,), jnp.int32).at[dest].set(tok, unique_indices=True)
    n_valid = (pends[-1] // tm_moe).astype(jnp.int32)
    blk = jnp.arange(n_blocks, dtype=jnp.int32)
    block_e = jnp.searchsorted(pends, jnp.minimum(blk, n_valid - 1) * tm_moe, side="right").astype(jnp.int32)
    block_e = jnp.clip(block_e, 0, N_EXPERTS - 1)
    gw = jnp.zeros((n, ROUTER_LANES), F32).at[:, :TOP_K].set(gate)
    return row_tok, dest.reshape(n, TOP_K), gw, block_e, n_valid.reshape(1)


def _tile(n, pref):
    t = pref
    while n % t:
        t //= 2
    return t


def kernel(x, c, ctx, c_ctx, w_mod, b_mod, w_in, pool_w, pool_scale, hg_lb_logits, hg_norm_g, mla_q_norm_g, mla_w_uq, mla_kv_norm_g, mla_w_ukv, w_out, ln1_g, ln1_b, ln2_g, ln2_b, router_group_w, router_group_b, router_expert_w, router_expert_b, expert_w1, expert_w3, expert_w2):
    bsz, n_lat, d = x.shape
    n_ctx = ctx.shape[1]
    depth = w_mod.shape[0]
    alpha = (2 * depth) ** 0.25
    lat_rows = bsz * n_lat
    ctx_rows = bsz * n_ctx
    n_tok = lat_rows + ctx_rows

    tm = _tile(np.gcd(n_lat, ctx_rows), 512)
    tp = _tile(np.gcd(n_lat, n_ctx), 256)
    tb_lat = _tile(n_lat, 512)
    tb_ctx = _tile(n_ctx, 512)
    tq_lat = _tile(n_lat, 512)
    tk = _tile(n_lat, 512)
    tq_ctx = _tile(n_ctx, 512)
    tm_moe = 256
    lat_tiles_per_batch = n_lat // tm
    mod_index = lambda i: jnp.minimum(i // lat_tiles_per_batch, bsz)

    c_rows = jnp.concatenate([jax.nn.silu(c), jax.nn.silu(c_ctx)[None, :],
                              jnp.zeros((-(bsz + 1) % 8, d), F32)], axis=0)
    mod_all = _mod_all(c_rows, w_mod, b_mod)
    lb_all = jnp.cumsum(jax.nn.softmax(hg_lb_logits.astype(F32), axis=0), axis=0)
    lb_all = lb_all - lb_all[0:1]
    log_lb = jnp.log(lb_all)
    log_1m_lb = jnp.log1p(-lb_all)
    w_aug = _prep_w_in(w_in)
    wq_p, wq_s, wk_p, wv_p = _prep_mla(mla_w_uq, mla_w_ukv)
    w_bd = _block_diag(pool_w)
    w_out_b = w_out.astype(BF16)
    w_router = jnp.concatenate([router_group_w, router_expert_w,
                                jnp.zeros((depth, d, ROUTER_LANES - N_GROUPS - N_EXPERTS), F32)], axis=-1)
    gavg = jnp.kron(jnp.eye(HG_HEADS, dtype=F32), jnp.full((HG_DV, HG_DV), 1.0 / HG_DV, F32)).astype(BF16)
    cos_t, sin_t = _rope_tables(n_lat, n_ctx, bsz)
    zero_state = jnp.zeros((bsz, HG_HEADS, HG_DV, HG_DK), F32)

    xt = jnp.concatenate([x.reshape(lat_rows, d), ctx.reshape(ctx_rows, d)], axis=0)
    for l in range(depth):
        mod = mod_all[l, :bsz + 1].reshape(bsz + 1, 6, 1, d)
        sh1, sc1, g1, sh2, sc2, g2 = (mod[:, k] for k in range(6))

        p_in, hq, hzf, hzb, hin, hg, cq, ckv, kr, krs = _in_proj(xt, sh1, sc1, w_aug[l], tm, mod_index)

        a_mix = _pool(p_in, w_bd[l], pool_scale[l][None, :], tp, lat_rows, n_lat, n_ctx)

        la_f, la_b = log_lb[l, 0][None, :], log_lb[l, 1][None, :]
        lc_f, lc_b = log_1m_lb[l, 0][None, :], log_1m_lb[l, 1][None, :]
        oc_f, s_f = _hgrn_scan(hq, hzf, hin, la_f, lc_f, zero_state, reverse=False, base_row=lat_rows,
                               n_seq=n_ctx, tb=tb_ctx)
        oc_b, s_b = _hgrn_scan(hq, hzb, hin, la_b, lc_b, zero_state, reverse=True, base_row=lat_rows,
                               n_seq=n_ctx, tb=tb_ctx)
        ol_f, _ = _hgrn_scan(hq, hzf, hin, la_f, lc_f, s_f, reverse=False, base_row=0, n_seq=n_lat, tb=tb_lat)
        ol_b, _ = _hgrn_scan(hq, hzb, hin, la_b, lc_b, s_b, reverse=True, base_row=0, n_seq=n_lat, tb=tb_lat)
        o_f = jnp.concatenate([ol_f, oc_f], axis=0)
        o_b = jnp.concatenate([ol_b, oc_b], axis=0)

        q_all, k_all, v_all = _mla_proj(cq, ckv, kr, krs, cos_t, sin_t, mla_q_norm_g[l][None, :],
                                        mla_kv_norm_g[l][None, :], wq_p[l], wq_s[l], wk_p[l], wv_p[l], tm)
        c_lat = _attention(q_all, k_all, v_all, bsz=bsz, n_q=n_lat, q_base=0, n_lat=n_lat, n_ctx=n_ctx,
                           ctx_base=lat_rows, tq=tq_lat, tk=tk, with_lat=True)
        c_ctx_o = _attention(q_all, k_all, v_all, bsz=bsz, n_q=n_ctx, q_base=lat_rows, n_lat=n_lat, n_ctx=n_ctx,
                             ctx_base=lat_rows, tq=tq_ctx, tk=tk, with_lat=False)
        c_att = jnp.concatenate([c_lat, c_ctx_o], axis=0)

        x1, h2, logits = _out_proj(a_mix, o_f, o_b, hg, c_att, xt, g1, sh2, sc2, ln1_g[l][None, :],
                                   ln1_b[l][None, :], hg_norm_g[l][None, :], gavg, w_out_b[l], w_router[l],
                                   tm, mod_index, alpha)

        row_tok, dest, gw, block_e, n_valid = _route(logits, router_group_b[l], router_expert_b[l], tm_moe)
        xr = jnp.take(h2, row_tok, axis=0)
        y = _moe_ffn(block_e, n_valid, xr, expert_w1[l], expert_w3[l], expert_w2[l], tm_moe)
        y0 = jnp.take(y, dest[:, 0], axis=0)
        y1 = jnp.take(y, dest[:, 1], axis=0)
        xt = _combine(x1, y0, y1, gw, g2, ln2_g[l][None, :], ln2_b[l][None, :], tm, mod_index, alpha)
    return xt[:lat_rows].reshape(bsz, n_lat, d)
```

```python
import functools

import numpy as np
import jax
import jax.numpy as jnp
from jax import lax
from jax.experimental import pallas as pl
from jax.experimental.pallas import tpu as pltpu

F32 = jnp.float32
BF16 = jnp.bfloat16

D_MODEL = 1024
GRID_W = 64
POOL_WINDOWS = (2, 4, 8, 16)
POOL_DIM = 256
POOL_GROUP_DIM = 64
HG_HEADS = 4
HG_DK = 128
HG_DV = 64
HG_QK = HG_HEADS * HG_DK
HG_V = HG_HEADS * HG_DV
HG_CHUNK = 64
HG_SUB = 16
MLA_HEADS = 8
MLA_NOPE = 64
MLA_ROPE = 32
MLA_V = 64
MLA_Q_LORA = 256
MLA_KV_LORA = 128
MLA_QK = MLA_NOPE + MLA_ROPE
MLA_SCALE = MLA_QK ** -0.5
LOG2_E = 1.4426950408889634
ROPE_BASE = 10000.0
HEAD_LANES = 128
N_GROUPS = 4
EXPERTS_PER_GROUP = 8
N_EXPERTS = N_GROUPS * EXPERTS_PER_GROUP
TOP_K = 2
D_EXPERT = 512
EPS = 1e-6
ROUTER_LANES = 128

IN_SEGS = (("p_in", 256), ("hq", 512), ("hzf", 512), ("hzb", 512), ("hin", 256), ("hg", 256),
           ("cq", 256), ("ckv", 128), ("kr", 128), ("krs", 128))
IN_AUG = sum(w for _, w in IN_SEGS)
W_IN_MAIN = 2688

VMEM_LIMIT = 56 * 1024 * 1024


def _cparams(sem):
    return pltpu.CompilerParams(dimension_semantics=sem, vmem_limit_bytes=VMEM_LIMIT)


def _split_bf16(a):
    hi = a.astype(BF16)
    lo = (a - hi.astype(F32)).astype(BF16)
    return hi, lo


def _dot(a, b):
    return jnp.dot(a, b, preferred_element_type=F32)


def _dot_nt(a, b):
    return lax.dot_general(a, b, (((1,), (1,)), ((), ())), preferred_element_type=F32)


def _dot_tn(a, b):
    return lax.dot_general(a, b, (((0,), (0,)), ((), ())), preferred_element_type=F32)


def _dot3(a, b):
    a_hi, a_lo = _split_bf16(a)
    b_hi, b_lo = _split_bf16(b)
    return _dot(a_hi, b_hi) + _dot(a_hi, b_lo) + _dot(a_lo, b_hi)


def _dot_exact_rhs(a, b_bf16):
    a_hi = a.astype(BF16)
    r1 = a - a_hi.astype(F32)
    a_mid = r1.astype(BF16)
    a_lo = (r1 - a_mid.astype(F32)).astype(BF16)
    return _dot(a_hi, b_bf16) + _dot(a_mid, b_bf16) + _dot(a_lo, b_bf16)


def _layer_norm(x):
    mu = jnp.mean(x, axis=-1, keepdims=True)
    xc = x - mu
    var = jnp.mean(xc * xc, axis=-1, keepdims=True)
    return xc * lax.rsqrt(var + EPS)


def _sigmoid(x):
    return 1.0 / (1.0 + jnp.exp(-x))


def _silu(x):
    return x * _sigmoid(x)


def _mod_kernel(a_ref, w_ref, b_ref, o_ref):
    o_ref[0] = _dot3(a_ref[...], w_ref[0]) + b_ref[0]


def _mod_all(c_rows, w_mod, b_mod):
    depth, d, n6 = w_mod.shape
    tn = 1536
    rows = c_rows.shape[0]
    return pl.pallas_call(
        _mod_kernel,
        grid=(depth, n6 // tn),
        in_specs=[pl.BlockSpec((rows, d), lambda l, j: (0, 0)),
                  pl.BlockSpec((1, d, tn), lambda l, j: (l, 0, j)),
                  pl.BlockSpec((1, 1, tn), lambda l, j: (l, 0, j))],
        out_specs=pl.BlockSpec((1, rows, tn), lambda l, j: (l, 0, j)),
        out_shape=jax.ShapeDtypeStruct((depth, rows, n6), F32),
        name="adaln_mod",
        compiler_params=_cparams(("arbitrary", "arbitrary")),
    )(c_rows, w_mod, b_mod.reshape(depth, 1, n6))


def _inproj_kernel(x_ref, sh_ref, sc_ref, w_ref, *out_refs):
    h = _layer_norm(x_ref[...]) * (1.0 + sc_ref[0]) + sh_ref[0]
    hb = h.astype(BF16)
    start = 0
    for (_, width), o_ref in zip(IN_SEGS, out_refs):
        o_ref[...] = _dot(hb, w_ref[:, start:start + width])
        start += width


def _in_proj(x, sh, sc, w_aug, tm, mod_index):
    n, d = x.shape
    row = lambda i: (i, 0)
    mod = lambda i: (mod_index(i), 0, 0)
    return pl.pallas_call(
        _inproj_kernel,
        grid=(n // tm,),
        in_specs=[pl.BlockSpec((tm, d), row),
                  pl.BlockSpec((1, 1, d), mod),
                  pl.BlockSpec((1, 1, d), mod),
                  pl.BlockSpec((d, IN_AUG), lambda i: (0, 0))],
        out_specs=[pl.BlockSpec((tm, w), row) for _, w in IN_SEGS],
        out_shape=[jax.ShapeDtypeStruct((n, w), F32) for _, w in IN_SEGS],
        name="in_proj",
        compiler_params=_cparams(("parallel",)),
    )(x, sh, sc, w_aug)


POOL_HALO = 8


def _pool_kernel(x_ref, prev_ref, next_ref, w_ref, scale_ref, o_ref, *, tp, lat_tiles, tpb_lat, tpb_ctx,
                 n_lat, n_ctx):
    i = pl.program_id(0)
    is_lat = i < lat_tiles
    tile_in_seq = jnp.where(is_lat, i % tpb_lat, (i - lat_tiles) % tpb_ctx)
    tiles_in_seq = jnp.where(is_lat, tpb_lat, tpb_ctx)
    n_seq = jnp.where(is_lat, n_lat, n_ctx)
    first = tile_in_seq == 0
    last = tile_in_seq == tiles_in_seq - 1
    x = x_ref[...]
    prev = jnp.where(first, 0.0, prev_ref[...])
    nxt = jnp.where(last, 0.0, next_ref[...])
    ext = jnp.concatenate([prev, x, nxt], axis=0)
    n_ext = tp + 2 * POOL_HALO

    def back(a, k):
        return pltpu.roll(a, k, 0)

    def fwd(a, k):
        return pltpu.roll(a, n_ext - k, 0)

    e2 = ext + back(ext, 1)
    e4 = back(e2, 1) + fwd(e2, 1)
    e8 = back(e4, 2) + fwd(e4, 2)
    e16 = back(e8, 4) + fwd(e8, 4)
    sl = slice(POOL_HALO, POOL_HALO + tp)
    lane = lax.broadcasted_iota(jnp.int32, (tp, POOL_DIM), 1)
    grp = lane >> 6
    wsum = jnp.where(grp == 0, e2[sl], jnp.where(grp == 1, e4[sl], jnp.where(grp == 2, e8[sl], e16[sl])))
    half = jnp.where(grp == 0, 1, jnp.where(grp == 1, 2, jnp.where(grp == 2, 4, 8)))
    t = tile_in_seq * tp + lax.broadcasted_iota(jnp.int32, (tp, POOL_DIM), 0)
    cnt = jnp.minimum(t + half, n_seq) - jnp.maximum(t - half, 0)
    pooled = wsum / cnt.astype(F32) - x
    o_ref[...] = _dot(pooled.astype(BF16), w_ref[...]) * scale_ref[...]


def _pool(p_in, w_bd, scale, tp, n_lat_rows, n_lat, n_ctx):
    n, c = p_in.shape
    hb = tp // POOL_HALO
    last_halo_block = n // POOL_HALO - 1
    kern = functools.partial(_pool_kernel, tp=tp, lat_tiles=n_lat_rows // tp, tpb_lat=n_lat // tp,
                             tpb_ctx=n_ctx // tp, n_lat=n_lat, n_ctx=n_ctx)
    return pl.pallas_call(
        kern,
        grid=(n // tp,),
        in_specs=[pl.BlockSpec((tp, c), lambda i: (i, 0)),
                  pl.BlockSpec((POOL_HALO, c), lambda i: (jnp.maximum(i * hb - 1, 0), 0)),
                  pl.BlockSpec((POOL_HALO, c), lambda i: (jnp.minimum((i + 1) * hb, last_halo_block), 0)),
                  pl.BlockSpec((c, c), lambda i: (0, 0)),
                  pl.BlockSpec((1, c), lambda i: (0, 0))],
        out_specs=pl.BlockSpec((tp, c), lambda i: (i, 0)),
        out_shape=jax.ShapeDtypeStruct((n, c), F32),
        name="pool",
        compiler_params=_cparams(("parallel",)),
    )(p_in, p_in, p_in, w_bd, scale)


def _hgrn_kernel(hq_ref, hz_ref, hin_ref, la_ref, lc_ref, s0_ref, o_ref, sT_ref, q_s, g_s, k_s, st_s,
                 *, reverse, nchunk):
    j = pl.program_id(1)

    @pl.when(j == 0)
    def _():
        st_s[...] = s0_ref[0]

    z = hz_ref[...]
    ls = jnp.minimum(z, 0.0) - jnp.log1p(jnp.exp(-jnp.abs(z)))
    lc = lc_ref[...]
    x2 = lc + ls
    la = la_ref[...]
    g_s[...] = jnp.maximum(la, x2) + jnp.log1p(jnp.exp(-jnp.abs(la - x2)))
    k_s[...] = jnp.exp(x2 - z)
    q_s[...] = _silu(hq_ref[...])

    C = HG_CHUNK
    row = lax.broadcasted_iota(jnp.int32, (C, C), 0)
    col = lax.broadcasted_iota(jnp.int32, (C, C), 1)
    tri = ((row <= col) if reverse else (row >= col)).astype(BF16)
    rmod = lax.broadcasted_iota(jnp.int32, (C, 1), 0) & (HG_SUB - 1)
    nsub = C // HG_SUB
    row_sub = row >> 4
    col_sub = col >> 4

    def chunk(c, carry):
        cc = (nchunk - 1 - c) if reverse else c
        r0 = pl.multiple_of(cc * C, C)
        b_all = _dot_exact_rhs_lhs(tri, g_s[pl.ds(r0, C), :])
        q_all = q_s[pl.ds(r0, C), :]
        k_all = k_s[pl.ds(r0, C), :]
        v_all = hin_ref[pl.ds(r0, C), :]
        outs = []
        for h in range(HG_HEADS):
            ks = slice(h * HG_DK, (h + 1) * HG_DK)
            b, q, k = b_all[:, ks], q_all[:, ks], k_all[:, ks]
            v = v_all[:, h * HG_DV:(h + 1) * HG_DV]
            st = st_s[h]
            b_last = b[0:1] if reverse else b[C - 1:C]
            o = _dot_nt((q * jnp.exp(b)).astype(BF16), st.astype(BF16))
            a_rows = []
            for i in range(nsub):
                rs = slice(i * HG_SUB, (i + 1) * HG_SUB)
                if (reverse and i == nsub - 1) or (not reverse and i == 0):
                    a_rows.append(jnp.zeros((HG_SUB, C), F32))
                    continue
                m = b[(i + 1) * HG_SUB - 1:(i + 1) * HG_SUB] if reverse else b[i * HG_SUB:i * HG_SUB + 1]
                qi = q[rs] * jnp.exp(b[rs] - m)
                ksc = k * jnp.exp(jnp.minimum(m - b, 0.0))
                a_rows.append(_dot_nt(qi.astype(BF16), ksc.astype(BF16)))
            a = jnp.concatenate(a_rows, axis=0)
            if reverse:
                a = jnp.where(col_sub > row_sub, a, 0.0)
            else:
                a = jnp.where(col_sub < row_sub, a, 0.0)
            o = o + _dot(a.astype(BF16), v.astype(BF16))
            for delta in range(HG_SUB):
                if delta == 0:
                    w = jnp.sum(q * k, axis=-1, keepdims=True)
                    o = o + w * v
                    continue
                shift = (C - delta) if reverse else delta
                kd = pltpu.roll(k, shift, 0)
                bd = pltpu.roll(b, shift, 0)
                vd = pltpu.roll(v, shift, 0)
                e = jnp.exp(jnp.minimum(b - bd, 0.0))
                w = jnp.sum(q * kd * e, axis=-1, keepdims=True)
                valid = (rmod + delta < HG_SUB) if reverse else (rmod >= delta)
                o = o + jnp.where(valid, w, 0.0) * vd
            outs.append(o)
            kdec = k * jnp.exp(b_last - b)
            st_s[h] = st * jnp.exp(b_last) + _dot_tn(v.astype(BF16), kdec.astype(BF16))
        o_ref[pl.ds(r0, C), :] = jnp.concatenate(outs, axis=1)
        return carry

    lax.fori_loop(0, nchunk, chunk, 0)

    @pl.when(j == pl.num_programs(1) - 1)
    def _():
        sT_ref[0] = st_s[...]


def _dot_exact_rhs_lhs(tri_bf16, g):
    g_hi = g.astype(BF16)
    r1 = g - g_hi.astype(F32)
    g_mid = r1.astype(BF16)
    g_lo = (r1 - g_mid.astype(F32)).astype(BF16)
    return _dot(tri_bf16, g_hi) + _dot(tri_bf16, g_mid) + _dot(tri_bf16, g_lo)


def _hgrn_scan(hq, hz, hin, la, lc, s0, *, reverse, base_row, n_seq, tb):
    bsz = s0.shape[0]
    nblk = n_seq // tb
    base_blk = base_row // tb

    def rows(b, j):
        jj = (nblk - 1 - j) if reverse else j
        return (base_blk + b * nblk + jj, 0)

    def orow(b, j):
        jj = (nblk - 1 - j) if reverse else j
        return (b * nblk + jj, 0)

    kern = functools.partial(_hgrn_kernel, reverse=reverse, nchunk=tb // HG_CHUNK)
    return pl.pallas_call(
        kern,
        grid=(bsz, nblk),
        in_specs=[pl.BlockSpec((tb, HG_QK), rows),
                  pl.BlockSpec((tb, HG_QK), rows),
                  pl.BlockSpec((tb, HG_V), rows),
                  pl.BlockSpec((1, HG_QK), lambda b, j: (0, 0)),
                  pl.BlockSpec((1, HG_QK), lambda b, j: (0, 0)),
                  pl.BlockSpec((1, HG_HEADS, HG_DV, HG_DK), lambda b, j: (b, 0, 0, 0))],
        out_specs=[pl.BlockSpec((tb, HG_V), orow),
                   pl.BlockSpec((1, HG_HEADS, HG_DV, HG_DK), lambda b, j: (b, 0, 0, 0))],
        out_shape=[jax.ShapeDtypeStruct((bsz * n_seq, HG_V), F32),
                   jax.ShapeDtypeStruct((bsz, HG_HEADS, HG_DV, HG_DK), F32)],
        scratch_shapes=[pltpu.VMEM((tb, HG_QK), F32), pltpu.VMEM((tb, HG_QK), F32),
                        pltpu.VMEM((tb, HG_QK), F32), pltpu.VMEM((HG_HEADS, HG_DV, HG_DK), F32)],
        name="hgrn_bwd" if reverse else "hgrn_fwd",
        compiler_params=_cparams(("parallel", "arbitrary")),
    )(hq, hz, hin, la, lc, s0)


def _rms(x, g):
    return x * lax.rsqrt(jnp.mean(x * x, axis=-1, keepdims=True) + EPS) * g


def _mla_proj_kernel(cq_ref, ckv_ref, kr_ref, krs_ref, cos_ref, sin_ref, qg_ref, kvg_ref,
                     wq_ref, wqs_ref, wk_ref, wvt_ref, q_out, k_out, vt_out):
    cos = cos_ref[...]
    sin = sin_ref[...]
    cos_h = jnp.concatenate([cos] * MLA_HEADS, axis=1)
    sin_h = jnp.concatenate([sin] * MLA_HEADS, axis=1)
    xq = _rms(cq_ref[...], qg_ref[...]).astype(BF16)
    q = _dot(xq, wq_ref[...]) * cos_h + _dot(xq, wqs_ref[...]) * sin_h
    q_out[...] = (q * (MLA_SCALE * LOG2_E)).astype(BF16)
    xkv = _rms(ckv_ref[...], kvg_ref[...]).astype(BF16)
    k_rope = kr_ref[...] * cos + krs_ref[...] * sin
    k = _dot(xkv, wk_ref[...]) + jnp.concatenate([k_rope] * MLA_HEADS, axis=1)
    k_out[...] = k.astype(BF16)
    vt_out[0] = _dot_nt(wvt_ref[...], xkv).astype(BF16)


def _mla_proj(cq, ckv, kr, krs, cos_t, sin_t, qg, kvg, wq, wqs, wk, wvt, tm):
    n = cq.shape[0]
    row = lambda i: (i, 0)
    full = lambda i: (0, 0)
    hq = MLA_HEADS * HEAD_LANES
    hv = MLA_HEADS * MLA_V
    return pl.pallas_call(
        _mla_proj_kernel,
        grid=(n // tm,),
        in_specs=[pl.BlockSpec((tm, MLA_Q_LORA), row), pl.BlockSpec((tm, MLA_KV_LORA), row),
                  pl.BlockSpec((tm, HEAD_LANES), row), pl.BlockSpec((tm, HEAD_LANES), row),
                  pl.BlockSpec((tm, HEAD_LANES), row), pl.BlockSpec((tm, HEAD_LANES), row),
                  pl.BlockSpec((1, MLA_Q_LORA), full), pl.BlockSpec((1, MLA_KV_LORA), full),
                  pl.BlockSpec((MLA_Q_LORA, hq), full), pl.BlockSpec((MLA_Q_LORA, hq), full),
                  pl.BlockSpec((MLA_KV_LORA, hq), full), pl.BlockSpec((hv, MLA_KV_LORA), full)],
        out_specs=[pl.BlockSpec((tm, hq), row), pl.BlockSpec((tm, hq), row),
                   pl.BlockSpec((1, hv, tm), lambda i: (i, 0, 0))],
        out_shape=[jax.ShapeDtypeStruct((n, hq), BF16), jax.ShapeDtypeStruct((n, hq), BF16),
                   jax.ShapeDtypeStruct((n // tm, hv, tm), BF16)],
        name="mla_proj",
        compiler_params=_cparams(("parallel",)),
    )(cq, ckv, kr, krs, cos_t, sin_t, qg, kvg, wq, wqs, wk, wvt)


ACC_ROWS = MLA_V + 16
LAT_UNROLL = 2


def _attn_kernel(*refs, tkc, n_lat_chunks, n_ctx_chunks):
    if n_lat_chunks:
        q_ref, kl_ref, vl_ref, kc_ref, vc_ref, o_ref, m_s, acc_s, s_s = refs
    else:
        q_ref, kc_ref, vc_ref, o_ref, m_s, acc_s, s_s = refs
    m_s[...] = jnp.full(m_s.shape, -jnp.inf, F32)
    acc_s[...] = jnp.zeros(acc_s.shape, F32)
    ones = jnp.ones((ACC_ROWS - MLA_V, tkc), BF16)

    def scores(a, k_ref, r0):
        hs = slice(a * HEAD_LANES, (a + 1) * HEAD_LANES)
        return _dot_nt(k_ref[pl.ds(r0, tkc), hs], q_ref[:, hs])

    def accumulate(a, s_t, vt):
        m_old = m_s[a]
        m_new = jnp.maximum(m_old, jnp.max(s_t, axis=0, keepdims=True))
        alpha = jnp.exp2(m_old - m_new)
        p_t = jnp.exp2(s_t - m_new).astype(BF16)
        vta = jnp.concatenate([vt[a * MLA_V:(a + 1) * MLA_V], ones], axis=0)
        acc_s[a] = alpha * acc_s[a] + _dot(vta, p_t)
        m_s[a] = m_new

    def run(items, s_first, after):
        s_next = s_first
        for idx, (a, _, _, vt) in enumerate(items):
            s_cur = s_next
            nxt = items[idx + 1] if idx + 1 < len(items) else after
            if nxt is not None:
                s_next = scores(*nxt[:3])
            accumulate(a, s_cur, vt)
        return s_next

    def lat_items(c0, dynamic):
        items = []
        for u in range(LAT_UNROLL):
            cc = c0 + u
            r0 = pl.multiple_of(cc * tkc, tkc) if dynamic else cc * tkc
            items += [(0, kl_ref, r0, vl_ref[cc]), (1, kl_ref, r0, vl_ref[cc])]
        return items

    ctx_items = []
    for c in range(n_ctx_chunks):
        ctx_items += [(0, kc_ref, c * tkc, vc_ref[c]), (1, kc_ref, c * tkc, vc_ref[c])]

    if n_lat_chunks:
        n_groups = n_lat_chunks // LAT_UNROLL
        s_s[...] = scores(0, kl_ref, 0)

        def body(g, carry):
            c0 = g * LAT_UNROLL
            r_next = pl.multiple_of((c0 + LAT_UNROLL) * tkc, tkc)
            s_s[...] = run(lat_items(c0, True), s_s[...], (0, kl_ref, r_next))
            return carry

        lax.fori_loop(0, n_groups - 1, body, 0)
        run(lat_items((n_groups - 1) * LAT_UNROLL, False) + ctx_items, s_s[...], None)
    else:
        run(ctx_items, scores(*ctx_items[0][:3]), None)
    outs = []
    for a in range(2):
        acc = acc_s[a]
        outs.append(acc[0:MLA_V] / acc[MLA_V:MLA_V + 1])
    o_ref[...] = jnp.concatenate(outs, axis=0).T.astype(o_ref.dtype)


def _attention(q, k, vt, *, bsz, n_q, q_base, n_lat, n_ctx, ctx_base, tq, tkc, with_lat):
    pairs = MLA_HEADS // 2
    qblocks = n_q // tq
    qb0 = q_base // tq
    cb0 = ctx_base // n_ctx
    kw = 2 * HEAD_LANES
    vw = 2 * MLA_V
    in_specs = [pl.BlockSpec((tq, kw), lambda b, p, i: (qb0 + b * qblocks + i, p))]
    args = [q]
    if with_lat:
        in_specs += [pl.BlockSpec((n_lat, kw), lambda b, p, i: (b, p)),
                     pl.BlockSpec((n_lat // tkc, vw, tkc), lambda b, p, i: (b, p, 0))]
        args += [k, vt]
    in_specs += [pl.BlockSpec((n_ctx, kw), lambda b, p, i: (cb0 + b, p)),
                 pl.BlockSpec((n_ctx // tkc, vw, tkc), lambda b, p, i: (cb0 + b, p, 0))]
    args += [k, vt]
    kern = functools.partial(_attn_kernel, tkc=tkc, n_lat_chunks=(n_lat // tkc) if with_lat else 0,
                             n_ctx_chunks=n_ctx // tkc)
    return pl.pallas_call(
        kern,
        grid=(bsz, pairs, qblocks),
        in_specs=in_specs,
        out_specs=pl.BlockSpec((tq, vw), lambda b, p, i: (b * qblocks + i, p)),
        out_shape=jax.ShapeDtypeStruct((bsz * n_q, MLA_HEADS * MLA_V), BF16),
        scratch_shapes=[pltpu.VMEM((2, 1, tq), F32), pltpu.VMEM((2, ACC_ROWS, tq), F32),
                        pltpu.VMEM((tkc, tq), F32)],
        name="attn_lat" if with_lat else "attn_ctx",
        compiler_params=_cparams(("parallel", "parallel", "arbitrary")),
    )(*args)


def _outproj_kernel(a_ref, of_ref, ob_ref, hg_ref, c_ref, x_ref, g1_ref, sh2_ref, sc2_ref,
                    lng_ref, lnb_ref, ng_ref, gavg_ref, w_ref, wr_ref, x1_out, h2_out, lg_out, *, alpha):
    o = of_ref[...] + ob_ref[...]
    o2_hi, o2_lo = _split_bf16(o * o)
    ms = _dot(o2_hi, gavg_ref[...]) + _dot(o2_lo, gavg_ref[...])
    b_lat = o * lax.rsqrt(ms + EPS) * ng_ref[...] * _silu(hg_ref[...])
    m = (_dot(a_ref[...].astype(BF16), w_ref[0:POOL_DIM, :])
         + _dot(b_lat.astype(BF16), w_ref[POOL_DIM:POOL_DIM + HG_V, :])
         + _dot(c_ref[...], w_ref[POOL_DIM + HG_V:, :]))
    x1 = _layer_norm(alpha * x_ref[...] + g1_ref[0] * m) * lng_ref[...] + lnb_ref[...]
    x1_out[...] = x1
    h2 = _layer_norm(x1) * (1.0 + sc2_ref[0]) + sh2_ref[0]
    h2_out[...] = h2.astype(BF16)
    lg_out[...] = _dot3(h2, wr_ref[...])


def _out_proj(a, o_f, o_b, hg, c_att, x, g1, sh2, sc2, ln_g, ln_b, norm_g, gavg, w_out, w_router, tm,
              mod_index, alpha):
    n, d = x.shape
    row = lambda i: (i, 0)
    full = lambda i: (0, 0)
    mod = lambda i: (mod_index(i), 0, 0)
    return pl.pallas_call(
        functools.partial(_outproj_kernel, alpha=alpha),
        grid=(n // tm,),
        in_specs=[pl.BlockSpec((tm, POOL_DIM), row), pl.BlockSpec((tm, HG_V), row), pl.BlockSpec((tm, HG_V), row),
                  pl.BlockSpec((tm, HG_V), row), pl.BlockSpec((tm, MLA_HEADS * MLA_V), row),
                  pl.BlockSpec((tm, d), row),
                  pl.BlockSpec((1, 1, d), mod), pl.BlockSpec((1, 1, d), mod), pl.BlockSpec((1, 1, d), mod),
                  pl.BlockSpec((1, d), full), pl.BlockSpec((1, d), full), pl.BlockSpec((1, HG_V), full),
                  pl.BlockSpec((HG_V, HG_V), full), pl.BlockSpec((d, d), full),
                  pl.BlockSpec((d, ROUTER_LANES), full)],
        out_specs=[pl.BlockSpec((tm, d), row), pl.BlockSpec((tm, d), row), pl.BlockSpec((tm, ROUTER_LANES), row)],
        out_shape=[jax.ShapeDtypeStruct((n, d), F32), jax.ShapeDtypeStruct((n, d), BF16),
                   jax.ShapeDtypeStruct((n, ROUTER_LANES), F32)],
        name="out_proj",
        compiler_params=_cparams(("parallel",)),
    )(a, o_f, o_b, hg, c_att, x, g1, sh2, sc2, ln_g, ln_b, norm_g, gavg, w_out, w_router)


def _moe_kernel(be_ref, nv_ref, x_ref, w1_ref, w3_ref, w2_ref, y_ref):
    i = pl.program_id(0)

    @pl.when(i < nv_ref[0])
    def _():
        x = x_ref[...]
        h1 = _dot(x, w1_ref[0].astype(BF16))
        h3 = _dot(x, w3_ref[0].astype(BF16))
        h = (_silu(h1) * h3).astype(BF16)
        y_ref[...] = _dot(h, w2_ref[0].astype(BF16)).astype(y_ref.dtype)

    @pl.when(i >= nv_ref[0])
    def _():
        y_ref[...] = jnp.zeros(y_ref.shape, y_ref.dtype)


def _moe_ffn(block_e, n_valid, xr, w1, w3, w2, tm):
    n_rows, d = xr.shape
    n_blocks = n_rows // tm
    grid_spec = pltpu.PrefetchScalarGridSpec(
        num_scalar_prefetch=2,
        grid=(n_blocks,),
        in_specs=[pl.BlockSpec((tm, d), lambda i, be, nv: (i, 0)),
                  pl.BlockSpec((1, d, D_EXPERT), lambda i, be, nv: (be[i], 0, 0)),
                  pl.BlockSpec((1, d, D_EXPERT), lambda i, be, nv: (be[i], 0, 0)),
                  pl.BlockSpec((1, D_EXPERT, d), lambda i, be, nv: (be[i], 0, 0))],
        out_specs=pl.BlockSpec((tm, d), lambda i, be, nv: (i, 0)),
    )
    return pl.pallas_call(
        _moe_kernel,
        grid_spec=grid_spec,
        out_shape=jax.ShapeDtypeStruct((n_rows, d), BF16),
        name="moe_ffn",
        compiler_params=_cparams(("arbitrary",)),
    )(block_e, n_valid, xr, w1, w3, w2)


def _combine_kernel(x_ref, y0_ref, y1_ref, gw_ref, g2_ref, lng_ref, lnb_ref, o_ref, *, alpha):
    gw = gw_ref[...]
    y = gw[:, 0:1] * y0_ref[...].astype(F32) + gw[:, 1:2] * y1_ref[...].astype(F32)
    o_ref[...] = _layer_norm(alpha * x_ref[...] + g2_ref[0] * y) * lng_ref[...] + lnb_ref[...]


def _combine(x, y0, y1, gw, g2, ln_g, ln_b, tm, mod_index, alpha):
    n, d = x.shape
    row = lambda i: (i, 0)
    full = lambda i: (0, 0)
    return pl.pallas_call(
        functools.partial(_combine_kernel, alpha=alpha),
        grid=(n // tm,),
        in_specs=[pl.BlockSpec((tm, d), row), pl.BlockSpec((tm, d), row), pl.BlockSpec((tm, d), row),
                  pl.BlockSpec((tm, ROUTER_LANES), row),
                  pl.BlockSpec((1, 1, d), lambda i: (mod_index(i), 0, 0)),
                  pl.BlockSpec((1, d), full), pl.BlockSpec((1, d), full)],
        out_specs=pl.BlockSpec((tm, d), row),
        out_shape=jax.ShapeDtypeStruct((n, d), F32),
        name="moe_combine",
        compiler_params=_cparams(("parallel",)),
    )(x, y0, y1, gw, g2, ln_g, ln_b)


def _rot_half_cols(w):
    lead = w.shape[:-1]
    wr = w.reshape(lead + (2, 2, MLA_ROPE // 4))
    return jnp.stack([-wr[..., 1, :], wr[..., 0, :]], axis=-2).reshape(w.shape)


def _rope_tables(n_lat, n_ctx, bsz):
    rows = n_lat // GRID_W
    row = jnp.repeat(jnp.arange(rows, dtype=F32), GRID_W)
    col = jnp.tile(jnp.arange(GRID_W, dtype=F32), rows)
    n_freq = MLA_ROPE // 4
    inv_freq = ROPE_BASE ** (-jnp.arange(n_freq, dtype=F32) / n_freq)
    ang = jnp.stack([row[:, None] * inv_freq, col[:, None] * inv_freq], axis=1)
    cos = jnp.broadcast_to(jnp.cos(ang)[:, :, None, :], (n_lat, 2, 2, n_freq)).reshape(n_lat, MLA_ROPE)
    sin = jnp.broadcast_to(jnp.sin(ang)[:, :, None, :], (n_lat, 2, 2, n_freq)).reshape(n_lat, MLA_ROPE)
    pad_l = MLA_NOPE
    pad_r = HEAD_LANES - MLA_QK
    cos_t = jnp.concatenate([jnp.ones((n_lat, pad_l), F32), cos, jnp.ones((n_lat, pad_r), F32)], axis=1)
    sin_t = jnp.concatenate([jnp.zeros((n_lat, pad_l), F32), sin, jnp.zeros((n_lat, pad_r), F32)], axis=1)
    cos_all = jnp.concatenate([jnp.tile(cos_t, (bsz, 1)), jnp.ones((bsz * n_ctx, HEAD_LANES), F32)], axis=0)
    sin_all = jnp.concatenate([jnp.tile(sin_t, (bsz, 1)), jnp.zeros((bsz * n_ctx, HEAD_LANES), F32)], axis=0)
    return cos_all, sin_all


def _prep_w_in(w_in):
    depth, d, _ = w_in.shape
    kr = w_in[:, :, W_IN_MAIN:]
    z_l = jnp.zeros((depth, d, MLA_NOPE), w_in.dtype)
    z_r = jnp.zeros((depth, d, HEAD_LANES - MLA_QK), w_in.dtype)
    return jnp.concatenate([w_in[:, :, :W_IN_MAIN], z_l, kr, z_r, z_l, _rot_half_cols(kr), z_r],
                           axis=-1).astype(BF16)


def _prep_mla(w_uq, w_ukv):
    depth = w_uq.shape[0]
    wq = w_uq.reshape(depth, MLA_Q_LORA, MLA_HEADS, MLA_QK)
    pad = jnp.zeros((depth, MLA_Q_LORA, MLA_HEADS, HEAD_LANES - MLA_QK), w_uq.dtype)
    wq_p = jnp.concatenate([wq, pad], axis=-1).reshape(depth, MLA_Q_LORA, MLA_HEADS * HEAD_LANES)
    wq_s = jnp.concatenate([jnp.zeros_like(wq[..., :MLA_NOPE]), _rot_half_cols(wq[..., MLA_NOPE:]), pad],
                           axis=-1).reshape(depth, MLA_Q_LORA, MLA_HEADS * HEAD_LANES)
    wkv = w_ukv.reshape(depth, MLA_KV_LORA, MLA_HEADS, MLA_NOPE + MLA_V)
    kpad = jnp.zeros((depth, MLA_KV_LORA, MLA_HEADS, HEAD_LANES - MLA_NOPE), w_ukv.dtype)
    wk = jnp.concatenate([wkv[..., :MLA_NOPE], kpad], axis=-1).reshape(depth, MLA_KV_LORA, MLA_HEADS * HEAD_LANES)
    wvt = jnp.swapaxes(wkv[..., MLA_NOPE:].reshape(depth, MLA_KV_LORA, MLA_HEADS * MLA_V), 1, 2)
    return wq_p.astype(BF16), wq_s.astype(BF16), wk.astype(BF16), wvt.astype(BF16)


def _block_diag(pool_w):
    depth, g, c, _ = pool_w.shape
    eye = jnp.eye(g, dtype=pool_w.dtype)
    return jnp.einsum("lgcd,gh->lgchd", pool_w, eye).reshape(depth, g * c, g * c).astype(BF16)


def _route(logits, b_rg, b_re, tm_moe):
    n = logits.shape[0]
    lg = logits[:, :N_GROUPS] + b_rg
    le = (logits[:, N_GROUPS:N_GROUPS + N_EXPERTS] + b_re).reshape(n, N_GROUPS, EXPERTS_PER_GROUP)
    pg = jax.nn.softmax(lg, axis=-1)
    g_sel = jnp.argmax(lg, axis=-1).astype(jnp.int32)
    g_hot = g_sel[:, None] == jnp.arange(N_GROUPS, dtype=jnp.int32)[None, :]
    w_g = jnp.sum(jnp.where(g_hot, pg, 0.0), axis=-1)
    le_sel = jnp.sum(jnp.where(g_hot[:, :, None], le, 0.0), axis=1)
    top_v, top_i = lax.top_k(le_sel, TOP_K)
    gate = jax.nn.softmax(top_v, axis=-1) * w_g[:, None]
    expert = (g_sel[:, None] * EXPERTS_PER_GROUP + top_i).astype(jnp.int32)

    n_assign = n * TOP_K
    e_flat = expert.reshape(-1)
    hot = e_flat[:, None] == jnp.arange(N_EXPERTS, dtype=jnp.int32)[None, :]
    csum = jnp.cumsum(hot.astype(jnp.int32), axis=0)
    counts = csum[-1]
    pcounts = (counts + tm_moe - 1) // tm_moe * tm_moe
    pends = jnp.cumsum(pcounts)
    pstarts = pends - pcounts
    dest = jnp.sum(jnp.where(hot, csum - 1 + pstarts[None, :], 0), axis=1)
    n_rows = (n_assign + N_EXPERTS * (tm_moe - 1) + tm_moe - 1) // tm_moe * tm_moe
    n_blocks = n_rows // tm_moe
    tok = jnp.repeat(jnp.arange(n, dtype=jnp.int32), TOP_K)
    row_tok = jnp.zeros((n_rows,), jnp.int32).at[dest].set(tok, unique_indices=True)
    n_valid = (pends[-1] // tm_moe).astype(jnp.int32)
    blk = jnp.arange(n_blocks, dtype=jnp.int32)
    block_e = jnp.searchsorted(pends, jnp.minimum(blk, n_valid - 1) * tm_moe, side="right").astype(jnp.int32)
    block_e = jnp.clip(block_e, 0, N_EXPERTS - 1)
    gw = jnp.zeros((n, ROUTER_LANES), F32).at[:, :TOP_K].set(gate)
    return row_tok, dest.reshape(n, TOP_K), gw, block_e, n_valid.reshape(1)


def _tile(n, pref):
    t = pref
    while n % t:
        t //= 2
    return t


def kernel(x, c, ctx, c_ctx, w_mod, b_mod, w_in, pool_w, pool_scale, hg_lb_logits, hg_norm_g, mla_q_norm_g, mla_w_uq, mla_kv_norm_g, mla_w_ukv, w_out, ln1_g, ln1_b, ln2_g, ln2_b, router_group_w, router_group_b, router_expert_w, router_expert_b, expert_w1, expert_w3, expert_w2):
    bsz, n_lat, d = x.shape
    n_ctx = ctx.shape[1]
    depth = w_mod.shape[0]
    alpha = (2 * depth) ** 0.25
    lat_rows = bsz * n_lat
    ctx_rows = bsz * n_ctx
    n_tok = lat_rows + ctx_rows

    tm = _tile(np.gcd(n_lat, ctx_rows), 512)
    tp = _tile(np.gcd(n_lat, n_ctx), 256)
    tb_lat = _tile(n_lat, 512)
    tb_ctx = _tile(n_ctx, 512)
    tq_lat = _tile(n_lat, 512)
    tkc = _tile(np.gcd(n_lat, n_ctx), 256)
    tq_ctx = _tile(n_ctx, 512)
    tm_moe = 256
    lat_tiles_per_batch = n_lat // tm
    mod_index = lambda i: jnp.minimum(i // lat_tiles_per_batch, bsz)

    c_rows = jnp.concatenate([jax.nn.silu(c), jax.nn.silu(c_ctx)[None, :],
                              jnp.zeros((-(bsz + 1) % 8, d), F32)], axis=0)
    mod_all = _mod_all(c_rows, w_mod, b_mod)
    lb_all = jnp.cumsum(jax.nn.softmax(hg_lb_logits.astype(F32), axis=0), axis=0)
    lb_all = lb_all - lb_all[0:1]
    log_lb = jnp.log(lb_all)
    log_1m_lb = jnp.log1p(-lb_all)
    w_aug = _prep_w_in(w_in)
    wq_p, wq_s, wk_p, wvt_p = _prep_mla(mla_w_uq, mla_w_ukv)
    w_bd = _block_diag(pool_w)
    w_out_b = w_out.astype(BF16)
    w_router = jnp.concatenate([router_group_w, router_expert_w,
                                jnp.zeros((depth, d, ROUTER_LANES - N_GROUPS - N_EXPERTS), F32)], axis=-1)
    gavg = jnp.kron(jnp.eye(HG_HEADS, dtype=F32), jnp.full((HG_DV, HG_DV), 1.0 / HG_DV, F32)).astype(BF16)
    cos_t, sin_t = _rope_tables(n_lat, n_ctx, bsz)
    zero_state = jnp.zeros((bsz, HG_HEADS, HG_DV, HG_DK), F32)

    xt = jnp.concatenate([x.reshape(lat_rows, d), ctx.reshape(ctx_rows, d)], axis=0)
    for l in range(depth):
        mod = mod_all[l, :bsz + 1].reshape(bsz + 1, 6, 1, d)
        sh1, sc1, g1, sh2, sc2, g2 = (mod[:, k] for k in range(6))

        p_in, hq, hzf, hzb, hin, hg, cq, ckv, kr, krs = _in_proj(xt, sh1, sc1, w_aug[l], tm, mod_index)

        a_mix = _pool(p_in, w_bd[l], pool_scale[l][None, :], tp, lat_rows, n_lat, n_ctx)

        la_f, la_b = log_lb[l, 0][None, :], log_lb[l, 1][None, :]
        lc_f, lc_b = log_1m_lb[l, 0][None, :], log_1m_lb[l, 1][None, :]
        oc_f, s_f = _hgrn_scan(hq, hzf, hin, la_f, lc_f, zero_state, reverse=False, base_row=lat_rows,
                               n_seq=n_ctx, tb=tb_ctx)
        oc_b, s_b = _hgrn_scan(hq, hzb, hin, la_b, lc_b, zero_state, reverse=True, base_row=lat_rows,
                               n_seq=n_ctx, tb=tb_ctx)
        ol_f, _ = _hgrn_scan(hq, hzf, hin, la_f, lc_f, s_f, reverse=False, base_row=0, n_seq=n_lat, tb=tb_lat)
        ol_b, _ = _hgrn_scan(hq, hzb, hin, la_b, lc_b, s_b, reverse=True, base_row=0, n_seq=n_lat, tb=tb_lat)
        o_f = jnp.concatenate([ol_f, oc_f], axis=0)
        o_b = jnp.concatenate([ol_b, oc_b], axis=0)

        q_all, k_all, vt_all = _mla_proj(cq, ckv, kr, krs, cos_t, sin_t, mla_q_norm_g[l][None, :],
                                         mla_kv_norm_g[l][None, :], wq_p[l], wq_s[l], wk_p[l], wvt_p[l], tkc)
        c_lat = _attention(q_all, k_all, vt_all, bsz=bsz, n_q=n_lat, q_base=0, n_lat=n_lat, n_ctx=n_ctx,
                           ctx_base=lat_rows, tq=tq_lat, tkc=tkc, with_lat=True)
        c_ctx_o = _attention(q_all, k_all, vt_all, bsz=bsz, n_q=n_ctx, q_base=lat_rows, n_lat=n_lat, n_ctx=n_ctx,
                             ctx_base=lat_rows, tq=tq_ctx, tkc=tkc, with_lat=False)
        c_att = jnp.concatenate([c_lat, c_ctx_o], axis=0)

        x1, h2, logits = _out_proj(a_mix, o_f, o_b, hg, c_att, xt, g1, sh2, sc2, ln1_g[l][None, :],
                                   ln1_b[l][None, :], hg_norm_g[l][None, :], gavg, w_out_b[l], w_router[l],
                                   tm, mod_index, alpha)

        row_tok, dest, gw, block_e, n_valid = _route(logits, router_group_b[l], router_expert_b[l], tm_moe)
        xr = jnp.take(h2, row_tok, axis=0)
        y = _moe_ffn(block_e, n_valid, xr, expert_w1[l], expert_w3[l], expert_w2[l], tm_moe)
        y0 = jnp.take(y, dest[:, 0], axis=0)
        y1 = jnp.take(y, dest[:, 1], axis=0)
        xt = _combine(x1, y0, y1, gw, g2, ln2_g[l][None, :], ln2_b[l][None, :], tm, mod_index, alpha)
    return xt[:lat_rows].reshape(bsz, n_lat, d)
```

```python
import functools

import numpy as np
import jax
import jax.numpy as jnp
from jax import lax
from jax.experimental import pallas as pl
from jax.experimental.pallas import tpu as pltpu

F32 = jnp.float32
BF16 = jnp.bfloat16

D_MODEL = 1024
GRID_W = 64
POOL_WINDOWS = (2, 4, 8, 16)
POOL_DIM = 256
POOL_GROUP_DIM = 64
HG_HEADS = 4
HG_DK = 128
HG_DV = 64
HG_QK = HG_HEADS * HG_DK
HG_V = HG_HEADS * HG_DV
HG_CHUNK = 64
HG_SUB = 16
MLA_HEADS = 8
MLA_NOPE = 64
MLA_ROPE = 32
MLA_V = 64
MLA_Q_LORA = 256
MLA_KV_LORA = 128
MLA_QK = MLA_NOPE + MLA_ROPE
MLA_SCALE = MLA_QK ** -0.5
LOG2_E = 1.4426950408889634
ROPE_BASE = 10000.0
HEAD_LANES = 128
N_GROUPS = 4
EXPERTS_PER_GROUP = 8
N_EXPERTS = N_GROUPS * EXPERTS_PER_GROUP
TOP_K = 2
D_EXPERT = 512
EPS = 1e-6
ROUTER_LANES = 128

IN_SEGS = (("p_in", 256), ("hq", 512), ("hzf", 512), ("hzb", 512), ("hin", 256), ("hg", 256),
           ("cq", 256), ("ckv", 128), ("kr", 128), ("krs", 128))
IN_AUG = sum(w for _, w in IN_SEGS)
W_IN_MAIN = 2688

VMEM_LIMIT = 56 * 1024 * 1024


def _cparams(sem):
    return pltpu.CompilerParams(dimension_semantics=sem, vmem_limit_bytes=VMEM_LIMIT)


def _split_bf16(a):
    hi = a.astype(BF16)
    lo = (a - hi.astype(F32)).astype(BF16)
    return hi, lo


def _dot(a, b):
    return jnp.dot(a, b, preferred_element_type=F32)


def _dot_nt(a, b):
    return lax.dot_general(a, b, (((1,), (1,)), ((), ())), preferred_element_type=F32)


def _dot_tn(a, b):
    return lax.dot_general(a, b, (((0,), (0,)), ((), ())), preferred_element_type=F32)


def _dot3(a, b):
    a_hi, a_lo = _split_bf16(a)
    b_hi, b_lo = _split_bf16(b)
    return _dot(a_hi, b_hi) + _dot(a_hi, b_lo) + _dot(a_lo, b_hi)


def _dot_exact_rhs(a, b_bf16):
    a_hi = a.astype(BF16)
    r1 = a - a_hi.astype(F32)
    a_mid = r1.astype(BF16)
    a_lo = (r1 - a_mid.astype(F32)).astype(BF16)
    return _dot(a_hi, b_bf16) + _dot(a_mid, b_bf16) + _dot(a_lo, b_bf16)


def _layer_norm(x):
    mu = jnp.mean(x, axis=-1, keepdims=True)
    xc = x - mu
    var = jnp.mean(xc * xc, axis=-1, keepdims=True)
    return xc * lax.rsqrt(var + EPS)


def _sigmoid(x):
    return 1.0 / (1.0 + jnp.exp(-x))


def _silu(x):
    return x * _sigmoid(x)


def _mod_kernel(a_ref, w_ref, b_ref, o_ref):
    o_ref[0] = _dot3(a_ref[...], w_ref[0]) + b_ref[0]


def _mod_all(c_rows, w_mod, b_mod):
    depth, d, n6 = w_mod.shape
    tn = 1536
    rows = c_rows.shape[0]
    return pl.pallas_call(
        _mod_kernel,
        grid=(depth, n6 // tn),
        in_specs=[pl.BlockSpec((rows, d), lambda l, j: (0, 0)),
                  pl.BlockSpec((1, d, tn), lambda l, j: (l, 0, j)),
                  pl.BlockSpec((1, 1, tn), lambda l, j: (l, 0, j))],
        out_specs=pl.BlockSpec((1, rows, tn), lambda l, j: (l, 0, j)),
        out_shape=jax.ShapeDtypeStruct((depth, rows, n6), F32),
        name="adaln_mod",
        compiler_params=_cparams(("arbitrary", "arbitrary")),
    )(c_rows, w_mod, b_mod.reshape(depth, 1, n6))


def _inproj_kernel(x_ref, sh_ref, sc_ref, w_ref, *out_refs):
    h = _layer_norm(x_ref[...]) * (1.0 + sc_ref[0]) + sh_ref[0]
    hb = h.astype(BF16)
    start = 0
    for (_, width), o_ref in zip(IN_SEGS, out_refs):
        o_ref[...] = _dot(hb, w_ref[:, start:start + width])
        start += width


def _in_proj(x, sh, sc, w_aug, tm, mod_index):
    n, d = x.shape
    row = lambda i: (i, 0)
    mod = lambda i: (mod_index(i), 0, 0)
    return pl.pallas_call(
        _inproj_kernel,
        grid=(n // tm,),
        in_specs=[pl.BlockSpec((tm, d), row),
                  pl.BlockSpec((1, 1, d), mod),
                  pl.BlockSpec((1, 1, d), mod),
                  pl.BlockSpec((d, IN_AUG), lambda i: (0, 0))],
        out_specs=[pl.BlockSpec((tm, w), row) for _, w in IN_SEGS],
        out_shape=[jax.ShapeDtypeStruct((n, w), F32) for _, w in IN_SEGS],
        name="in_proj",
        compiler_params=_cparams(("parallel",)),
    )(x, sh, sc, w_aug)


POOL_HALO = 8


def _pool_kernel(x_ref, prev_ref, next_ref, w_ref, scale_ref, o_ref, *, tp, lat_tiles, tpb_lat, tpb_ctx,
                 n_lat, n_ctx):
    i = pl.program_id(0)
    is_lat = i < lat_tiles
    tile_in_seq = jnp.where(is_lat, i % tpb_lat, (i - lat_tiles) % tpb_ctx)
    tiles_in_seq = jnp.where(is_lat, tpb_lat, tpb_ctx)
    n_seq = jnp.where(is_lat, n_lat, n_ctx)
    first = tile_in_seq == 0
    last = tile_in_seq == tiles_in_seq - 1
    x = x_ref[...]
    prev = jnp.where(first, 0.0, prev_ref[...])
    nxt = jnp.where(last, 0.0, next_ref[...])
    ext = jnp.concatenate([prev, x, nxt], axis=0)
    n_ext = tp + 2 * POOL_HALO

    def back(a, k):
        return pltpu.roll(a, k, 0)

    def fwd(a, k):
        return pltpu.roll(a, n_ext - k, 0)

    e2 = ext + back(ext, 1)
    e4 = back(e2, 1) + fwd(e2, 1)
    e8 = back(e4, 2) + fwd(e4, 2)
    e16 = back(e8, 4) + fwd(e8, 4)
    sl = slice(POOL_HALO, POOL_HALO + tp)
    lane = lax.broadcasted_iota(jnp.int32, (tp, POOL_DIM), 1)
    grp = lane >> 6
    wsum = jnp.where(grp == 0, e2[sl], jnp.where(grp == 1, e4[sl], jnp.where(grp == 2, e8[sl], e16[sl])))
    half = jnp.where(grp == 0, 1, jnp.where(grp == 1, 2, jnp.where(grp == 2, 4, 8)))
    t = tile_in_seq * tp + lax.broadcasted_iota(jnp.int32, (tp, POOL_DIM), 0)
    cnt = jnp.minimum(t + half, n_seq) - jnp.maximum(t - half, 0)
    pooled = wsum / cnt.astype(F32) - x
    o_ref[...] = _dot(pooled.astype(BF16), w_ref[...]) * scale_ref[...]


def _pool(p_in, w_bd, scale, tp, n_lat_rows, n_lat, n_ctx):
    n, c = p_in.shape
    hb = tp // POOL_HALO
    last_halo_block = n // POOL_HALO - 1
    kern = functools.partial(_pool_kernel, tp=tp, lat_tiles=n_lat_rows // tp, tpb_lat=n_lat // tp,
                             tpb_ctx=n_ctx // tp, n_lat=n_lat, n_ctx=n_ctx)
    return pl.pallas_call(
        kern,
        grid=(n // tp,),
        in_specs=[pl.BlockSpec((tp, c), lambda i: (i, 0)),
                  pl.BlockSpec((POOL_HALO, c), lambda i: (jnp.maximum(i * hb - 1, 0), 0)),
                  pl.BlockSpec((POOL_HALO, c), lambda i: (jnp.minimum((i + 1) * hb, last_halo_block), 0)),
                  pl.BlockSpec((c, c), lambda i: (0, 0)),
                  pl.BlockSpec((1, c), lambda i: (0, 0))],
        out_specs=pl.BlockSpec((tp, c), lambda i: (i, 0)),
        out_shape=jax.ShapeDtypeStruct((n, c), F32),
        name="pool",
        compiler_params=_cparams(("parallel",)),
    )(p_in, p_in, p_in, w_bd, scale)


def _hgrn_kernel(hq_ref, hz_ref, hin_ref, la_ref, lc_ref, s0_ref, o_ref, sT_ref, q_s, g_s, k_s, st_s,
                 *, reverse, nchunk):
    j = pl.program_id(1)

    @pl.when(j == 0)
    def _():
        st_s[...] = s0_ref[0]

    z = hz_ref[...]
    ls = jnp.minimum(z, 0.0) - jnp.log1p(jnp.exp(-jnp.abs(z)))
    lc = lc_ref[...]
    x2 = lc + ls
    la = la_ref[...]
    g_s[...] = jnp.maximum(la, x2) + jnp.log1p(jnp.exp(-jnp.abs(la - x2)))
    k_s[...] = jnp.exp(x2 - z)
    q_s[...] = _silu(hq_ref[...])

    C = HG_CHUNK
    row = lax.broadcasted_iota(jnp.int32, (C, C), 0)
    col = lax.broadcasted_iota(jnp.int32, (C, C), 1)
    tri = ((row <= col) if reverse else (row >= col)).astype(BF16)
    rmod = lax.broadcasted_iota(jnp.int32, (C, 1), 0) & (HG_SUB - 1)
    nsub = C // HG_SUB
    row_sub = row >> 4
    col_sub = col >> 4

    def chunk(c, carry):
        cc = (nchunk - 1 - c) if reverse else c
        r0 = pl.multiple_of(cc * C, C)
        b_all = _dot_exact_rhs_lhs(tri, g_s[pl.ds(r0, C), :])
        q_all = q_s[pl.ds(r0, C), :]
        k_all = k_s[pl.ds(r0, C), :]
        v_all = hin_ref[pl.ds(r0, C), :]
        outs = []
        for h in range(HG_HEADS):
            ks = slice(h * HG_DK, (h + 1) * HG_DK)
            b, q, k = b_all[:, ks], q_all[:, ks], k_all[:, ks]
            v = v_all[:, h * HG_DV:(h + 1) * HG_DV]
            st = st_s[h]
            b_last = b[0:1] if reverse else b[C - 1:C]
            o = _dot_nt((q * jnp.exp(b)).astype(BF16), st.astype(BF16))
            a_rows = []
            for i in range(nsub):
                rs = slice(i * HG_SUB, (i + 1) * HG_SUB)
                if (reverse and i == nsub - 1) or (not reverse and i == 0):
                    a_rows.append(jnp.zeros((HG_SUB, C), F32))
                    continue
                m = b[(i + 1) * HG_SUB - 1:(i + 1) * HG_SUB] if reverse else b[i * HG_SUB:i * HG_SUB + 1]
                qi = q[rs] * jnp.exp(b[rs] - m)
                ksc = k * jnp.exp(jnp.minimum(m - b, 0.0))
                a_rows.append(_dot_nt(qi.astype(BF16), ksc.astype(BF16)))
            a = jnp.concatenate(a_rows, axis=0)
            if reverse:
                a = jnp.where(col_sub > row_sub, a, 0.0)
            else:
                a = jnp.where(col_sub < row_sub, a, 0.0)
            o = o + _dot(a.astype(BF16), v.astype(BF16))
            for delta in range(HG_SUB):
                if delta == 0:
                    w = jnp.sum(q * k, axis=-1, keepdims=True)
                    o = o + w * v
                    continue
                shift = (C - delta) if reverse else delta
                kd = pltpu.roll(k, shift, 0)
                bd = pltpu.roll(b, shift, 0)
                vd = pltpu.roll(v, shift, 0)
                e = jnp.exp(jnp.minimum(b - bd, 0.0))
                w = jnp.sum(q * kd * e, axis=-1, keepdims=True)
                valid = (rmod + delta < HG_SUB) if reverse else (rmod >= delta)
                o = o + jnp.where(valid, w, 0.0) * vd
            outs.append(o)
            kdec = k * jnp.exp(b_last - b)
            st_s[h] = st * jnp.exp(b_last) + _dot_tn(v.astype(BF16), kdec.astype(BF16))
        o_ref[pl.ds(r0, C), :] = jnp.concatenate(outs, axis=1)
        return carry

    lax.fori_loop(0, nchunk, chunk, 0)

    @pl.when(j == pl.num_programs(1) - 1)
    def _():
        sT_ref[0] = st_s[...]


def _dot_exact_rhs_lhs(tri_bf16, g):
    g_hi = g.astype(BF16)
    r1 = g - g_hi.astype(F32)
    g_mid = r1.astype(BF16)
    g_lo = (r1 - g_mid.astype(F32)).astype(BF16)
    return _dot(tri_bf16, g_hi) + _dot(tri_bf16, g_mid) + _dot(tri_bf16, g_lo)


def _hgrn_scan(hq, hz, hin, la, lc, s0, *, reverse, base_row, n_seq, tb):
    bsz = s0.shape[0]
    nblk = n_seq // tb
    base_blk = base_row // tb

    def rows(b, j):
        jj = (nblk - 1 - j) if reverse else j
        return (base_blk + b * nblk + jj, 0)

    def orow(b, j):
        jj = (nblk - 1 - j) if reverse else j
        return (b * nblk + jj, 0)

    kern = functools.partial(_hgrn_kernel, reverse=reverse, nchunk=tb // HG_CHUNK)
    return pl.pallas_call(
        kern,
        grid=(bsz, nblk),
        in_specs=[pl.BlockSpec((tb, HG_QK), rows),
                  pl.BlockSpec((tb, HG_QK), rows),
                  pl.BlockSpec((tb, HG_V), rows),
                  pl.BlockSpec((1, HG_QK), lambda b, j: (0, 0)),
                  pl.BlockSpec((1, HG_QK), lambda b, j: (0, 0)),
                  pl.BlockSpec((1, HG_HEADS, HG_DV, HG_DK), lambda b, j: (b, 0, 0, 0))],
        out_specs=[pl.BlockSpec((tb, HG_V), orow),
                   pl.BlockSpec((1, HG_HEADS, HG_DV, HG_DK), lambda b, j: (b, 0, 0, 0))],
        out_shape=[jax.ShapeDtypeStruct((bsz * n_seq, HG_V), F32),
                   jax.ShapeDtypeStruct((bsz, HG_HEADS, HG_DV, HG_DK), F32)],
        scratch_shapes=[pltpu.VMEM((tb, HG_QK), F32), pltpu.VMEM((tb, HG_QK), F32),
                        pltpu.VMEM((tb, HG_QK), F32), pltpu.VMEM((HG_HEADS, HG_DV, HG_DK), F32)],
        name="hgrn_bwd" if reverse else "hgrn_fwd",
        compiler_params=_cparams(("parallel", "arbitrary")),
    )(hq, hz, hin, la, lc, s0)


def _rms(x, g):
    return x * lax.rsqrt(jnp.mean(x * x, axis=-1, keepdims=True) + EPS) * g


def _mla_proj_kernel(cq_ref, ckv_ref, kr_ref, krs_ref, cos_ref, sin_ref, qg_ref, kvg_ref,
                     wq_ref, wqs_ref, wk_ref, wvt_ref, q_out, k_out, vt_out):
    cos = cos_ref[...]
    sin = sin_ref[...]
    cos_h = jnp.concatenate([cos] * MLA_HEADS, axis=1)
    sin_h = jnp.concatenate([sin] * MLA_HEADS, axis=1)
    xq = _rms(cq_ref[...], qg_ref[...]).astype(BF16)
    q = _dot(xq, wq_ref[...]) * cos_h + _dot(xq, wqs_ref[...]) * sin_h
    q_out[...] = (q * (MLA_SCALE * LOG2_E)).astype(BF16)
    xkv = _rms(ckv_ref[...], kvg_ref[...]).astype(BF16)
    k_rope = kr_ref[...] * cos + krs_ref[...] * sin
    k = _dot(xkv, wk_ref[...]) + jnp.concatenate([k_rope] * MLA_HEADS, axis=1)
    k_out[...] = k.astype(BF16)
    vt_out[0] = _dot_nt(wvt_ref[...], xkv).astype(BF16)


def _mla_proj(cq, ckv, kr, krs, cos_t, sin_t, qg, kvg, wq, wqs, wk, wvt, tm):
    n = cq.shape[0]
    row = lambda i: (i, 0)
    full = lambda i: (0, 0)
    hq = MLA_HEADS * HEAD_LANES
    hv = MLA_HEADS * MLA_V
    return pl.pallas_call(
        _mla_proj_kernel,
        grid=(n // tm,),
        in_specs=[pl.BlockSpec((tm, MLA_Q_LORA), row), pl.BlockSpec((tm, MLA_KV_LORA), row),
                  pl.BlockSpec((tm, HEAD_LANES), row), pl.BlockSpec((tm, HEAD_LANES), row),
                  pl.BlockSpec((tm, HEAD_LANES), row), pl.BlockSpec((tm, HEAD_LANES), row),
                  pl.BlockSpec((1, MLA_Q_LORA), full), pl.BlockSpec((1, MLA_KV_LORA), full),
                  pl.BlockSpec((MLA_Q_LORA, hq), full), pl.BlockSpec((MLA_Q_LORA, hq), full),
                  pl.BlockSpec((MLA_KV_LORA, hq), full), pl.BlockSpec((hv, MLA_KV_LORA), full)],
        out_specs=[pl.BlockSpec((tm, hq), row), pl.BlockSpec((tm, hq), row),
                   pl.BlockSpec((1, hv, tm), lambda i: (i, 0, 0))],
        out_shape=[jax.ShapeDtypeStruct((n, hq), BF16), jax.ShapeDtypeStruct((n, hq), BF16),
                   jax.ShapeDtypeStruct((n // tm, hv, tm), BF16)],
        name="mla_proj",
        compiler_params=_cparams(("parallel",)),
    )(cq, ckv, kr, krs, cos_t, sin_t, qg, kvg, wq, wqs, wk, wvt)


ACC_ROWS = MLA_V + 16
LAT_UNROLL = 2


def _attn_kernel(*refs, tkc, n_lat_chunks, n_ctx_chunks):
    if n_lat_chunks:
        q_ref, kl_ref, vl_ref, kc_ref, vc_ref, o_ref, m_s, acc_s, s_s = refs
    else:
        q_ref, kc_ref, vc_ref, o_ref, m_s, acc_s, s_s = refs
    m_s[...] = jnp.full(m_s.shape, -jnp.inf, F32)
    acc_s[...] = jnp.zeros(acc_s.shape, F32)
    ones = jnp.ones((ACC_ROWS - MLA_V, tkc), BF16)

    def scores(a, k_ref, r0):
        hs = slice(a * HEAD_LANES, (a + 1) * HEAD_LANES)
        return _dot_nt(k_ref[pl.ds(r0, tkc), hs], q_ref[:, hs])

    def accumulate(a, s_t, vt):
        m_old = m_s[a]
        m_new = jnp.maximum(m_old, jnp.max(s_t, axis=0, keepdims=True))
        alpha = jnp.exp2(m_old - m_new)
        p_t = jnp.exp2(s_t - m_new).astype(BF16)
        vta = jnp.concatenate([vt[a * MLA_V:(a + 1) * MLA_V], ones], axis=0)
        acc_s[a] = alpha * acc_s[a] + _dot(vta, p_t)
        m_s[a] = m_new

    def run(items, s_first, after):
        s_next = s_first
        for idx, (a, _, _, vt) in enumerate(items):
            s_cur = s_next
            nxt = items[idx + 1] if idx + 1 < len(items) else after
            if nxt is not None:
                s_next = scores(*nxt[:3])
            accumulate(a, s_cur, vt)
        return s_next

    def lat_items(c0, dynamic):
        items = []
        for u in range(LAT_UNROLL):
            cc = c0 + u
            r0 = pl.multiple_of(cc * tkc, tkc) if dynamic else cc * tkc
            items += [(0, kl_ref, r0, vl_ref[cc]), (1, kl_ref, r0, vl_ref[cc])]
        return items

    ctx_items = []
    for c in range(n_ctx_chunks):
        ctx_items += [(0, kc_ref, c * tkc, vc_ref[c]), (1, kc_ref, c * tkc, vc_ref[c])]

    if n_lat_chunks:
        n_groups = n_lat_chunks // LAT_UNROLL
        s_s[...] = scores(0, kl_ref, 0)

        def body(g, carry):
            c0 = g * LAT_UNROLL
            r_next = pl.multiple_of((c0 + LAT_UNROLL) * tkc, tkc)
            s_s[...] = run(lat_items(c0, True), s_s[...], (0, kl_ref, r_next))
            return carry

        lax.fori_loop(0, n_groups - 1, body, 0)
        run(lat_items((n_groups - 1) * LAT_UNROLL, False) + ctx_items, s_s[...], None)
    else:
        run(ctx_items, scores(*ctx_items[0][:3]), None)
    outs = []
    for a in range(2):
        acc = acc_s[a]
        outs.append(acc[0:MLA_V] / acc[MLA_V:MLA_V + 1])
    o_ref[...] = jnp.concatenate(outs, axis=0).T.astype(o_ref.dtype)


def _attention(q, k, vt, *, bsz, n_q, q_base, n_lat, n_ctx, ctx_base, tq, tkc, with_lat):
    pairs = MLA_HEADS // 2
    qblocks = n_q // tq
    qb0 = q_base // tq
    cb0 = ctx_base // n_ctx
    kw = 2 * HEAD_LANES
    vw = 2 * MLA_V
    in_specs = [pl.BlockSpec((tq, kw), lambda b, p, i: (qb0 + b * qblocks + i, p))]
    args = [q]
    if with_lat:
        in_specs += [pl.BlockSpec((n_lat, kw), lambda b, p, i: (b, p)),
                     pl.BlockSpec((n_lat // tkc, vw, tkc), lambda b, p, i: (b, p, 0))]
        args += [k, vt]
    in_specs += [pl.BlockSpec((n_ctx, kw), lambda b, p, i: (cb0 + b, p)),
                 pl.BlockSpec((n_ctx // tkc, vw, tkc), lambda b, p, i: (cb0 + b, p, 0))]
    args += [k, vt]
    kern = functools.partial(_attn_kernel, tkc=tkc, n_lat_chunks=(n_lat // tkc) if with_lat else 0,
                             n_ctx_chunks=n_ctx // tkc)
    return pl.pallas_call(
        kern,
        grid=(bsz, pairs, qblocks),
        in_specs=in_specs,
        out_specs=pl.BlockSpec((tq, vw), lambda b, p, i: (b * qblocks + i, p)),
        out_shape=jax.ShapeDtypeStruct((bsz * n_q, MLA_HEADS * MLA_V), BF16),
        scratch_shapes=[pltpu.VMEM((2, 1, tq), F32), pltpu.VMEM((2, ACC_ROWS, tq), F32),
                        pltpu.VMEM((tkc, tq), F32)],
        name="attn_lat" if with_lat else "attn_ctx",
        compiler_params=_cparams(("parallel", "parallel", "arbitrary")),
    )(*args)


def _outproj_kernel(a_ref, of_ref, ob_ref, hg_ref, c_ref, x_ref, g1_ref, sh2_ref, sc2_ref,
                    lng_ref, lnb_ref, ng_ref, gavg_ref, w_ref, wr_ref, x1_out, h2_out, lg_out, *, alpha):
    o = of_ref[...] + ob_ref[...]
    o2_hi, o2_lo = _split_bf16(o * o)
    ms = _dot(o2_hi, gavg_ref[...]) + _dot(o2_lo, gavg_ref[...])
    b_lat = o * lax.rsqrt(ms + EPS) * ng_ref[...] * _silu(hg_ref[...])
    m = (_dot(a_ref[...].astype(BF16), w_ref[0:POOL_DIM, :])
         + _dot(b_lat.astype(BF16), w_ref[POOL_DIM:POOL_DIM + HG_V, :])
         + _dot(c_ref[...], w_ref[POOL_DIM + HG_V:, :]))
    x1 = _layer_norm(alpha * x_ref[...] + g1_ref[0] * m) * lng_ref[...] + lnb_ref[...]
    x1_out[...] = x1
    h2 = _layer_norm(x1) * (1.0 + sc2_ref[0]) + sh2_ref[0]
    h2_out[...] = h2
    lg_out[...] = _dot3(h2, wr_ref[...])


def _out_proj(a, o_f, o_b, hg, c_att, x, g1, sh2, sc2, ln_g, ln_b, norm_g, gavg, w_out, w_router, tm,
              mod_index, alpha):
    n, d = x.shape
    row = lambda i: (i, 0)
    full = lambda i: (0, 0)
    mod = lambda i: (mod_index(i), 0, 0)
    return pl.pallas_call(
        functools.partial(_outproj_kernel, alpha=alpha),
        grid=(n // tm,),
        in_specs=[pl.BlockSpec((tm, POOL_DIM), row), pl.BlockSpec((tm, HG_V), row), pl.BlockSpec((tm, HG_V), row),
                  pl.BlockSpec((tm, HG_V), row), pl.BlockSpec((tm, MLA_HEADS * MLA_V), row),
                  pl.BlockSpec((tm, d), row),
                  pl.BlockSpec((1, 1, d), mod), pl.BlockSpec((1, 1, d), mod), pl.BlockSpec((1, 1, d), mod),
                  pl.BlockSpec((1, d), full), pl.BlockSpec((1, d), full), pl.BlockSpec((1, HG_V), full),
                  pl.BlockSpec((HG_V, HG_V), full), pl.BlockSpec((d, d), full),
                  pl.BlockSpec((d, ROUTER_LANES), full)],
        out_specs=[pl.BlockSpec((tm, d), row), pl.BlockSpec((tm, d), row), pl.BlockSpec((tm, ROUTER_LANES), row)],
        out_shape=[jax.ShapeDtypeStruct((n, d), F32), jax.ShapeDtypeStruct((n, d), F32),
                   jax.ShapeDtypeStruct((n, ROUTER_LANES), F32)],
        name="out_proj",
        compiler_params=_cparams(("parallel",)),
    )(a, o_f, o_b, hg, c_att, x, g1, sh2, sc2, ln_g, ln_b, norm_g, gavg, w_out, w_router)


def _moe_kernel(be_ref, nv_ref, idx_cur, idx_next, h_hbm, w1_ref, w3_ref, w2_ref, y_ref,
                xbuf, sem, w1b, w3b, w2b, *, tm):
    i = pl.program_id(0)
    n_valid = nv_ref[0]
    slot = i % 2

    def gather(idx_ref, dst_slot):
        def row(r, carry):
            tok = idx_ref[0, 0, r]
            pltpu.make_async_copy(h_hbm.at[pl.ds(tok, 1)], xbuf.at[dst_slot, pl.ds(r, 1)],
                                  sem.at[dst_slot]).start()
            return carry

        lax.fori_loop(0, tm, row, 0, unroll=8)

    @pl.when(i == 0)
    def _():
        gather(idx_cur, 0)

    @pl.when(i + 1 < n_valid)
    def _():
        gather(idx_next, 1 - slot)

    @pl.when((i == 0) | (be_ref[i] != be_ref[jnp.maximum(i - 1, 0)]))
    def _():
        w1b[...] = w1_ref[0].astype(BF16)
        w3b[...] = w3_ref[0].astype(BF16)
        w2b[...] = w2_ref[0].astype(BF16)

    @pl.when(i < n_valid)
    def _():
        pltpu.make_async_copy(h_hbm.at[pl.ds(0, tm)], xbuf.at[slot], sem.at[slot]).wait()
        x = xbuf[slot].astype(BF16)
        h1 = _dot(x, w1b[...])
        h3 = _dot(x, w3b[...])
        h = (_silu(h1) * h3).astype(BF16)
        y_ref[...] = _dot(h, w2b[...]).astype(y_ref.dtype)

    @pl.when(i >= n_valid)
    def _():
        y_ref[...] = jnp.zeros(y_ref.shape, y_ref.dtype)


def _moe_ffn(block_e, n_valid, row_tok, h, w1, w3, w2, tm):
    n_blocks = row_tok.shape[0]
    d = h.shape[1]
    last = n_blocks - 1
    grid_spec = pltpu.PrefetchScalarGridSpec(
        num_scalar_prefetch=2,
        grid=(n_blocks,),
        in_specs=[pl.BlockSpec((1, 1, tm), lambda i, be, nv: (i, 0, 0), memory_space=pltpu.SMEM),
                  pl.BlockSpec((1, 1, tm), lambda i, be, nv: (jnp.minimum(i + 1, last), 0, 0),
                               memory_space=pltpu.SMEM),
                  pl.BlockSpec(memory_space=pl.ANY),
                  pl.BlockSpec((1, d, D_EXPERT), lambda i, be, nv: (be[i], 0, 0)),
                  pl.BlockSpec((1, d, D_EXPERT), lambda i, be, nv: (be[i], 0, 0)),
                  pl.BlockSpec((1, D_EXPERT, d), lambda i, be, nv: (be[i], 0, 0))],
        out_specs=pl.BlockSpec((tm, d), lambda i, be, nv: (i, 0)),
        scratch_shapes=[pltpu.VMEM((2, tm, d), F32), pltpu.SemaphoreType.DMA((2,)),
                        pltpu.VMEM((d, D_EXPERT), BF16), pltpu.VMEM((d, D_EXPERT), BF16),
                        pltpu.VMEM((D_EXPERT, d), BF16)],
    )
    return pl.pallas_call(
        functools.partial(_moe_kernel, tm=tm),
        grid_spec=grid_spec,
        out_shape=jax.ShapeDtypeStruct((n_blocks * tm, d), BF16),
        name="moe_ffn",
        compiler_params=_cparams(("arbitrary",)),
    )(block_e, n_valid, row_tok, row_tok, h, w1, w3, w2)


def _combine_kernel(x_ref, y0_ref, y1_ref, gw_ref, g2_ref, lng_ref, lnb_ref, o_ref, *, alpha):
    gw = gw_ref[...]
    y = gw[:, 0:1] * y0_ref[...].astype(F32) + gw[:, 1:2] * y1_ref[...].astype(F32)
    o_ref[...] = _layer_norm(alpha * x_ref[...] + g2_ref[0] * y) * lng_ref[...] + lnb_ref[...]


def _combine(x, y0, y1, gw, g2, ln_g, ln_b, tm, mod_index, alpha):
    n, d = x.shape
    row = lambda i: (i, 0)
    full = lambda i: (0, 0)
    return pl.pallas_call(
        functools.partial(_combine_kernel, alpha=alpha),
        grid=(n // tm,),
        in_specs=[pl.BlockSpec((tm, d), row), pl.BlockSpec((tm, d), row), pl.BlockSpec((tm, d), row),
                  pl.BlockSpec((tm, ROUTER_LANES), row),
                  pl.BlockSpec((1, 1, d), lambda i: (mod_index(i), 0, 0)),
                  pl.BlockSpec((1, d), full), pl.BlockSpec((1, d), full)],
        out_specs=pl.BlockSpec((tm, d), row),
        out_shape=jax.ShapeDtypeStruct((n, d), F32),
        name="moe_combine",
        compiler_params=_cparams(("parallel",)),
    )(x, y0, y1, gw, g2, ln_g, ln_b)


def _rot_half_cols(w):
    lead = w.shape[:-1]
    wr = w.reshape(lead + (2, 2, MLA_ROPE // 4))
    return jnp.stack([-wr[..., 1, :], wr[..., 0, :]], axis=-2).reshape(w.shape)


def _rope_tables(n_lat, n_ctx, bsz):
    rows = n_lat // GRID_W
    row = jnp.repeat(jnp.arange(rows, dtype=F32), GRID_W)
    col = jnp.tile(jnp.arange(GRID_W, dtype=F32), rows)
    n_freq = MLA_ROPE // 4
    inv_freq = ROPE_BASE ** (-jnp.arange(n_freq, dtype=F32) / n_freq)
    ang = jnp.stack([row[:, None] * inv_freq, col[:, None] * inv_freq], axis=1)
    cos = jnp.broadcast_to(jnp.cos(ang)[:, :, None, :], (n_lat, 2, 2, n_freq)).reshape(n_lat, MLA_ROPE)
    sin = jnp.broadcast_to(jnp.sin(ang)[:, :, None, :], (n_lat, 2, 2, n_freq)).reshape(n_lat, MLA_ROPE)
    pad_l = MLA_NOPE
    pad_r = HEAD_LANES - MLA_QK
    cos_t = jnp.concatenate([jnp.ones((n_lat, pad_l), F32), cos, jnp.ones((n_lat, pad_r), F32)], axis=1)
    sin_t = jnp.concatenate([jnp.zeros((n_lat, pad_l), F32), sin, jnp.zeros((n_lat, pad_r), F32)], axis=1)
    cos_all = jnp.concatenate([jnp.tile(cos_t, (bsz, 1)), jnp.ones((bsz * n_ctx, HEAD_LANES), F32)], axis=0)
    sin_all = jnp.concatenate([jnp.tile(sin_t, (bsz, 1)), jnp.zeros((bsz * n_ctx, HEAD_LANES), F32)], axis=0)
    return cos_all, sin_all


def _prep_w_in(w_in):
    depth, d, _ = w_in.shape
    kr = w_in[:, :, W_IN_MAIN:]
    z_l = jnp.zeros((depth, d, MLA_NOPE), w_in.dtype)
    z_r = jnp.zeros((depth, d, HEAD_LANES - MLA_QK), w_in.dtype)
    return jnp.concatenate([w_in[:, :, :W_IN_MAIN], z_l, kr, z_r, z_l, _rot_half_cols(kr), z_r],
                           axis=-1).astype(BF16)


def _prep_mla(w_uq, w_ukv):
    depth = w_uq.shape[0]
    wq = w_uq.reshape(depth, MLA_Q_LORA, MLA_HEADS, MLA_QK)
    pad = jnp.zeros((depth, MLA_Q_LORA, MLA_HEADS, HEAD_LANES - MLA_QK), w_uq.dtype)
    wq_p = jnp.concatenate([wq, pad], axis=-1).reshape(depth, MLA_Q_LORA, MLA_HEADS * HEAD_LANES)
    wq_s = jnp.concatenate([jnp.zeros_like(wq[..., :MLA_NOPE]), _rot_half_cols(wq[..., MLA_NOPE:]), pad],
                           axis=-1).reshape(depth, MLA_Q_LORA, MLA_HEADS * HEAD_LANES)
    wkv = w_ukv.reshape(depth, MLA_KV_LORA, MLA_HEADS, MLA_NOPE + MLA_V)
    kpad = jnp.zeros((depth, MLA_KV_LORA, MLA_HEADS, HEAD_LANES - MLA_NOPE), w_ukv.dtype)
    wk = jnp.concatenate([wkv[..., :MLA_NOPE], kpad], axis=-1).reshape(depth, MLA_KV_LORA, MLA_HEADS * HEAD_LANES)
    wvt = jnp.swapaxes(wkv[..., MLA_NOPE:].reshape(depth, MLA_KV_LORA, MLA_HEADS * MLA_V), 1, 2)
    return wq_p.astype(BF16), wq_s.astype(BF16), wk.astype(BF16), wvt.astype(BF16)


def _block_diag(pool_w):
    depth, g, c, _ = pool_w.shape
    eye = jnp.eye(g, dtype=pool_w.dtype)
    return jnp.einsum("lgcd,gh->lgchd", pool_w, eye).reshape(depth, g * c, g * c).astype(BF16)


def _route(logits, b_rg, b_re, tm_moe):
    n = logits.shape[0]
    lg = logits[:, :N_GROUPS] + b_rg
    le = (logits[:, N_GROUPS:N_GROUPS + N_EXPERTS] + b_re).reshape(n, N_GROUPS, EXPERTS_PER_GROUP)
    pg = jax.nn.softmax(lg, axis=-1)
    g_sel = jnp.argmax(lg, axis=-1).astype(jnp.int32)
    g_hot = g_sel[:, None] == jnp.arange(N_GROUPS, dtype=jnp.int32)[None, :]
    w_g = jnp.sum(jnp.where(g_hot, pg, 0.0), axis=-1)
    le_sel = jnp.sum(jnp.where(g_hot[:, :, None], le, 0.0), axis=1)
    top_v, top_i = lax.top_k(le_sel, TOP_K)
    gate = jax.nn.softmax(top_v, axis=-1) * w_g[:, None]
    expert = (g_sel[:, None] * EXPERTS_PER_GROUP + top_i).astype(jnp.int32)

    n_assign = n * TOP_K
    e_flat = expert.reshape(-1)
    hot = e_flat[:, None] == jnp.arange(N_EXPERTS, dtype=jnp.int32)[None, :]
    csum = jnp.cumsum(hot.astype(jnp.int32), axis=0)
    counts = csum[-1]
    pcounts = (counts + tm_moe - 1) // tm_moe * tm_moe
    pends = jnp.cumsum(pcounts)
    pstarts = pends - pcounts
    dest = jnp.sum(jnp.where(hot, csum - 1 + pstarts[None, :], 0), axis=1)
    n_rows = (n_assign + N_EXPERTS * (tm_moe - 1) + tm_moe - 1) // tm_moe * tm_moe
    n_blocks = n_rows // tm_moe
    tok = jnp.repeat(jnp.arange(n, dtype=jnp.int32), TOP_K)
    row_tok = jnp.zeros((n_rows,), jnp.int32).at[dest].set(tok, unique_indices=True)
    n_valid = (pends[-1] // tm_moe).astype(jnp.int32)
    blk = jnp.arange(n_blocks, dtype=jnp.int32)
    block_e = jnp.searchsorted(pends, jnp.minimum(blk, n_valid - 1) * tm_moe, side="right").astype(jnp.int32)
    block_e = jnp.clip(block_e, 0, N_EXPERTS - 1)
    gw = jnp.zeros((n, ROUTER_LANES), F32).at[:, :TOP_K].set(gate)
    return row_tok.reshape(n_blocks, 1, tm_moe), dest.reshape(n, TOP_K), gw, block_e, n_valid.reshape(1)


def _tile(n, pref):
    t = pref
    while n % t:
        t //= 2
    return t


def kernel(x, c, ctx, c_ctx, w_mod, b_mod, w_in, pool_w, pool_scale, hg_lb_logits, hg_norm_g, mla_q_norm_g, mla_w_uq, mla_kv_norm_g, mla_w_ukv, w_out, ln1_g, ln1_b, ln2_g, ln2_b, router_group_w, router_group_b, router_expert_w, router_expert_b, expert_w1, expert_w3, expert_w2):
    bsz, n_lat, d = x.shape
    n_ctx = ctx.shape[1]
    depth = w_mod.shape[0]
    alpha = (2 * depth) ** 0.25
    lat_rows = bsz * n_lat
    ctx_rows = bsz * n_ctx
    n_tok = lat_rows + ctx_rows

    tm = _tile(np.gcd(n_lat, ctx_rows), 512)
    tp = _tile(np.gcd(n_lat, n_ctx), 256)
    tb_lat = _tile(n_lat, 512)
    tb_ctx = _tile(n_ctx, 512)
    tq_lat = _tile(n_lat, 512)
    tkc = _tile(np.gcd(n_lat, n_ctx), 256)
    tq_ctx = _tile(n_ctx, 512)
    tm_moe = 256
    lat_tiles_per_batch = n_lat // tm
    mod_index = lambda i: jnp.minimum(i // lat_tiles_per_batch, bsz)

    c_rows = jnp.concatenate([jax.nn.silu(c), jax.nn.silu(c_ctx)[None, :],
                              jnp.zeros((-(bsz + 1) % 8, d), F32)], axis=0)
    mod_all = _mod_all(c_rows, w_mod, b_mod)
    lb_all = jnp.cumsum(jax.nn.softmax(hg_lb_logits.astype(F32), axis=0), axis=0)
    lb_all = lb_all - lb_all[0:1]
    log_lb = jnp.log(lb_all)
    log_1m_lb = jnp.log1p(-lb_all)
    w_aug = _prep_w_in(w_in)
    wq_p, wq_s, wk_p, wvt_p = _prep_mla(mla_w_uq, mla_w_ukv)
    w_bd = _block_diag(pool_w)
    w_out_b = w_out.astype(BF16)
    w_router = jnp.concatenate([router_group_w, router_expert_w,
                                jnp.zeros((depth, d, ROUTER_LANES - N_GROUPS - N_EXPERTS), F32)], axis=-1)
    gavg = jnp.kron(jnp.eye(HG_HEADS, dtype=F32), jnp.full((HG_DV, HG_DV), 1.0 / HG_DV, F32)).astype(BF16)
    cos_t, sin_t = _rope_tables(n_lat, n_ctx, bsz)
    zero_state = jnp.zeros((bsz, HG_HEADS, HG_DV, HG_DK), F32)

    xt = jnp.concatenate([x.reshape(lat_rows, d), ctx.reshape(ctx_rows, d)], axis=0)
    for l in range(depth):
        mod = mod_all[l, :bsz + 1].reshape(bsz + 1, 6, 1, d)
        sh1, sc1, g1, sh2, sc2, g2 = (mod[:, k] for k in range(6))

        p_in, hq, hzf, hzb, hin, hg, cq, ckv, kr, krs = _in_proj(xt, sh1, sc1, w_aug[l], tm, mod_index)

        a_mix = _pool(p_in, w_bd[l], pool_scale[l][None, :], tp, lat_rows, n_lat, n_ctx)

        la_f, la_b = log_lb[l, 0][None, :], log_lb[l, 1][None, :]
        lc_f, lc_b = log_1m_lb[l, 0][None, :], log_1m_lb[l, 1][None, :]
        oc_f, s_f = _hgrn_scan(hq, hzf, hin, la_f, lc_f, zero_state, reverse=False, base_row=lat_rows,
                               n_seq=n_ctx, tb=tb_ctx)
        oc_b, s_b = _hgrn_scan(hq, hzb, hin, la_b, lc_b, zero_state, reverse=True, base_row=lat_rows,
                               n_seq=n_ctx, tb=tb_ctx)
        ol_f, _ = _hgrn_scan(hq, hzf, hin, la_f, lc_f, s_f, reverse=False, base_row=0, n_seq=n_lat, tb=tb_lat)
        ol_b, _ = _hgrn_scan(hq, hzb, hin, la_b, lc_b, s_b, reverse=True, base_row=0, n_seq=n_lat, tb=tb_lat)
        o_f = jnp.concatenate([ol_f, oc_f], axis=0)
        o_b = jnp.concatenate([ol_b, oc_b], axis=0)

        q_all, k_all, vt_all = _mla_proj(cq, ckv, kr, krs, cos_t, sin_t, mla_q_norm_g[l][None, :],
                                         mla_kv_norm_g[l][None, :], wq_p[l], wq_s[l], wk_p[l], wvt_p[l], tkc)
        c_lat = _attention(q_all, k_all, vt_all, bsz=bsz, n_q=n_lat, q_base=0, n_lat=n_lat, n_ctx=n_ctx,
                           ctx_base=lat_rows, tq=tq_lat, tkc=tkc, with_lat=True)
        c_ctx_o = _attention(q_all, k_all, vt_all, bsz=bsz, n_q=n_ctx, q_base=lat_rows, n_lat=n_lat, n_ctx=n_ctx,
                             ctx_base=lat_rows, tq=tq_ctx, tkc=tkc, with_lat=False)
        c_att = jnp.concatenate([c_lat, c_ctx_o], axis=0)

        x1, h2, logits = _out_proj(a_mix, o_f, o_b, hg, c_att, xt, g1, sh2, sc2, ln1_g[l][None, :],
                                   ln1_b[l][None, :], hg_norm_g[l][None, :], gavg, w_out_b[l], w_router[l],
                                   tm, mod_index, alpha)

        row_tok, dest, gw, block_e, n_valid = _route(logits, router_group_b[l], router_expert_b[l], tm_moe)
        y = _moe_ffn(block_e, n_valid, row_tok, h2, expert_w1[l], expert_w3[l], expert_w2[l], tm_moe)
        y0 = jnp.take(y, dest[:, 0], axis=0)
        y1 = jnp.take(y, dest[:, 1], axis=0)
        xt = _combine(x1, y0, y1, gw, g2, ln2_g[l][None, :], ln2_b[l][None, :], tm, mod_index, alpha)
    return xt[:lat_rows].reshape(bsz, n_lat, d)
```

```python
import functools

import numpy as np
import jax
import jax.numpy as jnp
from jax import lax
from jax.experimental import pallas as pl
from jax.experimental.pallas import tpu as pltpu

F32 = jnp.float32
BF16 = jnp.bfloat16

D_MODEL = 1024
GRID_W = 64
POOL_WINDOWS = (2, 4, 8, 16)
POOL_DIM = 256
POOL_GROUP_DIM = 64
HG_HEADS = 4
HG_DK = 128
HG_DV = 64
HG_QK = HG_HEADS * HG_DK
HG_V = HG_HEADS * HG_DV
HG_CHUNK = 64
HG_SUB = 16
MLA_HEADS = 8
MLA_NOPE = 64
MLA_ROPE = 32
MLA_V = 64
MLA_Q_LORA = 256
MLA_KV_LORA = 128
MLA_QK = MLA_NOPE + MLA_ROPE
MLA_SCALE = MLA_QK ** -0.5
LOG2_E = 1.4426950408889634
ROPE_BASE = 10000.0
HEAD_LANES = 128
N_GROUPS = 4
EXPERTS_PER_GROUP = 8
N_EXPERTS = N_GROUPS * EXPERTS_PER_GROUP
TOP_K = 2
D_EXPERT = 512
EPS = 1e-6
ROUTER_LANES = 128
COUNT_BLOCK = 128

IN_SEGS = (("p_in", 256), ("hq", 512), ("hzf", 512), ("hzb", 512), ("hin", 256), ("hg", 256),
           ("cq", 256), ("ckv", 128), ("kr", 128), ("krs", 128))
IN_AUG = sum(w for _, w in IN_SEGS)
W_IN_MAIN = 2688

VMEM_LIMIT = 56 * 1024 * 1024


def _cparams(sem):
    return pltpu.CompilerParams(dimension_semantics=sem, vmem_limit_bytes=VMEM_LIMIT)


def _split_bf16(a):
    hi = a.astype(BF16)
    lo = (a - hi.astype(F32)).astype(BF16)
    return hi, lo


def _dot(a, b):
    return jnp.dot(a, b, preferred_element_type=F32)


def _dot_nt(a, b):
    return lax.dot_general(a, b, (((1,), (1,)), ((), ())), preferred_element_type=F32)


def _dot_tn(a, b):
    return lax.dot_general(a, b, (((0,), (0,)), ((), ())), preferred_element_type=F32)


def _dot3(a, b):
    a_hi, a_lo = _split_bf16(a)
    b_hi, b_lo = _split_bf16(b)
    return _dot(a_hi, b_hi) + _dot(a_hi, b_lo) + _dot(a_lo, b_hi)


def _dot_exact_rhs(a, b_bf16):
    a_hi = a.astype(BF16)
    r1 = a - a_hi.astype(F32)
    a_mid = r1.astype(BF16)
    a_lo = (r1 - a_mid.astype(F32)).astype(BF16)
    return _dot(a_hi, b_bf16) + _dot(a_mid, b_bf16) + _dot(a_lo, b_bf16)


def _layer_norm(x):
    mu = jnp.mean(x, axis=-1, keepdims=True)
    xc = x - mu
    var = jnp.mean(xc * xc, axis=-1, keepdims=True)
    return xc * lax.rsqrt(var + EPS)


def _sigmoid(x):
    return 1.0 / (1.0 + jnp.exp(-x))


def _silu(x):
    return x * _sigmoid(x)


def _mod_kernel(a_ref, w_ref, b_ref, o_ref):
    o_ref[0] = _dot3(a_ref[...], w_ref[0]) + b_ref[0]


def _mod_all(c_rows, w_mod, b_mod):
    depth, d, n6 = w_mod.shape
    tn = 1536
    rows = c_rows.shape[0]
    return pl.pallas_call(
        _mod_kernel,
        grid=(depth, n6 // tn),
        in_specs=[pl.BlockSpec((rows, d), lambda l, j: (0, 0)),
                  pl.BlockSpec((1, d, tn), lambda l, j: (l, 0, j)),
                  pl.BlockSpec((1, 1, tn), lambda l, j: (l, 0, j))],
        out_specs=pl.BlockSpec((1, rows, tn), lambda l, j: (l, 0, j)),
        out_shape=jax.ShapeDtypeStruct((depth, rows, n6), F32),
        name="adaln_mod",
        compiler_params=_cparams(("arbitrary", "arbitrary")),
    )(c_rows, w_mod, b_mod.reshape(depth, 1, n6))


def _inproj_kernel(x_ref, sh_ref, sc_ref, w_ref, *out_refs):
    h = _layer_norm(x_ref[...]) * (1.0 + sc_ref[0]) + sh_ref[0]
    hb = h.astype(BF16)
    start = 0
    for (_, width), o_ref in zip(IN_SEGS, out_refs):
        o_ref[...] = _dot(hb, w_ref[:, start:start + width])
        start += width


def _in_proj(x, sh, sc, w_aug, tm, mod_index):
    n, d = x.shape
    row = lambda i: (i, 0)
    mod = lambda i: (mod_index(i), 0, 0)
    return pl.pallas_call(
        _inproj_kernel,
        grid=(n // tm,),
        in_specs=[pl.BlockSpec((tm, d), row),
                  pl.BlockSpec((1, 1, d), mod),
                  pl.BlockSpec((1, 1, d), mod),
                  pl.BlockSpec((d, IN_AUG), lambda i: (0, 0))],
        out_specs=[pl.BlockSpec((tm, w), row) for _, w in IN_SEGS],
        out_shape=[jax.ShapeDtypeStruct((n, w), F32) for _, w in IN_SEGS],
        name="in_proj",
        compiler_params=_cparams(("parallel",)),
    )(x, sh, sc, w_aug)


POOL_HALO = 8


def _pool_kernel(x_ref, prev_ref, next_ref, w_ref, scale_ref, o_ref, *, tp, lat_tiles, tpb_lat, tpb_ctx,
                 n_lat, n_ctx):
    i = pl.program_id(0)
    is_lat = i < lat_tiles
    tile_in_seq = jnp.where(is_lat, i % tpb_lat, (i - lat_tiles) % tpb_ctx)
    tiles_in_seq = jnp.where(is_lat, tpb_lat, tpb_ctx)
    n_seq = jnp.where(is_lat, n_lat, n_ctx)
    first = tile_in_seq == 0
    last = tile_in_seq == tiles_in_seq - 1
    x = x_ref[...]
    prev = jnp.where(first, 0.0, prev_ref[...])
    nxt = jnp.where(last, 0.0, next_ref[...])
    ext = jnp.concatenate([prev, x, nxt], axis=0)
    n_ext = tp + 2 * POOL_HALO

    def back(a, k):
        return pltpu.roll(a, k, 0)

    def fwd(a, k):
        return pltpu.roll(a, n_ext - k, 0)

    e2 = ext + back(ext, 1)
    e4 = back(e2, 1) + fwd(e2, 1)
    e8 = back(e4, 2) + fwd(e4, 2)
    e16 = back(e8, 4) + fwd(e8, 4)
    sl = slice(POOL_HALO, POOL_HALO + tp)
    lane = lax.broadcasted_iota(jnp.int32, (tp, POOL_DIM), 1)
    grp = lane >> 6
    wsum = jnp.where(grp == 0, e2[sl], jnp.where(grp == 1, e4[sl], jnp.where(grp == 2, e8[sl], e16[sl])))
    half = jnp.where(grp == 0, 1, jnp.where(grp == 1, 2, jnp.where(grp == 2, 4, 8)))
    t = tile_in_seq * tp + lax.broadcasted_iota(jnp.int32, (tp, POOL_DIM), 0)
    cnt = jnp.minimum(t + half, n_seq) - jnp.maximum(t - half, 0)
    pooled = wsum / cnt.astype(F32) - x
    o_ref[...] = _dot(pooled.astype(BF16), w_ref[...]) * scale_ref[...]


def _pool(p_in, w_bd, scale, tp, n_lat_rows, n_lat, n_ctx):
    n, c = p_in.shape
    hb = tp // POOL_HALO
    last_halo_block = n // POOL_HALO - 1
    kern = functools.partial(_pool_kernel, tp=tp, lat_tiles=n_lat_rows // tp, tpb_lat=n_lat // tp,
                             tpb_ctx=n_ctx // tp, n_lat=n_lat, n_ctx=n_ctx)
    return pl.pallas_call(
        kern,
        grid=(n // tp,),
        in_specs=[pl.BlockSpec((tp, c), lambda i: (i, 0)),
                  pl.BlockSpec((POOL_HALO, c), lambda i: (jnp.maximum(i * hb - 1, 0), 0)),
                  pl.BlockSpec((POOL_HALO, c), lambda i: (jnp.minimum((i + 1) * hb, last_halo_block), 0)),
                  pl.BlockSpec((c, c), lambda i: (0, 0)),
                  pl.BlockSpec((1, c), lambda i: (0, 0))],
        out_specs=pl.BlockSpec((tp, c), lambda i: (i, 0)),
        out_shape=jax.ShapeDtypeStruct((n, c), F32),
        name="pool",
        compiler_params=_cparams(("parallel",)),
    )(p_in, p_in, p_in, w_bd, scale)


def _hgrn_kernel(hq_ref, hz_ref, hin_ref, la_ref, lc_ref, s0_ref, o_ref, sT_ref, q_s, g_s, k_s, st_s,
                 *, reverse, nchunk):
    j = pl.program_id(1)

    @pl.when(j == 0)
    def _():
        st_s[...] = s0_ref[0]

    z = hz_ref[...]
    ls = jnp.minimum(z, 0.0) - jnp.log1p(jnp.exp(-jnp.abs(z)))
    lc = lc_ref[...]
    x2 = lc + ls
    la = la_ref[...]
    g_s[...] = jnp.maximum(la, x2) + jnp.log1p(jnp.exp(-jnp.abs(la - x2)))
    k_s[...] = jnp.exp(x2 - z)
    q_s[...] = _silu(hq_ref[...])

    C = HG_CHUNK
    row = lax.broadcasted_iota(jnp.int32, (C, C), 0)
    col = lax.broadcasted_iota(jnp.int32, (C, C), 1)
    tri = ((row <= col) if reverse else (row >= col)).astype(BF16)
    rmod = lax.broadcasted_iota(jnp.int32, (C, 1), 0) & (HG_SUB - 1)
    nsub = C // HG_SUB
    row_sub = row >> 4
    col_sub = col >> 4

    def chunk(c, carry):
        cc = (nchunk - 1 - c) if reverse else c
        r0 = pl.multiple_of(cc * C, C)
        b_all = _dot_exact_rhs_lhs(tri, g_s[pl.ds(r0, C), :])
        q_all = q_s[pl.ds(r0, C), :]
        k_all = k_s[pl.ds(r0, C), :]
        v_all = hin_ref[pl.ds(r0, C), :]
        outs = []
        for h in range(HG_HEADS):
            ks = slice(h * HG_DK, (h + 1) * HG_DK)
            b, q, k = b_all[:, ks], q_all[:, ks], k_all[:, ks]
            v = v_all[:, h * HG_DV:(h + 1) * HG_DV]
            st = st_s[h]
            b_last = b[0:1] if reverse else b[C - 1:C]
            o = _dot_nt((q * jnp.exp(b)).astype(BF16), st.astype(BF16))
            a_rows = []
            for i in range(nsub):
                rs = slice(i * HG_SUB, (i + 1) * HG_SUB)
                if (reverse and i == nsub - 1) or (not reverse and i == 0):
                    a_rows.append(jnp.zeros((HG_SUB, C), F32))
                    continue
                m = b[(i + 1) * HG_SUB - 1:(i + 1) * HG_SUB] if reverse else b[i * HG_SUB:i * HG_SUB + 1]
                qi = q[rs] * jnp.exp(b[rs] - m)
                ksc = k * jnp.exp(m - b)
                a_rows.append(_dot_nt(qi.astype(BF16), ksc.astype(BF16)))
            a = jnp.concatenate(a_rows, axis=0)
            if reverse:
                a = jnp.where(col_sub > row_sub, a, 0.0)
            else:
                a = jnp.where(col_sub < row_sub, a, 0.0)
            o = o + _dot(a.astype(BF16), v.astype(BF16))
            for delta in range(HG_SUB):
                if delta == 0:
                    w = jnp.sum(q * k, axis=-1, keepdims=True)
                    o = o + w * v
                    continue
                shift = (C - delta) if reverse else delta
                kd = pltpu.roll(k, shift, 0)
                bd = pltpu.roll(b, shift, 0)
                vd = pltpu.roll(v, shift, 0)
                e = jnp.exp(b - bd)
                w = jnp.sum(q * kd * e, axis=-1, keepdims=True)
                valid = (rmod + delta < HG_SUB) if reverse else (rmod >= delta)
                o = o + jnp.where(valid, w, 0.0) * vd
            outs.append(o)
            kdec = k * jnp.exp(b_last - b)
            st_s[h] = st * jnp.exp(b_last) + _dot_tn(v.astype(BF16), kdec.astype(BF16))
        o_ref[pl.ds(r0, C), :] = jnp.concatenate(outs, axis=1)
        return carry

    lax.fori_loop(0, nchunk, chunk, 0)

    @pl.when(j == pl.num_programs(1) - 1)
    def _():
        sT_ref[0] = st_s[...]


def _dot_exact_rhs_lhs(tri_bf16, g):
    g_hi = g.astype(BF16)
    r1 = g - g_hi.astype(F32)
    g_mid = r1.astype(BF16)
    g_lo = (r1 - g_mid.astype(F32)).astype(BF16)
    return _dot(tri_bf16, g_hi) + _dot(tri_bf16, g_mid) + _dot(tri_bf16, g_lo)


def _hgrn_scan(hq, hz, hin, la, lc, s0, *, reverse, base_row, n_seq, tb):
    bsz = s0.shape[0]
    nblk = n_seq // tb
    base_blk = base_row // tb

    def rows(b, j):
        jj = (nblk - 1 - j) if reverse else j
        return (base_blk + b * nblk + jj, 0)

    def orow(b, j):
        jj = (nblk - 1 - j) if reverse else j
        return (b * nblk + jj, 0)

    kern = functools.partial(_hgrn_kernel, reverse=reverse, nchunk=tb // HG_CHUNK)
    return pl.pallas_call(
        kern,
        grid=(bsz, nblk),
        in_specs=[pl.BlockSpec((tb, HG_QK), rows),
                  pl.BlockSpec((tb, HG_QK), rows),
                  pl.BlockSpec((tb, HG_V), rows),
                  pl.BlockSpec((1, HG_QK), lambda b, j: (0, 0)),
                  pl.BlockSpec((1, HG_QK), lambda b, j: (0, 0)),
                  pl.BlockSpec((1, HG_HEADS, HG_DV, HG_DK), lambda b, j: (b, 0, 0, 0))],
        out_specs=[pl.BlockSpec((tb, HG_V), orow),
                   pl.BlockSpec((1, HG_HEADS, HG_DV, HG_DK), lambda b, j: (b, 0, 0, 0))],
        out_shape=[jax.ShapeDtypeStruct((bsz * n_seq, HG_V), F32),
                   jax.ShapeDtypeStruct((bsz, HG_HEADS, HG_DV, HG_DK), F32)],
        scratch_shapes=[pltpu.VMEM((tb, HG_QK), F32), pltpu.VMEM((tb, HG_QK), F32),
                        pltpu.VMEM((tb, HG_QK), F32), pltpu.VMEM((HG_HEADS, HG_DV, HG_DK), F32)],
        name="hgrn_bwd" if reverse else "hgrn_fwd",
        compiler_params=_cparams(("parallel", "arbitrary")),
    )(hq, hz, hin, la, lc, s0)


def _rms(x, g):
    return x * lax.rsqrt(jnp.mean(x * x, axis=-1, keepdims=True) + EPS) * g


def _mla_proj_kernel(cq_ref, ckv_ref, kr_ref, krs_ref, cos_ref, sin_ref, qg_ref, kvg_ref,
                     wq_ref, wqs_ref, wk_ref, wvt_ref, q_out, k_out, vt_out):
    cos = cos_ref[...]
    sin = sin_ref[...]
    cos_h = jnp.concatenate([cos] * MLA_HEADS, axis=1)
    sin_h = jnp.concatenate([sin] * MLA_HEADS, axis=1)
    xq = _rms(cq_ref[...], qg_ref[...]).astype(BF16)
    q = _dot(xq, wq_ref[...]) * cos_h + _dot(xq, wqs_ref[...]) * sin_h
    q_out[...] = (q * (MLA_SCALE * LOG2_E)).astype(BF16)
    xkv = _rms(ckv_ref[...], kvg_ref[...]).astype(BF16)
    k_rope = kr_ref[...] * cos + krs_ref[...] * sin
    k = _dot(xkv, wk_ref[...]) + jnp.concatenate([k_rope] * MLA_HEADS, axis=1)
    k_out[...] = k.astype(BF16)
    vt_out[0] = _dot_nt(wvt_ref[...], xkv).astype(BF16)


def _mla_proj(cq, ckv, kr, krs, cos_t, sin_t, qg, kvg, wq, wqs, wk, wvt, tm):
    n = cq.shape[0]
    row = lambda i: (i, 0)
    full = lambda i: (0, 0)
    hq = MLA_HEADS * HEAD_LANES
    hv = MLA_HEADS * MLA_V
    return pl.pallas_call(
        _mla_proj_kernel,
        grid=(n // tm,),
        in_specs=[pl.BlockSpec((tm, MLA_Q_LORA), row), pl.BlockSpec((tm, MLA_KV_LORA), row),
                  pl.BlockSpec((tm, HEAD_LANES), row), pl.BlockSpec((tm, HEAD_LANES), row),
                  pl.BlockSpec((tm, HEAD_LANES), row), pl.BlockSpec((tm, HEAD_LANES), row),
                  pl.BlockSpec((1, MLA_Q_LORA), full), pl.BlockSpec((1, MLA_KV_LORA), full),
                  pl.BlockSpec((MLA_Q_LORA, hq), full), pl.BlockSpec((MLA_Q_LORA, hq), full),
                  pl.BlockSpec((MLA_KV_LORA, hq), full), pl.BlockSpec((hv, MLA_KV_LORA), full)],
        out_specs=[pl.BlockSpec((tm, hq), row), pl.BlockSpec((tm, hq), row),
                   pl.BlockSpec((1, hv, tm), lambda i: (i, 0, 0))],
        out_shape=[jax.ShapeDtypeStruct((n, hq), BF16), jax.ShapeDtypeStruct((n, hq), BF16),
                   jax.ShapeDtypeStruct((n // tm, hv, tm), BF16)],
        name="mla_proj",
        compiler_params=_cparams(("parallel",)),
    )(cq, ckv, kr, krs, cos_t, sin_t, qg, kvg, wq, wqs, wk, wvt)


ACC_ROWS = MLA_V + 16
LAT_UNROLL = 4


def _attn_kernel(*refs, tkc, n_lat_chunks, n_ctx_chunks):
    if n_lat_chunks:
        q_ref, kl_ref, vl_ref, kc_ref, vc_ref, o_ref, m_s, acc_s, s_s = refs
    else:
        q_ref, kc_ref, vc_ref, o_ref, m_s, acc_s, s_s = refs
    m_s[...] = jnp.full(m_s.shape, -jnp.inf, F32)
    acc_s[...] = jnp.zeros(acc_s.shape, F32)
    ones = jnp.ones((ACC_ROWS - MLA_V, tkc), BF16)

    def scores(a, k_ref, r0):
        hs = slice(a * HEAD_LANES, (a + 1) * HEAD_LANES)
        return _dot_nt(k_ref[pl.ds(r0, tkc), hs], q_ref[:, hs])

    def accumulate(a, s_t, vt):
        m_old = m_s[a]
        m_new = jnp.maximum(m_old, jnp.max(s_t, axis=0, keepdims=True))
        alpha = jnp.exp2(m_old - m_new)
        p_t = jnp.exp2(s_t - m_new).astype(BF16)
        vta = jnp.concatenate([vt[a * MLA_V:(a + 1) * MLA_V], ones], axis=0)
        acc_s[a] = alpha * acc_s[a] + _dot(vta, p_t)
        m_s[a] = m_new

    def run(items, s_first, after):
        s_next = s_first
        for idx, (a, _, _, vt) in enumerate(items):
            s_cur = s_next
            nxt = items[idx + 1] if idx + 1 < len(items) else after
            if nxt is not None:
                s_next = scores(*nxt[:3])
            accumulate(a, s_cur, vt)
        return s_next

    def lat_items(c0, dynamic):
        items = []
        for u in range(LAT_UNROLL):
            cc = c0 + u
            r0 = pl.multiple_of(cc * tkc, tkc) if dynamic else cc * tkc
            items += [(0, kl_ref, r0, vl_ref[cc]), (1, kl_ref, r0, vl_ref[cc])]
        return items

    ctx_items = []
    for c in range(n_ctx_chunks):
        ctx_items += [(0, kc_ref, c * tkc, vc_ref[c]), (1, kc_ref, c * tkc, vc_ref[c])]

    if n_lat_chunks:
        n_groups = n_lat_chunks // LAT_UNROLL
        s_s[...] = scores(0, kl_ref, 0)

        def body(g, carry):
            c0 = g * LAT_UNROLL
            r_next = pl.multiple_of((c0 + LAT_UNROLL) * tkc, tkc)
            s_s[...] = run(lat_items(c0, True), s_s[...], (0, kl_ref, r_next))
            return carry

        lax.fori_loop(0, n_groups - 1, body, 0)
        run(lat_items((n_groups - 1) * LAT_UNROLL, False) + ctx_items, s_s[...], None)
    else:
        run(ctx_items, scores(*ctx_items[0][:3]), None)
    outs = []
    for a in range(2):
        acc = acc_s[a]
        outs.append(acc[0:MLA_V] / acc[MLA_V:MLA_V + 1])
    o_ref[...] = jnp.concatenate(outs, axis=0).T.astype(o_ref.dtype)


def _attention(q, k, vt, *, bsz, n_q, q_base, n_lat, n_ctx, ctx_base, tq, tkc, with_lat):
    pairs = MLA_HEADS // 2
    qblocks = n_q // tq
    qb0 = q_base // tq
    cb0 = ctx_base // n_ctx
    kw = 2 * HEAD_LANES
    vw = 2 * MLA_V
    in_specs = [pl.BlockSpec((tq, kw), lambda b, p, i: (qb0 + b * qblocks + i, p))]
    args = [q]
    if with_lat:
        in_specs += [pl.BlockSpec((n_lat, kw), lambda b, p, i: (b, p)),
                     pl.BlockSpec((n_lat // tkc, vw, tkc), lambda b, p, i: (b, p, 0))]
        args += [k, vt]
    in_specs += [pl.BlockSpec((n_ctx, kw), lambda b, p, i: (cb0 + b, p)),
                 pl.BlockSpec((n_ctx // tkc, vw, tkc), lambda b, p, i: (cb0 + b, p, 0))]
    args += [k, vt]
    kern = functools.partial(_attn_kernel, tkc=tkc, n_lat_chunks=(n_lat // tkc) if with_lat else 0,
                             n_ctx_chunks=n_ctx // tkc)
    return pl.pallas_call(
        kern,
        grid=(bsz, pairs, qblocks),
        in_specs=in_specs,
        out_specs=pl.BlockSpec((tq, vw), lambda b, p, i: (b * qblocks + i, p)),
        out_shape=jax.ShapeDtypeStruct((bsz * n_q, MLA_HEADS * MLA_V), BF16),
        scratch_shapes=[pltpu.VMEM((2, 1, tq), F32), pltpu.VMEM((2, ACC_ROWS, tq), F32),
                        pltpu.VMEM((tkc, tq), F32)],
        name="attn_lat" if with_lat else "attn_ctx",
        compiler_params=_cparams(("parallel", "parallel", "arbitrary")),
    )(*args)


def _outproj_kernel(a_ref, of_ref, ob_ref, hg_ref, c_ref, x_ref, g1_ref, sh2_ref, sc2_ref,
                    lng_ref, lnb_ref, ng_ref, gavg_ref, w_ref, wr_ref, x1_out, h2_out, lg_out, *, alpha):
    o = of_ref[...] + ob_ref[...]
    o2_hi, o2_lo = _split_bf16(o * o)
    ms = _dot(o2_hi, gavg_ref[...]) + _dot(o2_lo, gavg_ref[...])
    b_lat = o * lax.rsqrt(ms + EPS) * ng_ref[...] * _silu(hg_ref[...])
    m = (_dot(a_ref[...].astype(BF16), w_ref[0:POOL_DIM, :])
         + _dot(b_lat.astype(BF16), w_ref[POOL_DIM:POOL_DIM + HG_V, :])
         + _dot(c_ref[...], w_ref[POOL_DIM + HG_V:, :]))
    x1 = _layer_norm(alpha * x_ref[...] + g1_ref[0] * m) * lng_ref[...] + lnb_ref[...]
    x1_out[...] = x1
    h2 = _layer_norm(x1) * (1.0 + sc2_ref[0]) + sh2_ref[0]
    h2_out[...] = h2
    lg_out[...] = _dot3(h2, wr_ref[...])


def _out_proj(a, o_f, o_b, hg, c_att, x, g1, sh2, sc2, ln_g, ln_b, norm_g, gavg, w_out, w_router, tm,
              mod_index, alpha):
    n, d = x.shape
    row = lambda i: (i, 0)
    full = lambda i: (0, 0)
    mod = lambda i: (mod_index(i), 0, 0)
    return pl.pallas_call(
        functools.partial(_outproj_kernel, alpha=alpha),
        grid=(n // tm,),
        in_specs=[pl.BlockSpec((tm, POOL_DIM), row), pl.BlockSpec((tm, HG_V), row), pl.BlockSpec((tm, HG_V), row),
                  pl.BlockSpec((tm, HG_V), row), pl.BlockSpec((tm, MLA_HEADS * MLA_V), row),
                  pl.BlockSpec((tm, d), row),
                  pl.BlockSpec((1, 1, d), mod), pl.BlockSpec((1, 1, d), mod), pl.BlockSpec((1, 1, d), mod),
                  pl.BlockSpec((1, d), full), pl.BlockSpec((1, d), full), pl.BlockSpec((1, HG_V), full),
                  pl.BlockSpec((HG_V, HG_V), full), pl.BlockSpec((d, d), full),
                  pl.BlockSpec((d, ROUTER_LANES), full)],
        out_specs=[pl.BlockSpec((tm, d), row), pl.BlockSpec((tm, d), row), pl.BlockSpec((tm, ROUTER_LANES), row)],
        out_shape=[jax.ShapeDtypeStruct((n, d), F32), jax.ShapeDtypeStruct((n, d), F32),
                   jax.ShapeDtypeStruct((n, ROUTER_LANES), F32)],
        name="out_proj",
        compiler_params=_cparams(("parallel",)),
    )(a, o_f, o_b, hg, c_att, x, g1, sh2, sc2, ln_g, ln_b, norm_g, gavg, w_out, w_router)


def _moe_kernel(be_ref, nv_ref, idx_cur, idx_next, h_hbm, w1_ref, w3_ref, w2_ref, y_ref,
                xbuf, sem, w1b, w3b, w2b, *, tm):
    i = pl.program_id(0)
    n_valid = nv_ref[0]
    slot = i % 2

    def gather(idx_ref, dst_slot):
        def row(r, carry):
            tok = idx_ref[0, 0, r]
            pltpu.make_async_copy(h_hbm.at[pl.ds(tok, 1)], xbuf.at[dst_slot, pl.ds(r, 1)],
                                  sem.at[dst_slot]).start()
            return carry

        lax.fori_loop(0, tm, row, 0, unroll=8)

    @pl.when(i == 0)
    def _():
        gather(idx_cur, 0)

    @pl.when(i + 1 < n_valid)
    def _():
        gather(idx_next, 1 - slot)

    @pl.when((i == 0) | (be_ref[i] != be_ref[jnp.maximum(i - 1, 0)]))
    def _():
        w1b[...] = w1_ref[0, 0].astype(BF16)
        w3b[...] = w3_ref[0, 0].astype(BF16)
        w2b[...] = w2_ref[0, 0].astype(BF16)

    @pl.when(i < n_valid)
    def _():
        pltpu.make_async_copy(h_hbm.at[pl.ds(0, tm)], xbuf.at[slot], sem.at[slot]).wait()
        x = xbuf[slot].astype(BF16)
        h1 = _dot(x, w1b[...])
        h3 = _dot(x, w3b[...])
        h = (_silu(h1) * h3).astype(BF16)
        y_ref[...] = _dot(h, w2b[...]).astype(y_ref.dtype)

    @pl.when(i >= n_valid)
    def _():
        y_ref[...] = jnp.zeros(y_ref.shape, y_ref.dtype)


def _moe_ffn(block_e, n_valid, row_tok, h, w1, w3, w2, layer, tm):
    n_blocks = row_tok.shape[0]
    d = h.shape[1]
    last = n_blocks - 1
    grid_spec = pltpu.PrefetchScalarGridSpec(
        num_scalar_prefetch=2,
        grid=(n_blocks,),
        in_specs=[pl.BlockSpec((1, 1, tm), lambda i, be, nv: (i, 0, 0), memory_space=pltpu.SMEM),
                  pl.BlockSpec((1, 1, tm), lambda i, be, nv: (jnp.minimum(i + 1, last), 0, 0),
                               memory_space=pltpu.SMEM),
                  pl.BlockSpec(memory_space=pl.ANY),
                  pl.BlockSpec((1, 1, d, D_EXPERT), lambda i, be, nv: (layer, be[i], 0, 0)),
                  pl.BlockSpec((1, 1, d, D_EXPERT), lambda i, be, nv: (layer, be[i], 0, 0)),
                  pl.BlockSpec((1, 1, D_EXPERT, d), lambda i, be, nv: (layer, be[i], 0, 0))],
        out_specs=pl.BlockSpec((tm, d), lambda i, be, nv: (i, 0)),
        scratch_shapes=[pltpu.VMEM((2, tm, d), F32), pltpu.SemaphoreType.DMA((2,)),
                        pltpu.VMEM((d, D_EXPERT), BF16), pltpu.VMEM((d, D_EXPERT), BF16),
                        pltpu.VMEM((D_EXPERT, d), BF16)],
    )
    return pl.pallas_call(
        functools.partial(_moe_kernel, tm=tm),
        grid_spec=grid_spec,
        out_shape=jax.ShapeDtypeStruct((n_blocks * tm, d), BF16),
        name="moe_ffn",
        compiler_params=_cparams(("arbitrary",)),
    )(block_e, n_valid, row_tok, row_tok, h, w1, w3, w2)


def _combine_kernel(x_ref, y0_ref, y1_ref, gw_ref, g2_ref, lng_ref, lnb_ref, o_ref, *, alpha):
    gw = gw_ref[...]
    y = gw[:, 0:1] * y0_ref[...].astype(F32) + gw[:, 1:2] * y1_ref[...].astype(F32)
    o_ref[...] = _layer_norm(alpha * x_ref[...] + g2_ref[0] * y) * lng_ref[...] + lnb_ref[...]


def _combine(x, y0, y1, gw, g2, ln_g, ln_b, tm, mod_index, alpha):
    n, d = x.shape
    row = lambda i: (i, 0)
    full = lambda i: (0, 0)
    return pl.pallas_call(
        functools.partial(_combine_kernel, alpha=alpha),
        grid=(n // tm,),
        in_specs=[pl.BlockSpec((tm, d), row), pl.BlockSpec((tm, d), row), pl.BlockSpec((tm, d), row),
                  pl.BlockSpec((tm, ROUTER_LANES), row),
                  pl.BlockSpec((1, 1, d), lambda i: (mod_index(i), 0, 0)),
                  pl.BlockSpec((1, d), full), pl.BlockSpec((1, d), full)],
        out_specs=pl.BlockSpec((tm, d), row),
        out_shape=jax.ShapeDtypeStruct((n, d), F32),
        name="moe_combine",
        compiler_params=_cparams(("parallel",)),
    )(x, y0, y1, gw, g2, ln_g, ln_b)


def _rot_half_cols(w):
    lead = w.shape[:-1]
    wr = w.reshape(lead + (2, 2, MLA_ROPE // 4))
    return jnp.stack([-wr[..., 1, :], wr[..., 0, :]], axis=-2).reshape(w.shape)


def _rope_tables(n_lat, n_ctx, bsz):
    rows = n_lat // GRID_W
    row = jnp.repeat(jnp.arange(rows, dtype=F32), GRID_W)
    col = jnp.tile(jnp.arange(GRID_W, dtype=F32), rows)
    n_freq = MLA_ROPE // 4
    inv_freq = ROPE_BASE ** (-jnp.arange(n_freq, dtype=F32) / n_freq)
    ang = jnp.stack([row[:, None] * inv_freq, col[:, None] * inv_freq], axis=1)
    cos = jnp.broadcast_to(jnp.cos(ang)[:, :, None, :], (n_lat, 2, 2, n_freq)).reshape(n_lat, MLA_ROPE)
    sin = jnp.broadcast_to(jnp.sin(ang)[:, :, None, :], (n_lat, 2, 2, n_freq)).reshape(n_lat, MLA_ROPE)
    pad_l = MLA_NOPE
    pad_r = HEAD_LANES - MLA_QK
    cos_t = jnp.concatenate([jnp.ones((n_lat, pad_l), F32), cos, jnp.ones((n_lat, pad_r), F32)], axis=1)
    sin_t = jnp.concatenate([jnp.zeros((n_lat, pad_l), F32), sin, jnp.zeros((n_lat, pad_r), F32)], axis=1)
    cos_all = jnp.concatenate([jnp.tile(cos_t, (bsz, 1)), jnp.ones((bsz * n_ctx, HEAD_LANES), F32)], axis=0)
    sin_all = jnp.concatenate([jnp.tile(sin_t, (bsz, 1)), jnp.zeros((bsz * n_ctx, HEAD_LANES), F32)], axis=0)
    return cos_all, sin_all


def _prep_w_in(w_in):
    depth, d, _ = w_in.shape
    kr = w_in[:, :, W_IN_MAIN:]
    z_l = jnp.zeros((depth, d, MLA_NOPE), w_in.dtype)
    z_r = jnp.zeros((depth, d, HEAD_LANES - MLA_QK), w_in.dtype)
    return jnp.concatenate([w_in[:, :, :W_IN_MAIN], z_l, kr, z_r, z_l, _rot_half_cols(kr), z_r],
                           axis=-1).astype(BF16)


def _prep_mla(w_uq, w_ukv):
    depth = w_uq.shape[0]
    wq = w_uq.reshape(depth, MLA_Q_LORA, MLA_HEADS, MLA_QK)
    pad = jnp.zeros((depth, MLA_Q_LORA, MLA_HEADS, HEAD_LANES - MLA_QK), w_uq.dtype)
    wq_p = jnp.concatenate([wq, pad], axis=-1).reshape(depth, MLA_Q_LORA, MLA_HEADS * HEAD_LANES)
    wq_s = jnp.concatenate([jnp.zeros_like(wq[..., :MLA_NOPE]), _rot_half_cols(wq[..., MLA_NOPE:]), pad],
                           axis=-1).reshape(depth, MLA_Q_LORA, MLA_HEADS * HEAD_LANES)
    wkv = w_ukv.reshape(depth, MLA_KV_LORA, MLA_HEADS, MLA_NOPE + MLA_V)
    kpad = jnp.zeros((depth, MLA_KV_LORA, MLA_HEADS, HEAD_LANES - MLA_NOPE), w_ukv.dtype)
    wk = jnp.concatenate([wkv[..., :MLA_NOPE], kpad], axis=-1).reshape(depth, MLA_KV_LORA, MLA_HEADS * HEAD_LANES)
    wvt = jnp.swapaxes(wkv[..., MLA_NOPE:].reshape(depth, MLA_KV_LORA, MLA_HEADS * MLA_V), 1, 2)
    return wq_p.astype(BF16), wq_s.astype(BF16), wk.astype(BF16), wvt.astype(BF16)


def _block_diag(pool_w):
    depth, g, c, _ = pool_w.shape
    eye = jnp.eye(g, dtype=pool_w.dtype)
    return jnp.einsum("lgcd,gh->lgchd", pool_w, eye).reshape(depth, g * c, g * c).astype(BF16)


def _route(logits, b_rg, b_re, tm_moe):
    n = logits.shape[0]
    lg = logits[:, :N_GROUPS] + b_rg
    le = (logits[:, N_GROUPS:N_GROUPS + N_EXPERTS] + b_re).reshape(n, N_GROUPS, EXPERTS_PER_GROUP)
    pg = jax.nn.softmax(lg, axis=-1)
    g_sel = jnp.argmax(lg, axis=-1).astype(jnp.int32)
    g_hot = g_sel[:, None] == jnp.arange(N_GROUPS, dtype=jnp.int32)[None, :]
    w_g = jnp.sum(jnp.where(g_hot, pg, 0.0), axis=-1)
    le_sel = jnp.sum(jnp.where(g_hot[:, :, None], le, 0.0), axis=1)
    top_v, top_i = lax.top_k(le_sel, TOP_K)
    gate = jax.nn.softmax(top_v, axis=-1) * w_g[:, None]
    expert = (g_sel[:, None] * EXPERTS_PER_GROUP + top_i).astype(jnp.int32)

    n_assign = n * TOP_K
    e_flat = expert.reshape(-1)
    hot = e_flat[:, None] == jnp.arange(N_EXPERTS, dtype=jnp.int32)[None, :]
    nb = n_assign // COUNT_BLOCK
    tri = jnp.tril(jnp.ones((COUNT_BLOCK, COUNT_BLOCK), BF16))
    win = jnp.einsum("ts,bse->bte", tri, hot.astype(BF16).reshape(nb, COUNT_BLOCK, N_EXPERTS),
                     preferred_element_type=F32)
    bsum = win[:, -1, :]
    boff = jnp.cumsum(bsum, axis=0) - bsum
    csum = (win + boff[:, None, :]).reshape(n_assign, N_EXPERTS).astype(jnp.int32)
    counts = (boff[-1] + bsum[-1]).astype(jnp.int32)
    pcounts = (counts + tm_moe - 1) // tm_moe * tm_moe
    pends = jnp.cumsum(pcounts)
    pstarts = pends - pcounts
    starts = jnp.cumsum(counts) - counts
    dest = jnp.sum(jnp.where(hot, csum - 1 + pstarts[None, :], 0), axis=1)
    n_rows = (n_assign + N_EXPERTS * (tm_moe - 1) + tm_moe - 1) // tm_moe * tm_moe
    n_blocks = n_rows // tm_moe
    n_valid = (pends[-1] // tm_moe).astype(jnp.int32)
    blk = jnp.arange(n_blocks, dtype=jnp.int32)
    block_e = jnp.searchsorted(pends, jnp.minimum(blk, n_valid - 1) * tm_moe, side="right").astype(jnp.int32)
    block_e = jnp.clip(block_e, 0, N_EXPERTS - 1)
    a_sorted = jnp.sort(e_flat * n_assign + jnp.arange(n_assign, dtype=jnp.int32)) % n_assign
    pos = (blk * tm_moe - pstarts[block_e])[:, None] + jnp.arange(tm_moe, dtype=jnp.int32)[None, :]
    src = jnp.clip(starts[block_e][:, None] + pos, 0, n_assign - 1)
    row_tok = jnp.where(pos < counts[block_e][:, None], a_sorted[src] // TOP_K, 0)
    gw = jnp.zeros((n, ROUTER_LANES), F32).at[:, :TOP_K].set(gate)
    return row_tok.reshape(n_blocks, 1, tm_moe), dest.reshape(n, TOP_K), gw, block_e, n_valid.reshape(1)


def _tile(n, pref):
    t = pref
    while n % t:
        t //= 2
    return t


def kernel(x, c, ctx, c_ctx, w_mod, b_mod, w_in, pool_w, pool_scale, hg_lb_logits, hg_norm_g, mla_q_norm_g, mla_w_uq, mla_kv_norm_g, mla_w_ukv, w_out, ln1_g, ln1_b, ln2_g, ln2_b, router_group_w, router_group_b, router_expert_w, router_expert_b, expert_w1, expert_w3, expert_w2):
    bsz, n_lat, d = x.shape
    n_ctx = ctx.shape[1]
    depth = w_mod.shape[0]
    alpha = (2 * depth) ** 0.25
    lat_rows = bsz * n_lat
    ctx_rows = bsz * n_ctx
    n_tok = lat_rows + ctx_rows

    tm = _tile(np.gcd(n_lat, ctx_rows), 512)
    tp = _tile(np.gcd(n_lat, n_ctx), 256)
    tb_lat = _tile(n_lat, 512)
    tb_ctx = _tile(n_ctx, 512)
    tq_lat = _tile(n_lat, 512)
    tkc = _tile(np.gcd(n_lat, n_ctx), 256)
    tq_ctx = _tile(n_ctx, 512)
    tm_moe = 256
    lat_tiles_per_batch = n_lat // tm
    mod_index = lambda i: jnp.minimum(i // lat_tiles_per_batch, bsz)

    c_rows = jnp.concatenate([jax.nn.silu(c), jax.nn.silu(c_ctx)[None, :],
                              jnp.zeros((-(bsz + 1) % 8, d), F32)], axis=0)
    mod_all = _mod_all(c_rows, w_mod, b_mod)
    lb_all = jnp.cumsum(jax.nn.softmax(hg_lb_logits.astype(F32), axis=0), axis=0)
    lb_all = lb_all - lb_all[0:1]
    log_lb = jnp.log(lb_all)
    log_1m_lb = jnp.log1p(-lb_all)
    w_aug = _prep_w_in(w_in)
    wq_p, wq_s, wk_p, wvt_p = _prep_mla(mla_w_uq, mla_w_ukv)
    w_bd = _block_diag(pool_w)
    w_out_b = w_out.astype(BF16)
    w_router = jnp.concatenate([router_group_w, router_expert_w,
                                jnp.zeros((depth, d, ROUTER_LANES - N_GROUPS - N_EXPERTS), F32)], axis=-1)
    gavg = jnp.kron(jnp.eye(HG_HEADS, dtype=F32), jnp.full((HG_DV, HG_DV), 1.0 / HG_DV, F32)).astype(BF16)
    cos_t, sin_t = _rope_tables(n_lat, n_ctx, bsz)
    zero_state = jnp.zeros((bsz, HG_HEADS, HG_DV, HG_DK), F32)

    xt = jnp.concatenate([x.reshape(lat_rows, d), ctx.reshape(ctx_rows, d)], axis=0)
    for l in range(depth):
        mod = mod_all[l, :bsz + 1].reshape(bsz + 1, 6, 1, d)
        sh1, sc1, g1, sh2, sc2, g2 = (mod[:, k] for k in range(6))

        p_in, hq, hzf, hzb, hin, hg, cq, ckv, kr, krs = _in_proj(xt, sh1, sc1, w_aug[l], tm, mod_index)

        a_mix = _pool(p_in, w_bd[l], pool_scale[l][None, :], tp, lat_rows, n_lat, n_ctx)

        la_f, la_b = log_lb[l, 0][None, :], log_lb[l, 1][None, :]
        lc_f, lc_b = log_1m_lb[l, 0][None, :], log_1m_lb[l, 1][None, :]
        oc_f, s_f = _hgrn_scan(hq, hzf, hin, la_f, lc_f, zero_state, reverse=False, base_row=lat_rows,
                               n_seq=n_ctx, tb=tb_ctx)
        oc_b, s_b = _hgrn_scan(hq, hzb, hin, la_b, lc_b, zero_state, reverse=True, base_row=lat_rows,
                               n_seq=n_ctx, tb=tb_ctx)
        ol_f, _ = _hgrn_scan(hq, hzf, hin, la_f, lc_f, s_f, reverse=False, base_row=0, n_seq=n_lat, tb=tb_lat)
        ol_b, _ = _hgrn_scan(hq, hzb, hin, la_b, lc_b, s_b, reverse=True, base_row=0, n_seq=n_lat, tb=tb_lat)
        o_f = jnp.concatenate([ol_f, oc_f], axis=0)
        o_b = jnp.concatenate([ol_b, oc_b], axis=0)

        q_all, k_all, vt_all = _mla_proj(cq, ckv, kr, krs, cos_t, sin_t, mla_q_norm_g[l][None, :],
                                         mla_kv_norm_g[l][None, :], wq_p[l], wq_s[l], wk_p[l], wvt_p[l], tkc)
        c_lat = _attention(q_all, k_all, vt_all, bsz=bsz, n_q=n_lat, q_base=0, n_lat=n_lat, n_ctx=n_ctx,
                           ctx_base=lat_rows, tq=tq_lat, tkc=tkc, with_lat=True)
        c_ctx_o = _attention(q_all, k_all, vt_all, bsz=bsz, n_q=n_ctx, q_base=lat_rows, n_lat=n_lat, n_ctx=n_ctx,
                             ctx_base=lat_rows, tq=tq_ctx, tkc=tkc, with_lat=False)
        c_att = jnp.concatenate([c_lat, c_ctx_o], axis=0)

        x1, h2, logits = _out_proj(a_mix, o_f, o_b, hg, c_att, xt, g1, sh2, sc2, ln1_g[l][None, :],
                                   ln1_b[l][None, :], hg_norm_g[l][None, :], gavg, w_out_b[l], w_router[l],
                                   tm, mod_index, alpha)

        row_tok, dest, gw, block_e, n_valid = _route(logits, router_group_b[l], router_expert_b[l], tm_moe)
        y = _moe_ffn(block_e, n_valid, row_tok, h2, expert_w1, expert_w3, expert_w2, l, tm_moe)
        y0 = jnp.take(y, dest[:, 0], axis=0)
        y1 = jnp.take(y, dest[:, 1], axis=0)
        xt = _combine(x1, y0, y1, gw, g2, ln2_g[l][None, :], ln2_b[l][None, :], tm, mod_index, alpha)
    return xt[:lat_rows].reshape(bsz, n_lat, d)
```

```python
import functools

import numpy as np
import jax
import jax.numpy as jnp
from jax import lax
from jax.experimental import pallas as pl
from jax.experimental.pallas import tpu as pltpu

F32 = jnp.float32
BF16 = jnp.bfloat16

D_MODEL = 1024
GRID_W = 64
POOL_WINDOWS = (2, 4, 8, 16)
POOL_DIM = 256
POOL_GROUP_DIM = 64
HG_HEADS = 4
HG_DK = 128
HG_DV = 64
HG_QK = HG_HEADS * HG_DK
HG_V = HG_HEADS * HG_DV
HG_CHUNK = 64
HG_SUB = 16
MLA_HEADS = 8
MLA_NOPE = 64
MLA_ROPE = 32
MLA_V = 64
MLA_Q_LORA = 256
MLA_KV_LORA = 128
MLA_QK = MLA_NOPE + MLA_ROPE
MLA_SCALE = MLA_QK ** -0.5
LOG2_E = 1.4426950408889634
ROPE_BASE = 10000.0
HEAD_LANES = 128
N_GROUPS = 4
EXPERTS_PER_GROUP = 8
N_EXPERTS = N_GROUPS * EXPERTS_PER_GROUP
TOP_K = 2
D_EXPERT = 512
EPS = 1e-6
ROUTER_LANES = 128
COUNT_BLOCK = 128
MOE_GATHER_PHASES = 4

IN_SEGS = (("p_in", 256), ("hq", 512), ("hzf", 512), ("hzb", 512), ("hin", 256), ("hg", 256),
           ("cq", 256), ("ckv", 128), ("kr", 128), ("krs", 128))
IN_AUG = sum(w for _, w in IN_SEGS)
W_IN_MAIN = 2688

VMEM_LIMIT = 56 * 1024 * 1024


def _cparams(sem):
    return pltpu.CompilerParams(dimension_semantics=sem, vmem_limit_bytes=VMEM_LIMIT)


def _split_bf16(a):
    hi = a.astype(BF16)
    lo = (a - hi.astype(F32)).astype(BF16)
    return hi, lo


def _dot(a, b):
    return jnp.dot(a, b, preferred_element_type=F32)


def _dot_nt(a, b):
    return lax.dot_general(a, b, (((1,), (1,)), ((), ())), preferred_element_type=F32)


def _dot_tn(a, b):
    return lax.dot_general(a, b, (((0,), (0,)), ((), ())), preferred_element_type=F32)


def _dot3(a, b):
    a_hi, a_lo = _split_bf16(a)
    b_hi, b_lo = _split_bf16(b)
    return _dot(a_hi, b_hi) + _dot(a_hi, b_lo) + _dot(a_lo, b_hi)


def _dot_exact_rhs(a, b_bf16):
    a_hi = a.astype(BF16)
    r1 = a - a_hi.astype(F32)
    a_mid = r1.astype(BF16)
    a_lo = (r1 - a_mid.astype(F32)).astype(BF16)
    return _dot(a_hi, b_bf16) + _dot(a_mid, b_bf16) + _dot(a_lo, b_bf16)


def _layer_norm(x):
    mu = jnp.mean(x, axis=-1, keepdims=True)
    xc = x - mu
    var = jnp.mean(xc * xc, axis=-1, keepdims=True)
    return xc * lax.rsqrt(var + EPS)


def _sigmoid(x):
    return 1.0 / (1.0 + jnp.exp(-x))


def _silu(x):
    return x * _sigmoid(x)


def _mod_kernel(a_ref, w_ref, b_ref, o_ref):
    o_ref[0] = _dot3(a_ref[...], w_ref[0]) + b_ref[0]


def _mod_all(c_rows, w_mod, b_mod):
    depth, d, n6 = w_mod.shape
    tn = 1536
    rows = c_rows.shape[0]
    return pl.pallas_call(
        _mod_kernel,
        grid=(depth, n6 // tn),
        in_specs=[pl.BlockSpec((rows, d), lambda l, j: (0, 0)),
                  pl.BlockSpec((1, d, tn), lambda l, j: (l, 0, j)),
                  pl.BlockSpec((1, 1, tn), lambda l, j: (l, 0, j))],
        out_specs=pl.BlockSpec((1, rows, tn), lambda l, j: (l, 0, j)),
        out_shape=jax.ShapeDtypeStruct((depth, rows, n6), F32),
        name="adaln_mod",
        compiler_params=_cparams(("arbitrary", "arbitrary")),
    )(c_rows, w_mod, b_mod.reshape(depth, 1, n6))


def _inproj_kernel(x_ref, sh_ref, sc_ref, w_ref, *out_refs):
    h = _layer_norm(x_ref[...]) * (1.0 + sc_ref[0]) + sh_ref[0]
    hb = h.astype(BF16)
    start = 0
    for (_, width), o_ref in zip(IN_SEGS, out_refs):
        o_ref[...] = _dot(hb, w_ref[:, start:start + width])
        start += width


def _in_proj(x, sh, sc, w_aug, tm, mod_index):
    n, d = x.shape
    row = lambda i: (i, 0)
    mod = lambda i: (mod_index(i), 0, 0)
    return pl.pallas_call(
        _inproj_kernel,
        grid=(n // tm,),
        in_specs=[pl.BlockSpec((tm, d), row),
                  pl.BlockSpec((1, 1, d), mod),
                  pl.BlockSpec((1, 1, d), mod),
                  pl.BlockSpec((d, IN_AUG), lambda i: (0, 0))],
        out_specs=[pl.BlockSpec((tm, w), row) for _, w in IN_SEGS],
        out_shape=[jax.ShapeDtypeStruct((n, w), F32) for _, w in IN_SEGS],
        name="in_proj",
        compiler_params=_cparams(("parallel",)),
    )(x, sh, sc, w_aug)


POOL_HALO = 8


def _pool_kernel(x_ref, prev_ref, next_ref, w_ref, scale_ref, o_ref, *, tp, lat_tiles, tpb_lat, tpb_ctx,
                 n_lat, n_ctx):
    i = pl.program_id(0)
    is_lat = i < lat_tiles
    tile_in_seq = jnp.where(is_lat, i % tpb_lat, (i - lat_tiles) % tpb_ctx)
    tiles_in_seq = jnp.where(is_lat, tpb_lat, tpb_ctx)
    n_seq = jnp.where(is_lat, n_lat, n_ctx)
    first = tile_in_seq == 0
    last = tile_in_seq == tiles_in_seq - 1
    x = x_ref[...]
    prev = jnp.where(first, 0.0, prev_ref[...])
    nxt = jnp.where(last, 0.0, next_ref[...])
    ext = jnp.concatenate([prev, x, nxt], axis=0)
    n_ext = tp + 2 * POOL_HALO

    def back(a, k):
        return pltpu.roll(a, k, 0)

    def fwd(a, k):
        return pltpu.roll(a, n_ext - k, 0)

    e2 = ext + back(ext, 1)
    e4 = back(e2, 1) + fwd(e2, 1)
    e8 = back(e4, 2) + fwd(e4, 2)
    e16 = back(e8, 4) + fwd(e8, 4)
    sl = slice(POOL_HALO, POOL_HALO + tp)
    lane = lax.broadcasted_iota(jnp.int32, (tp, POOL_DIM), 1)
    grp = lane >> 6
    wsum = jnp.where(grp == 0, e2[sl], jnp.where(grp == 1, e4[sl], jnp.where(grp == 2, e8[sl], e16[sl])))
    half = jnp.where(grp == 0, 1, jnp.where(grp == 1, 2, jnp.where(grp == 2, 4, 8)))
    t = tile_in_seq * tp + lax.broadcasted_iota(jnp.int32, (tp, POOL_DIM), 0)
    cnt = jnp.minimum(t + half, n_seq) - jnp.maximum(t - half, 0)
    pooled = wsum / cnt.astype(F32) - x
    o_ref[...] = _dot(pooled.astype(BF16), w_ref[...]) * scale_ref[...]


def _pool(p_in, w_bd, scale, tp, n_lat_rows, n_lat, n_ctx):
    n, c = p_in.shape
    hb = tp // POOL_HALO
    last_halo_block = n // POOL_HALO - 1
    kern = functools.partial(_pool_kernel, tp=tp, lat_tiles=n_lat_rows // tp, tpb_lat=n_lat // tp,
                             tpb_ctx=n_ctx // tp, n_lat=n_lat, n_ctx=n_ctx)
    return pl.pallas_call(
        kern,
        grid=(n // tp,),
        in_specs=[pl.BlockSpec((tp, c), lambda i: (i, 0)),
                  pl.BlockSpec((POOL_HALO, c), lambda i: (jnp.maximum(i * hb - 1, 0), 0)),
                  pl.BlockSpec((POOL_HALO, c), lambda i: (jnp.minimum((i + 1) * hb, last_halo_block), 0)),
                  pl.BlockSpec((c, c), lambda i: (0, 0)),
                  pl.BlockSpec((1, c), lambda i: (0, 0))],
        out_specs=pl.BlockSpec((tp, c), lambda i: (i, 0)),
        out_shape=jax.ShapeDtypeStruct((n, c), F32),
        name="pool",
        compiler_params=_cparams(("parallel",)),
    )(p_in, p_in, p_in, w_bd, scale)


def _hgrn_kernel(hq_ref, hz_ref, hin_ref, la_ref, lc_ref, s0_ref, o_ref, sT_ref, q_s, g_s, k_s, st_s,
                 *, reverse, nchunk):
    j = pl.program_id(1)

    @pl.when(j == 0)
    def _():
        st_s[...] = s0_ref[0]

    z = hz_ref[...]
    ls = jnp.minimum(z, 0.0) - jnp.log1p(jnp.exp(-jnp.abs(z)))
    lc = lc_ref[...]
    x2 = lc + ls
    la = la_ref[...]
    g_s[...] = jnp.maximum(la, x2) + jnp.log1p(jnp.exp(-jnp.abs(la - x2)))
    k_s[...] = jnp.exp(x2 - z)
    q_s[...] = _silu(hq_ref[...])

    C = HG_CHUNK
    row = lax.broadcasted_iota(jnp.int32, (C, C), 0)
    col = lax.broadcasted_iota(jnp.int32, (C, C), 1)
    tri = ((row <= col) if reverse else (row >= col)).astype(BF16)
    rmod = lax.broadcasted_iota(jnp.int32, (C, 1), 0) & (HG_SUB - 1)
    nsub = C // HG_SUB
    row_sub = row >> 4
    col_sub = col >> 4

    def chunk(c, carry):
        cc = (nchunk - 1 - c) if reverse else c
        r0 = pl.multiple_of(cc * C, C)
        b_all = _dot_exact_rhs_lhs(tri, g_s[pl.ds(r0, C), :])
        q_all = q_s[pl.ds(r0, C), :]
        k_all = k_s[pl.ds(r0, C), :]
        v_all = hin_ref[pl.ds(r0, C), :]
        outs = []
        for h in range(HG_HEADS):
            ks = slice(h * HG_DK, (h + 1) * HG_DK)
            b, q, k = b_all[:, ks], q_all[:, ks], k_all[:, ks]
            v = v_all[:, h * HG_DV:(h + 1) * HG_DV]
            st = st_s[h]
            b_last = b[0:1] if reverse else b[C - 1:C]
            o = _dot_nt((q * jnp.exp(b)).astype(BF16), st.astype(BF16))
            a_rows = []
            for i in range(nsub):
                rs = slice(i * HG_SUB, (i + 1) * HG_SUB)
                if (reverse and i == nsub - 1) or (not reverse and i == 0):
                    a_rows.append(jnp.zeros((HG_SUB, C), F32))
                    continue
                m = b[(i + 1) * HG_SUB - 1:(i + 1) * HG_SUB] if reverse else b[i * HG_SUB:i * HG_SUB + 1]
                qi = q[rs] * jnp.exp(b[rs] - m)
                ksc = k * jnp.exp(m - b)
                a_rows.append(_dot_nt(qi.astype(BF16), ksc.astype(BF16)))
            a = jnp.concatenate(a_rows, axis=0)
            if reverse:
                a = jnp.where(col_sub > row_sub, a, 0.0)
            else:
                a = jnp.where(col_sub < row_sub, a, 0.0)
            o = o + _dot(a.astype(BF16), v.astype(BF16))
            for delta in range(HG_SUB):
                if delta == 0:
                    w = jnp.sum(q * k, axis=-1, keepdims=True)
                    o = o + w * v
                    continue
                shift = (C - delta) if reverse else delta
                kd = pltpu.roll(k, shift, 0)
                bd = pltpu.roll(b, shift, 0)
                vd = pltpu.roll(v, shift, 0)
                e = jnp.exp(b - bd)
                w = jnp.sum(q * kd * e, axis=-1, keepdims=True)
                valid = (rmod + delta < HG_SUB) if reverse else (rmod >= delta)
                o = o + jnp.where(valid, w, 0.0) * vd
            outs.append(o)
            kdec = k * jnp.exp(b_last - b)
            st_s[h] = st * jnp.exp(b_last) + _dot_tn(v.astype(BF16), kdec.astype(BF16))
        o_ref[pl.ds(r0, C), :] = jnp.concatenate(outs, axis=1)
        return carry

    lax.fori_loop(0, nchunk, chunk, 0)

    @pl.when(j == pl.num_programs(1) - 1)
    def _():
        sT_ref[0] = st_s[...]


def _dot_exact_rhs_lhs(tri_bf16, g):
    g_hi = g.astype(BF16)
    r1 = g - g_hi.astype(F32)
    g_mid = r1.astype(BF16)
    g_lo = (r1 - g_mid.astype(F32)).astype(BF16)
    return _dot(tri_bf16, g_hi) + _dot(tri_bf16, g_mid) + _dot(tri_bf16, g_lo)


def _hgrn_scan(hq, hz, hin, la, lc, s0, *, reverse, base_row, n_seq, tb):
    bsz = s0.shape[0]
    nblk = n_seq // tb
    base_blk = base_row // tb

    def rows(b, j):
        jj = (nblk - 1 - j) if reverse else j
        return (base_blk + b * nblk + jj, 0)

    def orow(b, j):
        jj = (nblk - 1 - j) if reverse else j
        return (b * nblk + jj, 0)

    kern = functools.partial(_hgrn_kernel, reverse=reverse, nchunk=tb // HG_CHUNK)
    return pl.pallas_call(
        kern,
        grid=(bsz, nblk),
        in_specs=[pl.BlockSpec((tb, HG_QK), rows),
                  pl.BlockSpec((tb, HG_QK), rows),
                  pl.BlockSpec((tb, HG_V), rows),
                  pl.BlockSpec((1, HG_QK), lambda b, j: (0, 0)),
                  pl.BlockSpec((1, HG_QK), lambda b, j: (0, 0)),
                  pl.BlockSpec((1, HG_HEADS, HG_DV, HG_DK), lambda b, j: (b, 0, 0, 0))],
        out_specs=[pl.BlockSpec((tb, HG_V), orow),
                   pl.BlockSpec((1, HG_HEADS, HG_DV, HG_DK), lambda b, j: (b, 0, 0, 0))],
        out_shape=[jax.ShapeDtypeStruct((bsz * n_seq, HG_V), F32),
                   jax.ShapeDtypeStruct((bsz, HG_HEADS, HG_DV, HG_DK), F32)],
        scratch_shapes=[pltpu.VMEM((tb, HG_QK), F32), pltpu.VMEM((tb, HG_QK), F32),
                        pltpu.VMEM((tb, HG_QK), F32), pltpu.VMEM((HG_HEADS, HG_DV, HG_DK), F32)],
        name="hgrn_bwd" if reverse else "hgrn_fwd",
        compiler_params=_cparams(("parallel", "arbitrary")),
    )(hq, hz, hin, la, lc, s0)


def _rms(x, g):
    return x * lax.rsqrt(jnp.mean(x * x, axis=-1, keepdims=True) + EPS) * g


def _mla_proj_kernel(cq_ref, ckv_ref, kr_ref, krs_ref, cos_ref, sin_ref, qg_ref, kvg_ref,
                     wq_ref, wqs_ref, wk_ref, wvt_ref, q_out, k_out, vt_out):
    cos = cos_ref[...]
    sin = sin_ref[...]
    cos_h = jnp.concatenate([cos] * MLA_HEADS, axis=1)
    sin_h = jnp.concatenate([sin] * MLA_HEADS, axis=1)
    xq = _rms(cq_ref[...], qg_ref[...]).astype(BF16)
    q = _dot(xq, wq_ref[...]) * cos_h + _dot(xq, wqs_ref[...]) * sin_h
    q_out[...] = (q * (MLA_SCALE * LOG2_E)).astype(BF16)
    xkv = _rms(ckv_ref[...], kvg_ref[...]).astype(BF16)
    k_rope = kr_ref[...] * cos + krs_ref[...] * sin
    k = _dot(xkv, wk_ref[...]) + jnp.concatenate([k_rope] * MLA_HEADS, axis=1)
    k_out[...] = k.astype(BF16)
    vt_out[0] = _dot_nt(wvt_ref[...], xkv).astype(BF16)


def _mla_proj(cq, ckv, kr, krs, cos_t, sin_t, qg, kvg, wq, wqs, wk, wvt, tm):
    n = cq.shape[0]
    row = lambda i: (i, 0)
    full = lambda i: (0, 0)
    hq = MLA_HEADS * HEAD_LANES
    hv = MLA_HEADS * MLA_V
    return pl.pallas_call(
        _mla_proj_kernel,
        grid=(n // tm,),
        in_specs=[pl.BlockSpec((tm, MLA_Q_LORA), row), pl.BlockSpec((tm, MLA_KV_LORA), row),
                  pl.BlockSpec((tm, HEAD_LANES), row), pl.BlockSpec((tm, HEAD_LANES), row),
                  pl.BlockSpec((tm, HEAD_LANES), row), pl.BlockSpec((tm, HEAD_LANES), row),
                  pl.BlockSpec((1, MLA_Q_LORA), full), pl.BlockSpec((1, MLA_KV_LORA), full),
                  pl.BlockSpec((MLA_Q_LORA, hq), full), pl.BlockSpec((MLA_Q_LORA, hq), full),
                  pl.BlockSpec((MLA_KV_LORA, hq), full), pl.BlockSpec((hv, MLA_KV_LORA), full)],
        out_specs=[pl.BlockSpec((tm, hq), row), pl.BlockSpec((tm, hq), row),
                   pl.BlockSpec((1, hv, tm), lambda i: (i, 0, 0))],
        out_shape=[jax.ShapeDtypeStruct((n, hq), BF16), jax.ShapeDtypeStruct((n, hq), BF16),
                   jax.ShapeDtypeStruct((n // tm, hv, tm), BF16)],
        name="mla_proj",
        compiler_params=_cparams(("parallel",)),
    )(cq, ckv, kr, krs, cos_t, sin_t, qg, kvg, wq, wqs, wk, wvt)


ACC_ROWS = MLA_V + 16
LAT_UNROLL = 4
LOOKAHEAD = 2


def _attn_kernel(*refs, tkc, n_lat_chunks, n_ctx_chunks):
    if n_lat_chunks:
        q_ref, kl_ref, vl_ref, kc_ref, vc_ref, o_ref, m_s, acc_s, s_s = refs
    else:
        q_ref, kc_ref, vc_ref, o_ref, m_s, acc_s, s_s = refs
    m_s[...] = jnp.full(m_s.shape, -jnp.inf, F32)
    acc_s[...] = jnp.zeros(acc_s.shape, F32)
    ones = jnp.ones((ACC_ROWS - MLA_V, tkc), BF16)

    def scores(a, k_ref, r0):
        hs = slice(a * HEAD_LANES, (a + 1) * HEAD_LANES)
        return _dot_nt(k_ref[pl.ds(r0, tkc), hs], q_ref[:, hs])

    def accumulate(a, s_t, vt):
        m_old = m_s[a]
        m_new = jnp.maximum(m_old, jnp.max(s_t, axis=0, keepdims=True))
        alpha = jnp.exp2(m_old - m_new)
        p_t = jnp.exp2(s_t - m_new).astype(BF16)
        vta = jnp.concatenate([vt[a * MLA_V:(a + 1) * MLA_V], ones], axis=0)
        acc_s[a] = alpha * acc_s[a] + _dot(vta, p_t)
        m_s[a] = m_new

    def run(items, queue, after):
        upcoming = items[LOOKAHEAD:] + after
        queue = list(queue)
        for idx, (a, _, _, vt) in enumerate(items):
            if idx < len(upcoming):
                queue.append(scores(*upcoming[idx][:3]))
            accumulate(a, queue.pop(0), vt)
        return queue

    ahead_chunks = (LOOKAHEAD + 1) // 2

    def lat_items(c0, dynamic, count=LAT_UNROLL):
        items = []
        for u in range(count):
            cc = c0 + u
            r0 = pl.multiple_of(cc * tkc, tkc) if dynamic else cc * tkc
            vt = vl_ref[cc] if count == LAT_UNROLL else None
            items += [(0, kl_ref, r0, vt), (1, kl_ref, r0, vt)]
        return items

    ctx_items = []
    for c in range(n_ctx_chunks):
        ctx_items += [(0, kc_ref, c * tkc, vc_ref[c]), (1, kc_ref, c * tkc, vc_ref[c])]

    if n_lat_chunks:
        n_groups = n_lat_chunks // LAT_UNROLL
        for d, it in enumerate(lat_items(0, False, count=ahead_chunks)[:LOOKAHEAD]):
            s_s[d] = scores(*it[:3])

        def body(g, carry):
            c0 = g * LAT_UNROLL
            after = lat_items(c0 + LAT_UNROLL, True, count=ahead_chunks)[:LOOKAHEAD]
            queue = run(lat_items(c0, True), [s_s[d] for d in range(LOOKAHEAD)], after)
            for d in range(LOOKAHEAD):
                s_s[d] = queue[d]
            return carry

        lax.fori_loop(0, n_groups - 1, body, 0)
        run(lat_items((n_groups - 1) * LAT_UNROLL, False) + ctx_items, [s_s[d] for d in range(LOOKAHEAD)], [])
    else:
        run(ctx_items, [scores(*it[:3]) for it in ctx_items[:LOOKAHEAD]], [])
    outs = []
    for a in range(2):
        acc = acc_s[a]
        outs.append(acc[0:MLA_V] / acc[MLA_V:MLA_V + 1])
    o_ref[...] = jnp.concatenate(outs, axis=0).T.astype(o_ref.dtype)


def _attention(q, k, vt, *, bsz, n_q, q_base, n_lat, n_ctx, ctx_base, tq, tkc, with_lat):
    pairs = MLA_HEADS // 2
    qblocks = n_q // tq
    qb0 = q_base // tq
    cb0 = ctx_base // n_ctx
    kw = 2 * HEAD_LANES
    vw = 2 * MLA_V
    in_specs = [pl.BlockSpec((tq, kw), lambda b, p, i: (qb0 + b * qblocks + i, p))]
    args = [q]
    if with_lat:
        in_specs += [pl.BlockSpec((n_lat, kw), lambda b, p, i: (b, p)),
                     pl.BlockSpec((n_lat // tkc, vw, tkc), lambda b, p, i: (b, p, 0))]
        args += [k, vt]
    in_specs += [pl.BlockSpec((n_ctx, kw), lambda b, p, i: (cb0 + b, p)),
                 pl.BlockSpec((n_ctx // tkc, vw, tkc), lambda b, p, i: (cb0 + b, p, 0))]
    args += [k, vt]
    kern = functools.partial(_attn_kernel, tkc=tkc, n_lat_chunks=(n_lat // tkc) if with_lat else 0,
                             n_ctx_chunks=n_ctx // tkc)
    return pl.pallas_call(
        kern,
        grid=(bsz, pairs, qblocks),
        in_specs=in_specs,
        out_specs=pl.BlockSpec((tq, vw), lambda b, p, i: (b * qblocks + i, p)),
        out_shape=jax.ShapeDtypeStruct((bsz * n_q, MLA_HEADS * MLA_V), BF16),
        scratch_shapes=[pltpu.VMEM((2, 1, tq), F32), pltpu.VMEM((2, ACC_ROWS, tq), F32),
                        pltpu.VMEM((LOOKAHEAD, tkc, tq), F32)],
        name="attn_lat" if with_lat else "attn_ctx",
        compiler_params=_cparams(("parallel", "parallel", "arbitrary")),
    )(*args)


def _outproj_kernel(a_ref, of_ref, ob_ref, hg_ref, c_ref, x_ref, g1_ref, sh2_ref, sc2_ref,
                    lng_ref, lnb_ref, ng_ref, gavg_ref, w_ref, wr_ref, x1_out, h2_out, lg_out, *, alpha):
    o = of_ref[...] + ob_ref[...]
    o2_hi, o2_lo = _split_bf16(o * o)
    ms = _dot(o2_hi, gavg_ref[...]) + _dot(o2_lo, gavg_ref[...])
    b_lat = o * lax.rsqrt(ms + EPS) * ng_ref[...] * _silu(hg_ref[...])
    m = (_dot(a_ref[...].astype(BF16), w_ref[0:POOL_DIM, :])
         + _dot(b_lat.astype(BF16), w_ref[POOL_DIM:POOL_DIM + HG_V, :])
         + _dot(c_ref[...], w_ref[POOL_DIM + HG_V:, :]))
    x1 = _layer_norm(alpha * x_ref[...] + g1_ref[0] * m) * lng_ref[...] + lnb_ref[...]
    x1_out[...] = x1
    h2 = _layer_norm(x1) * (1.0 + sc2_ref[0]) + sh2_ref[0]
    h2_out[...] = h2
    lg_out[...] = _dot3(h2, wr_ref[...])


def _out_proj(a, o_f, o_b, hg, c_att, x, g1, sh2, sc2, ln_g, ln_b, norm_g, gavg, w_out, w_router, tm,
              mod_index, alpha):
    n, d = x.shape
    row = lambda i: (i, 0)
    full = lambda i: (0, 0)
    mod = lambda i: (mod_index(i), 0, 0)
    return pl.pallas_call(
        functools.partial(_outproj_kernel, alpha=alpha),
        grid=(n // tm,),
        in_specs=[pl.BlockSpec((tm, POOL_DIM), row), pl.BlockSpec((tm, HG_V), row), pl.BlockSpec((tm, HG_V), row),
                  pl.BlockSpec((tm, HG_V), row), pl.BlockSpec((tm, MLA_HEADS * MLA_V), row),
                  pl.BlockSpec((tm, d), row),
                  pl.BlockSpec((1, 1, d), mod), pl.BlockSpec((1, 1, d), mod), pl.BlockSpec((1, 1, d), mod),
                  pl.BlockSpec((1, d), full), pl.BlockSpec((1, d), full), pl.BlockSpec((1, HG_V), full),
                  pl.BlockSpec((HG_V, HG_V), full), pl.BlockSpec((d, d), full),
                  pl.BlockSpec((d, ROUTER_LANES), full)],
        out_specs=[pl.BlockSpec((tm, d), row), pl.BlockSpec((tm, d), row), pl.BlockSpec((tm, ROUTER_LANES), row)],
        out_shape=[jax.ShapeDtypeStruct((n, d), F32), jax.ShapeDtypeStruct((n, d), F32),
                   jax.ShapeDtypeStruct((n, ROUTER_LANES), F32)],
        name="out_proj",
        compiler_params=_cparams(("parallel",)),
    )(a, o_f, o_b, hg, c_att, x, g1, sh2, sc2, ln_g, ln_b, norm_g, gavg, w_out, w_router)


def _moe_kernel(be_ref, idx_cur, idx_next, h_hbm, w1_ref, w3_ref, w2_ref, y_ref,
                xbuf, sem, w1b, w3b, w2b, *, tm):
    i = pl.program_id(0)
    last = pl.num_programs(0) - 1
    slot = i % 2
    rows_per_phase = tm // MOE_GATHER_PHASES

    def row_copy(idx_ref, dst_slot, r):
        return pltpu.make_async_copy(h_hbm.at[pl.ds(idx_ref[0, 0, r], 1)], xbuf.at[dst_slot, pl.ds(r, 1)],
                                     sem.at[dst_slot])

    def wait_slot(s):
        pltpu.make_async_copy(h_hbm.at[pl.ds(0, tm)], xbuf.at[s], sem.at[s]).wait()

    def prefetch(phase):
        for r in range(phase * rows_per_phase, (phase + 1) * rows_per_phase):
            row_copy(idx_next, 1 - slot, r).start()

    @pl.when(i == 0)
    def _():
        def row(r, carry):
            row_copy(idx_cur, 0, r).start()
            return carry

        lax.fori_loop(0, tm, row, 0, unroll=8)

    @pl.when((i == 0) | (be_ref[i] != be_ref[jnp.maximum(i - 1, 0)]))
    def _():
        w1b[...] = w1_ref[0, 0].astype(BF16)
        w3b[...] = w3_ref[0, 0].astype(BF16)
        w2b[...] = w2_ref[0, 0].astype(BF16)

    wait_slot(slot)
    x = xbuf[slot].astype(BF16)
    prefetch(0)
    h1 = _dot(x, w1b[...])
    prefetch(1)
    h3 = _dot(x, w3b[...])
    prefetch(2)
    h = (_silu(h1) * h3).astype(BF16)
    y = _dot(h, w2b[...])
    prefetch(3)
    y_ref[...] = y.astype(y_ref.dtype)

    @pl.when(i == last)
    def _():
        wait_slot(1 - slot)


def _moe_ffn(block_e, row_tok, h, w1, w3, w2, layer, tm):
    n_blocks = row_tok.shape[0]
    d = h.shape[1]
    last = n_blocks - 1
    grid_spec = pltpu.PrefetchScalarGridSpec(
        num_scalar_prefetch=1,
        grid=(n_blocks,),
        in_specs=[pl.BlockSpec((1, 1, tm), lambda i, be: (i, 0, 0), memory_space=pltpu.SMEM),
                  pl.BlockSpec((1, 1, tm), lambda i, be: (jnp.minimum(i + 1, last), 0, 0),
                               memory_space=pltpu.SMEM),
                  pl.BlockSpec(memory_space=pl.ANY),
                  pl.BlockSpec((1, 1, d, D_EXPERT), lambda i, be: (layer, be[i], 0, 0)),
                  pl.BlockSpec((1, 1, d, D_EXPERT), lambda i, be: (layer, be[i], 0, 0)),
                  pl.BlockSpec((1, 1, D_EXPERT, d), lambda i, be: (layer, be[i], 0, 0))],
        out_specs=pl.BlockSpec((tm, d), lambda i, be: (i, 0)),
        scratch_shapes=[pltpu.VMEM((2, tm, d), F32), pltpu.SemaphoreType.DMA((2,)),
                        pltpu.VMEM((d, D_EXPERT), BF16), pltpu.VMEM((d, D_EXPERT), BF16),
                        pltpu.VMEM((D_EXPERT, d), BF16)],
    )
    return pl.pallas_call(
        functools.partial(_moe_kernel, tm=tm),
        grid_spec=grid_spec,
        out_shape=jax.ShapeDtypeStruct((n_blocks * tm, d), BF16),
        name="moe_ffn",
        compiler_params=_cparams(("arbitrary",)),
    )(block_e, row_tok, row_tok, h, w1, w3, w2)


def _combine_kernel(x_ref, y0_ref, y1_ref, gw_ref, g2_ref, lng_ref, lnb_ref, o_ref, *, alpha):
    gw = gw_ref[...]
    y = gw[:, 0:1] * y0_ref[...].astype(F32) + gw[:, 1:2] * y1_ref[...].astype(F32)
    o_ref[...] = _layer_norm(alpha * x_ref[...] + g2_ref[0] * y) * lng_ref[...] + lnb_ref[...]


def _combine(x, y0, y1, gw, g2, ln_g, ln_b, tm, mod_index, alpha):
    n, d = x.shape
    row = lambda i: (i, 0)
    full = lambda i: (0, 0)
    return pl.pallas_call(
        functools.partial(_combine_kernel, alpha=alpha),
        grid=(n // tm,),
        in_specs=[pl.BlockSpec((tm, d), row), pl.BlockSpec((tm, d), row), pl.BlockSpec((tm, d), row),
                  pl.BlockSpec((tm, ROUTER_LANES), row),
                  pl.BlockSpec((1, 1, d), lambda i: (mod_index(i), 0, 0)),
                  pl.BlockSpec((1, d), full), pl.BlockSpec((1, d), full)],
        out_specs=pl.BlockSpec((tm, d), row),
        out_shape=jax.ShapeDtypeStruct((n, d), F32),
        name="moe_combine",
        compiler_params=_cparams(("parallel",)),
    )(x, y0, y1, gw, g2, ln_g, ln_b)


def _rot_half_cols(w):
    lead = w.shape[:-1]
    wr = w.reshape(lead + (2, 2, MLA_ROPE // 4))
    return jnp.stack([-wr[..., 1, :], wr[..., 0, :]], axis=-2).reshape(w.shape)


def _rope_tables(n_lat, n_ctx, bsz):
    rows = n_lat // GRID_W
    row = jnp.repeat(jnp.arange(rows, dtype=F32), GRID_W)
    col = jnp.tile(jnp.arange(GRID_W, dtype=F32), rows)
    n_freq = MLA_ROPE // 4
    inv_freq = ROPE_BASE ** (-jnp.arange(n_freq, dtype=F32) / n_freq)
    ang = jnp.stack([row[:, None] * inv_freq, col[:, None] * inv_freq], axis=1)
    cos = jnp.broadcast_to(jnp.cos(ang)[:, :, None, :], (n_lat, 2, 2, n_freq)).reshape(n_lat, MLA_ROPE)
    sin = jnp.broadcast_to(jnp.sin(ang)[:, :, None, :], (n_lat, 2, 2, n_freq)).reshape(n_lat, MLA_ROPE)
    pad_l = MLA_NOPE
    pad_r = HEAD_LANES - MLA_QK
    cos_t = jnp.concatenate([jnp.ones((n_lat, pad_l), F32), cos, jnp.ones((n_lat, pad_r), F32)], axis=1)
    sin_t = jnp.concatenate([jnp.zeros((n_lat, pad_l), F32), sin, jnp.zeros((n_lat, pad_r), F32)], axis=1)
    cos_all = jnp.concatenate([jnp.tile(cos_t, (bsz, 1)), jnp.ones((bsz * n_ctx, HEAD_LANES), F32)], axis=0)
    sin_all = jnp.concatenate([jnp.tile(sin_t, (bsz, 1)), jnp.zeros((bsz * n_ctx, HEAD_LANES), F32)], axis=0)
    return cos_all, sin_all


def _prep_w_in(w_in):
    depth, d, _ = w_in.shape
    kr = w_in[:, :, W_IN_MAIN:]
    z_l = jnp.zeros((depth, d, MLA_NOPE), w_in.dtype)
    z_r = jnp.zeros((depth, d, HEAD_LANES - MLA_QK), w_in.dtype)
    return jnp.concatenate([w_in[:, :, :W_IN_MAIN], z_l, kr, z_r, z_l, _rot_half_cols(kr), z_r],
                           axis=-1).astype(BF16)


def _prep_mla(w_uq, w_ukv):
    depth = w_uq.shape[0]
    wq = w_uq.reshape(depth, MLA_Q_LORA, MLA_HEADS, MLA_QK)
    pad = jnp.zeros((depth, MLA_Q_LORA, MLA_HEADS, HEAD_LANES - MLA_QK), w_uq.dtype)
    wq_p = jnp.concatenate([wq, pad], axis=-1).reshape(depth, MLA_Q_LORA, MLA_HEADS * HEAD_LANES)
    wq_s = jnp.concatenate([jnp.zeros_like(wq[..., :MLA_NOPE]), _rot_half_cols(wq[..., MLA_NOPE:]), pad],
                           axis=-1).reshape(depth, MLA_Q_LORA, MLA_HEADS * HEAD_LANES)
    wkv = w_ukv.reshape(depth, MLA_KV_LORA, MLA_HEADS, MLA_NOPE + MLA_V)
    kpad = jnp.zeros((depth, MLA_KV_LORA, MLA_HEADS, HEAD_LANES - MLA_NOPE), w_ukv.dtype)
    wk = jnp.concatenate([wkv[..., :MLA_NOPE], kpad], axis=-1).reshape(depth, MLA_KV_LORA, MLA_HEADS * HEAD_LANES)
    wvt = jnp.swapaxes(wkv[..., MLA_NOPE:].reshape(depth, MLA_KV_LORA, MLA_HEADS * MLA_V), 1, 2)
    return wq_p.astype(BF16), wq_s.astype(BF16), wk.astype(BF16), wvt.astype(BF16)


def _block_diag(pool_w):
    depth, g, c, _ = pool_w.shape
    eye = jnp.eye(g, dtype=pool_w.dtype)
    return jnp.einsum("lgcd,gh->lgchd", pool_w, eye).reshape(depth, g * c, g * c).astype(BF16)


def _route(logits, b_rg, b_re, tm_moe):
    n = logits.shape[0]
    lg = logits[:, :N_GROUPS] + b_rg
    le = (logits[:, N_GROUPS:N_GROUPS + N_EXPERTS] + b_re).reshape(n, N_GROUPS, EXPERTS_PER_GROUP)
    pg = jax.nn.softmax(lg, axis=-1)
    g_sel = jnp.argmax(lg, axis=-1).astype(jnp.int32)
    g_hot = g_sel[:, None] == jnp.arange(N_GROUPS, dtype=jnp.int32)[None, :]
    w_g = jnp.sum(jnp.where(g_hot, pg, 0.0), axis=-1)
    le_sel = jnp.sum(jnp.where(g_hot[:, :, None], le, 0.0), axis=1)
    top_v, top_i = lax.top_k(le_sel, TOP_K)
    gate = jax.nn.softmax(top_v, axis=-1) * w_g[:, None]
    expert = (g_sel[:, None] * EXPERTS_PER_GROUP + top_i).astype(jnp.int32)

    n_assign = n * TOP_K
    e_flat = expert.reshape(-1)
    hot = e_flat[:, None] == jnp.arange(N_EXPERTS, dtype=jnp.int32)[None, :]
    nb = n_assign // COUNT_BLOCK
    tri = jnp.tril(jnp.ones((COUNT_BLOCK, COUNT_BLOCK), BF16))
    win = jnp.einsum("ts,bse->bte", tri, hot.astype(BF16).reshape(nb, COUNT_BLOCK, N_EXPERTS),
                     preferred_element_type=F32)
    bsum = win[:, -1, :]
    boff = jnp.cumsum(bsum, axis=0) - bsum
    csum = (win + boff[:, None, :]).reshape(n_assign, N_EXPERTS).astype(jnp.int32)
    counts = (boff[-1] + bsum[-1]).astype(jnp.int32)
    pcounts = (counts + tm_moe - 1) // tm_moe * tm_moe
    pends = jnp.cumsum(pcounts)
    pstarts = pends - pcounts
    starts = jnp.cumsum(counts) - counts
    dest = jnp.sum(jnp.where(hot, csum - 1 + pstarts[None, :], 0), axis=1)
    n_rows = (n_assign + N_EXPERTS * (tm_moe - 1) + tm_moe - 1) // tm_moe * tm_moe
    n_blocks = n_rows // tm_moe
    n_valid = (pends[-1] // tm_moe).astype(jnp.int32)
    blk = jnp.arange(n_blocks, dtype=jnp.int32)
    first_row = jnp.minimum(blk, n_valid - 1) * tm_moe
    block_e = jnp.sum((pends[None, :] <= first_row[:, None]).astype(jnp.int32), axis=1)
    block_e = jnp.clip(block_e, 0, N_EXPERTS - 1)
    a_sorted = jnp.sort(e_flat * n_assign + jnp.arange(n_assign, dtype=jnp.int32)) % n_assign
    pos = (blk * tm_moe - pstarts[block_e])[:, None] + jnp.arange(tm_moe, dtype=jnp.int32)[None, :]
    src = jnp.clip(starts[block_e][:, None] + pos, 0, n_assign - 1)
    row_tok = jnp.where(pos < counts[block_e][:, None], a_sorted[src] // TOP_K, 0)
    gw = jnp.zeros((n, ROUTER_LANES), F32).at[:, :TOP_K].set(gate)
    return row_tok.reshape(n_blocks, 1, tm_moe), dest.reshape(n, TOP_K), gw, block_e


def _tile(n, pref):
    t = pref
    while n % t:
        t //= 2
    return t


def kernel(x, c, ctx, c_ctx, w_mod, b_mod, w_in, pool_w, pool_scale, hg_lb_logits, hg_norm_g, mla_q_norm_g, mla_w_uq, mla_kv_norm_g, mla_w_ukv, w_out, ln1_g, ln1_b, ln2_g, ln2_b, router_group_w, router_group_b, router_expert_w, router_expert_b, expert_w1, expert_w3, expert_w2):
    bsz, n_lat, d = x.shape
    n_ctx = ctx.shape[1]
    depth = w_mod.shape[0]
    alpha = (2 * depth) ** 0.25
    lat_rows = bsz * n_lat
    ctx_rows = bsz * n_ctx
    n_tok = lat_rows + ctx_rows

    tm = _tile(np.gcd(n_lat, ctx_rows), 512)
    tp = _tile(np.gcd(n_lat, n_ctx), 256)
    tb_lat = _tile(n_lat, 512)
    tb_ctx = _tile(n_ctx, 512)
    tq_lat = _tile(n_lat, 512)
    tkc = _tile(np.gcd(n_lat, n_ctx), 256)
    tq_ctx = _tile(n_ctx, 512)
    tm_moe = 256
    lat_tiles_per_batch = n_lat // tm
    mod_index = lambda i: jnp.minimum(i // lat_tiles_per_batch, bsz)

    c_rows = jnp.concatenate([jax.nn.silu(c), jax.nn.silu(c_ctx)[None, :],
                              jnp.zeros((-(bsz + 1) % 8, d), F32)], axis=0)
    mod_all = _mod_all(c_rows, w_mod, b_mod)
    lb_all = jnp.cumsum(jax.nn.softmax(hg_lb_logits.astype(F32), axis=0), axis=0)
    lb_all = lb_all - lb_all[0:1]
    log_lb = jnp.log(lb_all)
    log_1m_lb = jnp.log1p(-lb_all)
    w_aug = _prep_w_in(w_in)
    wq_p, wq_s, wk_p, wvt_p = _prep_mla(mla_w_uq, mla_w_ukv)
    w_bd = _block_diag(pool_w)
    w_out_b = w_out.astype(BF16)
    w_router = jnp.concatenate([router_group_w, router_expert_w,
                                jnp.zeros((depth, d, ROUTER_LANES - N_GROUPS - N_EXPERTS), F32)], axis=-1)
    gavg = jnp.kron(jnp.eye(HG_HEADS, dtype=F32), jnp.full((HG_DV, HG_DV), 1.0 / HG_DV, F32)).astype(BF16)
    cos_t, sin_t = _rope_tables(n_lat, n_ctx, bsz)
    zero_state = jnp.zeros((bsz, HG_HEADS, HG_DV, HG_DK), F32)

    xt = jnp.concatenate([x.reshape(lat_rows, d), ctx.reshape(ctx_rows, d)], axis=0)
    for l in range(depth):
        mod = mod_all[l, :bsz + 1].reshape(bsz + 1, 6, 1, d)
        sh1, sc1, g1, sh2, sc2, g2 = (mod[:, k] for k in range(6))

        p_in, hq, hzf, hzb, hin, hg, cq, ckv, kr, krs = _in_proj(xt, sh1, sc1, w_aug[l], tm, mod_index)

        a_mix = _pool(p_in, w_bd[l], pool_scale[l][None, :], tp, lat_rows, n_lat, n_ctx)

        la_f, la_b = log_lb[l, 0][None, :], log_lb[l, 1][None, :]
        lc_f, lc_b = log_1m_lb[l, 0][None, :], log_1m_lb[l, 1][None, :]
        oc_f, s_f = _hgrn_scan(hq, hzf, hin, la_f, lc_f, zero_state, reverse=False, base_row=lat_rows,
                               n_seq=n_ctx, tb=tb_ctx)
        oc_b, s_b = _hgrn_scan(hq, hzb, hin, la_b, lc_b, zero_state, reverse=True, base_row=lat_rows,
                               n_seq=n_ctx, tb=tb_ctx)
        ol_f, _ = _hgrn_scan(hq, hzf, hin, la_f, lc_f, s_f, reverse=False, base_row=0, n_seq=n_lat, tb=tb_lat)
        ol_b, _ = _hgrn_scan(hq, hzb, hin, la_b, lc_b, s_b, reverse=True, base_row=0, n_seq=n_lat, tb=tb_lat)
        o_f = jnp.concatenate([ol_f, oc_f], axis=0)
        o_b = jnp.concatenate([ol_b, oc_b], axis=0)

        q_all, k_all, vt_all = _mla_proj(cq, ckv, kr, krs, cos_t, sin_t, mla_q_norm_g[l][None, :],
                                         mla_kv_norm_g[l][None, :], wq_p[l], wq_s[l], wk_p[l], wvt_p[l], tkc)
        c_lat = _attention(q_all, k_all, vt_all, bsz=bsz, n_q=n_lat, q_base=0, n_lat=n_lat, n_ctx=n_ctx,
                           ctx_base=lat_rows, tq=tq_lat, tkc=tkc, with_lat=True)
        c_ctx_o = _attention(q_all, k_all, vt_all, bsz=bsz, n_q=n_ctx, q_base=lat_rows, n_lat=n_lat, n_ctx=n_ctx,
                             ctx_base=lat_rows, tq=tq_ctx, tkc=tkc, with_lat=False)
        c_att = jnp.concatenate([c_lat, c_ctx_o], axis=0)

        x1, h2, logits = _out_proj(a_mix, o_f, o_b, hg, c_att, xt, g1, sh2, sc2, ln1_g[l][None, :],
                                   ln1_b[l][None, :], hg_norm_g[l][None, :], gavg, w_out_b[l], w_router[l],
                                   tm, mod_index, alpha)

        row_tok, dest, gw, block_e = _route(logits, router_group_b[l], router_expert_b[l], tm_moe)
        y = _moe_ffn(block_e, row_tok, h2, expert_w1, expert_w3, expert_w2, l, tm_moe)
        y0 = jnp.take(y, dest[:, 0], axis=0)
        y1 = jnp.take(y, dest[:, 1], axis=0)
        xt = _combine(x1, y0, y1, gw, g2, ln2_g[l][None, :], ln2_b[l][None, :], tm, mod_index, alpha)
    return xt[:lat_rows].reshape(bsz, n_lat, d)
```

```python
import functools

import numpy as np
import jax
import jax.numpy as jnp
from jax import lax
from jax.experimental import pallas as pl
from jax.experimental.pallas import tpu as pltpu

F32 = jnp.float32
BF16 = jnp.bfloat16

D_MODEL = 1024
GRID_W = 64
POOL_WINDOWS = (2, 4, 8, 16)
POOL_DIM = 256
POOL_GROUP_DIM = 64
HG_HEADS = 4
HG_DK = 128
HG_DV = 64
HG_QK = HG_HEADS * HG_DK
HG_V = HG_HEADS * HG_DV
HG_CHUNK = 64
HG_SUB = 16
MLA_HEADS = 8
MLA_NOPE = 64
MLA_ROPE = 32
MLA_V = 64
MLA_Q_LORA = 256
MLA_KV_LORA = 128
MLA_QK = MLA_NOPE + MLA_ROPE
MLA_SCALE = MLA_QK ** -0.5
LOG2_E = 1.4426950408889634
ROPE_BASE = 10000.0
HEAD_LANES = 128
N_GROUPS = 4
EXPERTS_PER_GROUP = 8
N_EXPERTS = N_GROUPS * EXPERTS_PER_GROUP
TOP_K = 2
D_EXPERT = 512
EPS = 1e-6
ROUTER_LANES = 128
COUNT_BLOCK = 128
IN_SEGS = (("p_in", 256), ("hq", 512), ("hzf", 512), ("hzb", 512), ("hin", 256), ("hg", 256),
           ("cq", 256), ("ckv", 128), ("kr", 128), ("krs", 128))
IN_AUG = sum(w for _, w in IN_SEGS)
W_IN_MAIN = 2688

VMEM_LIMIT = 56 * 1024 * 1024


def _cparams(sem):
    return pltpu.CompilerParams(dimension_semantics=sem, vmem_limit_bytes=VMEM_LIMIT)


def _split_bf16(a):
    hi = a.astype(BF16)
    lo = (a - hi.astype(F32)).astype(BF16)
    return hi, lo


def _dot(a, b):
    return jnp.dot(a, b, preferred_element_type=F32)


def _dot_nt(a, b):
    return lax.dot_general(a, b, (((1,), (1,)), ((), ())), preferred_element_type=F32)


def _dot_tn(a, b):
    return lax.dot_general(a, b, (((0,), (0,)), ((), ())), preferred_element_type=F32)


def _dot3(a, b):
    a_hi, a_lo = _split_bf16(a)
    b_hi, b_lo = _split_bf16(b)
    return _dot(a_hi, b_hi) + _dot(a_hi, b_lo) + _dot(a_lo, b_hi)


def _dot_exact_rhs(a, b_bf16):
    a_hi = a.astype(BF16)
    r1 = a - a_hi.astype(F32)
    a_mid = r1.astype(BF16)
    a_lo = (r1 - a_mid.astype(F32)).astype(BF16)
    return _dot(a_hi, b_bf16) + _dot(a_mid, b_bf16) + _dot(a_lo, b_bf16)


def _layer_norm(x):
    mu = jnp.mean(x, axis=-1, keepdims=True)
    xc = x - mu
    var = jnp.mean(xc * xc, axis=-1, keepdims=True)
    return xc * lax.rsqrt(var + EPS)


def _sigmoid(x):
    return 1.0 / (1.0 + jnp.exp(-x))


def _silu(x):
    return x * _sigmoid(x)


def _mod_kernel(a_ref, w_ref, b_ref, o_ref):
    o_ref[0] = _dot3(a_ref[...], w_ref[0]) + b_ref[0]


def _mod_all(c_rows, w_mod, b_mod):
    depth, d, n6 = w_mod.shape
    tn = 1536
    rows = c_rows.shape[0]
    return pl.pallas_call(
        _mod_kernel,
        grid=(depth, n6 // tn),
        in_specs=[pl.BlockSpec((rows, d), lambda l, j: (0, 0)),
                  pl.BlockSpec((1, d, tn), lambda l, j: (l, 0, j)),
                  pl.BlockSpec((1, 1, tn), lambda l, j: (l, 0, j))],
        out_specs=pl.BlockSpec((1, rows, tn), lambda l, j: (l, 0, j)),
        out_shape=jax.ShapeDtypeStruct((depth, rows, n6), F32),
        name="adaln_mod",
        compiler_params=_cparams(("arbitrary", "arbitrary")),
    )(c_rows, w_mod, b_mod.reshape(depth, 1, n6))


def _inproj_kernel(x_ref, sh_ref, sc_ref, w_ref, *out_refs):
    h = _layer_norm(x_ref[...]) * (1.0 + sc_ref[0]) + sh_ref[0]
    hb = h.astype(BF16)
    start = 0
    for (_, width), o_ref in zip(IN_SEGS, out_refs):
        o_ref[...] = _dot(hb, w_ref[:, start:start + width])
        start += width


def _in_proj(x, sh, sc, w_aug, tm, mod_index):
    n, d = x.shape
    row = lambda i: (i, 0)
    mod = lambda i: (mod_index(i), 0, 0)
    return pl.pallas_call(
        _inproj_kernel,
        grid=(n // tm,),
        in_specs=[pl.BlockSpec((tm, d), row),
                  pl.BlockSpec((1, 1, d), mod),
                  pl.BlockSpec((1, 1, d), mod),
                  pl.BlockSpec((d, IN_AUG), lambda i: (0, 0))],
        out_specs=[pl.BlockSpec((tm, w), row) for _, w in IN_SEGS],
        out_shape=[jax.ShapeDtypeStruct((n, w), F32) for _, w in IN_SEGS],
        name="in_proj",
        compiler_params=_cparams(("parallel",)),
    )(x, sh, sc, w_aug)


POOL_HALO = 8


def _pool_kernel(x_ref, prev_ref, next_ref, w_ref, scale_ref, o_ref, *, tp, lat_tiles, tpb_lat, tpb_ctx,
                 n_lat, n_ctx):
    i = pl.program_id(0)
    is_lat = i < lat_tiles
    tile_in_seq = jnp.where(is_lat, i % tpb_lat, (i - lat_tiles) % tpb_ctx)
    tiles_in_seq = jnp.where(is_lat, tpb_lat, tpb_ctx)
    n_seq = jnp.where(is_lat, n_lat, n_ctx)
    first = tile_in_seq == 0
    last = tile_in_seq == tiles_in_seq - 1
    x = x_ref[...]
    prev = jnp.where(first, 0.0, prev_ref[...])
    nxt = jnp.where(last, 0.0, next_ref[...])
    ext = jnp.concatenate([prev, x, nxt], axis=0)
    n_ext = tp + 2 * POOL_HALO

    def back(a, k):
        return pltpu.roll(a, k, 0)

    def fwd(a, k):
        return pltpu.roll(a, n_ext - k, 0)

    e2 = ext + back(ext, 1)
    e4 = back(e2, 1) + fwd(e2, 1)
    e8 = back(e4, 2) + fwd(e4, 2)
    e16 = back(e8, 4) + fwd(e8, 4)
    sl = slice(POOL_HALO, POOL_HALO + tp)
    lane = lax.broadcasted_iota(jnp.int32, (tp, POOL_DIM), 1)
    grp = lane >> 6
    wsum = jnp.where(grp == 0, e2[sl], jnp.where(grp == 1, e4[sl], jnp.where(grp == 2, e8[sl], e16[sl])))
    half = jnp.where(grp == 0, 1, jnp.where(grp == 1, 2, jnp.where(grp == 2, 4, 8)))
    t = tile_in_seq * tp + lax.broadcasted_iota(jnp.int32, (tp, POOL_DIM), 0)
    cnt = jnp.minimum(t + half, n_seq) - jnp.maximum(t - half, 0)
    pooled = wsum / cnt.astype(F32) - x
    o_ref[...] = _dot(pooled.astype(BF16), w_ref[...]) * scale_ref[...]


def _pool(p_in, w_bd, scale, tp, n_lat_rows, n_lat, n_ctx):
    n, c = p_in.shape
    hb = tp // POOL_HALO
    last_halo_block = n // POOL_HALO - 1
    kern = functools.partial(_pool_kernel, tp=tp, lat_tiles=n_lat_rows // tp, tpb_lat=n_lat // tp,
                             tpb_ctx=n_ctx // tp, n_lat=n_lat, n_ctx=n_ctx)
    return pl.pallas_call(
        kern,
        grid=(n // tp,),
        in_specs=[pl.BlockSpec((tp, c), lambda i: (i, 0)),
                  pl.BlockSpec((POOL_HALO, c), lambda i: (jnp.maximum(i * hb - 1, 0), 0)),
                  pl.BlockSpec((POOL_HALO, c), lambda i: (jnp.minimum((i + 1) * hb, last_halo_block), 0)),
                  pl.BlockSpec((c, c), lambda i: (0, 0)),
                  pl.BlockSpec((1, c), lambda i: (0, 0))],
        out_specs=pl.BlockSpec((tp, c), lambda i: (i, 0)),
        out_shape=jax.ShapeDtypeStruct((n, c), F32),
        name="pool",
        compiler_params=_cparams(("parallel",)),
    )(p_in, p_in, p_in, w_bd, scale)


def _hgrn_kernel(hq_ref, hz_ref, hin_ref, la_ref, lc_ref, s0_ref, o_ref, sT_ref, q_s, g_s, k_s, st_s,
                 *, reverse, nchunk):
    j = pl.program_id(1)

    @pl.when(j == 0)
    def _():
        st_s[...] = s0_ref[0]

    z = hz_ref[...]
    ls = jnp.minimum(z, 0.0) - jnp.log1p(jnp.exp(-jnp.abs(z)))
    lc = lc_ref[...]
    x2 = lc + ls
    la = la_ref[...]
    g_s[...] = jnp.maximum(la, x2) + jnp.log1p(jnp.exp(-jnp.abs(la - x2)))
    k_s[...] = jnp.exp(x2 - z)
    q_s[...] = _silu(hq_ref[...])

    C = HG_CHUNK
    row = lax.broadcasted_iota(jnp.int32, (C, C), 0)
    col = lax.broadcasted_iota(jnp.int32, (C, C), 1)
    tri = ((row <= col) if reverse else (row >= col)).astype(BF16)
    rmod = lax.broadcasted_iota(jnp.int32, (C, 1), 0) & (HG_SUB - 1)
    nsub = C // HG_SUB
    row_sub = row >> 4
    col_sub = col >> 4

    def chunk(c, carry):
        cc = (nchunk - 1 - c) if reverse else c
        r0 = pl.multiple_of(cc * C, C)
        b_all = _dot_exact_rhs_lhs(tri, g_s[pl.ds(r0, C), :])
        q_all = q_s[pl.ds(r0, C), :]
        k_all = k_s[pl.ds(r0, C), :]
        v_all = hin_ref[pl.ds(r0, C), :]
        outs = []
        for h in range(HG_HEADS):
            ks = slice(h * HG_DK, (h + 1) * HG_DK)
            b, q, k = b_all[:, ks], q_all[:, ks], k_all[:, ks]
            v = v_all[:, h * HG_DV:(h + 1) * HG_DV]
            st = st_s[h]
            b_last = b[0:1] if reverse else b[C - 1:C]
            o = _dot_nt((q * jnp.exp(b)).astype(BF16), st.astype(BF16))
            a_rows = []
            for i in range(nsub):
                rs = slice(i * HG_SUB, (i + 1) * HG_SUB)
                if (reverse and i == nsub - 1) or (not reverse and i == 0):
                    a_rows.append(jnp.zeros((HG_SUB, C), F32))
                    continue
                m = b[(i + 1) * HG_SUB - 1:(i + 1) * HG_SUB] if reverse else b[i * HG_SUB:i * HG_SUB + 1]
                qi = q[rs] * jnp.exp(b[rs] - m)
                ksc = k * jnp.exp(m - b)
                a_rows.append(_dot_nt(qi.astype(BF16), ksc.astype(BF16)))
            a = jnp.concatenate(a_rows, axis=0)
            if reverse:
                a = jnp.where(col_sub > row_sub, a, 0.0)
            else:
                a = jnp.where(col_sub < row_sub, a, 0.0)
            o = o + _dot(a.astype(BF16), v.astype(BF16))
            for delta in range(HG_SUB):
                if delta == 0:
                    w = jnp.sum(q * k, axis=-1, keepdims=True)
                    o = o + w * v
                    continue
                shift = (C - delta) if reverse else delta
                kd = pltpu.roll(k, shift, 0)
                bd = pltpu.roll(b, shift, 0)
                vd = pltpu.roll(v, shift, 0)
                e = jnp.exp(b - bd)
                w = jnp.sum(q * kd * e, axis=-1, keepdims=True)
                valid = (rmod + delta < HG_SUB) if reverse else (rmod >= delta)
                o = o + jnp.where(valid, w, 0.0) * vd
            outs.append(o)
            kdec = k * jnp.exp(b_last - b)
            st_s[h] = st * jnp.exp(b_last) + _dot_tn(v.astype(BF16), kdec.astype(BF16))
        o_ref[pl.ds(r0, C), :] = jnp.concatenate(outs, axis=1)
        return carry

    lax.fori_loop(0, nchunk, chunk, 0)

    @pl.when(j == pl.num_programs(1) - 1)
    def _():
        sT_ref[0] = st_s[...]


def _dot_exact_rhs_lhs(tri_bf16, g):
    g_hi = g.astype(BF16)
    r1 = g - g_hi.astype(F32)
    g_mid = r1.astype(BF16)
    g_lo = (r1 - g_mid.astype(F32)).astype(BF16)
    return _dot(tri_bf16, g_hi) + _dot(tri_bf16, g_mid) + _dot(tri_bf16, g_lo)


def _hgrn_scan(hq, hz, hin, la, lc, s0, *, reverse, base_row, n_seq, tb):
    bsz = s0.shape[0]
    nblk = n_seq // tb
    base_blk = base_row // tb

    def rows(b, j):
        jj = (nblk - 1 - j) if reverse else j
        return (base_blk + b * nblk + jj, 0)

    def orow(b, j):
        jj = (nblk - 1 - j) if reverse else j
        return (b * nblk + jj, 0)

    kern = functools.partial(_hgrn_kernel, reverse=reverse, nchunk=tb // HG_CHUNK)
    return pl.pallas_call(
        kern,
        grid=(bsz, nblk),
        in_specs=[pl.BlockSpec((tb, HG_QK), rows),
                  pl.BlockSpec((tb, HG_QK), rows),
                  pl.BlockSpec((tb, HG_V), rows),
                  pl.BlockSpec((1, HG_QK), lambda b, j: (0, 0)),
                  pl.BlockSpec((1, HG_QK), lambda b, j: (0, 0)),
                  pl.BlockSpec((1, HG_HEADS, HG_DV, HG_DK), lambda b, j: (b, 0, 0, 0))],
        out_specs=[pl.BlockSpec((tb, HG_V), orow),
                   pl.BlockSpec((1, HG_HEADS, HG_DV, HG_DK), lambda b, j: (b, 0, 0, 0))],
        out_shape=[jax.ShapeDtypeStruct((bsz * n_seq, HG_V), F32),
                   jax.ShapeDtypeStruct((bsz, HG_HEADS, HG_DV, HG_DK), F32)],
        scratch_shapes=[pltpu.VMEM((tb, HG_QK), F32), pltpu.VMEM((tb, HG_QK), F32),
                        pltpu.VMEM((tb, HG_QK), F32), pltpu.VMEM((HG_HEADS, HG_DV, HG_DK), F32)],
        name="hgrn_bwd" if reverse else "hgrn_fwd",
        compiler_params=_cparams(("parallel", "arbitrary")),
    )(hq, hz, hin, la, lc, s0)


def _rms(x, g):
    return x * lax.rsqrt(jnp.mean(x * x, axis=-1, keepdims=True) + EPS) * g


def _mla_proj_kernel(cq_ref, ckv_ref, kr_ref, krs_ref, cos_ref, sin_ref, qg_ref, kvg_ref,
                     wq_ref, wqs_ref, wk_ref, wvt_ref, q_out, k_out, vt_out):
    cos = cos_ref[...]
    sin = sin_ref[...]
    cos_h = jnp.concatenate([cos] * MLA_HEADS, axis=1)
    sin_h = jnp.concatenate([sin] * MLA_HEADS, axis=1)
    xq = _rms(cq_ref[...], qg_ref[...]).astype(BF16)
    q = _dot(xq, wq_ref[...]) * cos_h + _dot(xq, wqs_ref[...]) * sin_h
    q_out[...] = (q * (MLA_SCALE * LOG2_E)).astype(BF16)
    xkv = _rms(ckv_ref[...], kvg_ref[...]).astype(BF16)
    k_rope = kr_ref[...] * cos + krs_ref[...] * sin
    k = _dot(xkv, wk_ref[...]) + jnp.concatenate([k_rope] * MLA_HEADS, axis=1)
    k_out[...] = k.astype(BF16)
    vt_out[0] = _dot_nt(wvt_ref[...], xkv).astype(BF16)


def _mla_proj(cq, ckv, kr, krs, cos_t, sin_t, qg, kvg, wq, wqs, wk, wvt, tm):
    n = cq.shape[0]
    row = lambda i: (i, 0)
    full = lambda i: (0, 0)
    hq = MLA_HEADS * HEAD_LANES
    hv = MLA_HEADS * MLA_V
    return pl.pallas_call(
        _mla_proj_kernel,
        grid=(n // tm,),
        in_specs=[pl.BlockSpec((tm, MLA_Q_LORA), row), pl.BlockSpec((tm, MLA_KV_LORA), row),
                  pl.BlockSpec((tm, HEAD_LANES), row), pl.BlockSpec((tm, HEAD_LANES), row),
                  pl.BlockSpec((tm, HEAD_LANES), row), pl.BlockSpec((tm, HEAD_LANES), row),
                  pl.BlockSpec((1, MLA_Q_LORA), full), pl.BlockSpec((1, MLA_KV_LORA), full),
                  pl.BlockSpec((MLA_Q_LORA, hq), full), pl.BlockSpec((MLA_Q_LORA, hq), full),
                  pl.BlockSpec((MLA_KV_LORA, hq), full), pl.BlockSpec((hv, MLA_KV_LORA), full)],
        out_specs=[pl.BlockSpec((tm, hq), row), pl.BlockSpec((tm, hq), row),
                   pl.BlockSpec((1, hv, tm), lambda i: (i, 0, 0))],
        out_shape=[jax.ShapeDtypeStruct((n, hq), BF16), jax.ShapeDtypeStruct((n, hq), BF16),
                   jax.ShapeDtypeStruct((n // tm, hv, tm), BF16)],
        name="mla_proj",
        compiler_params=_cparams(("parallel",)),
    )(cq, ckv, kr, krs, cos_t, sin_t, qg, kvg, wq, wqs, wk, wvt)


ACC_ROWS = MLA_V + 16
LAT_UNROLL = 4
LOOKAHEAD = 2


def _attn_kernel(*refs, tkc, n_lat_chunks, n_ctx_chunks):
    if n_lat_chunks:
        q_ref, kl_ref, vl_ref, kc_ref, vc_ref, o_ref, m_s, acc_s, s_s = refs
    else:
        q_ref, kc_ref, vc_ref, o_ref, m_s, acc_s, s_s = refs
    m_s[...] = jnp.full(m_s.shape, -jnp.inf, F32)
    acc_s[...] = jnp.zeros(acc_s.shape, F32)
    ones = jnp.ones((ACC_ROWS - MLA_V, tkc), BF16)

    def scores(a, k_ref, r0):
        hs = slice(a * HEAD_LANES, (a + 1) * HEAD_LANES)
        return _dot_nt(k_ref[pl.ds(r0, tkc), hs], q_ref[:, hs])

    def accumulate(a, s_t, vt):
        m_old = m_s[a]
        m_new = jnp.maximum(m_old, jnp.max(s_t, axis=0, keepdims=True))
        alpha = jnp.exp2(m_old - m_new)
        p_t = jnp.exp2(s_t - m_new).astype(BF16)
        vta = jnp.concatenate([vt[a * MLA_V:(a + 1) * MLA_V], ones], axis=0)
        acc_s[a] = alpha * acc_s[a] + _dot(vta, p_t)
        m_s[a] = m_new

    def run(items, queue, after):
        upcoming = items[LOOKAHEAD:] + after
        queue = list(queue)
        for idx, (a, _, _, vt) in enumerate(items):
            if idx < len(upcoming):
                queue.append(scores(*upcoming[idx][:3]))
            accumulate(a, queue.pop(0), vt)
        return queue

    ahead_chunks = (LOOKAHEAD + 1) // 2

    def lat_items(c0, dynamic, count=LAT_UNROLL):
        items = []
        for u in range(count):
            cc = c0 + u
            r0 = pl.multiple_of(cc * tkc, tkc) if dynamic else cc * tkc
            vt = vl_ref[cc] if count == LAT_UNROLL else None
            items += [(0, kl_ref, r0, vt), (1, kl_ref, r0, vt)]
        return items

    ctx_items = []
    for c in range(n_ctx_chunks):
        ctx_items += [(0, kc_ref, c * tkc, vc_ref[c]), (1, kc_ref, c * tkc, vc_ref[c])]

    if n_lat_chunks:
        n_groups = n_lat_chunks // LAT_UNROLL
        for d, it in enumerate(lat_items(0, False, count=ahead_chunks)[:LOOKAHEAD]):
            s_s[d] = scores(*it[:3])

        def body(g, carry):
            c0 = g * LAT_UNROLL
            after = lat_items(c0 + LAT_UNROLL, True, count=ahead_chunks)[:LOOKAHEAD]
            queue = run(lat_items(c0, True), [s_s[d] for d in range(LOOKAHEAD)], after)
            for d in range(LOOKAHEAD):
                s_s[d] = queue[d]
            return carry

        lax.fori_loop(0, n_groups - 1, body, 0)
        run(lat_items((n_groups - 1) * LAT_UNROLL, False) + ctx_items, [s_s[d] for d in range(LOOKAHEAD)], [])
    else:
        run(ctx_items, [scores(*it[:3]) for it in ctx_items[:LOOKAHEAD]], [])
    outs = []
    for a in range(2):
        acc = acc_s[a]
        outs.append(acc[0:MLA_V] / acc[MLA_V:MLA_V + 1])
    o_ref[...] = jnp.concatenate(outs, axis=0).T.astype(o_ref.dtype)


def _attention(q, k, vt, *, bsz, n_q, q_base, n_lat, n_ctx, ctx_base, tq, tkc, with_lat):
    pairs = MLA_HEADS // 2
    qblocks = n_q // tq
    qb0 = q_base // tq
    cb0 = ctx_base // n_ctx
    kw = 2 * HEAD_LANES
    vw = 2 * MLA_V
    in_specs = [pl.BlockSpec((tq, kw), lambda b, p, i: (qb0 + b * qblocks + i, p))]
    args = [q]
    if with_lat:
        in_specs += [pl.BlockSpec((n_lat, kw), lambda b, p, i: (b, p)),
                     pl.BlockSpec((n_lat // tkc, vw, tkc), lambda b, p, i: (b, p, 0))]
        args += [k, vt]
    in_specs += [pl.BlockSpec((n_ctx, kw), lambda b, p, i: (cb0 + b, p)),
                 pl.BlockSpec((n_ctx // tkc, vw, tkc), lambda b, p, i: (cb0 + b, p, 0))]
    args += [k, vt]
    kern = functools.partial(_attn_kernel, tkc=tkc, n_lat_chunks=(n_lat // tkc) if with_lat else 0,
                             n_ctx_chunks=n_ctx // tkc)
    return pl.pallas_call(
        kern,
        grid=(bsz, pairs, qblocks),
        in_specs=in_specs,
        out_specs=pl.BlockSpec((tq, vw), lambda b, p, i: (b * qblocks + i, p)),
        out_shape=jax.ShapeDtypeStruct((bsz * n_q, MLA_HEADS * MLA_V), BF16),
        scratch_shapes=[pltpu.VMEM((2, 1, tq), F32), pltpu.VMEM((2, ACC_ROWS, tq), F32),
                        pltpu.VMEM((LOOKAHEAD, tkc, tq), F32)],
        name="attn_lat" if with_lat else "attn_ctx",
        compiler_params=_cparams(("parallel", "parallel", "arbitrary")),
    )(*args)


def _outproj_kernel(a_ref, of_ref, ob_ref, hg_ref, c_ref, x_ref, g1_ref, sh2_ref, sc2_ref,
                    lng_ref, lnb_ref, ng_ref, gavg_ref, w_ref, wr_ref, x1_out, h2_out, lg_out, *, alpha):
    o = of_ref[...] + ob_ref[...]
    o2_hi, o2_lo = _split_bf16(o * o)
    ms = _dot(o2_hi, gavg_ref[...]) + _dot(o2_lo, gavg_ref[...])
    b_lat = o * lax.rsqrt(ms + EPS) * ng_ref[...] * _silu(hg_ref[...])
    m = (_dot(a_ref[...].astype(BF16), w_ref[0:POOL_DIM, :])
         + _dot(b_lat.astype(BF16), w_ref[POOL_DIM:POOL_DIM + HG_V, :])
         + _dot(c_ref[...], w_ref[POOL_DIM + HG_V:, :]))
    x1 = _layer_norm(alpha * x_ref[...] + g1_ref[0] * m) * lng_ref[...] + lnb_ref[...]
    x1_out[...] = x1
    h2 = _layer_norm(x1) * (1.0 + sc2_ref[0]) + sh2_ref[0]
    h2_out[...] = h2
    lg_out[...] = _dot3(h2, wr_ref[...])


def _out_proj(a, o_f, o_b, hg, c_att, x, g1, sh2, sc2, ln_g, ln_b, norm_g, gavg, w_out, w_router, tm,
              mod_index, alpha):
    n, d = x.shape
    row = lambda i: (i, 0)
    full = lambda i: (0, 0)
    mod = lambda i: (mod_index(i), 0, 0)
    return pl.pallas_call(
        functools.partial(_outproj_kernel, alpha=alpha),
        grid=(n // tm,),
        in_specs=[pl.BlockSpec((tm, POOL_DIM), row), pl.BlockSpec((tm, HG_V), row), pl.BlockSpec((tm, HG_V), row),
                  pl.BlockSpec((tm, HG_V), row), pl.BlockSpec((tm, MLA_HEADS * MLA_V), row),
                  pl.BlockSpec((tm, d), row),
                  pl.BlockSpec((1, 1, d), mod), pl.BlockSpec((1, 1, d), mod), pl.BlockSpec((1, 1, d), mod),
                  pl.BlockSpec((1, d), full), pl.BlockSpec((1, d), full), pl.BlockSpec((1, HG_V), full),
                  pl.BlockSpec((HG_V, HG_V), full), pl.BlockSpec((d, d), full),
                  pl.BlockSpec((d, ROUTER_LANES), full)],
        out_specs=[pl.BlockSpec((tm, d), row), pl.BlockSpec((tm, d), row), pl.BlockSpec((tm, ROUTER_LANES), row)],
        out_shape=[jax.ShapeDtypeStruct((n, d), F32), jax.ShapeDtypeStruct((n, d), F32),
                   jax.ShapeDtypeStruct((n, ROUTER_LANES), F32)],
        name="out_proj",
        compiler_params=_cparams(("parallel",)),
    )(a, o_f, o_b, hg, c_att, x, g1, sh2, sc2, ln_g, ln_b, norm_g, gavg, w_out, w_router)


def _moe_kernel(be_ref, nv_ref, x_ref, w1_ref, w3_ref, w2_ref, y_ref, w1b, w3b, w2b):
    i = pl.program_id(0)

    @pl.when((i == 0) | (be_ref[i] != be_ref[jnp.maximum(i - 1, 0)]))
    def _():
        w1b[...] = w1_ref[0, 0].astype(BF16)
        w3b[...] = w3_ref[0, 0].astype(BF16)
        w2b[...] = w2_ref[0, 0].astype(BF16)

    @pl.when(i < nv_ref[0])
    def _():
        x = x_ref[...].astype(BF16)
        h1 = _dot(x, w1b[...])
        h3 = _dot(x, w3b[...])
        h = (_silu(h1) * h3).astype(BF16)
        y_ref[...] = _dot(h, w2b[...]).astype(y_ref.dtype)

    @pl.when(i >= nv_ref[0])
    def _():
        y_ref[...] = jnp.zeros(y_ref.shape, y_ref.dtype)


def _moe_ffn(block_e, n_valid, xr, w1, w3, w2, layer, tm):
    n_rows, d = xr.shape
    n_blocks = n_rows // tm
    grid_spec = pltpu.PrefetchScalarGridSpec(
        num_scalar_prefetch=2,
        grid=(n_blocks,),
        in_specs=[pl.BlockSpec((tm, d), lambda i, be, nv: (jnp.minimum(i, nv[0] - 1), 0)),
                  pl.BlockSpec((1, 1, d, D_EXPERT), lambda i, be, nv: (layer, be[i], 0, 0)),
                  pl.BlockSpec((1, 1, d, D_EXPERT), lambda i, be, nv: (layer, be[i], 0, 0)),
                  pl.BlockSpec((1, 1, D_EXPERT, d), lambda i, be, nv: (layer, be[i], 0, 0))],
        out_specs=pl.BlockSpec((tm, d), lambda i, be, nv: (i, 0)),
        scratch_shapes=[pltpu.VMEM((d, D_EXPERT), BF16), pltpu.VMEM((d, D_EXPERT), BF16),
                        pltpu.VMEM((D_EXPERT, d), BF16)],
    )
    return pl.pallas_call(
        _moe_kernel,
        grid_spec=grid_spec,
        out_shape=jax.ShapeDtypeStruct((n_rows, d), BF16),
        name="moe_ffn",
        compiler_params=_cparams(("arbitrary",)),
    )(block_e, n_valid, xr, w1, w3, w2)


def _combine_kernel(x_ref, y0_ref, y1_ref, gw_ref, g2_ref, lng_ref, lnb_ref, o_ref, *, alpha):
    gw = gw_ref[...]
    y = gw[:, 0:1] * y0_ref[...].astype(F32) + gw[:, 1:2] * y1_ref[...].astype(F32)
    o_ref[...] = _layer_norm(alpha * x_ref[...] + g2_ref[0] * y) * lng_ref[...] + lnb_ref[...]


def _combine(x, y0, y1, gw, g2, ln_g, ln_b, tm, mod_index, alpha):
    n, d = x.shape
    row = lambda i: (i, 0)
    full = lambda i: (0, 0)
    return pl.pallas_call(
        functools.partial(_combine_kernel, alpha=alpha),
        grid=(n // tm,),
        in_specs=[pl.BlockSpec((tm, d), row), pl.BlockSpec((tm, d), row), pl.BlockSpec((tm, d), row),
                  pl.BlockSpec((tm, ROUTER_LANES), row),
                  pl.BlockSpec((1, 1, d), lambda i: (mod_index(i), 0, 0)),
                  pl.BlockSpec((1, d), full), pl.BlockSpec((1, d), full)],
        out_specs=pl.BlockSpec((tm, d), row),
        out_shape=jax.ShapeDtypeStruct((n, d), F32),
        name="moe_combine",
        compiler_params=_cparams(("parallel",)),
    )(x, y0, y1, gw, g2, ln_g, ln_b)


def _rot_half_cols(w):
    lead = w.shape[:-1]
    wr = w.reshape(lead + (2, 2, MLA_ROPE // 4))
    return jnp.stack([-wr[..., 1, :], wr[..., 0, :]], axis=-2).reshape(w.shape)


def _rope_tables(n_lat, n_ctx, bsz):
    rows = n_lat // GRID_W
    row = jnp.repeat(jnp.arange(rows, dtype=F32), GRID_W)
    col = jnp.tile(jnp.arange(GRID_W, dtype=F32), rows)
    n_freq = MLA_ROPE // 4
    inv_freq = ROPE_BASE ** (-jnp.arange(n_freq, dtype=F32) / n_freq)
    ang = jnp.stack([row[:, None] * inv_freq, col[:, None] * inv_freq], axis=1)
    cos = jnp.broadcast_to(jnp.cos(ang)[:, :, None, :], (n_lat, 2, 2, n_freq)).reshape(n_lat, MLA_ROPE)
    sin = jnp.broadcast_to(jnp.sin(ang)[:, :, None, :], (n_lat, 2, 2, n_freq)).reshape(n_lat, MLA_ROPE)
    pad_l = MLA_NOPE
    pad_r = HEAD_LANES - MLA_QK
    cos_t = jnp.concatenate([jnp.ones((n_lat, pad_l), F32), cos, jnp.ones((n_lat, pad_r), F32)], axis=1)
    sin_t = jnp.concatenate([jnp.zeros((n_lat, pad_l), F32), sin, jnp.zeros((n_lat, pad_r), F32)], axis=1)
    cos_all = jnp.concatenate([jnp.tile(cos_t, (bsz, 1)), jnp.ones((bsz * n_ctx, HEAD_LANES), F32)], axis=0)
    sin_all = jnp.concatenate([jnp.tile(sin_t, (bsz, 1)), jnp.zeros((bsz * n_ctx, HEAD_LANES), F32)], axis=0)
    return cos_all, sin_all


def _prep_w_in(w_in):
    depth, d, _ = w_in.shape
    kr = w_in[:, :, W_IN_MAIN:]
    z_l = jnp.zeros((depth, d, MLA_NOPE), w_in.dtype)
    z_r = jnp.zeros((depth, d, HEAD_LANES - MLA_QK), w_in.dtype)
    return jnp.concatenate([w_in[:, :, :W_IN_MAIN], z_l, kr, z_r, z_l, _rot_half_cols(kr), z_r],
                           axis=-1).astype(BF16)


def _prep_mla(w_uq, w_ukv):
    depth = w_uq.shape[0]
    wq = w_uq.reshape(depth, MLA_Q_LORA, MLA_HEADS, MLA_QK)
    pad = jnp.zeros((depth, MLA_Q_LORA, MLA_HEADS, HEAD_LANES - MLA_QK), w_uq.dtype)
    wq_p = jnp.concatenate([wq, pad], axis=-1).reshape(depth, MLA_Q_LORA, MLA_HEADS * HEAD_LANES)
    wq_s = jnp.concatenate([jnp.zeros_like(wq[..., :MLA_NOPE]), _rot_half_cols(wq[..., MLA_NOPE:]), pad],
                           axis=-1).reshape(depth, MLA_Q_LORA, MLA_HEADS * HEAD_LANES)
    wkv = w_ukv.reshape(depth, MLA_KV_LORA, MLA_HEADS, MLA_NOPE + MLA_V)
    kpad = jnp.zeros((depth, MLA_KV_LORA, MLA_HEADS, HEAD_LANES - MLA_NOPE), w_ukv.dtype)
    wk = jnp.concatenate([wkv[..., :MLA_NOPE], kpad], axis=-1).reshape(depth, MLA_KV_LORA, MLA_HEADS * HEAD_LANES)
    wvt = jnp.swapaxes(wkv[..., MLA_NOPE:].reshape(depth, MLA_KV_LORA, MLA_HEADS * MLA_V), 1, 2)
    return wq_p.astype(BF16), wq_s.astype(BF16), wk.astype(BF16), wvt.astype(BF16)


def _block_diag(pool_w):
    depth, g, c, _ = pool_w.shape
    eye = jnp.eye(g, dtype=pool_w.dtype)
    return jnp.einsum("lgcd,gh->lgchd", pool_w, eye).reshape(depth, g * c, g * c).astype(BF16)


def _route(logits, b_rg, b_re, tm_moe):
    n = logits.shape[0]
    lg = logits[:, :N_GROUPS] + b_rg
    le = (logits[:, N_GROUPS:N_GROUPS + N_EXPERTS] + b_re).reshape(n, N_GROUPS, EXPERTS_PER_GROUP)
    pg = jax.nn.softmax(lg, axis=-1)
    g_sel = jnp.argmax(lg, axis=-1).astype(jnp.int32)
    g_hot = g_sel[:, None] == jnp.arange(N_GROUPS, dtype=jnp.int32)[None, :]
    w_g = jnp.sum(jnp.where(g_hot, pg, 0.0), axis=-1)
    le_sel = jnp.sum(jnp.where(g_hot[:, :, None], le, 0.0), axis=1)
    top_v, top_i = lax.top_k(le_sel, TOP_K)
    gate = jax.nn.softmax(top_v, axis=-1) * w_g[:, None]
    expert = (g_sel[:, None] * EXPERTS_PER_GROUP + top_i).astype(jnp.int32)

    n_assign = n * TOP_K
    e_flat = expert.reshape(-1)
    hot = e_flat[:, None] == jnp.arange(N_EXPERTS, dtype=jnp.int32)[None, :]
    nb = n_assign // COUNT_BLOCK
    tri = jnp.tril(jnp.ones((COUNT_BLOCK, COUNT_BLOCK), BF16))
    win = jnp.einsum("ts,bse->bte", tri, hot.astype(BF16).reshape(nb, COUNT_BLOCK, N_EXPERTS),
                     preferred_element_type=F32)
    bsum = win[:, -1, :]
    boff = jnp.cumsum(bsum, axis=0) - bsum
    csum = (win + boff[:, None, :]).reshape(n_assign, N_EXPERTS).astype(jnp.int32)
    counts = (boff[-1] + bsum[-1]).astype(jnp.int32)
    pcounts = (counts + tm_moe - 1) // tm_moe * tm_moe
    pends = jnp.cumsum(pcounts)
    pstarts = pends - pcounts
    starts = jnp.cumsum(counts) - counts
    dest = jnp.sum(jnp.where(hot, csum - 1 + pstarts[None, :], 0), axis=1)
    n_rows = (n_assign + N_EXPERTS * (tm_moe - 1) + tm_moe - 1) // tm_moe * tm_moe
    n_blocks = n_rows // tm_moe
    n_valid = (pends[-1] // tm_moe).astype(jnp.int32)
    blk = jnp.arange(n_blocks, dtype=jnp.int32)
    first_row = jnp.minimum(blk, n_valid - 1) * tm_moe
    block_e = jnp.sum((pends[None, :] <= first_row[:, None]).astype(jnp.int32), axis=1)
    block_e = jnp.clip(block_e, 0, N_EXPERTS - 1)
    a_sorted = jnp.sort(e_flat * n_assign + jnp.arange(n_assign, dtype=jnp.int32)) % n_assign
    pos = (blk * tm_moe - pstarts[block_e])[:, None] + jnp.arange(tm_moe, dtype=jnp.int32)[None, :]
    src = jnp.clip(starts[block_e][:, None] + pos, 0, n_assign - 1)
    row_tok = jnp.where(pos < counts[block_e][:, None], a_sorted[src] // TOP_K, 0)
    gw = jnp.zeros((n, ROUTER_LANES), F32).at[:, :TOP_K].set(gate)
    return row_tok.reshape(n_rows), dest.reshape(n, TOP_K), gw, block_e, n_valid.reshape(1)


def _tile(n, pref):
    t = pref
    while n % t:
        t //= 2
    return t


def kernel(x, c, ctx, c_ctx, w_mod, b_mod, w_in, pool_w, pool_scale, hg_lb_logits, hg_norm_g, mla_q_norm_g, mla_w_uq, mla_kv_norm_g, mla_w_ukv, w_out, ln1_g, ln1_b, ln2_g, ln2_b, router_group_w, router_group_b, router_expert_w, router_expert_b, expert_w1, expert_w3, expert_w2):
    bsz, n_lat, d = x.shape
    n_ctx = ctx.shape[1]
    depth = w_mod.shape[0]
    alpha = (2 * depth) ** 0.25
    lat_rows = bsz * n_lat
    ctx_rows = bsz * n_ctx
    n_tok = lat_rows + ctx_rows

    tm = _tile(np.gcd(n_lat, ctx_rows), 512)
    tp = _tile(np.gcd(n_lat, n_ctx), 256)
    tb_lat = _tile(n_lat, 512)
    tb_ctx = _tile(n_ctx, 512)
    tq_lat = _tile(n_lat, 512)
    tkc = _tile(np.gcd(n_lat, n_ctx), 256)
    tq_ctx = _tile(n_ctx, 512)
    tm_moe = 256
    lat_tiles_per_batch = n_lat // tm
    mod_index = lambda i: jnp.minimum(i // lat_tiles_per_batch, bsz)

    c_rows = jnp.concatenate([jax.nn.silu(c), jax.nn.silu(c_ctx)[None, :],
                              jnp.zeros((-(bsz + 1) % 8, d), F32)], axis=0)
    mod_all = _mod_all(c_rows, w_mod, b_mod)
    lb_all = jnp.cumsum(jax.nn.softmax(hg_lb_logits.astype(F32), axis=0), axis=0)
    lb_all = lb_all - lb_all[0:1]
    log_lb = jnp.log(lb_all)
    log_1m_lb = jnp.log1p(-lb_all)
    w_aug = _prep_w_in(w_in)
    wq_p, wq_s, wk_p, wvt_p = _prep_mla(mla_w_uq, mla_w_ukv)
    w_bd = _block_diag(pool_w)
    w_out_b = w_out.astype(BF16)
    w_router = jnp.concatenate([router_group_w, router_expert_w,
                                jnp.zeros((depth, d, ROUTER_LANES - N_GROUPS - N_EXPERTS), F32)], axis=-1)
    gavg = jnp.kron(jnp.eye(HG_HEADS, dtype=F32), jnp.full((HG_DV, HG_DV), 1.0 / HG_DV, F32)).astype(BF16)
    cos_t, sin_t = _rope_tables(n_lat, n_ctx, bsz)
    zero_state = jnp.zeros((bsz, HG_HEADS, HG_DV, HG_DK), F32)

    xt = jnp.concatenate([x.reshape(lat_rows, d), ctx.reshape(ctx_rows, d)], axis=0)
    for l in range(depth):
        mod = mod_all[l, :bsz + 1].reshape(bsz + 1, 6, 1, d)
        sh1, sc1, g1, sh2, sc2, g2 = (mod[:, k] for k in range(6))

        p_in, hq, hzf, hzb, hin, hg, cq, ckv, kr, krs = _in_proj(xt, sh1, sc1, w_aug[l], tm, mod_index)

        a_mix = _pool(p_in, w_bd[l], pool_scale[l][None, :], tp, lat_rows, n_lat, n_ctx)

        la_f, la_b = log_lb[l, 0][None, :], log_lb[l, 1][None, :]
        lc_f, lc_b = log_1m_lb[l, 0][None, :], log_1m_lb[l, 1][None, :]
        oc_f, s_f = _hgrn_scan(hq, hzf, hin, la_f, lc_f, zero_state, reverse=False, base_row=lat_rows,
                               n_seq=n_ctx, tb=tb_ctx)
        oc_b, s_b = _hgrn_scan(hq, hzb, hin, la_b, lc_b, zero_state, reverse=True, base_row=lat_rows,
                               n_seq=n_ctx, tb=tb_ctx)
        ol_f, _ = _hgrn_scan(hq, hzf, hin, la_f, lc_f, s_f, reverse=False, base_row=0, n_seq=n_lat, tb=tb_lat)
        ol_b, _ = _hgrn_scan(hq, hzb, hin, la_b, lc_b, s_b, reverse=True, base_row=0, n_seq=n_lat, tb=tb_lat)
        o_f = jnp.concatenate([ol_f, oc_f], axis=0)
        o_b = jnp.concatenate([ol_b, oc_b], axis=0)

        q_all, k_all, vt_all = _mla_proj(cq, ckv, kr, krs, cos_t, sin_t, mla_q_norm_g[l][None, :],
                                         mla_kv_norm_g[l][None, :], wq_p[l], wq_s[l], wk_p[l], wvt_p[l], tkc)
        c_lat = _attention(q_all, k_all, vt_all, bsz=bsz, n_q=n_lat, q_base=0, n_lat=n_lat, n_ctx=n_ctx,
                           ctx_base=lat_rows, tq=tq_lat, tkc=tkc, with_lat=True)
        c_ctx_o = _attention(q_all, k_all, vt_all, bsz=bsz, n_q=n_ctx, q_base=lat_rows, n_lat=n_lat, n_ctx=n_ctx,
                             ctx_base=lat_rows, tq=tq_ctx, tkc=tkc, with_lat=False)
        c_att = jnp.concatenate([c_lat, c_ctx_o], axis=0)

        x1, h2, logits = _out_proj(a_mix, o_f, o_b, hg, c_att, xt, g1, sh2, sc2, ln1_g[l][None, :],
                                   ln1_b[l][None, :], hg_norm_g[l][None, :], gavg, w_out_b[l], w_router[l],
                                   tm, mod_index, alpha)

        row_tok, dest, gw, block_e, n_valid = _route(logits, router_group_b[l], router_expert_b[l], tm_moe)
        xr = jnp.take(h2, row_tok, axis=0)
        y = _moe_ffn(block_e, n_valid, xr, expert_w1, expert_w3, expert_w2, l, tm_moe)
        y0 = jnp.take(y, dest[:, 0], axis=0)
        y1 = jnp.take(y, dest[:, 1], axis=0)
        xt = _combine(x1, y0, y1, gw, g2, ln2_g[l][None, :], ln2_b[l][None, :], tm, mod_index, alpha)
    return xt[:lat_rows].reshape(bsz, n_lat, d)
```

```python
import functools

import numpy as np
import jax
import jax.numpy as jnp
from jax import lax
from jax.experimental import pallas as pl
from jax.experimental.pallas import tpu as pltpu

F32 = jnp.float32
BF16 = jnp.bfloat16

D_MODEL = 1024
GRID_W = 64
POOL_WINDOWS = (2, 4, 8, 16)
POOL_DIM = 256
POOL_GROUP_DIM = 64
HG_HEADS = 4
HG_DK = 128
HG_DV = 64
HG_QK = HG_HEADS * HG_DK
HG_V = HG_HEADS * HG_DV
HG_CHUNK = 64
HG_SUB = 16
MLA_HEADS = 8
MLA_NOPE = 64
MLA_ROPE = 32
MLA_V = 64
MLA_Q_LORA = 256
MLA_KV_LORA = 128
MLA_QK = MLA_NOPE + MLA_ROPE
MLA_SCALE = MLA_QK ** -0.5
LOG2_E = 1.4426950408889634
ROPE_BASE = 10000.0
HEAD_LANES = 128
N_GROUPS = 4
EXPERTS_PER_GROUP = 8
N_EXPERTS = N_GROUPS * EXPERTS_PER_GROUP
TOP_K = 2
D_EXPERT = 512
EPS = 1e-6
ROUTER_LANES = 128
COUNT_BLOCK = 128
IN_SEGS = (("p_in", 256), ("hq", 512), ("hzf", 512), ("hzb", 512), ("hin", 256), ("hg", 256),
           ("cq", 256), ("ckv", 128), ("kr", 128), ("krs", 128))
IN_AUG = sum(w for _, w in IN_SEGS)
W_IN_MAIN = 2688

VMEM_LIMIT = 56 * 1024 * 1024


def _cparams(sem):
    return pltpu.CompilerParams(dimension_semantics=sem, vmem_limit_bytes=VMEM_LIMIT)


def _split_bf16(a):
    hi = a.astype(BF16)
    lo = (a - hi.astype(F32)).astype(BF16)
    return hi, lo


def _dot(a, b):
    return jnp.dot(a, b, preferred_element_type=F32)


def _dot_nt(a, b):
    return lax.dot_general(a, b, (((1,), (1,)), ((), ())), preferred_element_type=F32)


def _dot_tn(a, b):
    return lax.dot_general(a, b, (((0,), (0,)), ((), ())), preferred_element_type=F32)


def _dot3(a, b):
    a_hi, a_lo = _split_bf16(a)
    b_hi, b_lo = _split_bf16(b)
    return _dot(a_hi, b_hi) + _dot(a_hi, b_lo) + _dot(a_lo, b_hi)


def _dot_exact_rhs(a, b_bf16):
    a_hi = a.astype(BF16)
    r1 = a - a_hi.astype(F32)
    a_mid = r1.astype(BF16)
    a_lo = (r1 - a_mid.astype(F32)).astype(BF16)
    return _dot(a_hi, b_bf16) + _dot(a_mid, b_bf16) + _dot(a_lo, b_bf16)


def _layer_norm(x):
    mu = jnp.mean(x, axis=-1, keepdims=True)
    xc = x - mu
    var = jnp.mean(xc * xc, axis=-1, keepdims=True)
    return xc * lax.rsqrt(var + EPS)


def _sigmoid(x):
    return 1.0 / (1.0 + jnp.exp(-x))


def _silu(x):
    return x * _sigmoid(x)


def _mod_kernel(a_ref, w_ref, b_ref, o_ref):
    o_ref[0] = _dot3(a_ref[...], w_ref[0]) + b_ref[0]


def _mod_all(c_rows, w_mod, b_mod):
    depth, d, n6 = w_mod.shape
    tn = 1536
    rows = c_rows.shape[0]
    return pl.pallas_call(
        _mod_kernel,
        grid=(depth, n6 // tn),
        in_specs=[pl.BlockSpec((rows, d), lambda l, j: (0, 0)),
                  pl.BlockSpec((1, d, tn), lambda l, j: (l, 0, j)),
                  pl.BlockSpec((1, 1, tn), lambda l, j: (l, 0, j))],
        out_specs=pl.BlockSpec((1, rows, tn), lambda l, j: (l, 0, j)),
        out_shape=jax.ShapeDtypeStruct((depth, rows, n6), F32),
        name="adaln_mod",
        compiler_params=_cparams(("arbitrary", "arbitrary")),
    )(c_rows, w_mod, b_mod.reshape(depth, 1, n6))


def _inproj_kernel(x_ref, sh_ref, sc_ref, w_ref, *out_refs):
    h = _layer_norm(x_ref[...]) * (1.0 + sc_ref[0]) + sh_ref[0]
    hb = h.astype(BF16)
    start = 0
    for (_, width), o_ref in zip(IN_SEGS, out_refs):
        o_ref[...] = _dot(hb, w_ref[:, start:start + width])
        start += width


def _in_proj(x, sh, sc, w_aug, tm, mod_index):
    n, d = x.shape
    row = lambda i: (i, 0)
    mod = lambda i: (mod_index(i), 0, 0)
    return pl.pallas_call(
        _inproj_kernel,
        grid=(n // tm,),
        in_specs=[pl.BlockSpec((tm, d), row),
                  pl.BlockSpec((1, 1, d), mod),
                  pl.BlockSpec((1, 1, d), mod),
                  pl.BlockSpec((d, IN_AUG), lambda i: (0, 0))],
        out_specs=[pl.BlockSpec((tm, w), row) for _, w in IN_SEGS],
        out_shape=[jax.ShapeDtypeStruct((n, w), F32) for _, w in IN_SEGS],
        name="in_proj",
        compiler_params=_cparams(("parallel",)),
    )(x, sh, sc, w_aug)


POOL_HALO = 8


def _pool_kernel(x_ref, prev_ref, next_ref, w_ref, scale_ref, o_ref, *, tp, lat_tiles, tpb_lat, tpb_ctx,
                 n_lat, n_ctx):
    i = pl.program_id(0)
    is_lat = i < lat_tiles
    tile_in_seq = jnp.where(is_lat, i % tpb_lat, (i - lat_tiles) % tpb_ctx)
    tiles_in_seq = jnp.where(is_lat, tpb_lat, tpb_ctx)
    n_seq = jnp.where(is_lat, n_lat, n_ctx)
    first = tile_in_seq == 0
    last = tile_in_seq == tiles_in_seq - 1
    x = x_ref[...]
    prev = jnp.where(first, 0.0, prev_ref[...])
    nxt = jnp.where(last, 0.0, next_ref[...])
    ext = jnp.concatenate([prev, x, nxt], axis=0)
    n_ext = tp + 2 * POOL_HALO

    def back(a, k):
        return pltpu.roll(a, k, 0)

    def fwd(a, k):
        return pltpu.roll(a, n_ext - k, 0)

    e2 = ext + back(ext, 1)
    e4 = back(e2, 1) + fwd(e2, 1)
    e8 = back(e4, 2) + fwd(e4, 2)
    e16 = back(e8, 4) + fwd(e8, 4)
    sl = slice(POOL_HALO, POOL_HALO + tp)
    lane = lax.broadcasted_iota(jnp.int32, (tp, POOL_DIM), 1)
    grp = lane >> 6
    wsum = jnp.where(grp == 0, e2[sl], jnp.where(grp == 1, e4[sl], jnp.where(grp == 2, e8[sl], e16[sl])))
    half = jnp.where(grp == 0, 1, jnp.where(grp == 1, 2, jnp.where(grp == 2, 4, 8)))
    t = tile_in_seq * tp + lax.broadcasted_iota(jnp.int32, (tp, POOL_DIM), 0)
    cnt = jnp.minimum(t + half, n_seq) - jnp.maximum(t - half, 0)
    pooled = wsum / cnt.astype(F32) - x
    o_ref[...] = _dot(pooled.astype(BF16), w_ref[...]) * scale_ref[...]


def _pool(p_in, w_bd, scale, tp, n_lat_rows, n_lat, n_ctx):
    n, c = p_in.shape
    hb = tp // POOL_HALO
    last_halo_block = n // POOL_HALO - 1
    kern = functools.partial(_pool_kernel, tp=tp, lat_tiles=n_lat_rows // tp, tpb_lat=n_lat // tp,
                             tpb_ctx=n_ctx // tp, n_lat=n_lat, n_ctx=n_ctx)
    return pl.pallas_call(
        kern,
        grid=(n // tp,),
        in_specs=[pl.BlockSpec((tp, c), lambda i: (i, 0)),
                  pl.BlockSpec((POOL_HALO, c), lambda i: (jnp.maximum(i * hb - 1, 0), 0)),
                  pl.BlockSpec((POOL_HALO, c), lambda i: (jnp.minimum((i + 1) * hb, last_halo_block), 0)),
                  pl.BlockSpec((c, c), lambda i: (0, 0)),
                  pl.BlockSpec((1, c), lambda i: (0, 0))],
        out_specs=pl.BlockSpec((tp, c), lambda i: (i, 0)),
        out_shape=jax.ShapeDtypeStruct((n, c), F32),
        name="pool",
        compiler_params=_cparams(("parallel",)),
    )(p_in, p_in, p_in, w_bd, scale)


def _hgrn_kernel(hq_ref, hz_ref, hin_ref, la_ref, lc_ref, s0_ref, o_ref, sT_ref, q_s, g_s, k_s, st_s,
                 *, reverse, nchunk):
    j = pl.program_id(1)

    @pl.when(j == 0)
    def _():
        st_s[...] = s0_ref[0]

    z = hz_ref[...]
    ls = jnp.minimum(z, 0.0) - jnp.log1p(jnp.exp(-jnp.abs(z)))
    lc = lc_ref[...]
    x2 = lc + ls
    la = la_ref[...]
    g_s[...] = jnp.maximum(la, x2) + jnp.log1p(jnp.exp(-jnp.abs(la - x2)))
    k_s[...] = jnp.exp(x2 - z)
    q_s[...] = _silu(hq_ref[...])

    C = HG_CHUNK
    row = lax.broadcasted_iota(jnp.int32, (C, C), 0)
    col = lax.broadcasted_iota(jnp.int32, (C, C), 1)
    tri = ((row <= col) if reverse else (row >= col)).astype(BF16)
    rmod = lax.broadcasted_iota(jnp.int32, (C, 1), 0) & (HG_SUB - 1)
    nsub = C // HG_SUB
    row_sub = row >> 4
    col_sub = col >> 4

    def chunk(c, carry):
        cc = (nchunk - 1 - c) if reverse else c
        r0 = pl.multiple_of(cc * C, C)
        b_all = _dot_exact_rhs_lhs(tri, g_s[pl.ds(r0, C), :])
        q_all = q_s[pl.ds(r0, C), :]
        k_all = k_s[pl.ds(r0, C), :]
        v_all = hin_ref[pl.ds(r0, C), :]
        outs = []
        for h in range(HG_HEADS):
            ks = slice(h * HG_DK, (h + 1) * HG_DK)
            b, q, k = b_all[:, ks], q_all[:, ks], k_all[:, ks]
            v = v_all[:, h * HG_DV:(h + 1) * HG_DV]
            st = st_s[h]
            b_last = b[0:1] if reverse else b[C - 1:C]
            o = _dot_nt((q * jnp.exp(b)).astype(BF16), st.astype(BF16))
            a_rows = []
            for i in range(nsub):
                rs = slice(i * HG_SUB, (i + 1) * HG_SUB)
                if (reverse and i == nsub - 1) or (not reverse and i == 0):
                    a_rows.append(jnp.zeros((HG_SUB, C), F32))
                    continue
                m = b[(i + 1) * HG_SUB - 1:(i + 1) * HG_SUB] if reverse else b[i * HG_SUB:i * HG_SUB + 1]
                qi = q[rs] * jnp.exp(b[rs] - m)
                ksc = k * jnp.exp(m - b)
                a_rows.append(_dot_nt(qi.astype(BF16), ksc.astype(BF16)))
            a = jnp.concatenate(a_rows, axis=0)
            if reverse:
                a = jnp.where(col_sub > row_sub, a, 0.0)
            else:
                a = jnp.where(col_sub < row_sub, a, 0.0)
            o = o + _dot(a.astype(BF16), v.astype(BF16))
            for delta in range(HG_SUB):
                if delta == 0:
                    w = jnp.sum(q * k, axis=-1, keepdims=True)
                    o = o + w * v
                    continue
                shift = (C - delta) if reverse else delta
                kd = pltpu.roll(k, shift, 0)
                bd = pltpu.roll(b, shift, 0)
                vd = pltpu.roll(v, shift, 0)
                e = jnp.exp(b - bd)
                w = jnp.sum(q * kd * e, axis=-1, keepdims=True)
                valid = (rmod + delta < HG_SUB) if reverse else (rmod >= delta)
                o = o + jnp.where(valid, w, 0.0) * vd
            outs.append(o)
            kdec = k * jnp.exp(b_last - b)
            st_s[h] = st * jnp.exp(b_last) + _dot_tn(v.astype(BF16), kdec.astype(BF16))
        o_ref[pl.ds(r0, C), :] = jnp.concatenate(outs, axis=1)
        return carry

    lax.fori_loop(0, nchunk, chunk, 0)

    @pl.when(j == pl.num_programs(1) - 1)
    def _():
        sT_ref[0] = st_s[...]


def _dot_exact_rhs_lhs(tri_bf16, g):
    g_hi = g.astype(BF16)
    r1 = g - g_hi.astype(F32)
    g_mid = r1.astype(BF16)
    g_lo = (r1 - g_mid.astype(F32)).astype(BF16)
    return _dot(tri_bf16, g_hi) + _dot(tri_bf16, g_mid) + _dot(tri_bf16, g_lo)


def _hgrn_scan(hq, hz, hin, la, lc, s0, *, reverse, base_row, n_seq, tb):
    bsz = s0.shape[0]
    nblk = n_seq // tb
    base_blk = base_row // tb

    def rows(b, j):
        jj = (nblk - 1 - j) if reverse else j
        return (base_blk + b * nblk + jj, 0)

    def orow(b, j):
        jj = (nblk - 1 - j) if reverse else j
        return (b * nblk + jj, 0)

    kern = functools.partial(_hgrn_kernel, reverse=reverse, nchunk=tb // HG_CHUNK)
    return pl.pallas_call(
        kern,
        grid=(bsz, nblk),
        in_specs=[pl.BlockSpec((tb, HG_QK), rows),
                  pl.BlockSpec((tb, HG_QK), rows),
                  pl.BlockSpec((tb, HG_V), rows),
                  pl.BlockSpec((1, HG_QK), lambda b, j: (0, 0)),
                  pl.BlockSpec((1, HG_QK), lambda b, j: (0, 0)),
                  pl.BlockSpec((1, HG_HEADS, HG_DV, HG_DK), lambda b, j: (b, 0, 0, 0))],
        out_specs=[pl.BlockSpec((tb, HG_V), orow),
                   pl.BlockSpec((1, HG_HEADS, HG_DV, HG_DK), lambda b, j: (b, 0, 0, 0))],
        out_shape=[jax.ShapeDtypeStruct((bsz * n_seq, HG_V), F32),
                   jax.ShapeDtypeStruct((bsz, HG_HEADS, HG_DV, HG_DK), F32)],
        scratch_shapes=[pltpu.VMEM((tb, HG_QK), F32), pltpu.VMEM((tb, HG_QK), F32),
                        pltpu.VMEM((tb, HG_QK), F32), pltpu.VMEM((HG_HEADS, HG_DV, HG_DK), F32)],
        name="hgrn_bwd" if reverse else "hgrn_fwd",
        compiler_params=_cparams(("parallel", "arbitrary")),
    )(hq, hz, hin, la, lc, s0)


def _rms(x, g):
    return x * lax.rsqrt(jnp.mean(x * x, axis=-1, keepdims=True) + EPS) * g


def _mla_proj_kernel(cq_ref, ckv_ref, kr_ref, krs_ref, cos_ref, sin_ref, qg_ref, kvg_ref,
                     wq_ref, wqs_ref, wk_ref, wvt_ref, q_out, k_out, vt_out):
    cos = cos_ref[...]
    sin = sin_ref[...]
    cos_h = jnp.concatenate([cos] * MLA_HEADS, axis=1)
    sin_h = jnp.concatenate([sin] * MLA_HEADS, axis=1)
    xq = _rms(cq_ref[...], qg_ref[...]).astype(BF16)
    q = _dot(xq, wq_ref[...]) * cos_h + _dot(xq, wqs_ref[...]) * sin_h
    q_out[...] = (q * (MLA_SCALE * LOG2_E)).astype(BF16)
    xkv = _rms(ckv_ref[...], kvg_ref[...]).astype(BF16)
    k_rope = kr_ref[...] * cos + krs_ref[...] * sin
    k = _dot(xkv, wk_ref[...]) + jnp.concatenate([k_rope] * MLA_HEADS, axis=1)
    k_out[...] = k.astype(BF16)
    vt_out[0] = _dot_nt(wvt_ref[...], xkv).astype(BF16)


def _mla_proj(cq, ckv, kr, krs, cos_t, sin_t, qg, kvg, wq, wqs, wk, wvt, tm):
    n = cq.shape[0]
    row = lambda i: (i, 0)
    full = lambda i: (0, 0)
    hq = MLA_HEADS * HEAD_LANES
    hv = MLA_HEADS * MLA_V
    return pl.pallas_call(
        _mla_proj_kernel,
        grid=(n // tm,),
        in_specs=[pl.BlockSpec((tm, MLA_Q_LORA), row), pl.BlockSpec((tm, MLA_KV_LORA), row),
                  pl.BlockSpec((tm, HEAD_LANES), row), pl.BlockSpec((tm, HEAD_LANES), row),
                  pl.BlockSpec((tm, HEAD_LANES), row), pl.BlockSpec((tm, HEAD_LANES), row),
                  pl.BlockSpec((1, MLA_Q_LORA), full), pl.BlockSpec((1, MLA_KV_LORA), full),
                  pl.BlockSpec((MLA_Q_LORA, hq), full), pl.BlockSpec((MLA_Q_LORA, hq), full),
                  pl.BlockSpec((MLA_KV_LORA, hq), full), pl.BlockSpec((hv, MLA_KV_LORA), full)],
        out_specs=[pl.BlockSpec((tm, hq), row), pl.BlockSpec((tm, hq), row),
                   pl.BlockSpec((1, hv, tm), lambda i: (i, 0, 0))],
        out_shape=[jax.ShapeDtypeStruct((n, hq), BF16), jax.ShapeDtypeStruct((n, hq), BF16),
                   jax.ShapeDtypeStruct((n // tm, hv, tm), BF16)],
        name="mla_proj",
        compiler_params=_cparams(("parallel",)),
    )(cq, ckv, kr, krs, cos_t, sin_t, qg, kvg, wq, wqs, wk, wvt)


ACC_ROWS = MLA_V + 16
LAT_UNROLL = 4
LOOKAHEAD = 2


def _attn_kernel(*refs, tkc, n_lat_chunks, n_ctx_chunks):
    if n_lat_chunks:
        q_ref, kl_ref, vl_ref, kc_ref, vc_ref, o_ref, m_s, acc_s, s_s = refs
    else:
        q_ref, kc_ref, vc_ref, o_ref, m_s, acc_s, s_s = refs
    m_s[...] = jnp.full(m_s.shape, -jnp.inf, F32)
    acc_s[...] = jnp.zeros(acc_s.shape, F32)
    ones = jnp.ones((ACC_ROWS - MLA_V, tkc), BF16)

    def scores(a, k_ref, r0):
        hs = slice(a * HEAD_LANES, (a + 1) * HEAD_LANES)
        return _dot_nt(k_ref[pl.ds(r0, tkc), hs], q_ref[:, hs])

    def accumulate(a, s_t, vt):
        m_old = m_s[a]
        m_new = jnp.maximum(m_old, jnp.max(s_t, axis=0, keepdims=True))
        alpha = jnp.exp2(m_old - m_new)
        p_t = jnp.exp2(s_t - m_new).astype(BF16)
        vta = jnp.concatenate([vt[a * MLA_V:(a + 1) * MLA_V], ones], axis=0)
        acc_s[a] = alpha * acc_s[a] + _dot(vta, p_t)
        m_s[a] = m_new

    def run(items, queue, after):
        upcoming = items[LOOKAHEAD:] + after
        queue = list(queue)
        for idx, (a, _, _, vt) in enumerate(items):
            if idx < len(upcoming):
                queue.append(scores(*upcoming[idx][:3]))
            accumulate(a, queue.pop(0), vt)
        return queue

    ahead_chunks = (LOOKAHEAD + 1) // 2

    def lat_items(c0, dynamic, count=LAT_UNROLL):
        items = []
        for u in range(count):
            cc = c0 + u
            r0 = pl.multiple_of(cc * tkc, tkc) if dynamic else cc * tkc
            vt = vl_ref[cc] if count == LAT_UNROLL else None
            items += [(0, kl_ref, r0, vt), (1, kl_ref, r0, vt)]
        return items

    ctx_items = []
    for c in range(n_ctx_chunks):
        ctx_items += [(0, kc_ref, c * tkc, vc_ref[c]), (1, kc_ref, c * tkc, vc_ref[c])]

    if n_lat_chunks:
        n_groups = n_lat_chunks // LAT_UNROLL
        for d, it in enumerate(lat_items(0, False, count=ahead_chunks)[:LOOKAHEAD]):
            s_s[d] = scores(*it[:3])

        def body(g, carry):
            c0 = g * LAT_UNROLL
            after = lat_items(c0 + LAT_UNROLL, True, count=ahead_chunks)[:LOOKAHEAD]
            queue = run(lat_items(c0, True), [s_s[d] for d in range(LOOKAHEAD)], after)
            for d in range(LOOKAHEAD):
                s_s[d] = queue[d]
            return carry

        lax.fori_loop(0, n_groups - 1, body, 0)
        run(lat_items((n_groups - 1) * LAT_UNROLL, False) + ctx_items, [s_s[d] for d in range(LOOKAHEAD)], [])
    else:
        run(ctx_items, [scores(*it[:3]) for it in ctx_items[:LOOKAHEAD]], [])
    outs = []
    for a in range(2):
        acc = acc_s[a]
        outs.append(acc[0:MLA_V] / acc[MLA_V:MLA_V + 1])
    o_ref[...] = jnp.concatenate(outs, axis=0).T.astype(o_ref.dtype)


def _attention(q, k, vt, *, bsz, n_q, q_base, n_lat, n_ctx, ctx_base, tq, tkc, with_lat):
    pairs = MLA_HEADS // 2
    qblocks = n_q // tq
    qb0 = q_base // tq
    cb0 = ctx_base // n_ctx
    kw = 2 * HEAD_LANES
    vw = 2 * MLA_V
    in_specs = [pl.BlockSpec((tq, kw), lambda b, p, i: (qb0 + b * qblocks + i, p))]
    args = [q]
    if with_lat:
        in_specs += [pl.BlockSpec((n_lat, kw), lambda b, p, i: (b, p)),
                     pl.BlockSpec((n_lat // tkc, vw, tkc), lambda b, p, i: (b, p, 0))]
        args += [k, vt]
    in_specs += [pl.BlockSpec((n_ctx, kw), lambda b, p, i: (cb0 + b, p)),
                 pl.BlockSpec((n_ctx // tkc, vw, tkc), lambda b, p, i: (cb0 + b, p, 0))]
    args += [k, vt]
    kern = functools.partial(_attn_kernel, tkc=tkc, n_lat_chunks=(n_lat // tkc) if with_lat else 0,
                             n_ctx_chunks=n_ctx // tkc)
    return pl.pallas_call(
        kern,
        grid=(bsz, pairs, qblocks),
        in_specs=in_specs,
        out_specs=pl.BlockSpec((tq, vw), lambda b, p, i: (b * qblocks + i, p)),
        out_shape=jax.ShapeDtypeStruct((bsz * n_q, MLA_HEADS * MLA_V), BF16),
        scratch_shapes=[pltpu.VMEM((2, 1, tq), F32), pltpu.VMEM((2, ACC_ROWS, tq), F32),
                        pltpu.VMEM((LOOKAHEAD, tkc, tq), F32)],
        name="attn_lat" if with_lat else "attn_ctx",
        compiler_params=_cparams(("parallel", "parallel", "arbitrary")),
    )(*args)


def _outproj_kernel(a_ref, of_ref, ob_ref, hg_ref, c_ref, x_ref, g1_ref, sh2_ref, sc2_ref,
                    lng_ref, lnb_ref, ng_ref, gavg_ref, w_ref, wr_ref, x1_out, h2_out, lg_out, *, alpha):
    o = of_ref[...] + ob_ref[...]
    o2_hi, o2_lo = _split_bf16(o * o)
    ms = _dot(o2_hi, gavg_ref[...]) + _dot(o2_lo, gavg_ref[...])
    b_lat = o * lax.rsqrt(ms + EPS) * ng_ref[...] * _silu(hg_ref[...])
    m = (_dot(a_ref[...].astype(BF16), w_ref[0:POOL_DIM, :])
         + _dot(b_lat.astype(BF16), w_ref[POOL_DIM:POOL_DIM + HG_V, :])
         + _dot(c_ref[...], w_ref[POOL_DIM + HG_V:, :]))
    x1 = _layer_norm(alpha * x_ref[...] + g1_ref[0] * m) * lng_ref[...] + lnb_ref[...]
    x1_out[...] = x1
    h2 = _layer_norm(x1) * (1.0 + sc2_ref[0]) + sh2_ref[0]
    h2_out[...] = h2
    lg_out[...] = _dot3(h2, wr_ref[...])


def _out_proj(a, o_f, o_b, hg, c_att, x, g1, sh2, sc2, ln_g, ln_b, norm_g, gavg, w_out, w_router, tm,
              mod_index, alpha):
    n, d = x.shape
    row = lambda i: (i, 0)
    full = lambda i: (0, 0)
    mod = lambda i: (mod_index(i), 0, 0)
    return pl.pallas_call(
        functools.partial(_outproj_kernel, alpha=alpha),
        grid=(n // tm,),
        in_specs=[pl.BlockSpec((tm, POOL_DIM), row), pl.BlockSpec((tm, HG_V), row), pl.BlockSpec((tm, HG_V), row),
                  pl.BlockSpec((tm, HG_V), row), pl.BlockSpec((tm, MLA_HEADS * MLA_V), row),
                  pl.BlockSpec((tm, d), row),
                  pl.BlockSpec((1, 1, d), mod), pl.BlockSpec((1, 1, d), mod), pl.BlockSpec((1, 1, d), mod),
                  pl.BlockSpec((1, d), full), pl.BlockSpec((1, d), full), pl.BlockSpec((1, HG_V), full),
                  pl.BlockSpec((HG_V, HG_V), full), pl.BlockSpec((d, d), full),
                  pl.BlockSpec((d, ROUTER_LANES), full)],
        out_specs=[pl.BlockSpec((tm, d), row), pl.BlockSpec((tm, d), row), pl.BlockSpec((tm, ROUTER_LANES), row)],
        out_shape=[jax.ShapeDtypeStruct((n, d), F32), jax.ShapeDtypeStruct((n, d), F32),
                   jax.ShapeDtypeStruct((n, ROUTER_LANES), F32)],
        name="out_proj",
        compiler_params=_cparams(("parallel",)),
    )(a, o_f, o_b, hg, c_att, x, g1, sh2, sc2, ln_g, ln_b, norm_g, gavg, w_out, w_router)


def _moe_kernel(be_ref, nv_ref, x_ref, w1_ref, w3_ref, w2_ref, y_ref, w1b, w3b, w2b):
    i = pl.program_id(0)

    @pl.when((i == 0) | (be_ref[i] != be_ref[jnp.maximum(i - 1, 0)]))
    def _():
        w1b[...] = w1_ref[0, 0].astype(BF16)
        w3b[...] = w3_ref[0, 0].astype(BF16)
        w2b[...] = w2_ref[0, 0].astype(BF16)

    @pl.when(i < nv_ref[0])
    def _():
        x = x_ref[...].astype(BF16)
        h1 = _dot(x, w1b[...])
        h3 = _dot(x, w3b[...])
        h = (_silu(h1) * h3).astype(BF16)
        y_ref[...] = _dot(h, w2b[...]).astype(y_ref.dtype)

    @pl.when(i >= nv_ref[0])
    def _():
        y_ref[...] = jnp.zeros(y_ref.shape, y_ref.dtype)


def _moe_ffn(block_e, n_valid, xr, w1, w3, w2, layer, tm):
    n_rows, d = xr.shape
    n_blocks = n_rows // tm
    grid_spec = pltpu.PrefetchScalarGridSpec(
        num_scalar_prefetch=2,
        grid=(n_blocks,),
        in_specs=[pl.BlockSpec((tm, d), lambda i, be, nv: (jnp.minimum(i, nv[0] - 1), 0)),
                  pl.BlockSpec((1, 1, d, D_EXPERT), lambda i, be, nv: (layer, be[i], 0, 0)),
                  pl.BlockSpec((1, 1, d, D_EXPERT), lambda i, be, nv: (layer, be[i], 0, 0)),
                  pl.BlockSpec((1, 1, D_EXPERT, d), lambda i, be, nv: (layer, be[i], 0, 0))],
        out_specs=pl.BlockSpec((tm, d), lambda i, be, nv: (i, 0)),
        scratch_shapes=[pltpu.VMEM((d, D_EXPERT), BF16), pltpu.VMEM((d, D_EXPERT), BF16),
                        pltpu.VMEM((D_EXPERT, d), BF16)],
    )
    return pl.pallas_call(
        _moe_kernel,
        grid_spec=grid_spec,
        out_shape=jax.ShapeDtypeStruct((n_rows, d), BF16),
        name="moe_ffn",
        compiler_params=_cparams(("arbitrary",)),
    )(block_e, n_valid, xr, w1, w3, w2)


def _combine_kernel(x_ref, y0_ref, y1_ref, gw_ref, g2_ref, lng_ref, lnb_ref, o_ref, *, alpha):
    gw = gw_ref[...]
    y = gw[:, 0:1] * y0_ref[...].astype(F32) + gw[:, 1:2] * y1_ref[...].astype(F32)
    o_ref[...] = _layer_norm(alpha * x_ref[...] + g2_ref[0] * y) * lng_ref[...] + lnb_ref[...]


def _combine(x, y0, y1, gw, g2, ln_g, ln_b, tm, mod_index, alpha):
    n, d = x.shape
    row = lambda i: (i, 0)
    full = lambda i: (0, 0)
    return pl.pallas_call(
        functools.partial(_combine_kernel, alpha=alpha),
        grid=(n // tm,),
        in_specs=[pl.BlockSpec((tm, d), row), pl.BlockSpec((tm, d), row), pl.BlockSpec((tm, d), row),
                  pl.BlockSpec((tm, ROUTER_LANES), row),
                  pl.BlockSpec((1, 1, d), lambda i: (mod_index(i), 0, 0)),
                  pl.BlockSpec((1, d), full), pl.BlockSpec((1, d), full)],
        out_specs=pl.BlockSpec((tm, d), row),
        out_shape=jax.ShapeDtypeStruct((n, d), F32),
        name="moe_combine",
        compiler_params=_cparams(("parallel",)),
    )(x, y0, y1, gw, g2, ln_g, ln_b)


def _rot_half_cols(w):
    lead = w.shape[:-1]
    wr = w.reshape(lead + (2, 2, MLA_ROPE // 4))
    return jnp.stack([-wr[..., 1, :], wr[..., 0, :]], axis=-2).reshape(w.shape)


def _rope_tables(n_lat, n_ctx, bsz):
    rows = n_lat // GRID_W
    row = jnp.repeat(jnp.arange(rows, dtype=F32), GRID_W)
    col = jnp.tile(jnp.arange(GRID_W, dtype=F32), rows)
    n_freq = MLA_ROPE // 4
    inv_freq = ROPE_BASE ** (-jnp.arange(n_freq, dtype=F32) / n_freq)
    ang = jnp.stack([row[:, None] * inv_freq, col[:, None] * inv_freq], axis=1)
    cos = jnp.broadcast_to(jnp.cos(ang)[:, :, None, :], (n_lat, 2, 2, n_freq)).reshape(n_lat, MLA_ROPE)
    sin = jnp.broadcast_to(jnp.sin(ang)[:, :, None, :], (n_lat, 2, 2, n_freq)).reshape(n_lat, MLA_ROPE)
    pad_l = MLA_NOPE
    pad_r = HEAD_LANES - MLA_QK
    cos_t = jnp.concatenate([jnp.ones((n_lat, pad_l), F32), cos, jnp.ones((n_lat, pad_r), F32)], axis=1)
    sin_t = jnp.concatenate([jnp.zeros((n_lat, pad_l), F32), sin, jnp.zeros((n_lat, pad_r), F32)], axis=1)
    cos_all = jnp.concatenate([jnp.tile(cos_t, (bsz, 1)), jnp.ones((bsz * n_ctx, HEAD_LANES), F32)], axis=0)
    sin_all = jnp.concatenate([jnp.tile(sin_t, (bsz, 1)), jnp.zeros((bsz * n_ctx, HEAD_LANES), F32)], axis=0)
    return cos_all, sin_all


def _prep_w_in(w_in):
    depth, d, _ = w_in.shape
    kr = w_in[:, :, W_IN_MAIN:]
    z_l = jnp.zeros((depth, d, MLA_NOPE), w_in.dtype)
    z_r = jnp.zeros((depth, d, HEAD_LANES - MLA_QK), w_in.dtype)
    return jnp.concatenate([w_in[:, :, :W_IN_MAIN], z_l, kr, z_r, z_l, _rot_half_cols(kr), z_r],
                           axis=-1).astype(BF16)


def _prep_mla(w_uq, w_ukv):
    depth = w_uq.shape[0]
    wq = w_uq.reshape(depth, MLA_Q_LORA, MLA_HEADS, MLA_QK)
    pad = jnp.zeros((depth, MLA_Q_LORA, MLA_HEADS, HEAD_LANES - MLA_QK), w_uq.dtype)
    wq_p = jnp.concatenate([wq, pad], axis=-1).reshape(depth, MLA_Q_LORA, MLA_HEADS * HEAD_LANES)
    wq_s = jnp.concatenate([jnp.zeros_like(wq[..., :MLA_NOPE]), _rot_half_cols(wq[..., MLA_NOPE:]), pad],
                           axis=-1).reshape(depth, MLA_Q_LORA, MLA_HEADS * HEAD_LANES)
    wkv = w_ukv.reshape(depth, MLA_KV_LORA, MLA_HEADS, MLA_NOPE + MLA_V)
    kpad = jnp.zeros((depth, MLA_KV_LORA, MLA_HEADS, HEAD_LANES - MLA_NOPE), w_ukv.dtype)
    wk = jnp.concatenate([wkv[..., :MLA_NOPE], kpad], axis=-1).reshape(depth, MLA_KV_LORA, MLA_HEADS * HEAD_LANES)
    wvt = jnp.swapaxes(wkv[..., MLA_NOPE:].reshape(depth, MLA_KV_LORA, MLA_HEADS * MLA_V), 1, 2)
    return wq_p.astype(BF16), wq_s.astype(BF16), wk.astype(BF16), wvt.astype(BF16)


def _block_diag(pool_w):
    depth, g, c, _ = pool_w.shape
    eye = jnp.eye(g, dtype=pool_w.dtype)
    return jnp.einsum("lgcd,gh->lgchd", pool_w, eye).reshape(depth, g * c, g * c).astype(BF16)


def _route(logits, b_rg, b_re, tm_moe):
    n = logits.shape[0]
    lg = logits[:, :N_GROUPS] + b_rg
    le = (logits[:, N_GROUPS:N_GROUPS + N_EXPERTS] + b_re).reshape(n, N_GROUPS, EXPERTS_PER_GROUP)
    pg = jax.nn.softmax(lg, axis=-1)
    g_sel = jnp.argmax(lg, axis=-1).astype(jnp.int32)
    g_hot = g_sel[:, None] == jnp.arange(N_GROUPS, dtype=jnp.int32)[None, :]
    w_g = jnp.sum(jnp.where(g_hot, pg, 0.0), axis=-1)
    le_sel = jnp.sum(jnp.where(g_hot[:, :, None], le, 0.0), axis=1)
    top_v, top_i = lax.top_k(le_sel, TOP_K)
    gate = jax.nn.softmax(top_v, axis=-1) * w_g[:, None]
    expert = (g_sel[:, None] * EXPERTS_PER_GROUP + top_i).astype(jnp.int32)

    n_assign = n * TOP_K
    e_flat = expert.reshape(-1)
    hot = e_flat[:, None] == jnp.arange(N_EXPERTS, dtype=jnp.int32)[None, :]
    nb = n_assign // COUNT_BLOCK
    tri = jnp.tril(jnp.ones((COUNT_BLOCK, COUNT_BLOCK), BF16))
    win = jnp.einsum("ts,bse->bte", tri, hot.astype(BF16).reshape(nb, COUNT_BLOCK, N_EXPERTS),
                     preferred_element_type=F32)
    bsum = win[:, -1, :]
    boff = jnp.cumsum(bsum, axis=0) - bsum
    csum = (win + boff[:, None, :]).reshape(n_assign, N_EXPERTS).astype(jnp.int32)
    counts = (boff[-1] + bsum[-1]).astype(jnp.int32)
    pcounts = (counts + tm_moe - 1) // tm_moe * tm_moe
    pends = jnp.cumsum(pcounts)
    pstarts = pends - pcounts
    starts = jnp.cumsum(counts) - counts
    dest = jnp.sum(jnp.where(hot, csum - 1 + pstarts[None, :], 0), axis=1)
    n_rows = (n_assign + N_EXPERTS * (tm_moe - 1) + tm_moe - 1) // tm_moe * tm_moe
    n_blocks = n_rows // tm_moe
    n_valid = (pends[-1] // tm_moe).astype(jnp.int32)
    blk = jnp.arange(n_blocks, dtype=jnp.int32)
    first_row = jnp.minimum(blk, n_valid - 1) * tm_moe
    block_e = jnp.sum((pends[None, :] <= first_row[:, None]).astype(jnp.int32), axis=1)
    block_e = jnp.clip(block_e, 0, N_EXPERTS - 1)
    a_sorted = jnp.sort(e_flat * n_assign + jnp.arange(n_assign, dtype=jnp.int32)) % n_assign
    pos = (blk * tm_moe - pstarts[block_e])[:, None] + jnp.arange(tm_moe, dtype=jnp.int32)[None, :]
    src = jnp.clip(starts[block_e][:, None] + pos, 0, n_assign - 1)
    filler = (blk[:, None] * tm_moe + jnp.arange(tm_moe, dtype=jnp.int32)[None, :]) % n
    row_tok = jnp.where(pos < counts[block_e][:, None], a_sorted[src] // TOP_K, filler)
    gw = jnp.zeros((n, ROUTER_LANES), F32).at[:, :TOP_K].set(gate)
    return row_tok.reshape(n_rows), dest.reshape(n, TOP_K), gw, block_e, n_valid.reshape(1)


def _tile(n, pref):
    t = pref
    while n % t:
        t //= 2
    return t


def kernel(x, c, ctx, c_ctx, w_mod, b_mod, w_in, pool_w, pool_scale, hg_lb_logits, hg_norm_g, mla_q_norm_g, mla_w_uq, mla_kv_norm_g, mla_w_ukv, w_out, ln1_g, ln1_b, ln2_g, ln2_b, router_group_w, router_group_b, router_expert_w, router_expert_b, expert_w1, expert_w3, expert_w2):
    bsz, n_lat, d = x.shape
    n_ctx = ctx.shape[1]
    depth = w_mod.shape[0]
    alpha = (2 * depth) ** 0.25
    lat_rows = bsz * n_lat
    ctx_rows = bsz * n_ctx
    n_tok = lat_rows + ctx_rows

    tm = _tile(np.gcd(n_lat, ctx_rows), 512)
    tp = _tile(np.gcd(n_lat, n_ctx), 256)
    tb_lat = _tile(n_lat, 512)
    tb_ctx = _tile(n_ctx, 512)
    tq_lat = _tile(n_lat, 512)
    tkc = _tile(np.gcd(n_lat, n_ctx), 256)
    tq_ctx = _tile(n_ctx, 512)
    tm_moe = 256
    lat_tiles_per_batch = n_lat // tm
    mod_index = lambda i: jnp.minimum(i // lat_tiles_per_batch, bsz)

    c_rows = jnp.concatenate([jax.nn.silu(c), jax.nn.silu(c_ctx)[None, :],
                              jnp.zeros((-(bsz + 1) % 8, d), F32)], axis=0)
    mod_all = _mod_all(c_rows, w_mod, b_mod)
    lb_all = jnp.cumsum(jax.nn.softmax(hg_lb_logits.astype(F32), axis=0), axis=0)
    lb_all = lb_all - lb_all[0:1]
    log_lb = jnp.log(lb_all)
    log_1m_lb = jnp.log1p(-lb_all)
    w_aug = _prep_w_in(w_in)
    wq_p, wq_s, wk_p, wvt_p = _prep_mla(mla_w_uq, mla_w_ukv)
    w_bd = _block_diag(pool_w)
    w_out_b = w_out.astype(BF16)
    w_router = jnp.concatenate([router_group_w, router_expert_w,
                                jnp.zeros((depth, d, ROUTER_LANES - N_GROUPS - N_EXPERTS), F32)], axis=-1)
    gavg = jnp.kron(jnp.eye(HG_HEADS, dtype=F32), jnp.full((HG_DV, HG_DV), 1.0 / HG_DV, F32)).astype(BF16)
    cos_t, sin_t = _rope_tables(n_lat, n_ctx, bsz)
    zero_state = jnp.zeros((bsz, HG_HEADS, HG_DV, HG_DK), F32)

    xt = jnp.concatenate([x.reshape(lat_rows, d), ctx.reshape(ctx_rows, d)], axis=0)
    for l in range(depth):
        mod = mod_all[l, :bsz + 1].reshape(bsz + 1, 6, 1, d)
        sh1, sc1, g1, sh2, sc2, g2 = (mod[:, k] for k in range(6))

        p_in, hq, hzf, hzb, hin, hg, cq, ckv, kr, krs = _in_proj(xt, sh1, sc1, w_aug[l], tm, mod_index)

        a_mix = _pool(p_in, w_bd[l], pool_scale[l][None, :], tp, lat_rows, n_lat, n_ctx)

        la_f, la_b = log_lb[l, 0][None, :], log_lb[l, 1][None, :]
        lc_f, lc_b = log_1m_lb[l, 0][None, :], log_1m_lb[l, 1][None, :]
        oc_f, s_f = _hgrn_scan(hq, hzf, hin, la_f, lc_f, zero_state, reverse=False, base_row=lat_rows,
                               n_seq=n_ctx, tb=tb_ctx)
        oc_b, s_b = _hgrn_scan(hq, hzb, hin, la_b, lc_b, zero_state, reverse=True, base_row=lat_rows,
                               n_seq=n_ctx, tb=tb_ctx)
        ol_f, _ = _hgrn_scan(hq, hzf, hin, la_f, lc_f, s_f, reverse=False, base_row=0, n_seq=n_lat, tb=tb_lat)
        ol_b, _ = _hgrn_scan(hq, hzb, hin, la_b, lc_b, s_b, reverse=True, base_row=0, n_seq=n_lat, tb=tb_lat)
        o_f = jnp.concatenate([ol_f, oc_f], axis=0)
        o_b = jnp.concatenate([ol_b, oc_b], axis=0)

        q_all, k_all, vt_all = _mla_proj(cq, ckv, kr, krs, cos_t, sin_t, mla_q_norm_g[l][None, :],
                                         mla_kv_norm_g[l][None, :], wq_p[l], wq_s[l], wk_p[l], wvt_p[l], tkc)
        c_lat = _attention(q_all, k_all, vt_all, bsz=bsz, n_q=n_lat, q_base=0, n_lat=n_lat, n_ctx=n_ctx,
                           ctx_base=lat_rows, tq=tq_lat, tkc=tkc, with_lat=True)
        c_ctx_o = _attention(q_all, k_all, vt_all, bsz=bsz, n_q=n_ctx, q_base=lat_rows, n_lat=n_lat, n_ctx=n_ctx,
                             ctx_base=lat_rows, tq=tq_ctx, tkc=tkc, with_lat=False)
        c_att = jnp.concatenate([c_lat, c_ctx_o], axis=0)

        x1, h2, logits = _out_proj(a_mix, o_f, o_b, hg, c_att, xt, g1, sh2, sc2, ln1_g[l][None, :],
                                   ln1_b[l][None, :], hg_norm_g[l][None, :], gavg, w_out_b[l], w_router[l],
                                   tm, mod_index, alpha)

        row_tok, dest, gw, block_e, n_valid = _route(logits, router_group_b[l], router_expert_b[l], tm_moe)
        xr = jnp.take(h2, row_tok, axis=0)
        y = _moe_ffn(block_e, n_valid, xr, expert_w1, expert_w3, expert_w2, l, tm_moe)
        y0 = jnp.take(y, dest[:, 0], axis=0)
        y1 = jnp.take(y, dest[:, 1], axis=0)
        xt = _combine(x1, y0, y1, gw, g2, ln2_g[l][None, :], ln2_b[l][None, :], tm, mod_index, alpha)
    return xt[:lat_rows].reshape(bsz, n_lat, d)
```

```python
import functools

import numpy as np
import jax
import jax.numpy as jnp
from jax import lax
from jax.experimental import pallas as pl
from jax.experimental.pallas import tpu as pltpu

F32 = jnp.float32
BF16 = jnp.bfloat16

D_MODEL = 1024
GRID_W = 64
POOL_WINDOWS = (2, 4, 8, 16)
POOL_DIM = 256
POOL_GROUP_DIM = 64
HG_HEADS = 4
HG_DK = 128
HG_DV = 64
HG_QK = HG_HEADS * HG_DK
HG_V = HG_HEADS * HG_DV
HG_CHUNK = 64
HG_SUB = 16
MLA_HEADS = 8
MLA_NOPE = 64
MLA_ROPE = 32
MLA_V = 64
MLA_Q_LORA = 256
MLA_KV_LORA = 128
MLA_QK = MLA_NOPE + MLA_ROPE
MLA_SCALE = MLA_QK ** -0.5
LOG2_E = 1.4426950408889634
ROPE_BASE = 10000.0
HEAD_LANES = 128
N_GROUPS = 4
EXPERTS_PER_GROUP = 8
N_EXPERTS = N_GROUPS * EXPERTS_PER_GROUP
TOP_K = 2
D_EXPERT = 512
EPS = 1e-6
ROUTER_LANES = 128
COUNT_BLOCK = 128
IN_SEGS = (("p_in", 256), ("hq", 512), ("hzf", 512), ("hzb", 512), ("hin", 256), ("hg", 256),
           ("cq", 256), ("ckv", 128), ("kr", 128), ("krs", 128))
IN_AUG = sum(w for _, w in IN_SEGS)
W_IN_MAIN = 2688

VMEM_LIMIT = 56 * 1024 * 1024


def _cparams(sem):
    return pltpu.CompilerParams(dimension_semantics=sem, vmem_limit_bytes=VMEM_LIMIT)


def _split_bf16(a):
    hi = a.astype(BF16)
    lo = (a - hi.astype(F32)).astype(BF16)
    return hi, lo


def _dot(a, b):
    return jnp.dot(a, b, preferred_element_type=F32)


def _dot_nt(a, b):
    return lax.dot_general(a, b, (((1,), (1,)), ((), ())), preferred_element_type=F32)


def _dot_tn(a, b):
    return lax.dot_general(a, b, (((0,), (0,)), ((), ())), preferred_element_type=F32)


def _dot3(a, b):
    a_hi, a_lo = _split_bf16(a)
    b_hi, b_lo = _split_bf16(b)
    return _dot(a_hi, b_hi) + _dot(a_hi, b_lo) + _dot(a_lo, b_hi)


def _dot_exact_rhs(a, b_bf16):
    a_hi = a.astype(BF16)
    r1 = a - a_hi.astype(F32)
    a_mid = r1.astype(BF16)
    a_lo = (r1 - a_mid.astype(F32)).astype(BF16)
    return _dot(a_hi, b_bf16) + _dot(a_mid, b_bf16) + _dot(a_lo, b_bf16)


def _layer_norm(x):
    mu = jnp.mean(x, axis=-1, keepdims=True)
    xc = x - mu
    var = jnp.mean(xc * xc, axis=-1, keepdims=True)
    return xc * lax.rsqrt(var + EPS)


def _sigmoid(x):
    return 1.0 / (1.0 + jnp.exp(-x))


def _silu(x):
    return x * _sigmoid(x)


def _mod_kernel(a_ref, w_ref, b_ref, o_ref):
    o_ref[0] = _dot3(a_ref[...], w_ref[0]) + b_ref[0]


def _mod_all(c_rows, w_mod, b_mod):
    depth, d, n6 = w_mod.shape
    tn = 1536
    rows = c_rows.shape[0]
    return pl.pallas_call(
        _mod_kernel,
        grid=(depth, n6 // tn),
        in_specs=[pl.BlockSpec((rows, d), lambda l, j: (0, 0)),
                  pl.BlockSpec((1, d, tn), lambda l, j: (l, 0, j)),
                  pl.BlockSpec((1, 1, tn), lambda l, j: (l, 0, j))],
        out_specs=pl.BlockSpec((1, rows, tn), lambda l, j: (l, 0, j)),
        out_shape=jax.ShapeDtypeStruct((depth, rows, n6), F32),
        name="adaln_mod",
        compiler_params=_cparams(("arbitrary", "arbitrary")),
    )(c_rows, w_mod, b_mod.reshape(depth, 1, n6))


def _inproj_kernel(x_ref, sh_ref, sc_ref, w_ref, *out_refs):
    h = _layer_norm(x_ref[...]) * (1.0 + sc_ref[0]) + sh_ref[0]
    hb = h.astype(BF16)
    start = 0
    for (_, width), o_ref in zip(IN_SEGS, out_refs):
        o_ref[...] = _dot(hb, w_ref[:, start:start + width])
        start += width


def _in_proj(x, sh, sc, w_aug, layer, tm, mod_index):
    n, d = x.shape
    row = lambda i: (i, 0)
    mod = lambda i: (mod_index(i), 0, 0)
    return pl.pallas_call(
        _inproj_kernel,
        grid=(n // tm,),
        in_specs=[pl.BlockSpec((tm, d), row),
                  pl.BlockSpec((1, 1, d), mod),
                  pl.BlockSpec((1, 1, d), mod),
                  pl.BlockSpec((None, d, IN_AUG), lambda i: (layer, 0, 0))],
        out_specs=[pl.BlockSpec((tm, w), row) for _, w in IN_SEGS],
        out_shape=[jax.ShapeDtypeStruct((n, w), F32) for _, w in IN_SEGS],
        name="in_proj",
        compiler_params=_cparams(("parallel",)),
    )(x, sh, sc, w_aug)


POOL_HALO = 8


def _pool_kernel(x_ref, prev_ref, next_ref, w_ref, scale_ref, o_ref, *, tp, lat_tiles, tpb_lat, tpb_ctx,
                 n_lat, n_ctx):
    i = pl.program_id(0)
    is_lat = i < lat_tiles
    tile_in_seq = jnp.where(is_lat, i % tpb_lat, (i - lat_tiles) % tpb_ctx)
    tiles_in_seq = jnp.where(is_lat, tpb_lat, tpb_ctx)
    n_seq = jnp.where(is_lat, n_lat, n_ctx)
    first = tile_in_seq == 0
    last = tile_in_seq == tiles_in_seq - 1
    x = x_ref[...]
    prev = jnp.where(first, 0.0, prev_ref[...])
    nxt = jnp.where(last, 0.0, next_ref[...])
    ext = jnp.concatenate([prev, x, nxt], axis=0)
    n_ext = tp + 2 * POOL_HALO

    def back(a, k):
        return pltpu.roll(a, k, 0)

    def fwd(a, k):
        return pltpu.roll(a, n_ext - k, 0)

    e2 = ext + back(ext, 1)
    e4 = back(e2, 1) + fwd(e2, 1)
    e8 = back(e4, 2) + fwd(e4, 2)
    e16 = back(e8, 4) + fwd(e8, 4)
    sl = slice(POOL_HALO, POOL_HALO + tp)
    lane = lax.broadcasted_iota(jnp.int32, (tp, POOL_DIM), 1)
    grp = lane >> 6
    wsum = jnp.where(grp == 0, e2[sl], jnp.where(grp == 1, e4[sl], jnp.where(grp == 2, e8[sl], e16[sl])))
    half = jnp.where(grp == 0, 1, jnp.where(grp == 1, 2, jnp.where(grp == 2, 4, 8)))
    t = tile_in_seq * tp + lax.broadcasted_iota(jnp.int32, (tp, POOL_DIM), 0)
    cnt = jnp.minimum(t + half, n_seq) - jnp.maximum(t - half, 0)
    pooled = wsum / cnt.astype(F32) - x
    o_ref[...] = _dot(pooled.astype(BF16), w_ref[...]) * scale_ref[...]


def _pool(p_in, w_bd, scale, tp, n_lat_rows, n_lat, n_ctx):
    n, c = p_in.shape
    hb = tp // POOL_HALO
    last_halo_block = n // POOL_HALO - 1
    kern = functools.partial(_pool_kernel, tp=tp, lat_tiles=n_lat_rows // tp, tpb_lat=n_lat // tp,
                             tpb_ctx=n_ctx // tp, n_lat=n_lat, n_ctx=n_ctx)
    return pl.pallas_call(
        kern,
        grid=(n // tp,),
        in_specs=[pl.BlockSpec((tp, c), lambda i: (i, 0)),
                  pl.BlockSpec((POOL_HALO, c), lambda i: (jnp.maximum(i * hb - 1, 0), 0)),
                  pl.BlockSpec((POOL_HALO, c), lambda i: (jnp.minimum((i + 1) * hb, last_halo_block), 0)),
                  pl.BlockSpec((c, c), lambda i: (0, 0)),
                  pl.BlockSpec((1, c), lambda i: (0, 0))],
        out_specs=pl.BlockSpec((tp, c), lambda i: (i, 0)),
        out_shape=jax.ShapeDtypeStruct((n, c), F32),
        name="pool",
        compiler_params=_cparams(("parallel",)),
    )(p_in, p_in, p_in, w_bd, scale)


def _hgrn_kernel(hq_ref, hz_ref, hin_ref, la_ref, lc_ref, s0_ref, o_ref, sT_ref, q_s, g_s, k_s, st_s,
                 *, reverse, nchunk):
    j = pl.program_id(1)

    @pl.when(j == 0)
    def _():
        st_s[...] = s0_ref[0]

    z = hz_ref[...]
    ls = jnp.minimum(z, 0.0) - jnp.log1p(jnp.exp(-jnp.abs(z)))
    lc = lc_ref[...]
    x2 = lc + ls
    la = la_ref[...]
    g_s[...] = jnp.maximum(la, x2) + jnp.log1p(jnp.exp(-jnp.abs(la - x2)))
    k_s[...] = jnp.exp(x2 - z)
    q_s[...] = _silu(hq_ref[...])

    C = HG_CHUNK
    row = lax.broadcasted_iota(jnp.int32, (C, C), 0)
    col = lax.broadcasted_iota(jnp.int32, (C, C), 1)
    tri = ((row <= col) if reverse else (row >= col)).astype(BF16)
    rmod = lax.broadcasted_iota(jnp.int32, (C, 1), 0) & (HG_SUB - 1)
    nsub = C // HG_SUB
    row_sub = row >> 4
    col_sub = col >> 4

    def chunk(c, carry):
        cc = (nchunk - 1 - c) if reverse else c
        r0 = pl.multiple_of(cc * C, C)
        b_all = _dot_exact_rhs_lhs(tri, g_s[pl.ds(r0, C), :])
        q_all = q_s[pl.ds(r0, C), :]
        k_all = k_s[pl.ds(r0, C), :]
        v_all = hin_ref[pl.ds(r0, C), :]
        outs = []
        for h in range(HG_HEADS):
            ks = slice(h * HG_DK, (h + 1) * HG_DK)
            b, q, k = b_all[:, ks], q_all[:, ks], k_all[:, ks]
            v = v_all[:, h * HG_DV:(h + 1) * HG_DV]
            st = st_s[h]
            b_last = b[0:1] if reverse else b[C - 1:C]
            o = _dot_nt((q * jnp.exp(b)).astype(BF16), st.astype(BF16))
            a_rows = []
            for i in range(nsub):
                rs = slice(i * HG_SUB, (i + 1) * HG_SUB)
                if (reverse and i == nsub - 1) or (not reverse and i == 0):
                    a_rows.append(jnp.zeros((HG_SUB, C), F32))
                    continue
                m = b[(i + 1) * HG_SUB - 1:(i + 1) * HG_SUB] if reverse else b[i * HG_SUB:i * HG_SUB + 1]
                qi = q[rs] * jnp.exp(b[rs] - m)
                ksc = k * jnp.exp(m - b)
                a_rows.append(_dot_nt(qi.astype(BF16), ksc.astype(BF16)))
            a = jnp.concatenate(a_rows, axis=0)
            if reverse:
                a = jnp.where(col_sub > row_sub, a, 0.0)
            else:
                a = jnp.where(col_sub < row_sub, a, 0.0)
            o = o + _dot(a.astype(BF16), v.astype(BF16))
            for delta in range(HG_SUB):
                if delta == 0:
                    w = jnp.sum(q * k, axis=-1, keepdims=True)
                    o = o + w * v
                    continue
                shift = (C - delta) if reverse else delta
                kd = pltpu.roll(k, shift, 0)
                bd = pltpu.roll(b, shift, 0)
                vd = pltpu.roll(v, shift, 0)
                e = jnp.exp(b - bd)
                w = jnp.sum(q * kd * e, axis=-1, keepdims=True)
                valid = (rmod + delta < HG_SUB) if reverse else (rmod >= delta)
                o = o + jnp.where(valid, w, 0.0) * vd
            outs.append(o)
            kdec = k * jnp.exp(b_last - b)
            st_s[h] = st * jnp.exp(b_last) + _dot_tn(v.astype(BF16), kdec.astype(BF16))
        o_ref[pl.ds(r0, C), :] = jnp.concatenate(outs, axis=1)
        return carry

    lax.fori_loop(0, nchunk, chunk, 0)

    @pl.when(j == pl.num_programs(1) - 1)
    def _():
        sT_ref[0] = st_s[...]


def _dot_exact_rhs_lhs(tri_bf16, g):
    g_hi = g.astype(BF16)
    r1 = g - g_hi.astype(F32)
    g_mid = r1.astype(BF16)
    g_lo = (r1 - g_mid.astype(F32)).astype(BF16)
    return _dot(tri_bf16, g_hi) + _dot(tri_bf16, g_mid) + _dot(tri_bf16, g_lo)


def _hgrn_scan(hq, hz, hin, la, lc, s0, *, reverse, base_row, n_seq, tb):
    bsz = s0.shape[0]
    nblk = n_seq // tb
    base_blk = base_row // tb

    def rows(b, j):
        jj = (nblk - 1 - j) if reverse else j
        return (base_blk + b * nblk + jj, 0)

    def orow(b, j):
        jj = (nblk - 1 - j) if reverse else j
        return (b * nblk + jj, 0)

    kern = functools.partial(_hgrn_kernel, reverse=reverse, nchunk=tb // HG_CHUNK)
    return pl.pallas_call(
        kern,
        grid=(bsz, nblk),
        in_specs=[pl.BlockSpec((tb, HG_QK), rows),
                  pl.BlockSpec((tb, HG_QK), rows),
                  pl.BlockSpec((tb, HG_V), rows),
                  pl.BlockSpec((1, HG_QK), lambda b, j: (0, 0)),
                  pl.BlockSpec((1, HG_QK), lambda b, j: (0, 0)),
                  pl.BlockSpec((1, HG_HEADS, HG_DV, HG_DK), lambda b, j: (b, 0, 0, 0))],
        out_specs=[pl.BlockSpec((tb, HG_V), orow),
                   pl.BlockSpec((1, HG_HEADS, HG_DV, HG_DK), lambda b, j: (b, 0, 0, 0))],
        out_shape=[jax.ShapeDtypeStruct((bsz * n_seq, HG_V), F32),
                   jax.ShapeDtypeStruct((bsz, HG_HEADS, HG_DV, HG_DK), F32)],
        scratch_shapes=[pltpu.VMEM((tb, HG_QK), F32), pltpu.VMEM((tb, HG_QK), F32),
                        pltpu.VMEM((tb, HG_QK), F32), pltpu.VMEM((HG_HEADS, HG_DV, HG_DK), F32)],
        name="hgrn_bwd" if reverse else "hgrn_fwd",
        compiler_params=_cparams(("parallel", "arbitrary")),
    )(hq, hz, hin, la, lc, s0)


def _rms(x, g):
    return x * lax.rsqrt(jnp.mean(x * x, axis=-1, keepdims=True) + EPS) * g


def _mla_proj_kernel(cq_ref, ckv_ref, kr_ref, krs_ref, cos_ref, sin_ref, qg_ref, kvg_ref,
                     wq_ref, wqs_ref, wk_ref, wvt_ref, q_out, k_out, vt_out):
    cos = cos_ref[...]
    sin = sin_ref[...]
    cos_h = jnp.concatenate([cos] * MLA_HEADS, axis=1)
    sin_h = jnp.concatenate([sin] * MLA_HEADS, axis=1)
    xq = _rms(cq_ref[...], qg_ref[...]).astype(BF16)
    q = _dot(xq, wq_ref[...]) * cos_h + _dot(xq, wqs_ref[...]) * sin_h
    q_out[...] = (q * (MLA_SCALE * LOG2_E)).astype(BF16)
    xkv = _rms(ckv_ref[...], kvg_ref[...]).astype(BF16)
    k_rope = kr_ref[...] * cos + krs_ref[...] * sin
    k = _dot(xkv, wk_ref[...]) + jnp.concatenate([k_rope] * MLA_HEADS, axis=1)
    k_out[...] = k.astype(BF16)
    vt_out[0] = _dot_nt(wvt_ref[...], xkv).astype(BF16)


def _mla_proj(cq, ckv, kr, krs, cos_t, sin_t, qg, kvg, wq, wqs, wk, wvt, layer, tm):
    n = cq.shape[0]
    row = lambda i: (i, 0)
    full = lambda i: (0, 0)
    lay = lambda i: (layer, 0, 0)
    hq = MLA_HEADS * HEAD_LANES
    hv = MLA_HEADS * MLA_V
    return pl.pallas_call(
        _mla_proj_kernel,
        grid=(n // tm,),
        in_specs=[pl.BlockSpec((tm, MLA_Q_LORA), row), pl.BlockSpec((tm, MLA_KV_LORA), row),
                  pl.BlockSpec((tm, HEAD_LANES), row), pl.BlockSpec((tm, HEAD_LANES), row),
                  pl.BlockSpec((tm, HEAD_LANES), row), pl.BlockSpec((tm, HEAD_LANES), row),
                  pl.BlockSpec((1, MLA_Q_LORA), full), pl.BlockSpec((1, MLA_KV_LORA), full),
                  pl.BlockSpec((None, MLA_Q_LORA, hq), lay), pl.BlockSpec((None, MLA_Q_LORA, hq), lay),
                  pl.BlockSpec((None, MLA_KV_LORA, hq), lay), pl.BlockSpec((None, hv, MLA_KV_LORA), lay)],
        out_specs=[pl.BlockSpec((tm, hq), row), pl.BlockSpec((tm, hq), row),
                   pl.BlockSpec((1, hv, tm), lambda i: (i, 0, 0))],
        out_shape=[jax.ShapeDtypeStruct((n, hq), BF16), jax.ShapeDtypeStruct((n, hq), BF16),
                   jax.ShapeDtypeStruct((n // tm, hv, tm), BF16)],
        name="mla_proj",
        compiler_params=_cparams(("parallel",)),
    )(cq, ckv, kr, krs, cos_t, sin_t, qg, kvg, wq, wqs, wk, wvt)


ACC_ROWS = MLA_V + 16
LAT_UNROLL = 4
LOOKAHEAD = 2


def _attn_kernel(*refs, tkc, n_lat_chunks, n_ctx_chunks):
    if n_lat_chunks:
        q_ref, kl_ref, vl_ref, kc_ref, vc_ref, o_ref, m_s, acc_s, s_s = refs
    else:
        q_ref, kc_ref, vc_ref, o_ref, m_s, acc_s, s_s = refs
    m_s[...] = jnp.full(m_s.shape, -jnp.inf, F32)
    acc_s[...] = jnp.zeros(acc_s.shape, F32)
    ones = jnp.ones((ACC_ROWS - MLA_V, tkc), BF16)

    def scores(a, k_ref, r0):
        hs = slice(a * HEAD_LANES, (a + 1) * HEAD_LANES)
        return _dot_nt(k_ref[pl.ds(r0, tkc), hs], q_ref[:, hs])

    def accumulate(a, s_t, vt):
        m_old = m_s[a]
        m_new = jnp.maximum(m_old, jnp.max(s_t, axis=0, keepdims=True))
        alpha = jnp.exp2(m_old - m_new)
        p_t = jnp.exp2(s_t - m_new).astype(BF16)
        vta = jnp.concatenate([vt[a * MLA_V:(a + 1) * MLA_V], ones], axis=0)
        acc_s[a] = alpha * acc_s[a] + _dot(vta, p_t)
        m_s[a] = m_new

    def run(items, queue, after):
        upcoming = items[LOOKAHEAD:] + after
        queue = list(queue)
        for idx, (a, _, _, vt) in enumerate(items):
            if idx < len(upcoming):
                queue.append(scores(*upcoming[idx][:3]))
            accumulate(a, queue.pop(0), vt)
        return queue

    ahead_chunks = (LOOKAHEAD + 1) // 2

    def lat_items(c0, dynamic, count=LAT_UNROLL):
        items = []
        for u in range(count):
            cc = c0 + u
            r0 = pl.multiple_of(cc * tkc, tkc) if dynamic else cc * tkc
            vt = vl_ref[cc] if count == LAT_UNROLL else None
            items += [(0, kl_ref, r0, vt), (1, kl_ref, r0, vt)]
        return items

    ctx_items = []
    for c in range(n_ctx_chunks):
        ctx_items += [(0, kc_ref, c * tkc, vc_ref[c]), (1, kc_ref, c * tkc, vc_ref[c])]

    if n_lat_chunks:
        n_groups = n_lat_chunks // LAT_UNROLL
        for d, it in enumerate(lat_items(0, False, count=ahead_chunks)[:LOOKAHEAD]):
            s_s[d] = scores(*it[:3])

        def body(g, carry):
            c0 = g * LAT_UNROLL
            after = lat_items(c0 + LAT_UNROLL, True, count=ahead_chunks)[:LOOKAHEAD]
            queue = run(lat_items(c0, True), [s_s[d] for d in range(LOOKAHEAD)], after)
            for d in range(LOOKAHEAD):
                s_s[d] = queue[d]
            return carry

        lax.fori_loop(0, n_groups - 1, body, 0)
        run(lat_items((n_groups - 1) * LAT_UNROLL, False) + ctx_items, [s_s[d] for d in range(LOOKAHEAD)], [])
    else:
        run(ctx_items, [scores(*it[:3]) for it in ctx_items[:LOOKAHEAD]], [])
    outs = []
    for a in range(2):
        acc = acc_s[a]
        outs.append(acc[0:MLA_V] / acc[MLA_V:MLA_V + 1])
    o_ref[...] = jnp.concatenate(outs, axis=0).T.astype(o_ref.dtype)


def _attention(q, k, vt, *, bsz, n_q, q_base, n_lat, n_ctx, ctx_base, tq, tkc, with_lat):
    pairs = MLA_HEADS // 2
    qblocks = n_q // tq
    qb0 = q_base // tq
    cb0 = ctx_base // n_ctx
    kw = 2 * HEAD_LANES
    vw = 2 * MLA_V
    in_specs = [pl.BlockSpec((tq, kw), lambda b, p, i: (qb0 + b * qblocks + i, p))]
    args = [q]
    if with_lat:
        in_specs += [pl.BlockSpec((n_lat, kw), lambda b, p, i: (b, p)),
                     pl.BlockSpec((n_lat // tkc, vw, tkc), lambda b, p, i: (b, p, 0))]
        args += [k, vt]
    in_specs += [pl.BlockSpec((n_ctx, kw), lambda b, p, i: (cb0 + b, p)),
                 pl.BlockSpec((n_ctx // tkc, vw, tkc), lambda b, p, i: (cb0 + b, p, 0))]
    args += [k, vt]
    kern = functools.partial(_attn_kernel, tkc=tkc, n_lat_chunks=(n_lat // tkc) if with_lat else 0,
                             n_ctx_chunks=n_ctx // tkc)
    return pl.pallas_call(
        kern,
        grid=(bsz, pairs, qblocks),
        in_specs=in_specs,
        out_specs=pl.BlockSpec((tq, vw), lambda b, p, i: (b * qblocks + i, p)),
        out_shape=jax.ShapeDtypeStruct((bsz * n_q, MLA_HEADS * MLA_V), BF16),
        scratch_shapes=[pltpu.VMEM((2, 1, tq), F32), pltpu.VMEM((2, ACC_ROWS, tq), F32),
                        pltpu.VMEM((LOOKAHEAD, tkc, tq), F32)],
        name="attn_lat" if with_lat else "attn_ctx",
        compiler_params=_cparams(("parallel", "parallel", "arbitrary")),
    )(*args)


def _outproj_kernel(a_ref, ofl_ref, obl_ref, cl_ref, ofc_ref, obc_ref, cc_ref, hg_ref, x_ref, g1_ref, sh2_ref,
                    sc2_ref, lng_ref, lnb_ref, ng_ref, gavg_ref, w_ref, wr_ref, x1_out, h2_out, lg_out,
                    *, alpha, lat_tiles):
    is_ctx = pl.program_id(0) >= lat_tiles
    o = jnp.where(is_ctx, ofc_ref[...] + obc_ref[...], ofl_ref[...] + obl_ref[...])
    c_att = jnp.where(is_ctx, cc_ref[...], cl_ref[...])
    o2_hi, o2_lo = _split_bf16(o * o)
    ms = _dot(o2_hi, gavg_ref[...]) + _dot(o2_lo, gavg_ref[...])
    b_lat = o * lax.rsqrt(ms + EPS) * ng_ref[...] * _silu(hg_ref[...])
    m = (_dot(a_ref[...].astype(BF16), w_ref[0:POOL_DIM, :])
         + _dot(b_lat.astype(BF16), w_ref[POOL_DIM:POOL_DIM + HG_V, :])
         + _dot(c_att, w_ref[POOL_DIM + HG_V:, :]))
    x1 = _layer_norm(alpha * x_ref[...] + g1_ref[0] * m) * lng_ref[...] + lnb_ref[...]
    x1_out[...] = x1
    h2 = _layer_norm(x1) * (1.0 + sc2_ref[0]) + sh2_ref[0]
    h2_out[...] = h2
    lg_out[...] = _dot3(h2, wr_ref[...])


def _out_proj(a, lat_parts, ctx_parts, hg, x, g1, sh2, sc2, ln_g, ln_b, norm_g, gavg, w_out, w_router, layer, tm,
              mod_index, alpha):
    n, d = x.shape
    lat_tiles = lat_parts[0].shape[0] // tm
    row = lambda i: (i, 0)
    lat_row = lambda i: (jnp.minimum(i, lat_tiles - 1), 0)
    ctx_row = lambda i: (jnp.maximum(i - lat_tiles, 0), 0)
    full = lambda i: (0, 0)
    lay = lambda i: (layer, 0, 0)
    mod = lambda i: (mod_index(i), 0, 0)
    hv = MLA_HEADS * MLA_V
    return pl.pallas_call(
        functools.partial(_outproj_kernel, alpha=alpha, lat_tiles=lat_tiles),
        grid=(n // tm,),
        in_specs=[pl.BlockSpec((tm, POOL_DIM), row),
                  pl.BlockSpec((tm, HG_V), lat_row), pl.BlockSpec((tm, HG_V), lat_row), pl.BlockSpec((tm, hv), lat_row),
                  pl.BlockSpec((tm, HG_V), ctx_row), pl.BlockSpec((tm, HG_V), ctx_row), pl.BlockSpec((tm, hv), ctx_row),
                  pl.BlockSpec((tm, HG_V), row),
                  pl.BlockSpec((tm, d), row),
                  pl.BlockSpec((1, 1, d), mod), pl.BlockSpec((1, 1, d), mod), pl.BlockSpec((1, 1, d), mod),
                  pl.BlockSpec((1, d), full), pl.BlockSpec((1, d), full), pl.BlockSpec((1, HG_V), full),
                  pl.BlockSpec((HG_V, HG_V), full), pl.BlockSpec((None, d, d), lay),
                  pl.BlockSpec((None, d, ROUTER_LANES), lay)],
        out_specs=[pl.BlockSpec((tm, d), row), pl.BlockSpec((tm, d), row), pl.BlockSpec((tm, ROUTER_LANES), row)],
        out_shape=[jax.ShapeDtypeStruct((n, d), F32), jax.ShapeDtypeStruct((n, d), F32),
                   jax.ShapeDtypeStruct((n, ROUTER_LANES), F32)],
        name="out_proj",
        compiler_params=_cparams(("parallel",)),
    )(a, *lat_parts, *ctx_parts, hg, x, g1, sh2, sc2, ln_g, ln_b, norm_g, gavg, w_out, w_router)


def _moe_kernel(be_ref, nv_ref, x_ref, w1_ref, w3_ref, w2_ref, y_ref, w1b, w3b, w2b):
    i = pl.program_id(0)

    @pl.when((i == 0) | (be_ref[i] != be_ref[jnp.maximum(i - 1, 0)]))
    def _():
        w1b[...] = w1_ref[0, 0].astype(BF16)
        w3b[...] = w3_ref[0, 0].astype(BF16)
        w2b[...] = w2_ref[0, 0].astype(BF16)

    @pl.when(i < nv_ref[0])
    def _():
        x = x_ref[...].astype(BF16)
        h1 = _dot(x, w1b[...])
        h3 = _dot(x, w3b[...])
        h = (_silu(h1) * h3).astype(BF16)
        y_ref[...] = _dot(h, w2b[...]).astype(y_ref.dtype)

    @pl.when(i >= nv_ref[0])
    def _():
        y_ref[...] = jnp.zeros(y_ref.shape, y_ref.dtype)


def _moe_ffn(block_e, n_valid, xr, w1, w3, w2, layer, tm):
    n_rows, d = xr.shape
    n_blocks = n_rows // tm
    grid_spec = pltpu.PrefetchScalarGridSpec(
        num_scalar_prefetch=2,
        grid=(n_blocks,),
        in_specs=[pl.BlockSpec((tm, d), lambda i, be, nv: (jnp.minimum(i, nv[0] - 1), 0)),
                  pl.BlockSpec((1, 1, d, D_EXPERT), lambda i, be, nv: (layer, be[i], 0, 0)),
                  pl.BlockSpec((1, 1, d, D_EXPERT), lambda i, be, nv: (layer, be[i], 0, 0)),
                  pl.BlockSpec((1, 1, D_EXPERT, d), lambda i, be, nv: (layer, be[i], 0, 0))],
        out_specs=pl.BlockSpec((tm, d), lambda i, be, nv: (i, 0)),
        scratch_shapes=[pltpu.VMEM((d, D_EXPERT), BF16), pltpu.VMEM((d, D_EXPERT), BF16),
                        pltpu.VMEM((D_EXPERT, d), BF16)],
    )
    return pl.pallas_call(
        _moe_kernel,
        grid_spec=grid_spec,
        out_shape=jax.ShapeDtypeStruct((n_rows, d), BF16),
        name="moe_ffn",
        compiler_params=_cparams(("arbitrary",)),
    )(block_e, n_valid, xr, w1, w3, w2)


def _combine_kernel(x_ref, y0_ref, y1_ref, gw_ref, g2_ref, lng_ref, lnb_ref, o_ref, *, alpha):
    gw = gw_ref[...]
    y = gw[:, 0:1] * y0_ref[...].astype(F32) + gw[:, 1:2] * y1_ref[...].astype(F32)
    o_ref[...] = _layer_norm(alpha * x_ref[...] + g2_ref[0] * y) * lng_ref[...] + lnb_ref[...]


def _combine(x, y0, y1, gw, g2, ln_g, ln_b, tm, mod_index, alpha):
    n, d = x.shape
    row = lambda i: (i, 0)
    full = lambda i: (0, 0)
    return pl.pallas_call(
        functools.partial(_combine_kernel, alpha=alpha),
        grid=(n // tm,),
        in_specs=[pl.BlockSpec((tm, d), row), pl.BlockSpec((tm, d), row), pl.BlockSpec((tm, d), row),
                  pl.BlockSpec((tm, ROUTER_LANES), row),
                  pl.BlockSpec((1, 1, d), lambda i: (mod_index(i), 0, 0)),
                  pl.BlockSpec((1, d), full), pl.BlockSpec((1, d), full)],
        out_specs=pl.BlockSpec((tm, d), row),
        out_shape=jax.ShapeDtypeStruct((n, d), F32),
        name="moe_combine",
        compiler_params=_cparams(("parallel",)),
    )(x, y0, y1, gw, g2, ln_g, ln_b)


def _rot_half_cols(w):
    lead = w.shape[:-1]
    wr = w.reshape(lead + (2, 2, MLA_ROPE // 4))
    return jnp.stack([-wr[..., 1, :], wr[..., 0, :]], axis=-2).reshape(w.shape)


def _rope_tables(n_lat, n_ctx, bsz):
    rows = n_lat // GRID_W
    row = jnp.repeat(jnp.arange(rows, dtype=F32), GRID_W)
    col = jnp.tile(jnp.arange(GRID_W, dtype=F32), rows)
    n_freq = MLA_ROPE // 4
    inv_freq = ROPE_BASE ** (-jnp.arange(n_freq, dtype=F32) / n_freq)
    ang = jnp.stack([row[:, None] * inv_freq, col[:, None] * inv_freq], axis=1)
    cos = jnp.broadcast_to(jnp.cos(ang)[:, :, None, :], (n_lat, 2, 2, n_freq)).reshape(n_lat, MLA_ROPE)
    sin = jnp.broadcast_to(jnp.sin(ang)[:, :, None, :], (n_lat, 2, 2, n_freq)).reshape(n_lat, MLA_ROPE)
    pad_l = MLA_NOPE
    pad_r = HEAD_LANES - MLA_QK
    cos_t = jnp.concatenate([jnp.ones((n_lat, pad_l), F32), cos, jnp.ones((n_lat, pad_r), F32)], axis=1)
    sin_t = jnp.concatenate([jnp.zeros((n_lat, pad_l), F32), sin, jnp.zeros((n_lat, pad_r), F32)], axis=1)
    cos_all = jnp.concatenate([jnp.tile(cos_t, (bsz, 1)), jnp.ones((bsz * n_ctx, HEAD_LANES), F32)], axis=0)
    sin_all = jnp.concatenate([jnp.tile(sin_t, (bsz, 1)), jnp.zeros((bsz * n_ctx, HEAD_LANES), F32)], axis=0)
    return cos_all, sin_all


def _prep_w_in(w_in):
    depth, d, _ = w_in.shape
    kr = w_in[:, :, W_IN_MAIN:]
    z_l = jnp.zeros((depth, d, MLA_NOPE), w_in.dtype)
    z_r = jnp.zeros((depth, d, HEAD_LANES - MLA_QK), w_in.dtype)
    return jnp.concatenate([w_in[:, :, :W_IN_MAIN], z_l, kr, z_r, z_l, _rot_half_cols(kr), z_r],
                           axis=-1).astype(BF16)


def _prep_mla(w_uq, w_ukv):
    depth = w_uq.shape[0]
    wq = w_uq.reshape(depth, MLA_Q_LORA, MLA_HEADS, MLA_QK)
    pad = jnp.zeros((depth, MLA_Q_LORA, MLA_HEADS, HEAD_LANES - MLA_QK), w_uq.dtype)
    wq_p = jnp.concatenate([wq, pad], axis=-1).reshape(depth, MLA_Q_LORA, MLA_HEADS * HEAD_LANES)
    wq_s = jnp.concatenate([jnp.zeros_like(wq[..., :MLA_NOPE]), _rot_half_cols(wq[..., MLA_NOPE:]), pad],
                           axis=-1).reshape(depth, MLA_Q_LORA, MLA_HEADS * HEAD_LANES)
    wkv = w_ukv.reshape(depth, MLA_KV_LORA, MLA_HEADS, MLA_NOPE + MLA_V)
    kpad = jnp.zeros((depth, MLA_KV_LORA, MLA_HEADS, HEAD_LANES - MLA_NOPE), w_ukv.dtype)
    wk = jnp.concatenate([wkv[..., :MLA_NOPE], kpad], axis=-1).reshape(depth, MLA_KV_LORA, MLA_HEADS * HEAD_LANES)
    wvt = jnp.swapaxes(wkv[..., MLA_NOPE:].reshape(depth, MLA_KV_LORA, MLA_HEADS * MLA_V), 1, 2)
    return wq_p.astype(BF16), wq_s.astype(BF16), wk.astype(BF16), wvt.astype(BF16)


def _block_diag(pool_w):
    depth, g, c, _ = pool_w.shape
    eye = jnp.eye(g, dtype=pool_w.dtype)
    return jnp.einsum("lgcd,gh->lgchd", pool_w, eye).reshape(depth, g * c, g * c).astype(BF16)


def _route(logits, b_rg, b_re, tm_moe):
    n = logits.shape[0]
    lg = logits[:, :N_GROUPS] + b_rg
    le = (logits[:, N_GROUPS:N_GROUPS + N_EXPERTS] + b_re).reshape(n, N_GROUPS, EXPERTS_PER_GROUP)
    pg = jax.nn.softmax(lg, axis=-1)
    g_sel = jnp.argmax(lg, axis=-1).astype(jnp.int32)
    g_hot = g_sel[:, None] == jnp.arange(N_GROUPS, dtype=jnp.int32)[None, :]
    w_g = jnp.sum(jnp.where(g_hot, pg, 0.0), axis=-1)
    le_sel = jnp.sum(jnp.where(g_hot[:, :, None], le, 0.0), axis=1)
    top_v, top_i = lax.top_k(le_sel, TOP_K)
    gate = jax.nn.softmax(top_v, axis=-1) * w_g[:, None]
    expert = (g_sel[:, None] * EXPERTS_PER_GROUP + top_i).astype(jnp.int32)

    n_assign = n * TOP_K
    e_flat = expert.reshape(-1)
    hot = e_flat[:, None] == jnp.arange(N_EXPERTS, dtype=jnp.int32)[None, :]
    nb = n_assign // COUNT_BLOCK
    tri = jnp.tril(jnp.ones((COUNT_BLOCK, COUNT_BLOCK), BF16))
    win = jnp.einsum("ts,bse->bte", tri, hot.astype(BF16).reshape(nb, COUNT_BLOCK, N_EXPERTS),
                     preferred_element_type=F32)
    bsum = win[:, -1, :]
    boff = jnp.cumsum(bsum, axis=0) - bsum
    csum = (win + boff[:, None, :]).reshape(n_assign, N_EXPERTS).astype(jnp.int32)
    counts = (boff[-1] + bsum[-1]).astype(jnp.int32)
    pcounts = (counts + tm_moe - 1) // tm_moe * tm_moe
    pends = jnp.cumsum(pcounts)
    pstarts = pends - pcounts
    starts = jnp.cumsum(counts) - counts
    dest = jnp.sum(jnp.where(hot, csum - 1 + pstarts[None, :], 0), axis=1)
    n_rows = (n_assign + N_EXPERTS * (tm_moe - 1) + tm_moe - 1) // tm_moe * tm_moe
    n_blocks = n_rows // tm_moe
    n_valid = (pends[-1] // tm_moe).astype(jnp.int32)
    blk = jnp.arange(n_blocks, dtype=jnp.int32)
    first_row = jnp.minimum(blk, n_valid - 1) * tm_moe
    block_e = jnp.sum((pends[None, :] <= first_row[:, None]).astype(jnp.int32), axis=1)
    block_e = jnp.clip(block_e, 0, N_EXPERTS - 1)
    a_sorted = jnp.sort(e_flat * n_assign + jnp.arange(n_assign, dtype=jnp.int32)) % n_assign
    pos = (blk * tm_moe - pstarts[block_e])[:, None] + jnp.arange(tm_moe, dtype=jnp.int32)[None, :]
    src = jnp.clip(starts[block_e][:, None] + pos, 0, n_assign - 1)
    filler = (blk[:, None] * tm_moe + jnp.arange(tm_moe, dtype=jnp.int32)[None, :]) % n
    row_tok = jnp.where(pos < counts[block_e][:, None], a_sorted[src] // TOP_K, filler)
    gw = jnp.zeros((n, ROUTER_LANES), F32).at[:, :TOP_K].set(gate)
    return row_tok.reshape(n_rows), dest.reshape(n, TOP_K), gw, block_e, n_valid.reshape(1)


def _tile(n, pref):
    t = pref
    while n % t:
        t //= 2
    return t


def kernel(x, c, ctx, c_ctx, w_mod, b_mod, w_in, pool_w, pool_scale, hg_lb_logits, hg_norm_g, mla_q_norm_g, mla_w_uq, mla_kv_norm_g, mla_w_ukv, w_out, ln1_g, ln1_b, ln2_g, ln2_b, router_group_w, router_group_b, router_expert_w, router_expert_b, expert_w1, expert_w3, expert_w2):
    bsz, n_lat, d = x.shape
    n_ctx = ctx.shape[1]
    depth = w_mod.shape[0]
    alpha = (2 * depth) ** 0.25
    lat_rows = bsz * n_lat
    ctx_rows = bsz * n_ctx
    n_tok = lat_rows + ctx_rows

    tm = _tile(np.gcd(n_lat, ctx_rows), 512)
    tp = _tile(np.gcd(n_lat, n_ctx), 256)
    tb_lat = _tile(n_lat, 512)
    tb_ctx = _tile(n_ctx, 512)
    tq_lat = _tile(n_lat, 512)
    tkc = _tile(np.gcd(n_lat, n_ctx), 256)
    tq_ctx = _tile(n_ctx, 512)
    tm_moe = 256
    lat_tiles_per_batch = n_lat // tm
    mod_index = lambda i: jnp.minimum(i // lat_tiles_per_batch, bsz)

    c_rows = jnp.concatenate([jax.nn.silu(c), jax.nn.silu(c_ctx)[None, :],
                              jnp.zeros((-(bsz + 1) % 8, d), F32)], axis=0)
    mod_all = _mod_all(c_rows, w_mod, b_mod)
    lb_all = jnp.cumsum(jax.nn.softmax(hg_lb_logits.astype(F32), axis=0), axis=0)
    lb_all = lb_all - lb_all[0:1]
    log_lb = jnp.log(lb_all)
    log_1m_lb = jnp.log1p(-lb_all)
    w_aug = _prep_w_in(w_in)
    wq_p, wq_s, wk_p, wvt_p = _prep_mla(mla_w_uq, mla_w_ukv)
    w_bd = _block_diag(pool_w)
    w_out_b = w_out.astype(BF16)
    w_router = jnp.concatenate([router_group_w, router_expert_w,
                                jnp.zeros((depth, d, ROUTER_LANES - N_GROUPS - N_EXPERTS), F32)], axis=-1)
    gavg = jnp.kron(jnp.eye(HG_HEADS, dtype=F32), jnp.full((HG_DV, HG_DV), 1.0 / HG_DV, F32)).astype(BF16)
    cos_t, sin_t = _rope_tables(n_lat, n_ctx, bsz)
    zero_state = jnp.zeros((bsz, HG_HEADS, HG_DV, HG_DK), F32)

    xt = jnp.concatenate([x.reshape(lat_rows, d), ctx.reshape(ctx_rows, d)], axis=0)
    for l in range(depth):
        mod = mod_all[l, :bsz + 1].reshape(bsz + 1, 6, 1, d)
        sh1, sc1, g1, sh2, sc2, g2 = (mod[:, k] for k in range(6))

        p_in, hq, hzf, hzb, hin, hg, cq, ckv, kr, krs = _in_proj(xt, sh1, sc1, w_aug, l, tm, mod_index)

        a_mix = _pool(p_in, w_bd[l], pool_scale[l][None, :], tp, lat_rows, n_lat, n_ctx)

        la_f, la_b = log_lb[l, 0][None, :], log_lb[l, 1][None, :]
        lc_f, lc_b = log_1m_lb[l, 0][None, :], log_1m_lb[l, 1][None, :]
        oc_f, s_f = _hgrn_scan(hq, hzf, hin, la_f, lc_f, zero_state, reverse=False, base_row=lat_rows,
                               n_seq=n_ctx, tb=tb_ctx)
        oc_b, s_b = _hgrn_scan(hq, hzb, hin, la_b, lc_b, zero_state, reverse=True, base_row=lat_rows,
                               n_seq=n_ctx, tb=tb_ctx)
        ol_f, _ = _hgrn_scan(hq, hzf, hin, la_f, lc_f, s_f, reverse=False, base_row=0, n_seq=n_lat, tb=tb_lat)
        ol_b, _ = _hgrn_scan(hq, hzb, hin, la_b, lc_b, s_b, reverse=True, base_row=0, n_seq=n_lat, tb=tb_lat)

        q_all, k_all, vt_all = _mla_proj(cq, ckv, kr, krs, cos_t, sin_t, mla_q_norm_g[l][None, :],
                                         mla_kv_norm_g[l][None, :], wq_p, wq_s, wk_p, wvt_p, l, tkc)
        c_lat = _attention(q_all, k_all, vt_all, bsz=bsz, n_q=n_lat, q_base=0, n_lat=n_lat, n_ctx=n_ctx,
                           ctx_base=lat_rows, tq=tq_lat, tkc=tkc, with_lat=True)
        c_ctx_o = _attention(q_all, k_all, vt_all, bsz=bsz, n_q=n_ctx, q_base=lat_rows, n_lat=n_lat, n_ctx=n_ctx,
                             ctx_base=lat_rows, tq=tq_ctx, tkc=tkc, with_lat=False)

        x1, h2, logits = _out_proj(a_mix, (ol_f, ol_b, c_lat), (oc_f, oc_b, c_ctx_o), hg, xt, g1, sh2, sc2,
                                   ln1_g[l][None, :],
                                   ln1_b[l][None, :], hg_norm_g[l][None, :], gavg, w_out_b, w_router, l,
                                   tm, mod_index, alpha)

        row_tok, dest, gw, block_e, n_valid = _route(logits, router_group_b[l], router_expert_b[l], tm_moe)
        xr = jnp.take(h2, row_tok, axis=0, mode="clip")
        y = _moe_ffn(block_e, n_valid, xr, expert_w1, expert_w3, expert_w2, l, tm_moe)
        y0 = jnp.take(y, dest[:, 0], axis=0, mode="clip")
        y1 = jnp.take(y, dest[:, 1], axis=0, mode="clip")
        xt = _combine(x1, y0, y1, gw, g2, ln2_g[l][None, :], ln2_b[l][None, :], tm, mod_index, alpha)
    return xt[:lat_rows].reshape(bsz, n_lat, d)
```

```python
import functools

import numpy as np
import jax
import jax.numpy as jnp
from jax import lax
from jax.experimental import pallas as pl
from jax.experimental.pallas import tpu as pltpu

F32 = jnp.float32
BF16 = jnp.bfloat16

D_MODEL = 1024
GRID_W = 64
POOL_WINDOWS = (2, 4, 8, 16)
POOL_DIM = 256
POOL_GROUP_DIM = 64
HG_HEADS = 4
HG_DK = 128
HG_DV = 64
HG_QK = HG_HEADS * HG_DK
HG_V = HG_HEADS * HG_DV
HG_CHUNK = 64
HG_SUB_LOG2 = 4
HG_SUB = 1 << HG_SUB_LOG2
MLA_HEADS = 8
MLA_NOPE = 64
MLA_ROPE = 32
MLA_V = 64
MLA_Q_LORA = 256
MLA_KV_LORA = 128
MLA_QK = MLA_NOPE + MLA_ROPE
MLA_SCALE = MLA_QK ** -0.5
LOG2_E = 1.4426950408889634
ROPE_BASE = 10000.0
HEAD_LANES = 128
N_GROUPS = 4
EXPERTS_PER_GROUP = 8
N_EXPERTS = N_GROUPS * EXPERTS_PER_GROUP
TOP_K = 2
D_EXPERT = 512
EPS = 1e-6
ROUTER_LANES = 128
COUNT_BLOCK = 128
OUT_ROW_SPLIT = 2
IN_SEGS = (("p_in", 256), ("hq", 512), ("hzf", 512), ("hzb", 512), ("hin", 256), ("hg", 256),
           ("cq", 256), ("ckv", 128), ("kr", 128), ("krs", 128))
IN_AUG = sum(w for _, w in IN_SEGS)
W_IN_MAIN = 2688

VMEM_LIMIT = 56 * 1024 * 1024


def _cparams(sem):
    return pltpu.CompilerParams(dimension_semantics=sem, vmem_limit_bytes=VMEM_LIMIT)


def _split_bf16(a):
    hi = a.astype(BF16)
    lo = (a - hi.astype(F32)).astype(BF16)
    return hi, lo


def _dot(a, b):
    return jnp.dot(a, b, preferred_element_type=F32)


def _dot_nt(a, b):
    return lax.dot_general(a, b, (((1,), (1,)), ((), ())), preferred_element_type=F32)


def _dot_tn(a, b):
    return lax.dot_general(a, b, (((0,), (0,)), ((), ())), preferred_element_type=F32)


def _dot3(a, b):
    a_hi, a_lo = _split_bf16(a)
    b_hi, b_lo = _split_bf16(b)
    return _dot(a_hi, b_hi) + _dot(a_hi, b_lo) + _dot(a_lo, b_hi)


def _dot_exact_rhs(a, b_bf16):
    a_hi = a.astype(BF16)
    r1 = a - a_hi.astype(F32)
    a_mid = r1.astype(BF16)
    a_lo = (r1 - a_mid.astype(F32)).astype(BF16)
    return _dot(a_hi, b_bf16) + _dot(a_mid, b_bf16) + _dot(a_lo, b_bf16)


def _layer_norm(x):
    mu = jnp.mean(x, axis=-1, keepdims=True)
    xc = x - mu
    var = jnp.mean(xc * xc, axis=-1, keepdims=True)
    return xc * lax.rsqrt(var + EPS)


def _sigmoid(x):
    return 1.0 / (1.0 + jnp.exp(-x))


def _silu(x):
    return x * _sigmoid(x)


def _mod_kernel(a_ref, w_ref, b_ref, o_ref):
    o_ref[0] = _dot3(a_ref[...], w_ref[0]) + b_ref[0]


def _mod_all(c_rows, w_mod, b_mod):
    depth, d, n6 = w_mod.shape
    tn = 1536
    rows = c_rows.shape[0]
    return pl.pallas_call(
        _mod_kernel,
        grid=(depth, n6 // tn),
        in_specs=[pl.BlockSpec((rows, d), lambda l, j: (0, 0)),
                  pl.BlockSpec((1, d, tn), lambda l, j: (l, 0, j)),
                  pl.BlockSpec((1, 1, tn), lambda l, j: (l, 0, j))],
        out_specs=pl.BlockSpec((1, rows, tn), lambda l, j: (l, 0, j)),
        out_shape=jax.ShapeDtypeStruct((depth, rows, n6), F32),
        name="adaln_mod",
        compiler_params=_cparams(("arbitrary", "arbitrary")),
    )(c_rows, w_mod, b_mod.reshape(depth, 1, n6))


def _inproj_kernel(x_ref, sh_ref, sc_ref, w_ref, *out_refs):
    h = _layer_norm(x_ref[...]) * (1.0 + sc_ref[0]) + sh_ref[0]
    hb = h.astype(BF16)
    start = 0
    for (_, width), o_ref in zip(IN_SEGS, out_refs):
        o_ref[...] = _dot(hb, w_ref[:, start:start + width])
        start += width


def _in_proj(x, sh, sc, w_aug, layer, tm, mod_index):
    n, d = x.shape
    row = lambda i: (i, 0)
    mod = lambda i: (mod_index(i), 0, 0)
    return pl.pallas_call(
        _inproj_kernel,
        grid=(n // tm,),
        in_specs=[pl.BlockSpec((tm, d), row),
                  pl.BlockSpec((1, 1, d), mod),
                  pl.BlockSpec((1, 1, d), mod),
                  pl.BlockSpec((None, d, IN_AUG), lambda i: (layer, 0, 0))],
        out_specs=[pl.BlockSpec((tm, w), row) for _, w in IN_SEGS],
        out_shape=[jax.ShapeDtypeStruct((n, w), F32) for _, w in IN_SEGS],
        name="in_proj",
        compiler_params=_cparams(("parallel",)),
    )(x, sh, sc, w_aug)


POOL_HALO = 8


def _pool_kernel(x_ref, prev_ref, next_ref, w_ref, scale_ref, o_ref, *, tp, lat_tiles, tpb_lat, tpb_ctx,
                 n_lat, n_ctx):
    i = pl.program_id(0)
    is_lat = i < lat_tiles
    tile_in_seq = jnp.where(is_lat, i % tpb_lat, (i - lat_tiles) % tpb_ctx)
    tiles_in_seq = jnp.where(is_lat, tpb_lat, tpb_ctx)
    n_seq = jnp.where(is_lat, n_lat, n_ctx)
    first = tile_in_seq == 0
    last = tile_in_seq == tiles_in_seq - 1
    x = x_ref[...]
    prev = jnp.where(first, 0.0, prev_ref[...])
    nxt = jnp.where(last, 0.0, next_ref[...])
    ext = jnp.concatenate([prev, x, nxt], axis=0)
    n_ext = tp + 2 * POOL_HALO

    def back(a, k):
        return pltpu.roll(a, k, 0)

    def fwd(a, k):
        return pltpu.roll(a, n_ext - k, 0)

    e2 = ext + back(ext, 1)
    e4 = back(e2, 1) + fwd(e2, 1)
    e8 = back(e4, 2) + fwd(e4, 2)
    e16 = back(e8, 4) + fwd(e8, 4)
    sl = slice(POOL_HALO, POOL_HALO + tp)
    lane = lax.broadcasted_iota(jnp.int32, (tp, POOL_DIM), 1)
    grp = lane >> 6
    wsum = jnp.where(grp == 0, e2[sl], jnp.where(grp == 1, e4[sl], jnp.where(grp == 2, e8[sl], e16[sl])))
    half = jnp.where(grp == 0, 1, jnp.where(grp == 1, 2, jnp.where(grp == 2, 4, 8)))
    t = tile_in_seq * tp + lax.broadcasted_iota(jnp.int32, (tp, POOL_DIM), 0)
    cnt = jnp.minimum(t + half, n_seq) - jnp.maximum(t - half, 0)
    pooled = wsum / cnt.astype(F32) - x
    o_ref[...] = _dot(pooled.astype(BF16), w_ref[...]) * scale_ref[...]


def _pool(p_in, w_bd, scale, tp, n_lat_rows, n_lat, n_ctx):
    n, c = p_in.shape
    hb = tp // POOL_HALO
    last_halo_block = n // POOL_HALO - 1
    kern = functools.partial(_pool_kernel, tp=tp, lat_tiles=n_lat_rows // tp, tpb_lat=n_lat // tp,
                             tpb_ctx=n_ctx // tp, n_lat=n_lat, n_ctx=n_ctx)
    return pl.pallas_call(
        kern,
        grid=(n // tp,),
        in_specs=[pl.BlockSpec((tp, c), lambda i: (i, 0)),
                  pl.BlockSpec((POOL_HALO, c), lambda i: (jnp.maximum(i * hb - 1, 0), 0)),
                  pl.BlockSpec((POOL_HALO, c), lambda i: (jnp.minimum((i + 1) * hb, last_halo_block), 0)),
                  pl.BlockSpec((c, c), lambda i: (0, 0)),
                  pl.BlockSpec((1, c), lambda i: (0, 0))],
        out_specs=pl.BlockSpec((tp, c), lambda i: (i, 0)),
        out_shape=jax.ShapeDtypeStruct((n, c), F32),
        name="pool",
        compiler_params=_cparams(("parallel",)),
    )(p_in, p_in, p_in, w_bd, scale)


def _hgrn_kernel(hq_ref, hz_ref, hin_ref, la_ref, lc_ref, s0_ref, o_ref, sT_ref, q_s, g_s, k_s, st_s,
                 *, reverse, nchunk):
    j = pl.program_id(1)

    @pl.when(j == 0)
    def _():
        st_s[...] = s0_ref[0]

    z = hz_ref[...]
    ls = jnp.minimum(z, 0.0) - jnp.log1p(jnp.exp(-jnp.abs(z)))
    lc = lc_ref[...]
    x2 = lc + ls
    la = la_ref[...]
    g_s[...] = jnp.maximum(la, x2) + jnp.log1p(jnp.exp(-jnp.abs(la - x2)))
    k_s[...] = jnp.exp(x2 - z)
    q_s[...] = _silu(hq_ref[...])

    C = HG_CHUNK
    row = lax.broadcasted_iota(jnp.int32, (C, C), 0)
    col = lax.broadcasted_iota(jnp.int32, (C, C), 1)
    tri = ((row <= col) if reverse else (row >= col)).astype(BF16)
    rmod = lax.broadcasted_iota(jnp.int32, (C, 1), 0) & (HG_SUB - 1)
    nsub = C // HG_SUB
    row_sub = row >> HG_SUB_LOG2
    col_sub = col >> HG_SUB_LOG2

    def chunk(c, carry):
        cc = (nchunk - 1 - c) if reverse else c
        r0 = pl.multiple_of(cc * C, C)
        b_all = _dot_exact_rhs_lhs(tri, g_s[pl.ds(r0, C), :])
        q_all = q_s[pl.ds(r0, C), :]
        k_all = k_s[pl.ds(r0, C), :]
        v_all = hin_ref[pl.ds(r0, C), :]
        outs = []
        for h in range(HG_HEADS):
            ks = slice(h * HG_DK, (h + 1) * HG_DK)
            b, q, k = b_all[:, ks], q_all[:, ks], k_all[:, ks]
            v = v_all[:, h * HG_DV:(h + 1) * HG_DV]
            st = st_s[h]
            b_last = b[0:1] if reverse else b[C - 1:C]
            o = _dot_nt((q * jnp.exp(b)).astype(BF16), st.astype(BF16))
            a_rows = []
            for i in range(nsub):
                rs = slice(i * HG_SUB, (i + 1) * HG_SUB)
                if (reverse and i == nsub - 1) or (not reverse and i == 0):
                    a_rows.append(jnp.zeros((HG_SUB, C), F32))
                    continue
                m = b[(i + 1) * HG_SUB - 1:(i + 1) * HG_SUB] if reverse else b[i * HG_SUB:i * HG_SUB + 1]
                qi = q[rs] * jnp.exp(b[rs] - m)
                ksc = k * jnp.exp(m - b)
                a_rows.append(_dot_nt(qi.astype(BF16), ksc.astype(BF16)))
            a = jnp.concatenate(a_rows, axis=0)
            if reverse:
                a = jnp.where(col_sub > row_sub, a, 0.0)
            else:
                a = jnp.where(col_sub < row_sub, a, 0.0)
            o = o + _dot(a.astype(BF16), v.astype(BF16))
            for delta in range(HG_SUB):
                if delta == 0:
                    w = jnp.sum(q * k, axis=-1, keepdims=True)
                    o = o + w * v
                    continue
                shift = (C - delta) if reverse else delta
                kd = pltpu.roll(k, shift, 0)
                bd = pltpu.roll(b, shift, 0)
                vd = pltpu.roll(v, shift, 0)
                e = jnp.exp(b - bd)
                w = jnp.sum(q * kd * e, axis=-1, keepdims=True)
                valid = (rmod + delta < HG_SUB) if reverse else (rmod >= delta)
                o = o + jnp.where(valid, w, 0.0) * vd
            outs.append(o)
            kdec = k * jnp.exp(b_last - b)
            st_s[h] = st * jnp.exp(b_last) + _dot_tn(v.astype(BF16), kdec.astype(BF16))
        o_ref[pl.ds(r0, C), :] = jnp.concatenate(outs, axis=1)
        return carry

    lax.fori_loop(0, nchunk, chunk, 0)

    @pl.when(j == pl.num_programs(1) - 1)
    def _():
        sT_ref[0] = st_s[...]


def _dot_exact_rhs_lhs(tri_bf16, g):
    g_hi = g.astype(BF16)
    r1 = g - g_hi.astype(F32)
    g_mid = r1.astype(BF16)
    g_lo = (r1 - g_mid.astype(F32)).astype(BF16)
    return _dot(tri_bf16, g_hi) + _dot(tri_bf16, g_mid) + _dot(tri_bf16, g_lo)


def _hgrn_scan(hq, hz, hin, la, lc, s0, *, reverse, base_row, n_seq, tb):
    bsz = s0.shape[0]
    nblk = n_seq // tb
    base_blk = base_row // tb

    def rows(b, j):
        jj = (nblk - 1 - j) if reverse else j
        return (base_blk + b * nblk + jj, 0)

    def orow(b, j):
        jj = (nblk - 1 - j) if reverse else j
        return (b * nblk + jj, 0)

    kern = functools.partial(_hgrn_kernel, reverse=reverse, nchunk=tb // HG_CHUNK)
    return pl.pallas_call(
        kern,
        grid=(bsz, nblk),
        in_specs=[pl.BlockSpec((tb, HG_QK), rows),
                  pl.BlockSpec((tb, HG_QK), rows),
                  pl.BlockSpec((tb, HG_V), rows),
                  pl.BlockSpec((1, HG_QK), lambda b, j: (0, 0)),
                  pl.BlockSpec((1, HG_QK), lambda b, j: (0, 0)),
                  pl.BlockSpec((1, HG_HEADS, HG_DV, HG_DK), lambda b, j: (b, 0, 0, 0))],
        out_specs=[pl.BlockSpec((tb, HG_V), orow),
                   pl.BlockSpec((1, HG_HEADS, HG_DV, HG_DK), lambda b, j: (b, 0, 0, 0))],
        out_shape=[jax.ShapeDtypeStruct((bsz * n_seq, HG_V), F32),
                   jax.ShapeDtypeStruct((bsz, HG_HEADS, HG_DV, HG_DK), F32)],
        scratch_shapes=[pltpu.VMEM((tb, HG_QK), F32), pltpu.VMEM((tb, HG_QK), F32),
                        pltpu.VMEM((tb, HG_QK), F32), pltpu.VMEM((HG_HEADS, HG_DV, HG_DK), F32)],
        name="hgrn_bwd" if reverse else "hgrn_fwd",
        compiler_params=_cparams(("parallel", "arbitrary")),
    )(hq, hz, hin, la, lc, s0)


def _rms(x, g):
    return x * lax.rsqrt(jnp.mean(x * x, axis=-1, keepdims=True) + EPS) * g


def _mla_proj_kernel(cq_ref, ckv_ref, kr_ref, krs_ref, cos_ref, sin_ref, qg_ref, kvg_ref,
                     wq_ref, wqs_ref, wk_ref, wvt_ref, q_out, k_out, vt_out):
    cos = cos_ref[...]
    sin = sin_ref[...]
    cos_h = jnp.concatenate([cos] * MLA_HEADS, axis=1)
    sin_h = jnp.concatenate([sin] * MLA_HEADS, axis=1)
    xq = _rms(cq_ref[...], qg_ref[...]).astype(BF16)
    q = _dot(xq, wq_ref[...]) * cos_h + _dot(xq, wqs_ref[...]) * sin_h
    q_out[...] = (q * (MLA_SCALE * LOG2_E)).astype(BF16)
    xkv = _rms(ckv_ref[...], kvg_ref[...]).astype(BF16)
    k_rope = kr_ref[...] * cos + krs_ref[...] * sin
    k = _dot(xkv, wk_ref[...]) + jnp.concatenate([k_rope] * MLA_HEADS, axis=1)
    k_out[...] = k.astype(BF16)
    vt_out[0] = _dot_nt(wvt_ref[...], xkv).astype(BF16)


def _mla_proj(cq, ckv, kr, krs, cos_t, sin_t, qg, kvg, wq, wqs, wk, wvt, layer, tm):
    n = cq.shape[0]
    row = lambda i: (i, 0)
    full = lambda i: (0, 0)
    lay = lambda i: (layer, 0, 0)
    hq = MLA_HEADS * HEAD_LANES
    hv = MLA_HEADS * MLA_V
    return pl.pallas_call(
        _mla_proj_kernel,
        grid=(n // tm,),
        in_specs=[pl.BlockSpec((tm, MLA_Q_LORA), row), pl.BlockSpec((tm, MLA_KV_LORA), row),
                  pl.BlockSpec((tm, HEAD_LANES), row), pl.BlockSpec((tm, HEAD_LANES), row),
                  pl.BlockSpec((tm, HEAD_LANES), row), pl.BlockSpec((tm, HEAD_LANES), row),
                  pl.BlockSpec((1, MLA_Q_LORA), full), pl.BlockSpec((1, MLA_KV_LORA), full),
                  pl.BlockSpec((None, MLA_Q_LORA, hq), lay), pl.BlockSpec((None, MLA_Q_LORA, hq), lay),
                  pl.BlockSpec((None, MLA_KV_LORA, hq), lay), pl.BlockSpec((None, hv, MLA_KV_LORA), lay)],
        out_specs=[pl.BlockSpec((tm, hq), row), pl.BlockSpec((tm, hq), row),
                   pl.BlockSpec((1, hv, tm), lambda i: (i, 0, 0))],
        out_shape=[jax.ShapeDtypeStruct((n, hq), BF16), jax.ShapeDtypeStruct((n, hq), BF16),
                   jax.ShapeDtypeStruct((n // tm, hv, tm), BF16)],
        name="mla_proj",
        compiler_params=_cparams(("parallel",)),
    )(cq, ckv, kr, krs, cos_t, sin_t, qg, kvg, wq, wqs, wk, wvt)


ACC_ROWS = MLA_V + 16
LAT_UNROLL = 8
LOOKAHEAD = 2


def _attn_kernel(*refs, tkc, n_lat_chunks, n_ctx_chunks):
    if n_lat_chunks:
        q_ref, kl_ref, vl_ref, kc_ref, vc_ref, o_ref, m_s, acc_s, s_s = refs
    else:
        q_ref, kc_ref, vc_ref, o_ref, m_s, acc_s, s_s = refs
    m_s[...] = jnp.full(m_s.shape, -jnp.inf, F32)
    acc_s[...] = jnp.zeros(acc_s.shape, F32)
    ones = jnp.ones((ACC_ROWS - MLA_V, tkc), BF16)

    def scores(a, k_ref, r0):
        hs = slice(a * HEAD_LANES, (a + 1) * HEAD_LANES)
        return _dot_nt(k_ref[pl.ds(r0, tkc), hs], q_ref[:, hs])

    def accumulate(a, s_t, vt):
        m_old = m_s[a]
        m_new = jnp.maximum(m_old, jnp.max(s_t, axis=0, keepdims=True))
        alpha = jnp.exp2(m_old - m_new)
        p_t = jnp.exp2(s_t - m_new).astype(BF16)
        vta = jnp.concatenate([vt[a * MLA_V:(a + 1) * MLA_V], ones], axis=0)
        acc_s[a] = alpha * acc_s[a] + _dot(vta, p_t)
        m_s[a] = m_new

    def run(items, queue, after):
        upcoming = items[LOOKAHEAD:] + after
        queue = list(queue)
        for idx, (a, _, _, vt) in enumerate(items):
            if idx < len(upcoming):
                queue.append(scores(*upcoming[idx][:3]))
            accumulate(a, queue.pop(0), vt)
        return queue

    ahead_chunks = (LOOKAHEAD + 1) // 2
    unroll = int(np.gcd(LAT_UNROLL, n_lat_chunks)) if n_lat_chunks else 1

    def lat_items(c0, dynamic, count=None):
        items = []
        for u in range(unroll if count is None else count):
            cc = c0 + u
            r0 = pl.multiple_of(cc * tkc, tkc) if dynamic else cc * tkc
            vt = vl_ref[cc] if count is None else None
            items += [(0, kl_ref, r0, vt), (1, kl_ref, r0, vt)]
        return items

    ctx_items = []
    for c in range(n_ctx_chunks):
        ctx_items += [(0, kc_ref, c * tkc, vc_ref[c]), (1, kc_ref, c * tkc, vc_ref[c])]

    if n_lat_chunks:
        n_groups = n_lat_chunks // unroll
        for d, it in enumerate(lat_items(0, False, count=ahead_chunks)[:LOOKAHEAD]):
            s_s[d] = scores(*it[:3])

        def body(g, carry):
            c0 = g * unroll
            after = lat_items(c0 + unroll, True, count=ahead_chunks)[:LOOKAHEAD]
            queue = run(lat_items(c0, True), [s_s[d] for d in range(LOOKAHEAD)], after)
            for d in range(LOOKAHEAD):
                s_s[d] = queue[d]
            return carry

        lax.fori_loop(0, n_groups - 1, body, 0)
        run(lat_items((n_groups - 1) * unroll, False) + ctx_items, [s_s[d] for d in range(LOOKAHEAD)], [])
    else:
        run(ctx_items, [scores(*it[:3]) for it in ctx_items[:LOOKAHEAD]], [])
    outs = []
    for a in range(2):
        acc = acc_s[a]
        outs.append(acc[0:MLA_V] / acc[MLA_V:MLA_V + 1])
    o_ref[...] = jnp.concatenate(outs, axis=0).T.astype(o_ref.dtype)


def _attention(q, k, vt, *, bsz, n_q, q_base, n_lat, n_ctx, ctx_base, tq, tkc, with_lat):
    pairs = MLA_HEADS // 2
    qblocks = n_q // tq
    qb0 = q_base // tq
    cb0 = ctx_base // n_ctx
    kw = 2 * HEAD_LANES
    vw = 2 * MLA_V
    in_specs = [pl.BlockSpec((tq, kw), lambda b, p, i: (qb0 + b * qblocks + i, p))]
    args = [q]
    if with_lat:
        in_specs += [pl.BlockSpec((n_lat, kw), lambda b, p, i: (b, p)),
                     pl.BlockSpec((n_lat // tkc, vw, tkc), lambda b, p, i: (b, p, 0))]
        args += [k, vt]
    in_specs += [pl.BlockSpec((n_ctx, kw), lambda b, p, i: (cb0 + b, p)),
                 pl.BlockSpec((n_ctx // tkc, vw, tkc), lambda b, p, i: (cb0 + b, p, 0))]
    args += [k, vt]
    kern = functools.partial(_attn_kernel, tkc=tkc, n_lat_chunks=(n_lat // tkc) if with_lat else 0,
                             n_ctx_chunks=n_ctx // tkc)
    return pl.pallas_call(
        kern,
        grid=(bsz, pairs, qblocks),
        in_specs=in_specs,
        out_specs=pl.BlockSpec((tq, vw), lambda b, p, i: (b * qblocks + i, p)),
        out_shape=jax.ShapeDtypeStruct((bsz * n_q, MLA_HEADS * MLA_V), BF16),
        scratch_shapes=[pltpu.VMEM((2, 1, tq), F32), pltpu.VMEM((2, ACC_ROWS, tq), F32),
                        pltpu.VMEM((LOOKAHEAD, tkc, tq), F32)],
        name="attn_lat" if with_lat else "attn_ctx",
        compiler_params=_cparams(("parallel", "parallel", "arbitrary")),
    )(*args)


def _outproj_kernel(a_ref, ofl_ref, obl_ref, cl_ref, ofc_ref, obc_ref, cc_ref, hg_ref, x_ref, g1_ref, sh2_ref,
                    sc2_ref, lng_ref, lnb_ref, ng_ref, gavg_ref, w_ref, wr_ref, x1_out, h2_out, lg_out,
                    *, alpha, lat_tiles):
    is_ctx = pl.program_id(0) >= lat_tiles
    tm = x_ref.shape[0]
    half = tm // OUT_ROW_SPLIT
    for r in range(OUT_ROW_SPLIT):
        rs = slice(r * half, (r + 1) * half)
        o = jnp.where(is_ctx, ofc_ref[rs, :] + obc_ref[rs, :], ofl_ref[rs, :] + obl_ref[rs, :])
        c_att = jnp.where(is_ctx, cc_ref[rs, :], cl_ref[rs, :])
        o2_hi, o2_lo = _split_bf16(o * o)
        ms = _dot(o2_hi, gavg_ref[...]) + _dot(o2_lo, gavg_ref[...])
        b_lat = o * lax.rsqrt(ms + EPS) * ng_ref[...] * _silu(hg_ref[rs, :])
        m = (_dot(a_ref[rs, :].astype(BF16), w_ref[0:POOL_DIM, :])
             + _dot(b_lat.astype(BF16), w_ref[POOL_DIM:POOL_DIM + HG_V, :])
             + _dot(c_att, w_ref[POOL_DIM + HG_V:, :]))
        x1 = _layer_norm(alpha * x_ref[rs, :] + g1_ref[0] * m) * lng_ref[...] + lnb_ref[...]
        x1_out[rs, :] = x1
        h2 = _layer_norm(x1) * (1.0 + sc2_ref[0]) + sh2_ref[0]
        h2_out[rs, :] = h2
        lg_out[rs, :] = _dot3(h2, wr_ref[...])


def _out_proj(a, lat_parts, ctx_parts, hg, x, g1, sh2, sc2, ln_g, ln_b, norm_g, gavg, w_out, w_router, layer, tm,
              mod_index, alpha):
    n, d = x.shape
    lat_tiles = lat_parts[0].shape[0] // tm
    row = lambda i: (i, 0)
    lat_row = lambda i: (jnp.minimum(i, lat_tiles - 1), 0)
    ctx_row = lambda i: (jnp.maximum(i - lat_tiles, 0), 0)
    full = lambda i: (0, 0)
    lay = lambda i: (layer, 0, 0)
    mod = lambda i: (mod_index(i), 0, 0)
    hv = MLA_HEADS * MLA_V
    return pl.pallas_call(
        functools.partial(_outproj_kernel, alpha=alpha, lat_tiles=lat_tiles),
        grid=(n // tm,),
        in_specs=[pl.BlockSpec((tm, POOL_DIM), row),
                  pl.BlockSpec((tm, HG_V), lat_row), pl.BlockSpec((tm, HG_V), lat_row), pl.BlockSpec((tm, hv), lat_row),
                  pl.BlockSpec((tm, HG_V), ctx_row), pl.BlockSpec((tm, HG_V), ctx_row), pl.BlockSpec((tm, hv), ctx_row),
                  pl.BlockSpec((tm, HG_V), row),
                  pl.BlockSpec((tm, d), row),
                  pl.BlockSpec((1, 1, d), mod), pl.BlockSpec((1, 1, d), mod), pl.BlockSpec((1, 1, d), mod),
                  pl.BlockSpec((1, d), full), pl.BlockSpec((1, d), full), pl.BlockSpec((1, HG_V), full),
                  pl.BlockSpec((HG_V, HG_V), full), pl.BlockSpec((None, d, d), lay),
                  pl.BlockSpec((None, d, ROUTER_LANES), lay)],
        out_specs=[pl.BlockSpec((tm, d), row), pl.BlockSpec((tm, d), row), pl.BlockSpec((tm, ROUTER_LANES), row)],
        out_shape=[jax.ShapeDtypeStruct((n, d), F32), jax.ShapeDtypeStruct((n, d), F32),
                   jax.ShapeDtypeStruct((n, ROUTER_LANES), F32)],
        name="out_proj",
        compiler_params=_cparams(("parallel",)),
    )(a, *lat_parts, *ctx_parts, hg, x, g1, sh2, sc2, ln_g, ln_b, norm_g, gavg, w_out, w_router)


def _moe_kernel(be_ref, nv_ref, x_ref, w1_ref, w3_ref, w2_ref, y_ref, w1b, w3b, w2b):
    i = pl.program_id(0)

    @pl.when((i == 0) | (be_ref[i] != be_ref[jnp.maximum(i - 1, 0)]))
    def _():
        w1b[...] = w1_ref[0, 0].astype(BF16)
        w3b[...] = w3_ref[0, 0].astype(BF16)
        w2b[...] = w2_ref[0, 0].astype(BF16)

    @pl.when(i < nv_ref[0])
    def _():
        x = x_ref[...].astype(BF16)
        h1 = _dot(x, w1b[...])
        h3 = _dot(x, w3b[...])
        h = (_silu(h1) * h3).astype(BF16)
        y_ref[...] = _dot(h, w2b[...]).astype(y_ref.dtype)

    @pl.when(i >= nv_ref[0])
    def _():
        y_ref[...] = jnp.zeros(y_ref.shape, y_ref.dtype)


def _moe_ffn(block_e, n_valid, xr, w1, w3, w2, layer, tm):
    n_rows, d = xr.shape
    n_blocks = n_rows // tm
    grid_spec = pltpu.PrefetchScalarGridSpec(
        num_scalar_prefetch=2,
        grid=(n_blocks,),
        in_specs=[pl.BlockSpec((tm, d), lambda i, be, nv: (jnp.minimum(i, nv[0] - 1), 0)),
                  pl.BlockSpec((1, 1, d, D_EXPERT), lambda i, be, nv: (layer, be[i], 0, 0)),
                  pl.BlockSpec((1, 1, d, D_EXPERT), lambda i, be, nv: (layer, be[i], 0, 0)),
                  pl.BlockSpec((1, 1, D_EXPERT, d), lambda i, be, nv: (layer, be[i], 0, 0))],
        out_specs=pl.BlockSpec((tm, d), lambda i, be, nv: (i, 0)),
        scratch_shapes=[pltpu.VMEM((d, D_EXPERT), BF16), pltpu.VMEM((d, D_EXPERT), BF16),
                        pltpu.VMEM((D_EXPERT, d), BF16)],
    )
    return pl.pallas_call(
        _moe_kernel,
        grid_spec=grid_spec,
        out_shape=jax.ShapeDtypeStruct((n_rows, d), BF16),
        name="moe_ffn",
        compiler_params=_cparams(("arbitrary",)),
    )(block_e, n_valid, xr, w1, w3, w2)


def _combine_kernel(x_ref, y0_ref, y1_ref, gw_ref, g2_ref, lng_ref, lnb_ref, o_ref, *, alpha):
    gw = gw_ref[...]
    y = gw[:, 0:1] * y0_ref[...].astype(F32) + gw[:, 1:2] * y1_ref[...].astype(F32)
    o_ref[...] = _layer_norm(alpha * x_ref[...] + g2_ref[0] * y) * lng_ref[...] + lnb_ref[...]


def _combine(x, y0, y1, gw, g2, ln_g, ln_b, tm, mod_index, alpha):
    n, d = x.shape
    row = lambda i: (i, 0)
    full = lambda i: (0, 0)
    return pl.pallas_call(
        functools.partial(_combine_kernel, alpha=alpha),
        grid=(n // tm,),
        in_specs=[pl.BlockSpec((tm, d), row), pl.BlockSpec((tm, d), row), pl.BlockSpec((tm, d), row),
                  pl.BlockSpec((tm, ROUTER_LANES), row),
                  pl.BlockSpec((1, 1, d), lambda i: (mod_index(i), 0, 0)),
                  pl.BlockSpec((1, d), full), pl.BlockSpec((1, d), full)],
        out_specs=pl.BlockSpec((tm, d), row),
        out_shape=jax.ShapeDtypeStruct((n, d), F32),
        name="moe_combine",
        compiler_params=_cparams(("parallel",)),
    )(x, y0, y1, gw, g2, ln_g, ln_b)


def _rot_half_cols(w):
    lead = w.shape[:-1]
    wr = w.reshape(lead + (2, 2, MLA_ROPE // 4))
    return jnp.stack([-wr[..., 1, :], wr[..., 0, :]], axis=-2).reshape(w.shape)


def _rope_tables(n_lat, n_ctx, bsz):
    rows = n_lat // GRID_W
    row = jnp.repeat(jnp.arange(rows, dtype=F32), GRID_W)
    col = jnp.tile(jnp.arange(GRID_W, dtype=F32), rows)
    n_freq = MLA_ROPE // 4
    inv_freq = ROPE_BASE ** (-jnp.arange(n_freq, dtype=F32) / n_freq)
    ang = jnp.stack([row[:, None] * inv_freq, col[:, None] * inv_freq], axis=1)
    cos = jnp.broadcast_to(jnp.cos(ang)[:, :, None, :], (n_lat, 2, 2, n_freq)).reshape(n_lat, MLA_ROPE)
    sin = jnp.broadcast_to(jnp.sin(ang)[:, :, None, :], (n_lat, 2, 2, n_freq)).reshape(n_lat, MLA_ROPE)
    pad_l = MLA_NOPE
    pad_r = HEAD_LANES - MLA_QK
    cos_t = jnp.concatenate([jnp.ones((n_lat, pad_l), F32), cos, jnp.ones((n_lat, pad_r), F32)], axis=1)
    sin_t = jnp.concatenate([jnp.zeros((n_lat, pad_l), F32), sin, jnp.zeros((n_lat, pad_r), F32)], axis=1)
    cos_all = jnp.concatenate([jnp.tile(cos_t, (bsz, 1)), jnp.ones((bsz * n_ctx, HEAD_LANES), F32)], axis=0)
    sin_all = jnp.concatenate([jnp.tile(sin_t, (bsz, 1)), jnp.zeros((bsz * n_ctx, HEAD_LANES), F32)], axis=0)
    return cos_all, sin_all


def _prep_w_in(w_in):
    depth, d, _ = w_in.shape
    kr = w_in[:, :, W_IN_MAIN:]
    z_l = jnp.zeros((depth, d, MLA_NOPE), w_in.dtype)
    z_r = jnp.zeros((depth, d, HEAD_LANES - MLA_QK), w_in.dtype)
    return jnp.concatenate([w_in[:, :, :W_IN_MAIN], z_l, kr, z_r, z_l, _rot_half_cols(kr), z_r],
                           axis=-1).astype(BF16)


def _prep_mla(w_uq, w_ukv):
    depth = w_uq.shape[0]
    wq = w_uq.reshape(depth, MLA_Q_LORA, MLA_HEADS, MLA_QK)
    pad = jnp.zeros((depth, MLA_Q_LORA, MLA_HEADS, HEAD_LANES - MLA_QK), w_uq.dtype)
    wq_p = jnp.concatenate([wq, pad], axis=-1).reshape(depth, MLA_Q_LORA, MLA_HEADS * HEAD_LANES)
    wq_s = jnp.concatenate([jnp.zeros_like(wq[..., :MLA_NOPE]), _rot_half_cols(wq[..., MLA_NOPE:]), pad],
                           axis=-1).reshape(depth, MLA_Q_LORA, MLA_HEADS * HEAD_LANES)
    wkv = w_ukv.reshape(depth, MLA_KV_LORA, MLA_HEADS, MLA_NOPE + MLA_V)
    kpad = jnp.zeros((depth, MLA_KV_LORA, MLA_HEADS, HEAD_LANES - MLA_NOPE), w_ukv.dtype)
    wk = jnp.concatenate([wkv[..., :MLA_NOPE], kpad], axis=-1).reshape(depth, MLA_KV_LORA, MLA_HEADS * HEAD_LANES)
    wvt = jnp.swapaxes(wkv[..., MLA_NOPE:].reshape(depth, MLA_KV_LORA, MLA_HEADS * MLA_V), 1, 2)
    return wq_p.astype(BF16), wq_s.astype(BF16), wk.astype(BF16), wvt.astype(BF16)


def _block_diag(pool_w):
    depth, g, c, _ = pool_w.shape
    eye = jnp.eye(g, dtype=pool_w.dtype)
    return jnp.einsum("lgcd,gh->lgchd", pool_w, eye).reshape(depth, g * c, g * c).astype(BF16)


def _route(logits, b_rg, b_re, tm_moe):
    n = logits.shape[0]
    lg = logits[:, :N_GROUPS] + b_rg
    le = (logits[:, N_GROUPS:N_GROUPS + N_EXPERTS] + b_re).reshape(n, N_GROUPS, EXPERTS_PER_GROUP)
    pg = jax.nn.softmax(lg, axis=-1)
    g_sel = jnp.argmax(lg, axis=-1).astype(jnp.int32)
    g_hot = g_sel[:, None] == jnp.arange(N_GROUPS, dtype=jnp.int32)[None, :]
    w_g = jnp.sum(jnp.where(g_hot, pg, 0.0), axis=-1)
    le_sel = jnp.sum(jnp.where(g_hot[:, :, None], le, 0.0), axis=1)
    top_v, top_i = lax.top_k(le_sel, TOP_K)
    gate = jax.nn.softmax(top_v, axis=-1) * w_g[:, None]
    expert = (g_sel[:, None] * EXPERTS_PER_GROUP + top_i).astype(jnp.int32)

    n_assign = n * TOP_K
    e_flat = expert.reshape(-1)
    hot = e_flat[:, None] == jnp.arange(N_EXPERTS, dtype=jnp.int32)[None, :]
    nb = n_assign // COUNT_BLOCK
    tri = jnp.tril(jnp.ones((COUNT_BLOCK, COUNT_BLOCK), BF16))
    win = jnp.einsum("ts,bse->bte", tri, hot.astype(BF16).reshape(nb, COUNT_BLOCK, N_EXPERTS),
                     preferred_element_type=F32)
    bsum = win[:, -1, :]
    boff = jnp.cumsum(bsum, axis=0) - bsum
    csum = (win + boff[:, None, :]).reshape(n_assign, N_EXPERTS).astype(jnp.int32)
    counts = (boff[-1] + bsum[-1]).astype(jnp.int32)
    pcounts = (counts + tm_moe - 1) // tm_moe * tm_moe
    pends = jnp.cumsum(pcounts)
    pstarts = pends - pcounts
    starts = jnp.cumsum(counts) - counts
    dest = jnp.sum(jnp.where(hot, csum - 1 + pstarts[None, :], 0), axis=1)
    n_rows = (n_assign + N_EXPERTS * (tm_moe - 1) + tm_moe - 1) // tm_moe * tm_moe
    n_blocks = n_rows // tm_moe
    n_valid = (pends[-1] // tm_moe).astype(jnp.int32)
    blk = jnp.arange(n_blocks, dtype=jnp.int32)
    first_row = jnp.minimum(blk, n_valid - 1) * tm_moe
    block_e = jnp.sum((pends[None, :] <= first_row[:, None]).astype(jnp.int32), axis=1)
    block_e = jnp.clip(block_e, 0, N_EXPERTS - 1)
    a_sorted = jnp.sort(e_flat * n_assign + jnp.arange(n_assign, dtype=jnp.int32)) % n_assign
    pos = (blk * tm_moe - pstarts[block_e])[:, None] + jnp.arange(tm_moe, dtype=jnp.int32)[None, :]
    src = jnp.clip(starts[block_e][:, None] + pos, 0, n_assign - 1)
    filler = (blk[:, None] * tm_moe + jnp.arange(tm_moe, dtype=jnp.int32)[None, :]) % n
    row_tok = jnp.where(pos < counts[block_e][:, None], a_sorted[src] // TOP_K, filler)
    gw = jnp.zeros((n, ROUTER_LANES), F32).at[:, :TOP_K].set(gate)
    return row_tok.reshape(n_rows), dest.reshape(n, TOP_K), gw, block_e, n_valid.reshape(1)


def _tile(n, pref):
    t = pref
    while n % t:
        t //= 2
    return t


def kernel(x, c, ctx, c_ctx, w_mod, b_mod, w_in, pool_w, pool_scale, hg_lb_logits, hg_norm_g, mla_q_norm_g, mla_w_uq, mla_kv_norm_g, mla_w_ukv, w_out, ln1_g, ln1_b, ln2_g, ln2_b, router_group_w, router_group_b, router_expert_w, router_expert_b, expert_w1, expert_w3, expert_w2):
    bsz, n_lat, d = x.shape
    n_ctx = ctx.shape[1]
    depth = w_mod.shape[0]
    alpha = (2 * depth) ** 0.25
    lat_rows = bsz * n_lat
    ctx_rows = bsz * n_ctx
    n_tok = lat_rows + ctx_rows

    tm = _tile(np.gcd(n_lat, ctx_rows), 512)
    tp = _tile(np.gcd(n_lat, n_ctx), 256)
    tb_lat = _tile(n_lat, 512)
    tb_ctx = _tile(n_ctx, 512)
    tq_lat = _tile(n_lat, 512)
    tkc = _tile(np.gcd(n_lat, n_ctx), 256)
    tq_ctx = _tile(n_ctx, 512)
    tm_moe = 256
    lat_tiles_per_batch = n_lat // tm
    mod_index = lambda i: jnp.minimum(i // lat_tiles_per_batch, bsz)

    c_rows = jnp.concatenate([jax.nn.silu(c), jax.nn.silu(c_ctx)[None, :],
                              jnp.zeros((-(bsz + 1) % 8, d), F32)], axis=0)
    mod_all = _mod_all(c_rows, w_mod, b_mod)
    lb_all = jnp.cumsum(jax.nn.softmax(hg_lb_logits.astype(F32), axis=0), axis=0)
    lb_all = lb_all - lb_all[0:1]
    log_lb = jnp.log(lb_all)
    log_1m_lb = jnp.log1p(-lb_all)
    w_aug = _prep_w_in(w_in)
    wq_p, wq_s, wk_p, wvt_p = _prep_mla(mla_w_uq, mla_w_ukv)
    w_bd = _block_diag(pool_w)
    w_out_b = w_out.astype(BF16)
    w_router = jnp.concatenate([router_group_w, router_expert_w,
                                jnp.zeros((depth, d, ROUTER_LANES - N_GROUPS - N_EXPERTS), F32)], axis=-1)
    gavg = jnp.kron(jnp.eye(HG_HEADS, dtype=F32), jnp.full((HG_DV, HG_DV), 1.0 / HG_DV, F32)).astype(BF16)
    cos_t, sin_t = _rope_tables(n_lat, n_ctx, bsz)
    zero_state = jnp.zeros((bsz, HG_HEADS, HG_DV, HG_DK), F32)

    xt = jnp.concatenate([x.reshape(lat_rows, d), ctx.reshape(ctx_rows, d)], axis=0)
    for l in range(depth):
        mod = mod_all[l, :bsz + 1].reshape(bsz + 1, 6, 1, d)
        sh1, sc1, g1, sh2, sc2, g2 = (mod[:, k] for k in range(6))

        p_in, hq, hzf, hzb, hin, hg, cq, ckv, kr, krs = _in_proj(xt, sh1, sc1, w_aug, l, tm, mod_index)

        a_mix = _pool(p_in, w_bd[l], pool_scale[l][None, :], tp, lat_rows, n_lat, n_ctx)

        la_f, la_b = log_lb[l, 0][None, :], log_lb[l, 1][None, :]
        lc_f, lc_b = log_1m_lb[l, 0][None, :], log_1m_lb[l, 1][None, :]
        oc_f, s_f = _hgrn_scan(hq, hzf, hin, la_f, lc_f, zero_state, reverse=False, base_row=lat_rows,
                               n_seq=n_ctx, tb=tb_ctx)
        oc_b, s_b = _hgrn_scan(hq, hzb, hin, la_b, lc_b, zero_state, reverse=True, base_row=lat_rows,
                               n_seq=n_ctx, tb=tb_ctx)
        ol_f, _ = _hgrn_scan(hq, hzf, hin, la_f, lc_f, s_f, reverse=False, base_row=0, n_seq=n_lat, tb=tb_lat)
        ol_b, _ = _hgrn_scan(hq, hzb, hin, la_b, lc_b, s_b, reverse=True, base_row=0, n_seq=n_lat, tb=tb_lat)

        q_all, k_all, vt_all = _mla_proj(cq, ckv, kr, krs, cos_t, sin_t, mla_q_norm_g[l][None, :],
                                         mla_kv_norm_g[l][None, :], wq_p, wq_s, wk_p, wvt_p, l, tkc)
        c_lat = _attention(q_all, k_all, vt_all, bsz=bsz, n_q=n_lat, q_base=0, n_lat=n_lat, n_ctx=n_ctx,
                           ctx_base=lat_rows, tq=tq_lat, tkc=tkc, with_lat=True)
        c_ctx_o = _attention(q_all, k_all, vt_all, bsz=bsz, n_q=n_ctx, q_base=lat_rows, n_lat=n_lat, n_ctx=n_ctx,
                             ctx_base=lat_rows, tq=tq_ctx, tkc=tkc, with_lat=False)

        x1, h2, logits = _out_proj(a_mix, (ol_f, ol_b, c_lat), (oc_f, oc_b, c_ctx_o), hg, xt, g1, sh2, sc2,
                                   ln1_g[l][None, :],
                                   ln1_b[l][None, :], hg_norm_g[l][None, :], gavg, w_out_b, w_router, l,
                                   tm, mod_index, alpha)

        row_tok, dest, gw, block_e, n_valid = _route(logits, router_group_b[l], router_expert_b[l], tm_moe)
        xr = jnp.take(h2, row_tok, axis=0, mode="clip")
        y = _moe_ffn(block_e, n_valid, xr, expert_w1, expert_w3, expert_w2, l, tm_moe)
        y0 = jnp.take(y, dest[:, 0], axis=0, mode="clip")
        y1 = jnp.take(y, dest[:, 1], axis=0, mode="clip")
        xt = _combine(x1, y0, y1, gw, g2, ln2_g[l][None, :], ln2_b[l][None, :], tm, mod_index, alpha)
    return xt[:lat_rows].reshape(bsz, n_lat, d)
```

```python
import functools

import numpy as np
import jax
import jax.numpy as jnp
from jax import lax
from jax.experimental import pallas as pl
from jax.experimental.pallas import tpu as pltpu

F32 = jnp.float32
BF16 = jnp.bfloat16

D_MODEL = 1024
GRID_W = 64
POOL_WINDOWS = (2, 4, 8, 16)
POOL_DIM = 256
POOL_GROUP_DIM = 64
HG_HEADS = 4
HG_DK = 128
HG_DV = 64
HG_QK = HG_HEADS * HG_DK
HG_V = HG_HEADS * HG_DV
HG_CHUNK = 64
HG_SUB_LOG2 = 4
HG_SUB = 1 << HG_SUB_LOG2
MLA_HEADS = 8
MLA_NOPE = 64
MLA_ROPE = 32
MLA_V = 64
MLA_Q_LORA = 256
MLA_KV_LORA = 128
MLA_QK = MLA_NOPE + MLA_ROPE
MLA_SCALE = MLA_QK ** -0.5
LOG2_E = 1.4426950408889634
ROPE_BASE = 10000.0
HEAD_LANES = 128
N_GROUPS = 4
EXPERTS_PER_GROUP = 8
N_EXPERTS = N_GROUPS * EXPERTS_PER_GROUP
TOP_K = 2
D_EXPERT = 512
EPS = 1e-6
ROUTER_LANES = 128
COUNT_BLOCK = 128
OUT_ROW_SPLIT = 2
IN_SEGS = (("p_in", 256), ("hq", 512), ("hzf", 512), ("hzb", 512), ("hin", 256), ("hg", 256),
           ("cq", 256), ("ckv", 128), ("kr", 128), ("krs", 128))
IN_AUG = sum(w for _, w in IN_SEGS)
W_IN_MAIN = 2688

VMEM_LIMIT = 56 * 1024 * 1024


def _cparams(sem):
    return pltpu.CompilerParams(dimension_semantics=sem, vmem_limit_bytes=VMEM_LIMIT)


def _split_bf16(a):
    hi = a.astype(BF16)
    lo = (a - hi.astype(F32)).astype(BF16)
    return hi, lo


def _dot(a, b):
    return jnp.dot(a, b, preferred_element_type=F32)


def _dot_nt(a, b):
    return lax.dot_general(a, b, (((1,), (1,)), ((), ())), preferred_element_type=F32)


def _dot_tn(a, b):
    return lax.dot_general(a, b, (((0,), (0,)), ((), ())), preferred_element_type=F32)


def _dot3(a, b):
    a_hi, a_lo = _split_bf16(a)
    b_hi, b_lo = _split_bf16(b)
    return _dot(a_hi, b_hi) + _dot(a_hi, b_lo) + _dot(a_lo, b_hi)


def _dot_exact_rhs(a, b_bf16):
    a_hi = a.astype(BF16)
    r1 = a - a_hi.astype(F32)
    a_mid = r1.astype(BF16)
    a_lo = (r1 - a_mid.astype(F32)).astype(BF16)
    return _dot(a_hi, b_bf16) + _dot(a_mid, b_bf16) + _dot(a_lo, b_bf16)


def _layer_norm(x):
    mu = jnp.mean(x, axis=-1, keepdims=True)
    xc = x - mu
    var = jnp.mean(xc * xc, axis=-1, keepdims=True)
    return xc * lax.rsqrt(var + EPS)


def _sigmoid(x):
    return 1.0 / (1.0 + jnp.exp(-x))


def _silu(x):
    return x * _sigmoid(x)


def _mod_kernel(a_ref, w_ref, b_ref, o_ref):
    o_ref[0] = _dot3(a_ref[...], w_ref[0]) + b_ref[0]


def _mod_all(c_rows, w_mod, b_mod):
    depth, d, n6 = w_mod.shape
    tn = 1536
    rows = c_rows.shape[0]
    return pl.pallas_call(
        _mod_kernel,
        grid=(depth, n6 // tn),
        in_specs=[pl.BlockSpec((rows, d), lambda l, j: (0, 0)),
                  pl.BlockSpec((1, d, tn), lambda l, j: (l, 0, j)),
                  pl.BlockSpec((1, 1, tn), lambda l, j: (l, 0, j))],
        out_specs=pl.BlockSpec((1, rows, tn), lambda l, j: (l, 0, j)),
        out_shape=jax.ShapeDtypeStruct((depth, rows, n6), F32),
        name="adaln_mod",
        compiler_params=_cparams(("arbitrary", "arbitrary")),
    )(c_rows, w_mod, b_mod.reshape(depth, 1, n6))


def _forget_gates(z, la, lc):
    ls = jnp.minimum(z, 0.0) - jnp.log1p(jnp.exp(-jnp.abs(z)))
    x2 = lc + ls
    g = jnp.maximum(la, x2) + jnp.log1p(jnp.exp(-jnp.abs(la - x2)))
    k = jnp.exp(x2 - z)
    return g, k


IN_OUTS = (("p_in", 256), ("q", 512), ("gf", 512), ("kf", 512), ("gb", 512), ("kb", 512), ("hin", 256), ("hg", 256),
           ("cq", 256), ("ckv", 128), ("kr", 128), ("krs", 128))


def _inproj_kernel(x_ref, sh_ref, sc_ref, gc_ref, w_ref, *out_refs):
    outs = dict(zip((name for name, _ in IN_OUTS), out_refs))
    h = _layer_norm(x_ref[...]) * (1.0 + sc_ref[0]) + sh_ref[0]
    hb = h.astype(BF16)
    start = 0
    for name, width in IN_SEGS:
        seg = _dot(hb, w_ref[:, start:start + width])
        start += width
        if name == "hq":
            outs["q"][...] = _silu(seg)
        elif name == "hzf":
            outs["gf"][...], outs["kf"][...] = _forget_gates(seg, gc_ref[0:1, :], gc_ref[1:2, :])
        elif name == "hzb":
            outs["gb"][...], outs["kb"][...] = _forget_gates(seg, gc_ref[2:3, :], gc_ref[3:4, :])
        else:
            outs[name][...] = seg


def _in_proj(x, sh, sc, gate_consts, w_aug, layer, tm, mod_index):
    n, d = x.shape
    row = lambda i: (i, 0)
    mod = lambda i: (mod_index(i), 0, 0)
    return pl.pallas_call(
        _inproj_kernel,
        grid=(n // tm,),
        in_specs=[pl.BlockSpec((tm, d), row),
                  pl.BlockSpec((1, 1, d), mod),
                  pl.BlockSpec((1, 1, d), mod),
                  pl.BlockSpec(gate_consts.shape, lambda i: (0, 0)),
                  pl.BlockSpec((None, d, IN_AUG), lambda i: (layer, 0, 0))],
        out_specs=[pl.BlockSpec((tm, w), row) for _, w in IN_OUTS],
        out_shape=[jax.ShapeDtypeStruct((n, w), F32) for _, w in IN_OUTS],
        name="in_proj",
        compiler_params=_cparams(("parallel",)),
    )(x, sh, sc, gate_consts, w_aug)


POOL_HALO = 8


def _pool_kernel(x_ref, prev_ref, next_ref, w_ref, scale_ref, o_ref, *, tp, lat_tiles, tpb_lat, tpb_ctx,
                 n_lat, n_ctx):
    i = pl.program_id(0)
    is_lat = i < lat_tiles
    tile_in_seq = jnp.where(is_lat, i % tpb_lat, (i - lat_tiles) % tpb_ctx)
    tiles_in_seq = jnp.where(is_lat, tpb_lat, tpb_ctx)
    n_seq = jnp.where(is_lat, n_lat, n_ctx)
    first = tile_in_seq == 0
    last = tile_in_seq == tiles_in_seq - 1
    x = x_ref[...]
    prev = jnp.where(first, 0.0, prev_ref[...])
    nxt = jnp.where(last, 0.0, next_ref[...])
    ext = jnp.concatenate([prev, x, nxt], axis=0)
    n_ext = tp + 2 * POOL_HALO

    def back(a, k):
        return pltpu.roll(a, k, 0)

    def fwd(a, k):
        return pltpu.roll(a, n_ext - k, 0)

    e2 = ext + back(ext, 1)
    e4 = back(e2, 1) + fwd(e2, 1)
    e8 = back(e4, 2) + fwd(e4, 2)
    e16 = back(e8, 4) + fwd(e8, 4)
    sl = slice(POOL_HALO, POOL_HALO + tp)
    lane = lax.broadcasted_iota(jnp.int32, (tp, POOL_DIM), 1)
    grp = lane >> 6
    wsum = jnp.where(grp == 0, e2[sl], jnp.where(grp == 1, e4[sl], jnp.where(grp == 2, e8[sl], e16[sl])))
    half = jnp.where(grp == 0, 1, jnp.where(grp == 1, 2, jnp.where(grp == 2, 4, 8)))
    t = tile_in_seq * tp + lax.broadcasted_iota(jnp.int32, (tp, POOL_DIM), 0)
    cnt = jnp.minimum(t + half, n_seq) - jnp.maximum(t - half, 0)
    pooled = wsum / cnt.astype(F32) - x
    o_ref[...] = _dot(pooled.astype(BF16), w_ref[...]) * scale_ref[...]


def _pool(p_in, w_bd, scale, tp, n_lat_rows, n_lat, n_ctx):
    n, c = p_in.shape
    hb = tp // POOL_HALO
    last_halo_block = n // POOL_HALO - 1
    kern = functools.partial(_pool_kernel, tp=tp, lat_tiles=n_lat_rows // tp, tpb_lat=n_lat // tp,
                             tpb_ctx=n_ctx // tp, n_lat=n_lat, n_ctx=n_ctx)
    return pl.pallas_call(
        kern,
        grid=(n // tp,),
        in_specs=[pl.BlockSpec((tp, c), lambda i: (i, 0)),
                  pl.BlockSpec((POOL_HALO, c), lambda i: (jnp.maximum(i * hb - 1, 0), 0)),
                  pl.BlockSpec((POOL_HALO, c), lambda i: (jnp.minimum((i + 1) * hb, last_halo_block), 0)),
                  pl.BlockSpec((c, c), lambda i: (0, 0)),
                  pl.BlockSpec((1, c), lambda i: (0, 0))],
        out_specs=pl.BlockSpec((tp, c), lambda i: (i, 0)),
        out_shape=jax.ShapeDtypeStruct((n, c), F32),
        name="pool",
        compiler_params=_cparams(("parallel",)),
    )(p_in, p_in, p_in, w_bd, scale)


def _hgrn_kernel(q_s, g_s, k_s, hin_ref, s0_ref, o_ref, sT_ref, st_s, *, reverse, nchunk):
    j = pl.program_id(1)

    @pl.when(j == 0)
    def _():
        st_s[...] = s0_ref[0]

    C = HG_CHUNK
    row = lax.broadcasted_iota(jnp.int32, (C, C), 0)
    col = lax.broadcasted_iota(jnp.int32, (C, C), 1)
    tri = ((row <= col) if reverse else (row >= col)).astype(BF16)
    rmod = lax.broadcasted_iota(jnp.int32, (C, 1), 0) & (HG_SUB - 1)
    nsub = C // HG_SUB
    row_sub = row >> HG_SUB_LOG2
    col_sub = col >> HG_SUB_LOG2

    def chunk(c, carry):
        cc = (nchunk - 1 - c) if reverse else c
        r0 = pl.multiple_of(cc * C, C)
        b_all = _dot_exact_rhs_lhs(tri, g_s[pl.ds(r0, C), :])
        q_all = q_s[pl.ds(r0, C), :]
        k_all = k_s[pl.ds(r0, C), :]
        v_all = hin_ref[pl.ds(r0, C), :]
        outs = []
        for h in range(HG_HEADS):
            ks = slice(h * HG_DK, (h + 1) * HG_DK)
            b, q, k = b_all[:, ks], q_all[:, ks], k_all[:, ks]
            v = v_all[:, h * HG_DV:(h + 1) * HG_DV]
            st = st_s[h]
            b_last = b[0:1] if reverse else b[C - 1:C]
            o = _dot_nt((q * jnp.exp(b)).astype(BF16), st.astype(BF16))
            a_rows = []
            for i in range(nsub):
                rs = slice(i * HG_SUB, (i + 1) * HG_SUB)
                if (reverse and i == nsub - 1) or (not reverse and i == 0):
                    a_rows.append(jnp.zeros((HG_SUB, C), F32))
                    continue
                m = b[(i + 1) * HG_SUB - 1:(i + 1) * HG_SUB] if reverse else b[i * HG_SUB:i * HG_SUB + 1]
                qi = q[rs] * jnp.exp(b[rs] - m)
                ksc = k * jnp.exp(m - b)
                a_rows.append(_dot_nt(qi.astype(BF16), ksc.astype(BF16)))
            a = jnp.concatenate(a_rows, axis=0)
            if reverse:
                a = jnp.where(col_sub > row_sub, a, 0.0)
            else:
                a = jnp.where(col_sub < row_sub, a, 0.0)
            o = o + _dot(a.astype(BF16), v.astype(BF16))
            for delta in range(HG_SUB):
                if delta == 0:
                    w = jnp.sum(q * k, axis=-1, keepdims=True)
                    o = o + w * v
                    continue
                shift = (C - delta) if reverse else delta
                kd = pltpu.roll(k, shift, 0)
                bd = pltpu.roll(b, shift, 0)
                vd = pltpu.roll(v, shift, 0)
                e = jnp.exp(b - bd)
                w = jnp.sum(q * kd * e, axis=-1, keepdims=True)
                valid = (rmod + delta < HG_SUB) if reverse else (rmod >= delta)
                o = o + jnp.where(valid, w, 0.0) * vd
            outs.append(o)
            kdec = k * jnp.exp(b_last - b)
            st_s[h] = st * jnp.exp(b_last) + _dot_tn(v.astype(BF16), kdec.astype(BF16))
        o_ref[pl.ds(r0, C), :] = jnp.concatenate(outs, axis=1)
        return carry

    lax.fori_loop(0, nchunk, chunk, 0)

    @pl.when(j == pl.num_programs(1) - 1)
    def _():
        sT_ref[0] = st_s[...]


def _dot_exact_rhs_lhs(tri_bf16, g):
    g_hi = g.astype(BF16)
    r1 = g - g_hi.astype(F32)
    g_mid = r1.astype(BF16)
    g_lo = (r1 - g_mid.astype(F32)).astype(BF16)
    return _dot(tri_bf16, g_hi) + _dot(tri_bf16, g_mid) + _dot(tri_bf16, g_lo)


def _hgrn_scan(q, g, k, hin, s0, *, reverse, base_row, n_seq, tb):
    bsz = s0.shape[0]
    nblk = n_seq // tb
    base_blk = base_row // tb

    def rows(b, j):
        jj = (nblk - 1 - j) if reverse else j
        return (base_blk + b * nblk + jj, 0)

    def orow(b, j):
        jj = (nblk - 1 - j) if reverse else j
        return (b * nblk + jj, 0)

    kern = functools.partial(_hgrn_kernel, reverse=reverse, nchunk=tb // HG_CHUNK)
    return pl.pallas_call(
        kern,
        grid=(bsz, nblk),
        in_specs=[pl.BlockSpec((tb, HG_QK), rows),
                  pl.BlockSpec((tb, HG_QK), rows),
                  pl.BlockSpec((tb, HG_QK), rows),
                  pl.BlockSpec((tb, HG_V), rows),
                  pl.BlockSpec((1, HG_HEADS, HG_DV, HG_DK), lambda b, j: (b, 0, 0, 0))],
        out_specs=[pl.BlockSpec((tb, HG_V), orow),
                   pl.BlockSpec((1, HG_HEADS, HG_DV, HG_DK), lambda b, j: (b, 0, 0, 0))],
        out_shape=[jax.ShapeDtypeStruct((bsz * n_seq, HG_V), F32),
                   jax.ShapeDtypeStruct((bsz, HG_HEADS, HG_DV, HG_DK), F32)],
        scratch_shapes=[pltpu.VMEM((HG_HEADS, HG_DV, HG_DK), F32)],
        name="hgrn_bwd" if reverse else "hgrn_fwd",
        compiler_params=_cparams(("parallel", "arbitrary")),
    )(q, g, k, hin, s0)


def _rms(x, g):
    return x * lax.rsqrt(jnp.mean(x * x, axis=-1, keepdims=True) + EPS) * g


def _mla_proj_kernel(cq_ref, ckv_ref, kr_ref, krs_ref, cos_ref, sin_ref, qg_ref, kvg_ref,
                     wq_ref, wqs_ref, wk_ref, wvt_ref, q_out, k_out, vt_out):
    cos = cos_ref[...]
    sin = sin_ref[...]
    cos_h = jnp.concatenate([cos] * MLA_HEADS, axis=1)
    sin_h = jnp.concatenate([sin] * MLA_HEADS, axis=1)
    xq = _rms(cq_ref[...], qg_ref[...]).astype(BF16)
    q = _dot(xq, wq_ref[...]) * cos_h + _dot(xq, wqs_ref[...]) * sin_h
    q_out[...] = (q * (MLA_SCALE * LOG2_E)).astype(BF16)
    xkv = _rms(ckv_ref[...], kvg_ref[...]).astype(BF16)
    k_rope = kr_ref[...] * cos + krs_ref[...] * sin
    k = _dot(xkv, wk_ref[...]) + jnp.concatenate([k_rope] * MLA_HEADS, axis=1)
    k_out[...] = k.astype(BF16)
    vt_out[0] = _dot_nt(wvt_ref[...], xkv).astype(BF16)


def _mla_proj(cq, ckv, kr, krs, cos_t, sin_t, qg, kvg, wq, wqs, wk, wvt, layer, tm):
    n = cq.shape[0]
    row = lambda i: (i, 0)
    full = lambda i: (0, 0)
    lay = lambda i: (layer, 0, 0)
    hq = MLA_HEADS * HEAD_LANES
    hv = MLA_HEADS * MLA_V
    return pl.pallas_call(
        _mla_proj_kernel,
        grid=(n // tm,),
        in_specs=[pl.BlockSpec((tm, MLA_Q_LORA), row), pl.BlockSpec((tm, MLA_KV_LORA), row),
                  pl.BlockSpec((tm, HEAD_LANES), row), pl.BlockSpec((tm, HEAD_LANES), row),
                  pl.BlockSpec((tm, HEAD_LANES), row), pl.BlockSpec((tm, HEAD_LANES), row),
                  pl.BlockSpec((1, MLA_Q_LORA), full), pl.BlockSpec((1, MLA_KV_LORA), full),
                  pl.BlockSpec((None, MLA_Q_LORA, hq), lay), pl.BlockSpec((None, MLA_Q_LORA, hq), lay),
                  pl.BlockSpec((None, MLA_KV_LORA, hq), lay), pl.BlockSpec((None, hv, MLA_KV_LORA), lay)],
        out_specs=[pl.BlockSpec((tm, hq), row), pl.BlockSpec((tm, hq), row),
                   pl.BlockSpec((1, hv, tm), lambda i: (i, 0, 0))],
        out_shape=[jax.ShapeDtypeStruct((n, hq), BF16), jax.ShapeDtypeStruct((n, hq), BF16),
                   jax.ShapeDtypeStruct((n // tm, hv, tm), BF16)],
        name="mla_proj",
        compiler_params=_cparams(("parallel",)),
    )(cq, ckv, kr, krs, cos_t, sin_t, qg, kvg, wq, wqs, wk, wvt)


ACC_ROWS = MLA_V + 16
LAT_UNROLL = 8
LOOKAHEAD = 2


def _attn_kernel(*refs, tkc, n_lat_chunks, n_ctx_chunks):
    if n_lat_chunks:
        q_ref, kl_ref, vl_ref, kc_ref, vc_ref, o_ref, m_s, acc_s, s_s = refs
    else:
        q_ref, kc_ref, vc_ref, o_ref, m_s, acc_s, s_s = refs
    m_s[...] = jnp.full(m_s.shape, -jnp.inf, F32)
    acc_s[...] = jnp.zeros(acc_s.shape, F32)
    ones = jnp.ones((ACC_ROWS - MLA_V, tkc), BF16)

    def scores(a, k_ref, r0):
        hs = slice(a * HEAD_LANES, (a + 1) * HEAD_LANES)
        return _dot_nt(k_ref[pl.ds(r0, tkc), hs], q_ref[:, hs])

    def accumulate(a, s_t, vt):
        m_old = m_s[a]
        m_new = jnp.maximum(m_old, jnp.max(s_t, axis=0, keepdims=True))
        alpha = jnp.exp2(m_old - m_new)
        p_t = jnp.exp2(s_t - m_new).astype(BF16)
        vta = jnp.concatenate([vt[a * MLA_V:(a + 1) * MLA_V], ones], axis=0)
        acc_s[a] = alpha * acc_s[a] + _dot(vta, p_t)
        m_s[a] = m_new

    def run(items, queue, after):
        upcoming = items[LOOKAHEAD:] + after
        queue = list(queue)
        for idx, (a, _, _, vt) in enumerate(items):
            if idx < len(upcoming):
                queue.append(scores(*upcoming[idx][:3]))
            accumulate(a, queue.pop(0), vt)
        return queue

    ahead_chunks = (LOOKAHEAD + 1) // 2
    unroll = int(np.gcd(LAT_UNROLL, n_lat_chunks)) if n_lat_chunks else 1

    def lat_items(c0, dynamic, count=None):
        items = []
        for u in range(unroll if count is None else count):
            cc = c0 + u
            r0 = pl.multiple_of(cc * tkc, tkc) if dynamic else cc * tkc
            vt = vl_ref[cc] if count is None else None
            items += [(0, kl_ref, r0, vt), (1, kl_ref, r0, vt)]
        return items

    ctx_items = []
    for c in range(n_ctx_chunks):
        ctx_items += [(0, kc_ref, c * tkc, vc_ref[c]), (1, kc_ref, c * tkc, vc_ref[c])]

    if n_lat_chunks:
        n_groups = n_lat_chunks // unroll
        for d, it in enumerate(lat_items(0, False, count=ahead_chunks)[:LOOKAHEAD]):
            s_s[d] = scores(*it[:3])

        def body(g, carry):
            c0 = g * unroll
            after = lat_items(c0 + unroll, True, count=ahead_chunks)[:LOOKAHEAD]
            queue = run(lat_items(c0, True), [s_s[d] for d in range(LOOKAHEAD)], after)
            for d in range(LOOKAHEAD):
                s_s[d] = queue[d]
            return carry

        lax.fori_loop(0, n_groups - 1, body, 0)
        run(lat_items((n_groups - 1) * unroll, False) + ctx_items, [s_s[d] for d in range(LOOKAHEAD)], [])
    else:
        run(ctx_items, [scores(*it[:3]) for it in ctx_items[:LOOKAHEAD]], [])
    outs = []
    for a in range(2):
        acc = acc_s[a]
        outs.append(acc[0:MLA_V] / acc[MLA_V:MLA_V + 1])
    o_ref[...] = jnp.concatenate(outs, axis=0).T.astype(o_ref.dtype)


def _attention(q, k, vt, *, bsz, n_q, q_base, n_lat, n_ctx, ctx_base, tq, tkc, with_lat):
    pairs = MLA_HEADS // 2
    qblocks = n_q // tq
    qb0 = q_base // tq
    cb0 = ctx_base // n_ctx
    kw = 2 * HEAD_LANES
    vw = 2 * MLA_V
    in_specs = [pl.BlockSpec((tq, kw), lambda b, p, i: (qb0 + b * qblocks + i, p))]
    args = [q]
    if with_lat:
        in_specs += [pl.BlockSpec((n_lat, kw), lambda b, p, i: (b, p)),
                     pl.BlockSpec((n_lat // tkc, vw, tkc), lambda b, p, i: (b, p, 0))]
        args += [k, vt]
    in_specs += [pl.BlockSpec((n_ctx, kw), lambda b, p, i: (cb0 + b, p)),
                 pl.BlockSpec((n_ctx // tkc, vw, tkc), lambda b, p, i: (cb0 + b, p, 0))]
    args += [k, vt]
    kern = functools.partial(_attn_kernel, tkc=tkc, n_lat_chunks=(n_lat // tkc) if with_lat else 0,
                             n_ctx_chunks=n_ctx // tkc)
    return pl.pallas_call(
        kern,
        grid=(bsz, pairs, qblocks),
        in_specs=in_specs,
        out_specs=pl.BlockSpec((tq, vw), lambda b, p, i: (b * qblocks + i, p)),
        out_shape=jax.ShapeDtypeStruct((bsz * n_q, MLA_HEADS * MLA_V), BF16),
        scratch_shapes=[pltpu.VMEM((2, 1, tq), F32), pltpu.VMEM((2, ACC_ROWS, tq), F32),
                        pltpu.VMEM((LOOKAHEAD, tkc, tq), F32)],
        name="attn_lat" if with_lat else "attn_ctx",
        compiler_params=_cparams(("parallel", "parallel", "arbitrary")),
    )(*args)


def _outproj_kernel(a_ref, ofl_ref, obl_ref, cl_ref, ofc_ref, obc_ref, cc_ref, hg_ref, x_ref, g1_ref, sh2_ref,
                    sc2_ref, lng_ref, lnb_ref, ng_ref, gavg_ref, w_ref, wr_ref, x1_out, h2_out, lg_out,
                    *, alpha, lat_tiles):
    is_ctx = pl.program_id(0) >= lat_tiles
    tm = x_ref.shape[0]
    half = tm // OUT_ROW_SPLIT
    for r in range(OUT_ROW_SPLIT):
        rs = slice(r * half, (r + 1) * half)
        o = jnp.where(is_ctx, ofc_ref[rs, :] + obc_ref[rs, :], ofl_ref[rs, :] + obl_ref[rs, :])
        c_att = jnp.where(is_ctx, cc_ref[rs, :], cl_ref[rs, :])
        o2_hi, o2_lo = _split_bf16(o * o)
        ms = _dot(o2_hi, gavg_ref[...]) + _dot(o2_lo, gavg_ref[...])
        b_lat = o * lax.rsqrt(ms + EPS) * ng_ref[...] * _silu(hg_ref[rs, :])
        m = (_dot(a_ref[rs, :].astype(BF16), w_ref[0:POOL_DIM, :])
             + _dot(b_lat.astype(BF16), w_ref[POOL_DIM:POOL_DIM + HG_V, :])
             + _dot(c_att, w_ref[POOL_DIM + HG_V:, :]))
        x1 = _layer_norm(alpha * x_ref[rs, :] + g1_ref[0] * m) * lng_ref[...] + lnb_ref[...]
        x1_out[rs, :] = x1
        h2 = _layer_norm(x1) * (1.0 + sc2_ref[0]) + sh2_ref[0]
        h2_out[rs, :] = h2
        lg_out[rs, :] = _dot3(h2, wr_ref[...])


def _out_proj(a, lat_parts, ctx_parts, hg, x, g1, sh2, sc2, ln_g, ln_b, norm_g, gavg, w_out, w_router, layer, tm,
              mod_index, alpha):
    n, d = x.shape
    lat_tiles = lat_parts[0].shape[0] // tm
    row = lambda i: (i, 0)
    lat_row = lambda i: (jnp.minimum(i, lat_tiles - 1), 0)
    ctx_row = lambda i: (jnp.maximum(i - lat_tiles, 0), 0)
    full = lambda i: (0, 0)
    lay = lambda i: (layer, 0, 0)
    mod = lambda i: (mod_index(i), 0, 0)
    hv = MLA_HEADS * MLA_V
    return pl.pallas_call(
        functools.partial(_outproj_kernel, alpha=alpha, lat_tiles=lat_tiles),
        grid=(n // tm,),
        in_specs=[pl.BlockSpec((tm, POOL_DIM), row),
                  pl.BlockSpec((tm, HG_V), lat_row), pl.BlockSpec((tm, HG_V), lat_row), pl.BlockSpec((tm, hv), lat_row),
                  pl.BlockSpec((tm, HG_V), ctx_row), pl.BlockSpec((tm, HG_V), ctx_row), pl.BlockSpec((tm, hv), ctx_row),
                  pl.BlockSpec((tm, HG_V), row),
                  pl.BlockSpec((tm, d), row),
                  pl.BlockSpec((1, 1, d), mod), pl.BlockSpec((1, 1, d), mod), pl.BlockSpec((1, 1, d), mod),
                  pl.BlockSpec((1, d), full), pl.BlockSpec((1, d), full), pl.BlockSpec((1, HG_V), full),
                  pl.BlockSpec((HG_V, HG_V), full), pl.BlockSpec((None, d, d), lay),
                  pl.BlockSpec((None, d, ROUTER_LANES), lay)],
        out_specs=[pl.BlockSpec((tm, d), row), pl.BlockSpec((tm, d), row), pl.BlockSpec((tm, ROUTER_LANES), row)],
        out_shape=[jax.ShapeDtypeStruct((n, d), F32), jax.ShapeDtypeStruct((n, d), F32),
                   jax.ShapeDtypeStruct((n, ROUTER_LANES), F32)],
        name="out_proj",
        compiler_params=_cparams(("parallel",)),
    )(a, *lat_parts, *ctx_parts, hg, x, g1, sh2, sc2, ln_g, ln_b, norm_g, gavg, w_out, w_router)


def _moe_kernel(be_ref, nv_ref, x_ref, w1_ref, w3_ref, w2_ref, y_ref, w1b, w3b, w2b):
    i = pl.program_id(0)

    @pl.when((i == 0) | (be_ref[i] != be_ref[jnp.maximum(i - 1, 0)]))
    def _():
        w1b[...] = w1_ref[0, 0].astype(BF16)
        w3b[...] = w3_ref[0, 0].astype(BF16)
        w2b[...] = w2_ref[0, 0].astype(BF16)

    @pl.when(i < nv_ref[0])
    def _():
        x = x_ref[...].astype(BF16)
        h1 = _dot(x, w1b[...])
        h3 = _dot(x, w3b[...])
        h = (_silu(h1) * h3).astype(BF16)
        y_ref[...] = _dot(h, w2b[...]).astype(y_ref.dtype)

    @pl.when(i >= nv_ref[0])
    def _():
        y_ref[...] = jnp.zeros(y_ref.shape, y_ref.dtype)


def _moe_ffn(block_e, n_valid, xr, w1, w3, w2, layer, tm):
    n_rows, d = xr.shape
    n_blocks = n_rows // tm
    grid_spec = pltpu.PrefetchScalarGridSpec(
        num_scalar_prefetch=2,
        grid=(n_blocks,),
        in_specs=[pl.BlockSpec((tm, d), lambda i, be, nv: (jnp.minimum(i, nv[0] - 1), 0)),
                  pl.BlockSpec((1, 1, d, D_EXPERT), lambda i, be, nv: (layer, be[i], 0, 0)),
                  pl.BlockSpec((1, 1, d, D_EXPERT), lambda i, be, nv: (layer, be[i], 0, 0)),
                  pl.BlockSpec((1, 1, D_EXPERT, d), lambda i, be, nv: (layer, be[i], 0, 0))],
        out_specs=pl.BlockSpec((tm, d), lambda i, be, nv: (i, 0)),
        scratch_shapes=[pltpu.VMEM((d, D_EXPERT), BF16), pltpu.VMEM((d, D_EXPERT), BF16),
                        pltpu.VMEM((D_EXPERT, d), BF16)],
    )
    return pl.pallas_call(
        _moe_kernel,
        grid_spec=grid_spec,
        out_shape=jax.ShapeDtypeStruct((n_rows, d), BF16),
        name="moe_ffn",
        compiler_params=_cparams(("arbitrary",)),
    )(block_e, n_valid, xr, w1, w3, w2)


def _combine_kernel(x_ref, y0_ref, y1_ref, gw_ref, g2_ref, lng_ref, lnb_ref, o_ref, *, alpha):
    gw = gw_ref[...]
    y = gw[:, 0:1] * y0_ref[...].astype(F32) + gw[:, 1:2] * y1_ref[...].astype(F32)
    o_ref[...] = _layer_norm(alpha * x_ref[...] + g2_ref[0] * y) * lng_ref[...] + lnb_ref[...]


def _combine(x, y0, y1, gw, g2, ln_g, ln_b, tm, mod_index, alpha):
    n, d = x.shape
    row = lambda i: (i, 0)
    full = lambda i: (0, 0)
    return pl.pallas_call(
        functools.partial(_combine_kernel, alpha=alpha),
        grid=(n // tm,),
        in_specs=[pl.BlockSpec((tm, d), row), pl.BlockSpec((tm, d), row), pl.BlockSpec((tm, d), row),
                  pl.BlockSpec((tm, ROUTER_LANES), row),
                  pl.BlockSpec((1, 1, d), lambda i: (mod_index(i), 0, 0)),
                  pl.BlockSpec((1, d), full), pl.BlockSpec((1, d), full)],
        out_specs=pl.BlockSpec((tm, d), row),
        out_shape=jax.ShapeDtypeStruct((n, d), F32),
        name="moe_combine",
        compiler_params=_cparams(("parallel",)),
    )(x, y0, y1, gw, g2, ln_g, ln_b)


def _rot_half_cols(w):
    lead = w.shape[:-1]
    wr = w.reshape(lead + (2, 2, MLA_ROPE // 4))
    return jnp.stack([-wr[..., 1, :], wr[..., 0, :]], axis=-2).reshape(w.shape)


def _rope_tables(n_lat, n_ctx, bsz):
    rows = n_lat // GRID_W
    row = jnp.repeat(jnp.arange(rows, dtype=F32), GRID_W)
    col = jnp.tile(jnp.arange(GRID_W, dtype=F32), rows)
    n_freq = MLA_ROPE // 4
    inv_freq = ROPE_BASE ** (-jnp.arange(n_freq, dtype=F32) / n_freq)
    ang = jnp.stack([row[:, None] * inv_freq, col[:, None] * inv_freq], axis=1)
    cos = jnp.broadcast_to(jnp.cos(ang)[:, :, None, :], (n_lat, 2, 2, n_freq)).reshape(n_lat, MLA_ROPE)
    sin = jnp.broadcast_to(jnp.sin(ang)[:, :, None, :], (n_lat, 2, 2, n_freq)).reshape(n_lat, MLA_ROPE)
    pad_l = MLA_NOPE
    pad_r = HEAD_LANES - MLA_QK
    cos_t = jnp.concatenate([jnp.ones((n_lat, pad_l), F32), cos, jnp.ones((n_lat, pad_r), F32)], axis=1)
    sin_t = jnp.concatenate([jnp.zeros((n_lat, pad_l), F32), sin, jnp.zeros((n_lat, pad_r), F32)], axis=1)
    cos_all = jnp.concatenate([jnp.tile(cos_t, (bsz, 1)), jnp.ones((bsz * n_ctx, HEAD_LANES), F32)], axis=0)
    sin_all = jnp.concatenate([jnp.tile(sin_t, (bsz, 1)), jnp.zeros((bsz * n_ctx, HEAD_LANES), F32)], axis=0)
    return cos_all, sin_all


def _prep_w_in(w_in):
    depth, d, _ = w_in.shape
    kr = w_in[:, :, W_IN_MAIN:]
    z_l = jnp.zeros((depth, d, MLA_NOPE), w_in.dtype)
    z_r = jnp.zeros((depth, d, HEAD_LANES - MLA_QK), w_in.dtype)
    return jnp.concatenate([w_in[:, :, :W_IN_MAIN], z_l, kr, z_r, z_l, _rot_half_cols(kr), z_r],
                           axis=-1).astype(BF16)


def _prep_mla(w_uq, w_ukv):
    depth = w_uq.shape[0]
    wq = w_uq.reshape(depth, MLA_Q_LORA, MLA_HEADS, MLA_QK)
    pad = jnp.zeros((depth, MLA_Q_LORA, MLA_HEADS, HEAD_LANES - MLA_QK), w_uq.dtype)
    wq_p = jnp.concatenate([wq, pad], axis=-1).reshape(depth, MLA_Q_LORA, MLA_HEADS * HEAD_LANES)
    wq_s = jnp.concatenate([jnp.zeros_like(wq[..., :MLA_NOPE]), _rot_half_cols(wq[..., MLA_NOPE:]), pad],
                           axis=-1).reshape(depth, MLA_Q_LORA, MLA_HEADS * HEAD_LANES)
    wkv = w_ukv.reshape(depth, MLA_KV_LORA, MLA_HEADS, MLA_NOPE + MLA_V)
    kpad = jnp.zeros((depth, MLA_KV_LORA, MLA_HEADS, HEAD_LANES - MLA_NOPE), w_ukv.dtype)
    wk = jnp.concatenate([wkv[..., :MLA_NOPE], kpad], axis=-1).reshape(depth, MLA_KV_LORA, MLA_HEADS * HEAD_LANES)
    wvt = jnp.swapaxes(wkv[..., MLA_NOPE:].reshape(depth, MLA_KV_LORA, MLA_HEADS * MLA_V), 1, 2)
    return wq_p.astype(BF16), wq_s.astype(BF16), wk.astype(BF16), wvt.astype(BF16)


def _block_diag(pool_w):
    depth, g, c, _ = pool_w.shape
    eye = jnp.eye(g, dtype=pool_w.dtype)
    return jnp.einsum("lgcd,gh->lgchd", pool_w, eye).reshape(depth, g * c, g * c).astype(BF16)


def _route(logits, b_rg, b_re, tm_moe):
    n = logits.shape[0]
    lg = logits[:, :N_GROUPS] + b_rg
    le = (logits[:, N_GROUPS:N_GROUPS + N_EXPERTS] + b_re).reshape(n, N_GROUPS, EXPERTS_PER_GROUP)
    pg = jax.nn.softmax(lg, axis=-1)
    g_sel = jnp.argmax(lg, axis=-1).astype(jnp.int32)
    g_hot = g_sel[:, None] == jnp.arange(N_GROUPS, dtype=jnp.int32)[None, :]
    w_g = jnp.sum(jnp.where(g_hot, pg, 0.0), axis=-1)
    le_sel = jnp.sum(jnp.where(g_hot[:, :, None], le, 0.0), axis=1)
    top_v, top_i = lax.top_k(le_sel, TOP_K)
    gate = jax.nn.softmax(top_v, axis=-1) * w_g[:, None]
    expert = (g_sel[:, None] * EXPERTS_PER_GROUP + top_i).astype(jnp.int32)

    n_assign = n * TOP_K
    e_flat = expert.reshape(-1)
    hot = e_flat[:, None] == jnp.arange(N_EXPERTS, dtype=jnp.int32)[None, :]
    nb = n_assign // COUNT_BLOCK
    tri = jnp.tril(jnp.ones((COUNT_BLOCK, COUNT_BLOCK), BF16))
    win = jnp.einsum("ts,bse->bte", tri, hot.astype(BF16).reshape(nb, COUNT_BLOCK, N_EXPERTS),
                     preferred_element_type=F32)
    bsum = win[:, -1, :]
    boff = jnp.cumsum(bsum, axis=0) - bsum
    csum = (win + boff[:, None, :]).reshape(n_assign, N_EXPERTS).astype(jnp.int32)
    counts = (boff[-1] + bsum[-1]).astype(jnp.int32)
    pcounts = (counts + tm_moe - 1) // tm_moe * tm_moe
    pends = jnp.cumsum(pcounts)
    pstarts = pends - pcounts
    starts = jnp.cumsum(counts) - counts
    dest = jnp.sum(jnp.where(hot, csum - 1 + pstarts[None, :], 0), axis=1)
    n_rows = (n_assign + N_EXPERTS * (tm_moe - 1) + tm_moe - 1) // tm_moe * tm_moe
    n_blocks = n_rows // tm_moe
    n_valid = (pends[-1] // tm_moe).astype(jnp.int32)
    blk = jnp.arange(n_blocks, dtype=jnp.int32)
    first_row = jnp.minimum(blk, n_valid - 1) * tm_moe
    block_e = jnp.sum((pends[None, :] <= first_row[:, None]).astype(jnp.int32), axis=1)
    block_e = jnp.clip(block_e, 0, N_EXPERTS - 1)
    a_sorted = jnp.sort(e_flat * n_assign + jnp.arange(n_assign, dtype=jnp.int32)) % n_assign
    pos = (blk * tm_moe - pstarts[block_e])[:, None] + jnp.arange(tm_moe, dtype=jnp.int32)[None, :]
    src = jnp.clip(starts[block_e][:, None] + pos, 0, n_assign - 1)
    filler = (blk[:, None] * tm_moe + jnp.arange(tm_moe, dtype=jnp.int32)[None, :]) % n
    row_tok = jnp.where(pos < counts[block_e][:, None], a_sorted[src] // TOP_K, filler)
    gw = jnp.zeros((n, ROUTER_LANES), F32).at[:, :TOP_K].set(gate)
    return row_tok.reshape(n_rows), dest.reshape(n, TOP_K), gw, block_e, n_valid.reshape(1)


def _tile(n, pref):
    t = pref
    while n % t:
        t //= 2
    return t


def kernel(x, c, ctx, c_ctx, w_mod, b_mod, w_in, pool_w, pool_scale, hg_lb_logits, hg_norm_g, mla_q_norm_g, mla_w_uq, mla_kv_norm_g, mla_w_ukv, w_out, ln1_g, ln1_b, ln2_g, ln2_b, router_group_w, router_group_b, router_expert_w, router_expert_b, expert_w1, expert_w3, expert_w2):
    bsz, n_lat, d = x.shape
    n_ctx = ctx.shape[1]
    depth = w_mod.shape[0]
    alpha = (2 * depth) ** 0.25
    lat_rows = bsz * n_lat
    ctx_rows = bsz * n_ctx
    n_tok = lat_rows + ctx_rows

    tm = _tile(np.gcd(n_lat, ctx_rows), 512)
    tp = _tile(np.gcd(n_lat, n_ctx), 256)
    tb_lat = _tile(n_lat, 512)
    tb_ctx = _tile(n_ctx, 512)
    tq_lat = _tile(n_lat, 512)
    tkc = _tile(np.gcd(n_lat, n_ctx), 256)
    tq_ctx = _tile(n_ctx, 512)
    tm_moe = 256
    lat_tiles_per_batch = n_lat // tm
    mod_index = lambda i: jnp.minimum(i // lat_tiles_per_batch, bsz)

    c_rows = jnp.concatenate([jax.nn.silu(c), jax.nn.silu(c_ctx)[None, :],
                              jnp.zeros((-(bsz + 1) % 8, d), F32)], axis=0)
    mod_all = _mod_all(c_rows, w_mod, b_mod)
    lb_all = jnp.cumsum(jax.nn.softmax(hg_lb_logits.astype(F32), axis=0), axis=0)
    lb_all = lb_all - lb_all[0:1]
    log_lb = jnp.log(lb_all)
    log_1m_lb = jnp.log1p(-lb_all)
    w_aug = _prep_w_in(w_in)
    wq_p, wq_s, wk_p, wvt_p = _prep_mla(mla_w_uq, mla_w_ukv)
    w_bd = _block_diag(pool_w)
    w_out_b = w_out.astype(BF16)
    w_router = jnp.concatenate([router_group_w, router_expert_w,
                                jnp.zeros((depth, d, ROUTER_LANES - N_GROUPS - N_EXPERTS), F32)], axis=-1)
    gavg = jnp.kron(jnp.eye(HG_HEADS, dtype=F32), jnp.full((HG_DV, HG_DV), 1.0 / HG_DV, F32)).astype(BF16)
    cos_t, sin_t = _rope_tables(n_lat, n_ctx, bsz)
    zero_state = jnp.zeros((bsz, HG_HEADS, HG_DV, HG_DK), F32)

    xt = jnp.concatenate([x.reshape(lat_rows, d), ctx.reshape(ctx_rows, d)], axis=0)
    for l in range(depth):
        mod = mod_all[l, :bsz + 1].reshape(bsz + 1, 6, 1, d)
        sh1, sc1, g1, sh2, sc2, g2 = (mod[:, k] for k in range(6))

        gate_consts = jnp.stack([log_lb[l, 0], log_1m_lb[l, 0], log_lb[l, 1], log_1m_lb[l, 1]], axis=0)
        p_in, hq, gf, kf, gb, kb, hin, hg, cq, ckv, kr, krs = _in_proj(xt, sh1, sc1, gate_consts, w_aug, l, tm,
                                                                       mod_index)

        a_mix = _pool(p_in, w_bd[l], pool_scale[l][None, :], tp, lat_rows, n_lat, n_ctx)

        oc_f, s_f = _hgrn_scan(hq, gf, kf, hin, zero_state, reverse=False, base_row=lat_rows, n_seq=n_ctx, tb=tb_ctx)
        oc_b, s_b = _hgrn_scan(hq, gb, kb, hin, zero_state, reverse=True, base_row=lat_rows, n_seq=n_ctx, tb=tb_ctx)
        ol_f, _ = _hgrn_scan(hq, gf, kf, hin, s_f, reverse=False, base_row=0, n_seq=n_lat, tb=tb_lat)
        ol_b, _ = _hgrn_scan(hq, gb, kb, hin, s_b, reverse=True, base_row=0, n_seq=n_lat, tb=tb_lat)

        q_all, k_all, vt_all = _mla_proj(cq, ckv, kr, krs, cos_t, sin_t, mla_q_norm_g[l][None, :],
                                         mla_kv_norm_g[l][None, :], wq_p, wq_s, wk_p, wvt_p, l, tkc)
        c_lat = _attention(q_all, k_all, vt_all, bsz=bsz, n_q=n_lat, q_base=0, n_lat=n_lat, n_ctx=n_ctx,
                           ctx_base=lat_rows, tq=tq_lat, tkc=tkc, with_lat=True)
        c_ctx_o = _attention(q_all, k_all, vt_all, bsz=bsz, n_q=n_ctx, q_base=lat_rows, n_lat=n_lat, n_ctx=n_ctx,
                             ctx_base=lat_rows, tq=tq_ctx, tkc=tkc, with_lat=False)

        x1, h2, logits = _out_proj(a_mix, (ol_f, ol_b, c_lat), (oc_f, oc_b, c_ctx_o), hg, xt, g1, sh2, sc2,
                                   ln1_g[l][None, :],
                                   ln1_b[l][None, :], hg_norm_g[l][None, :], gavg, w_out_b, w_router, l,
                                   tm, mod_index, alpha)

        row_tok, dest, gw, block_e, n_valid = _route(logits, router_group_b[l], router_expert_b[l], tm_moe)
        xr = jnp.take(h2, row_tok, axis=0, mode="clip")
        y = _moe_ffn(block_e, n_valid, xr, expert_w1, expert_w3, expert_w2, l, tm_moe)
        y0 = jnp.take(y, dest[:, 0], axis=0, mode="clip")
        y1 = jnp.take(y, dest[:, 1], axis=0, mode="clip")
        xt = _combine(x1, y0, y1, gw, g2, ln2_g[l][None, :], ln2_b[l][None, :], tm, mod_index, alpha)
    return xt[:lat_rows].reshape(bsz, n_lat, d)
```

```python
import functools

import numpy as np
import jax
import jax.numpy as jnp
from jax import lax
from jax.experimental import pallas as pl
from jax.experimental.pallas import tpu as pltpu

F32 = jnp.float32
BF16 = jnp.bfloat16

D_MODEL = 1024
GRID_W = 64
POOL_WINDOWS = (2, 4, 8, 16)
POOL_DIM = 256
POOL_GROUP_DIM = 64
HG_HEADS = 4
HG_DK = 128
HG_DV = 64
HG_QK = HG_HEADS * HG_DK
HG_V = HG_HEADS * HG_DV
HG_CHUNK = 64
HG_SUB_LOG2 = 4
HG_SUB = 1 << HG_SUB_LOG2
MLA_HEADS = 8
MLA_NOPE = 64
MLA_ROPE = 32
MLA_V = 64
MLA_Q_LORA = 256
MLA_KV_LORA = 128
MLA_QK = MLA_NOPE + MLA_ROPE
MLA_SCALE = MLA_QK ** -0.5
LOG2_E = 1.4426950408889634
ROPE_BASE = 10000.0
HEAD_LANES = 128
N_GROUPS = 4
EXPERTS_PER_GROUP = 8
N_EXPERTS = N_GROUPS * EXPERTS_PER_GROUP
TOP_K = 2
D_EXPERT = 512
EPS = 1e-6
ROUTER_LANES = 128
COUNT_BLOCK = 128
OUT_ROW_SPLIT = 2
IN_SEGS = (("p_in", 256), ("hq", 512), ("hzf", 512), ("hzb", 512), ("hin", 256), ("hg", 256),
           ("cq", 256), ("ckv", 128), ("kr", 128), ("krs", 128))
IN_AUG = sum(w for _, w in IN_SEGS)
W_IN_MAIN = 2688

VMEM_LIMIT = 56 * 1024 * 1024


def _cparams(sem):
    return pltpu.CompilerParams(dimension_semantics=sem, vmem_limit_bytes=VMEM_LIMIT)


def _split_bf16(a):
    hi = a.astype(BF16)
    lo = (a - hi.astype(F32)).astype(BF16)
    return hi, lo


def _dot(a, b):
    return jnp.dot(a, b, preferred_element_type=F32)


def _dot_nt(a, b):
    return lax.dot_general(a, b, (((1,), (1,)), ((), ())), preferred_element_type=F32)


def _dot_tn(a, b):
    return lax.dot_general(a, b, (((0,), (0,)), ((), ())), preferred_element_type=F32)


def _dot3(a, b):
    a_hi, a_lo = _split_bf16(a)
    b_hi, b_lo = _split_bf16(b)
    return _dot(a_hi, b_hi) + _dot(a_hi, b_lo) + _dot(a_lo, b_hi)


def _dot_exact_rhs(a, b_bf16):
    a_hi = a.astype(BF16)
    r1 = a - a_hi.astype(F32)
    a_mid = r1.astype(BF16)
    a_lo = (r1 - a_mid.astype(F32)).astype(BF16)
    return _dot(a_hi, b_bf16) + _dot(a_mid, b_bf16) + _dot(a_lo, b_bf16)


def _layer_norm(x):
    mu = jnp.mean(x, axis=-1, keepdims=True)
    xc = x - mu
    var = jnp.mean(xc * xc, axis=-1, keepdims=True)
    return xc * lax.rsqrt(var + EPS)


def _sigmoid(x):
    return 1.0 / (1.0 + jnp.exp(-x))


def _silu(x):
    return x * _sigmoid(x)


def _mod_kernel(a_ref, w_ref, b_ref, o_ref):
    o_ref[0] = _dot3(a_ref[...], w_ref[0]) + b_ref[0]


def _mod_all(c_rows, w_mod, b_mod):
    depth, d, n6 = w_mod.shape
    tn = 1536
    rows = c_rows.shape[0]
    return pl.pallas_call(
        _mod_kernel,
        grid=(depth, n6 // tn),
        in_specs=[pl.BlockSpec((rows, d), lambda l, j: (0, 0)),
                  pl.BlockSpec((1, d, tn), lambda l, j: (l, 0, j)),
                  pl.BlockSpec((1, 1, tn), lambda l, j: (l, 0, j))],
        out_specs=pl.BlockSpec((1, rows, tn), lambda l, j: (l, 0, j)),
        out_shape=jax.ShapeDtypeStruct((depth, rows, n6), F32),
        name="adaln_mod",
        compiler_params=_cparams(("arbitrary", "arbitrary")),
    )(c_rows, w_mod, b_mod.reshape(depth, 1, n6))


def _forget_gates(z, la, lc):
    ls = jnp.minimum(z, 0.0) - jnp.log1p(jnp.exp(-jnp.abs(z)))
    x2 = lc + ls
    g = jnp.maximum(la, x2) + jnp.log1p(jnp.exp(-jnp.abs(la - x2)))
    k = jnp.exp(x2 - z)
    return g, k


IN_OUTS = (("p_in", 256), ("q", 512), ("gf", 512), ("kf", 512), ("gb", 512), ("kb", 512), ("hin", 256), ("hg", 256),
           ("cq", 256), ("ckv", 128), ("kr", 128), ("krs", 128))


def _inproj_kernel(x_ref, sh_ref, sc_ref, gc_ref, w_ref, *out_refs):
    outs = dict(zip((name for name, _ in IN_OUTS), out_refs))
    h = _layer_norm(x_ref[...]) * (1.0 + sc_ref[0]) + sh_ref[0]
    hb = h.astype(BF16)
    def finish(name, seg):
        if name == "hq":
            outs["q"][...] = _silu(seg)
        elif name == "hzf":
            outs["gf"][...], outs["kf"][...] = _forget_gates(seg, gc_ref[0:1, :], gc_ref[1:2, :])
        elif name == "hzb":
            outs["gb"][...], outs["kb"][...] = _forget_gates(seg, gc_ref[2:3, :], gc_ref[3:4, :])
        else:
            outs[name][...] = seg

    start = 0
    pending = None
    for name, width in IN_SEGS:
        seg = _dot(hb, w_ref[:, start:start + width])
        start += width
        if pending is not None:
            finish(*pending)
        pending = (name, seg)
    finish(*pending)


def _in_proj(x, sh, sc, gate_consts, w_aug, layer, tm, mod_index):
    n, d = x.shape
    row = lambda i: (i, 0)
    mod = lambda i: (mod_index(i), 0, 0)
    return pl.pallas_call(
        _inproj_kernel,
        grid=(n // tm,),
        in_specs=[pl.BlockSpec((tm, d), row),
                  pl.BlockSpec((1, 1, d), mod),
                  pl.BlockSpec((1, 1, d), mod),
                  pl.BlockSpec(gate_consts.shape, lambda i: (0, 0)),
                  pl.BlockSpec((None, d, IN_AUG), lambda i: (layer, 0, 0))],
        out_specs=[pl.BlockSpec((tm, w), row) for _, w in IN_OUTS],
        out_shape=[jax.ShapeDtypeStruct((n, w), F32) for _, w in IN_OUTS],
        name="in_proj",
        compiler_params=_cparams(("parallel",)),
    )(x, sh, sc, gate_consts, w_aug)


POOL_HALO = 8


def _pool_kernel(x_ref, prev_ref, next_ref, w_ref, scale_ref, o_ref, *, tp, lat_tiles, tpb_lat, tpb_ctx,
                 n_lat, n_ctx):
    i = pl.program_id(0)
    is_lat = i < lat_tiles
    tile_in_seq = jnp.where(is_lat, i % tpb_lat, (i - lat_tiles) % tpb_ctx)
    tiles_in_seq = jnp.where(is_lat, tpb_lat, tpb_ctx)
    n_seq = jnp.where(is_lat, n_lat, n_ctx)
    first = tile_in_seq == 0
    last = tile_in_seq == tiles_in_seq - 1
    x = x_ref[...]
    prev = jnp.where(first, 0.0, prev_ref[...])
    nxt = jnp.where(last, 0.0, next_ref[...])
    ext = jnp.concatenate([prev, x, nxt], axis=0)
    n_ext = tp + 2 * POOL_HALO

    def back(a, k):
        return pltpu.roll(a, k, 0)

    def fwd(a, k):
        return pltpu.roll(a, n_ext - k, 0)

    e2 = ext + back(ext, 1)
    e4 = back(e2, 1) + fwd(e2, 1)
    e8 = back(e4, 2) + fwd(e4, 2)
    e16 = back(e8, 4) + fwd(e8, 4)
    sl = slice(POOL_HALO, POOL_HALO + tp)
    lane = lax.broadcasted_iota(jnp.int32, (tp, POOL_DIM), 1)
    grp = lane >> 6
    wsum = jnp.where(grp == 0, e2[sl], jnp.where(grp == 1, e4[sl], jnp.where(grp == 2, e8[sl], e16[sl])))
    half = jnp.where(grp == 0, 1, jnp.where(grp == 1, 2, jnp.where(grp == 2, 4, 8)))
    t = tile_in_seq * tp + lax.broadcasted_iota(jnp.int32, (tp, POOL_DIM), 0)
    cnt = jnp.minimum(t + half, n_seq) - jnp.maximum(t - half, 0)
    pooled = wsum / cnt.astype(F32) - x
    o_ref[...] = _dot(pooled.astype(BF16), w_ref[...]) * scale_ref[...]


def _pool(p_in, w_bd, scale, tp, n_lat_rows, n_lat, n_ctx):
    n, c = p_in.shape
    hb = tp // POOL_HALO
    last_halo_block = n // POOL_HALO - 1
    kern = functools.partial(_pool_kernel, tp=tp, lat_tiles=n_lat_rows // tp, tpb_lat=n_lat // tp,
                             tpb_ctx=n_ctx // tp, n_lat=n_lat, n_ctx=n_ctx)
    return pl.pallas_call(
        kern,
        grid=(n // tp,),
        in_specs=[pl.BlockSpec((tp, c), lambda i: (i, 0)),
                  pl.BlockSpec((POOL_HALO, c), lambda i: (jnp.maximum(i * hb - 1, 0), 0)),
                  pl.BlockSpec((POOL_HALO, c), lambda i: (jnp.minimum((i + 1) * hb, last_halo_block), 0)),
                  pl.BlockSpec((c, c), lambda i: (0, 0)),
                  pl.BlockSpec((1, c), lambda i: (0, 0))],
        out_specs=pl.BlockSpec((tp, c), lambda i: (i, 0)),
        out_shape=jax.ShapeDtypeStruct((n, c), F32),
        name="pool",
        compiler_params=_cparams(("parallel",)),
    )(p_in, p_in, p_in, w_bd, scale)


def _hgrn_kernel(q_s, g_s, k_s, hin_ref, s0_ref, o_ref, sT_ref, st_s, *, reverse, nchunk):
    j = pl.program_id(1)

    @pl.when(j == 0)
    def _():
        st_s[...] = s0_ref[0]

    C = HG_CHUNK
    row = lax.broadcasted_iota(jnp.int32, (C, C), 0)
    col = lax.broadcasted_iota(jnp.int32, (C, C), 1)
    tri = ((row <= col) if reverse else (row >= col)).astype(BF16)
    rmod = lax.broadcasted_iota(jnp.int32, (C, 1), 0) & (HG_SUB - 1)
    nsub = C // HG_SUB
    row_sub = row >> HG_SUB_LOG2
    col_sub = col >> HG_SUB_LOG2

    def chunk(c, carry):
        cc = (nchunk - 1 - c) if reverse else c
        r0 = pl.multiple_of(cc * C, C)
        b_all = _dot_exact_rhs_lhs(tri, g_s[pl.ds(r0, C), :])
        q_all = q_s[pl.ds(r0, C), :]
        k_all = k_s[pl.ds(r0, C), :]
        v_all = hin_ref[pl.ds(r0, C), :]
        outs = []
        for h in range(HG_HEADS):
            ks = slice(h * HG_DK, (h + 1) * HG_DK)
            b, q, k = b_all[:, ks], q_all[:, ks], k_all[:, ks]
            v = v_all[:, h * HG_DV:(h + 1) * HG_DV]
            st = st_s[h]
            b_last = b[0:1] if reverse else b[C - 1:C]
            o = _dot_nt((q * jnp.exp(b)).astype(BF16), st.astype(BF16))
            a_rows = []
            for i in range(nsub):
                rs = slice(i * HG_SUB, (i + 1) * HG_SUB)
                if (reverse and i == nsub - 1) or (not reverse and i == 0):
                    a_rows.append(jnp.zeros((HG_SUB, C), F32))
                    continue
                m = b[(i + 1) * HG_SUB - 1:(i + 1) * HG_SUB] if reverse else b[i * HG_SUB:i * HG_SUB + 1]
                qi = q[rs] * jnp.exp(b[rs] - m)
                ksc = k * jnp.exp(m - b)
                a_rows.append(_dot_nt(qi.astype(BF16), ksc.astype(BF16)))
            a = jnp.concatenate(a_rows, axis=0)
            if reverse:
                a = jnp.where(col_sub > row_sub, a, 0.0)
            else:
                a = jnp.where(col_sub < row_sub, a, 0.0)
            o = o + _dot(a.astype(BF16), v.astype(BF16))
            for delta in range(HG_SUB):
                if delta == 0:
                    w = jnp.sum(q * k, axis=-1, keepdims=True)
                    o = o + w * v
                    continue
                shift = (C - delta) if reverse else delta
                kd = pltpu.roll(k, shift, 0)
                bd = pltpu.roll(b, shift, 0)
                vd = pltpu.roll(v, shift, 0)
                e = jnp.exp(b - bd)
                w = jnp.sum(q * kd * e, axis=-1, keepdims=True)
                valid = (rmod + delta < HG_SUB) if reverse else (rmod >= delta)
                o = o + jnp.where(valid, w, 0.0) * vd
            outs.append(o)
            kdec = k * jnp.exp(b_last - b)
            st_s[h] = st * jnp.exp(b_last) + _dot_tn(v.astype(BF16), kdec.astype(BF16))
        o_ref[pl.ds(r0, C), :] = jnp.concatenate(outs, axis=1)
        return carry

    lax.fori_loop(0, nchunk, chunk, 0)

    @pl.when(j == pl.num_programs(1) - 1)
    def _():
        sT_ref[0] = st_s[...]


def _dot_exact_rhs_lhs(tri_bf16, g):
    g_hi = g.astype(BF16)
    r1 = g - g_hi.astype(F32)
    g_mid = r1.astype(BF16)
    g_lo = (r1 - g_mid.astype(F32)).astype(BF16)
    return _dot(tri_bf16, g_hi) + _dot(tri_bf16, g_mid) + _dot(tri_bf16, g_lo)


def _hgrn_scan(q, g, k, hin, s0, *, reverse, base_row, n_seq, tb):
    bsz = s0.shape[0]
    nblk = n_seq // tb
    base_blk = base_row // tb

    def rows(b, j):
        jj = (nblk - 1 - j) if reverse else j
        return (base_blk + b * nblk + jj, 0)

    def orow(b, j):
        jj = (nblk - 1 - j) if reverse else j
        return (b * nblk + jj, 0)

    kern = functools.partial(_hgrn_kernel, reverse=reverse, nchunk=tb // HG_CHUNK)
    return pl.pallas_call(
        kern,
        grid=(bsz, nblk),
        in_specs=[pl.BlockSpec((tb, HG_QK), rows),
                  pl.BlockSpec((tb, HG_QK), rows),
                  pl.BlockSpec((tb, HG_QK), rows),
                  pl.BlockSpec((tb, HG_V), rows),
                  pl.BlockSpec((1, HG_HEADS, HG_DV, HG_DK), lambda b, j: (b, 0, 0, 0))],
        out_specs=[pl.BlockSpec((tb, HG_V), orow),
                   pl.BlockSpec((1, HG_HEADS, HG_DV, HG_DK), lambda b, j: (b, 0, 0, 0))],
        out_shape=[jax.ShapeDtypeStruct((bsz * n_seq, HG_V), F32),
                   jax.ShapeDtypeStruct((bsz, HG_HEADS, HG_DV, HG_DK), F32)],
        scratch_shapes=[pltpu.VMEM((HG_HEADS, HG_DV, HG_DK), F32)],
        name="hgrn_bwd" if reverse else "hgrn_fwd",
        compiler_params=_cparams(("parallel", "arbitrary")),
    )(q, g, k, hin, s0)


def _rms(x, g):
    return x * lax.rsqrt(jnp.mean(x * x, axis=-1, keepdims=True) + EPS) * g


def _mla_proj_kernel(cq_ref, ckv_ref, kr_ref, krs_ref, cos_ref, sin_ref, qg_ref, kvg_ref,
                     wq_ref, wqs_ref, wk_ref, wvt_ref, q_out, k_out, vt_out):
    cos = cos_ref[...]
    sin = sin_ref[...]
    cos_h = jnp.concatenate([cos] * MLA_HEADS, axis=1)
    sin_h = jnp.concatenate([sin] * MLA_HEADS, axis=1)
    xq = _rms(cq_ref[...], qg_ref[...]).astype(BF16)
    q = _dot(xq, wq_ref[...]) * cos_h + _dot(xq, wqs_ref[...]) * sin_h
    q_out[...] = (q * (MLA_SCALE * LOG2_E)).astype(BF16)
    xkv = _rms(ckv_ref[...], kvg_ref[...]).astype(BF16)
    k_rope = kr_ref[...] * cos + krs_ref[...] * sin
    k = _dot(xkv, wk_ref[...]) + jnp.concatenate([k_rope] * MLA_HEADS, axis=1)
    k_out[...] = k.astype(BF16)
    vt_out[0] = _dot_nt(wvt_ref[...], xkv).astype(BF16)


def _mla_proj(cq, ckv, kr, krs, cos_t, sin_t, qg, kvg, wq, wqs, wk, wvt, layer, tm):
    n = cq.shape[0]
    row = lambda i: (i, 0)
    full = lambda i: (0, 0)
    lay = lambda i: (layer, 0, 0)
    hq = MLA_HEADS * HEAD_LANES
    hv = MLA_HEADS * MLA_V
    return pl.pallas_call(
        _mla_proj_kernel,
        grid=(n // tm,),
        in_specs=[pl.BlockSpec((tm, MLA_Q_LORA), row), pl.BlockSpec((tm, MLA_KV_LORA), row),
                  pl.BlockSpec((tm, HEAD_LANES), row), pl.BlockSpec((tm, HEAD_LANES), row),
                  pl.BlockSpec((tm, HEAD_LANES), row), pl.BlockSpec((tm, HEAD_LANES), row),
                  pl.BlockSpec((1, MLA_Q_LORA), full), pl.BlockSpec((1, MLA_KV_LORA), full),
                  pl.BlockSpec((None, MLA_Q_LORA, hq), lay), pl.BlockSpec((None, MLA_Q_LORA, hq), lay),
                  pl.BlockSpec((None, MLA_KV_LORA, hq), lay), pl.BlockSpec((None, hv, MLA_KV_LORA), lay)],
        out_specs=[pl.BlockSpec((tm, hq), row), pl.BlockSpec((tm, hq), row),
                   pl.BlockSpec((1, hv, tm), lambda i: (i, 0, 0))],
        out_shape=[jax.ShapeDtypeStruct((n, hq), BF16), jax.ShapeDtypeStruct((n, hq), BF16),
                   jax.ShapeDtypeStruct((n // tm, hv, tm), BF16)],
        name="mla_proj",
        compiler_params=_cparams(("parallel",)),
    )(cq, ckv, kr, krs, cos_t, sin_t, qg, kvg, wq, wqs, wk, wvt)


ACC_ROWS = MLA_V + 16
LAT_UNROLL = 8
LOOKAHEAD = 2


def _attn_kernel(*refs, tkc, n_lat_chunks, n_ctx_chunks):
    if n_lat_chunks:
        q_ref, kl_ref, vl_ref, kc_ref, vc_ref, o_ref, m_s, acc_s, s_s = refs
    else:
        q_ref, kc_ref, vc_ref, o_ref, m_s, acc_s, s_s = refs
    m_s[...] = jnp.full(m_s.shape, -jnp.inf, F32)
    acc_s[...] = jnp.zeros(acc_s.shape, F32)
    ones = jnp.ones((ACC_ROWS - MLA_V, tkc), BF16)

    def scores(a, k_ref, r0):
        hs = slice(a * HEAD_LANES, (a + 1) * HEAD_LANES)
        return _dot_nt(k_ref[pl.ds(r0, tkc), hs], q_ref[:, hs])

    def accumulate(a, s_t, vt):
        m_old = m_s[a]
        m_new = jnp.maximum(m_old, jnp.max(s_t, axis=0, keepdims=True))
        alpha = jnp.exp2(m_old - m_new)
        p_t = jnp.exp2(s_t - m_new).astype(BF16)
        vta = jnp.concatenate([vt[a * MLA_V:(a + 1) * MLA_V], ones], axis=0)
        acc_s[a] = alpha * acc_s[a] + _dot(vta, p_t)
        m_s[a] = m_new

    def run(items, queue, after):
        upcoming = items[LOOKAHEAD:] + after
        queue = list(queue)
        for idx, (a, _, _, vt) in enumerate(items):
            if idx < len(upcoming):
                queue.append(scores(*upcoming[idx][:3]))
            accumulate(a, queue.pop(0), vt)
        return queue

    ahead_chunks = (LOOKAHEAD + 1) // 2
    unroll = int(np.gcd(LAT_UNROLL, n_lat_chunks)) if n_lat_chunks else 1

    def lat_items(c0, dynamic, count=None):
        items = []
        for u in range(unroll if count is None else count):
            cc = c0 + u
            r0 = pl.multiple_of(cc * tkc, tkc) if dynamic else cc * tkc
            vt = vl_ref[cc] if count is None else None
            items += [(0, kl_ref, r0, vt), (1, kl_ref, r0, vt)]
        return items

    ctx_items = []
    for c in range(n_ctx_chunks):
        ctx_items += [(0, kc_ref, c * tkc, vc_ref[c]), (1, kc_ref, c * tkc, vc_ref[c])]

    if n_lat_chunks:
        n_groups = n_lat_chunks // unroll
        for d, it in enumerate(lat_items(0, False, count=ahead_chunks)[:LOOKAHEAD]):
            s_s[d] = scores(*it[:3])

        def body(g, carry):
            c0 = g * unroll
            after = lat_items(c0 + unroll, True, count=ahead_chunks)[:LOOKAHEAD]
            queue = run(lat_items(c0, True), [s_s[d] for d in range(LOOKAHEAD)], after)
            for d in range(LOOKAHEAD):
                s_s[d] = queue[d]
            return carry

        lax.fori_loop(0, n_groups - 1, body, 0)
        run(lat_items((n_groups - 1) * unroll, False) + ctx_items, [s_s[d] for d in range(LOOKAHEAD)], [])
    else:
        run(ctx_items, [scores(*it[:3]) for it in ctx_items[:LOOKAHEAD]], [])
    outs = []
    for a in range(2):
        acc = acc_s[a]
        outs.append(acc[0:MLA_V] / acc[MLA_V:MLA_V + 1])
    o_ref[...] = jnp.concatenate(outs, axis=0).T.astype(o_ref.dtype)


def _attention(q, k, vt, *, bsz, n_q, q_base, n_lat, n_ctx, ctx_base, tq, tkc, with_lat):
    pairs = MLA_HEADS // 2
    qblocks = n_q // tq
    qb0 = q_base // tq
    cb0 = ctx_base // n_ctx
    kw = 2 * HEAD_LANES
    vw = 2 * MLA_V
    in_specs = [pl.BlockSpec((tq, kw), lambda b, p, i: (qb0 + b * qblocks + i, p))]
    args = [q]
    if with_lat:
        in_specs += [pl.BlockSpec((n_lat, kw), lambda b, p, i: (b, p)),
                     pl.BlockSpec((n_lat // tkc, vw, tkc), lambda b, p, i: (b, p, 0))]
        args += [k, vt]
    in_specs += [pl.BlockSpec((n_ctx, kw), lambda b, p, i: (cb0 + b, p)),
                 pl.BlockSpec((n_ctx // tkc, vw, tkc), lambda b, p, i: (cb0 + b, p, 0))]
    args += [k, vt]
    kern = functools.partial(_attn_kernel, tkc=tkc, n_lat_chunks=(n_lat // tkc) if with_lat else 0,
                             n_ctx_chunks=n_ctx // tkc)
    return pl.pallas_call(
        kern,
        grid=(bsz, pairs, qblocks),
        in_specs=in_specs,
        out_specs=pl.BlockSpec((tq, vw), lambda b, p, i: (b * qblocks + i, p)),
        out_shape=jax.ShapeDtypeStruct((bsz * n_q, MLA_HEADS * MLA_V), BF16),
        scratch_shapes=[pltpu.VMEM((2, 1, tq), F32), pltpu.VMEM((2, ACC_ROWS, tq), F32),
                        pltpu.VMEM((LOOKAHEAD, tkc, tq), F32)],
        name="attn_lat" if with_lat else "attn_ctx",
        compiler_params=_cparams(("parallel", "parallel", "arbitrary")),
    )(*args)


def _outproj_kernel(a_ref, ofl_ref, obl_ref, cl_ref, ofc_ref, obc_ref, cc_ref, hg_ref, x_ref, g1_ref, sh2_ref,
                    sc2_ref, lng_ref, lnb_ref, ng_ref, gavg_ref, w_ref, wr_ref, x1_out, h2_out, lg_out,
                    *, alpha, lat_tiles):
    is_ctx = pl.program_id(0) >= lat_tiles
    tm = x_ref.shape[0]
    half = tm // OUT_ROW_SPLIT
    for r in range(OUT_ROW_SPLIT):
        rs = slice(r * half, (r + 1) * half)
        o = jnp.where(is_ctx, ofc_ref[rs, :] + obc_ref[rs, :], ofl_ref[rs, :] + obl_ref[rs, :])
        c_att = jnp.where(is_ctx, cc_ref[rs, :], cl_ref[rs, :])
        o2_hi, o2_lo = _split_bf16(o * o)
        ms = _dot(o2_hi, gavg_ref[...]) + _dot(o2_lo, gavg_ref[...])
        b_lat = o * lax.rsqrt(ms + EPS) * ng_ref[...] * _silu(hg_ref[rs, :])
        m = (_dot(a_ref[rs, :].astype(BF16), w_ref[0:POOL_DIM, :])
             + _dot(b_lat.astype(BF16), w_ref[POOL_DIM:POOL_DIM + HG_V, :])
             + _dot(c_att, w_ref[POOL_DIM + HG_V:, :]))
        x1 = _layer_norm(alpha * x_ref[rs, :] + g1_ref[0] * m) * lng_ref[...] + lnb_ref[...]
        x1_out[rs, :] = x1
        h2 = _layer_norm(x1) * (1.0 + sc2_ref[0]) + sh2_ref[0]
        h2_out[rs, :] = h2
        lg_out[rs, :] = _dot3(h2, wr_ref[...])


def _out_proj(a, lat_parts, ctx_parts, hg, x, g1, sh2, sc2, ln_g, ln_b, norm_g, gavg, w_out, w_router, layer, tm,
              mod_index, alpha):
    n, d = x.shape
    lat_tiles = lat_parts[0].shape[0] // tm
    row = lambda i: (i, 0)
    lat_row = lambda i: (jnp.minimum(i, lat_tiles - 1), 0)
    ctx_row = lambda i: (jnp.maximum(i - lat_tiles, 0), 0)
    full = lambda i: (0, 0)
    lay = lambda i: (layer, 0, 0)
    mod = lambda i: (mod_index(i), 0, 0)
    hv = MLA_HEADS * MLA_V
    return pl.pallas_call(
        functools.partial(_outproj_kernel, alpha=alpha, lat_tiles=lat_tiles),
        grid=(n // tm,),
        in_specs=[pl.BlockSpec((tm, POOL_DIM), row),
                  pl.BlockSpec((tm, HG_V), lat_row), pl.BlockSpec((tm, HG_V), lat_row), pl.BlockSpec((tm, hv), lat_row),
                  pl.BlockSpec((tm, HG_V), ctx_row), pl.BlockSpec((tm, HG_V), ctx_row), pl.BlockSpec((tm, hv), ctx_row),
                  pl.BlockSpec((tm, HG_V), row),
                  pl.BlockSpec((tm, d), row),
                  pl.BlockSpec((1, 1, d), mod), pl.BlockSpec((1, 1, d), mod), pl.BlockSpec((1, 1, d), mod),
                  pl.BlockSpec((1, d), full), pl.BlockSpec((1, d), full), pl.BlockSpec((1, HG_V), full),
                  pl.BlockSpec((HG_V, HG_V), full), pl.BlockSpec((None, d, d), lay),
                  pl.BlockSpec((None, d, ROUTER_LANES), lay)],
        out_specs=[pl.BlockSpec((tm, d), row), pl.BlockSpec((tm, d), row), pl.BlockSpec((tm, ROUTER_LANES), row)],
        out_shape=[jax.ShapeDtypeStruct((n, d), F32), jax.ShapeDtypeStruct((n, d), F32),
                   jax.ShapeDtypeStruct((n, ROUTER_LANES), F32)],
        name="out_proj",
        compiler_params=_cparams(("parallel",)),
    )(a, *lat_parts, *ctx_parts, hg, x, g1, sh2, sc2, ln_g, ln_b, norm_g, gavg, w_out, w_router)


def _moe_kernel(be_ref, nv_ref, x_ref, w1_ref, w3_ref, w2_ref, y_ref, w1b, w3b, w2b):
    i = pl.program_id(0)

    @pl.when((i == 0) | (be_ref[i] != be_ref[jnp.maximum(i - 1, 0)]))
    def _():
        w1b[...] = w1_ref[0, 0].astype(BF16)
        w3b[...] = w3_ref[0, 0].astype(BF16)
        w2b[...] = w2_ref[0, 0].astype(BF16)

    @pl.when(i < nv_ref[0])
    def _():
        x = x_ref[...].astype(BF16)
        h1 = _dot(x, w1b[...])
        h3 = _dot(x, w3b[...])
        h = (_silu(h1) * h3).astype(BF16)
        y_ref[...] = _dot(h, w2b[...]).astype(y_ref.dtype)

    @pl.when(i >= nv_ref[0])
    def _():
        y_ref[...] = jnp.zeros(y_ref.shape, y_ref.dtype)


def _moe_ffn(block_e, n_valid, xr, w1, w3, w2, layer, tm):
    n_rows, d = xr.shape
    n_blocks = n_rows // tm
    grid_spec = pltpu.PrefetchScalarGridSpec(
        num_scalar_prefetch=2,
        grid=(n_blocks,),
        in_specs=[pl.BlockSpec((tm, d), lambda i, be, nv: (jnp.minimum(i, nv[0] - 1), 0)),
                  pl.BlockSpec((1, 1, d, D_EXPERT), lambda i, be, nv: (layer, be[i], 0, 0)),
                  pl.BlockSpec((1, 1, d, D_EXPERT), lambda i, be, nv: (layer, be[i], 0, 0)),
                  pl.BlockSpec((1, 1, D_EXPERT, d), lambda i, be, nv: (layer, be[i], 0, 0))],
        out_specs=pl.BlockSpec((tm, d), lambda i, be, nv: (i, 0)),
        scratch_shapes=[pltpu.VMEM((d, D_EXPERT), BF16), pltpu.VMEM((d, D_EXPERT), BF16),
                        pltpu.VMEM((D_EXPERT, d), BF16)],
    )
    return pl.pallas_call(
        _moe_kernel,
        grid_spec=grid_spec,
        out_shape=jax.ShapeDtypeStruct((n_rows, d), BF16),
        name="moe_ffn",
        compiler_params=_cparams(("arbitrary",)),
    )(block_e, n_valid, xr, w1, w3, w2)


def _combine_kernel(x_ref, y0_ref, y1_ref, gw_ref, g2_ref, lng_ref, lnb_ref, o_ref, *, alpha):
    gw = gw_ref[...]
    y = gw[:, 0:1] * y0_ref[...].astype(F32) + gw[:, 1:2] * y1_ref[...].astype(F32)
    o_ref[...] = _layer_norm(alpha * x_ref[...] + g2_ref[0] * y) * lng_ref[...] + lnb_ref[...]


def _combine(x, y0, y1, gw, g2, ln_g, ln_b, tm, mod_index, alpha):
    n, d = x.shape
    row = lambda i: (i, 0)
    full = lambda i: (0, 0)
    return pl.pallas_call(
        functools.partial(_combine_kernel, alpha=alpha),
        grid=(n // tm,),
        in_specs=[pl.BlockSpec((tm, d), row), pl.BlockSpec((tm, d), row), pl.BlockSpec((tm, d), row),
                  pl.BlockSpec((tm, ROUTER_LANES), row),
                  pl.BlockSpec((1, 1, d), lambda i: (mod_index(i), 0, 0)),
                  pl.BlockSpec((1, d), full), pl.BlockSpec((1, d), full)],
        out_specs=pl.BlockSpec((tm, d), row),
        out_shape=jax.ShapeDtypeStruct((n, d), F32),
        name="moe_combine",
        compiler_params=_cparams(("parallel",)),
    )(x, y0, y1, gw, g2, ln_g, ln_b)


def _rot_half_cols(w):
    lead = w.shape[:-1]
    wr = w.reshape(lead + (2, 2, MLA_ROPE // 4))
    return jnp.stack([-wr[..., 1, :], wr[..., 0, :]], axis=-2).reshape(w.shape)


def _rope_tables(n_lat, n_ctx, bsz):
    rows = n_lat // GRID_W
    row = jnp.repeat(jnp.arange(rows, dtype=F32), GRID_W)
    col = jnp.tile(jnp.arange(GRID_W, dtype=F32), rows)
    n_freq = MLA_ROPE // 4
    inv_freq = ROPE_BASE ** (-jnp.arange(n_freq, dtype=F32) / n_freq)
    ang = jnp.stack([row[:, None] * inv_freq, col[:, None] * inv_freq], axis=1)
    cos = jnp.broadcast_to(jnp.cos(ang)[:, :, None, :], (n_lat, 2, 2, n_freq)).reshape(n_lat, MLA_ROPE)
    sin = jnp.broadcast_to(jnp.sin(ang)[:, :, None, :], (n_lat, 2, 2, n_freq)).reshape(n_lat, MLA_ROPE)
    pad_l = MLA_NOPE
    pad_r = HEAD_LANES - MLA_QK
    cos_t = jnp.concatenate([jnp.ones((n_lat, pad_l), F32), cos, jnp.ones((n_lat, pad_r), F32)], axis=1)
    sin_t = jnp.concatenate([jnp.zeros((n_lat, pad_l), F32), sin, jnp.zeros((n_lat, pad_r), F32)], axis=1)
    cos_all = jnp.concatenate([jnp.tile(cos_t, (bsz, 1)), jnp.ones((bsz * n_ctx, HEAD_LANES), F32)], axis=0)
    sin_all = jnp.concatenate([jnp.tile(sin_t, (bsz, 1)), jnp.zeros((bsz * n_ctx, HEAD_LANES), F32)], axis=0)
    return cos_all, sin_all


def _prep_w_in(w_in):
    depth, d, _ = w_in.shape
    kr = w_in[:, :, W_IN_MAIN:]
    z_l = jnp.zeros((depth, d, MLA_NOPE), w_in.dtype)
    z_r = jnp.zeros((depth, d, HEAD_LANES - MLA_QK), w_in.dtype)
    return jnp.concatenate([w_in[:, :, :W_IN_MAIN], z_l, kr, z_r, z_l, _rot_half_cols(kr), z_r],
                           axis=-1).astype(BF16)


def _prep_mla(w_uq, w_ukv):
    depth = w_uq.shape[0]
    wq = w_uq.reshape(depth, MLA_Q_LORA, MLA_HEADS, MLA_QK)
    pad = jnp.zeros((depth, MLA_Q_LORA, MLA_HEADS, HEAD_LANES - MLA_QK), w_uq.dtype)
    wq_p = jnp.concatenate([wq, pad], axis=-1).reshape(depth, MLA_Q_LORA, MLA_HEADS * HEAD_LANES)
    wq_s = jnp.concatenate([jnp.zeros_like(wq[..., :MLA_NOPE]), _rot_half_cols(wq[..., MLA_NOPE:]), pad],
                           axis=-1).reshape(depth, MLA_Q_LORA, MLA_HEADS * HEAD_LANES)
    wkv = w_ukv.reshape(depth, MLA_KV_LORA, MLA_HEADS, MLA_NOPE + MLA_V)
    kpad = jnp.zeros((depth, MLA_KV_LORA, MLA_HEADS, HEAD_LANES - MLA_NOPE), w_ukv.dtype)
    wk = jnp.concatenate([wkv[..., :MLA_NOPE], kpad], axis=-1).reshape(depth, MLA_KV_LORA, MLA_HEADS * HEAD_LANES)
    wvt = jnp.swapaxes(wkv[..., MLA_NOPE:].reshape(depth, MLA_KV_LORA, MLA_HEADS * MLA_V), 1, 2)
    return wq_p.astype(BF16), wq_s.astype(BF16), wk.astype(BF16), wvt.astype(BF16)


def _block_diag(pool_w):
    depth, g, c, _ = pool_w.shape
    eye = jnp.eye(g, dtype=pool_w.dtype)
    return jnp.einsum("lgcd,gh->lgchd", pool_w, eye).reshape(depth, g * c, g * c).astype(BF16)


def _route(logits, b_rg, b_re, tm_moe):
    n = logits.shape[0]
    lg = logits[:, :N_GROUPS] + b_rg
    le = (logits[:, N_GROUPS:N_GROUPS + N_EXPERTS] + b_re).reshape(n, N_GROUPS, EXPERTS_PER_GROUP)
    pg = jax.nn.softmax(lg, axis=-1)
    g_sel = jnp.argmax(lg, axis=-1).astype(jnp.int32)
    g_hot = g_sel[:, None] == jnp.arange(N_GROUPS, dtype=jnp.int32)[None, :]
    w_g = jnp.sum(jnp.where(g_hot, pg, 0.0), axis=-1)
    le_sel = jnp.sum(jnp.where(g_hot[:, :, None], le, 0.0), axis=1)
    top_v, top_i = lax.top_k(le_sel, TOP_K)
    gate = jax.nn.softmax(top_v, axis=-1) * w_g[:, None]
    expert = (g_sel[:, None] * EXPERTS_PER_GROUP + top_i).astype(jnp.int32)

    n_assign = n * TOP_K
    e_flat = expert.reshape(-1)
    hot = e_flat[:, None] == jnp.arange(N_EXPERTS, dtype=jnp.int32)[None, :]
    nb = n_assign // COUNT_BLOCK
    tri = jnp.tril(jnp.ones((COUNT_BLOCK, COUNT_BLOCK), BF16))
    win = jnp.einsum("ts,bse->bte", tri, hot.astype(BF16).reshape(nb, COUNT_BLOCK, N_EXPERTS),
                     preferred_element_type=F32)
    bsum = win[:, -1, :]
    boff = jnp.cumsum(bsum, axis=0) - bsum
    csum = (win + boff[:, None, :]).reshape(n_assign, N_EXPERTS).astype(jnp.int32)
    counts = (boff[-1] + bsum[-1]).astype(jnp.int32)
    pcounts = (counts + tm_moe - 1) // tm_moe * tm_moe
    pends = jnp.cumsum(pcounts)
    pstarts = pends - pcounts
    starts = jnp.cumsum(counts) - counts
    dest = jnp.sum(jnp.where(hot, csum - 1 + pstarts[None, :], 0), axis=1)
    n_rows = (n_assign + N_EXPERTS * (tm_moe - 1) + tm_moe - 1) // tm_moe * tm_moe
    n_blocks = n_rows // tm_moe
    n_valid = (pends[-1] // tm_moe).astype(jnp.int32)
    blk = jnp.arange(n_blocks, dtype=jnp.int32)
    first_row = jnp.minimum(blk, n_valid - 1) * tm_moe
    block_e = jnp.sum((pends[None, :] <= first_row[:, None]).astype(jnp.int32), axis=1)
    block_e = jnp.clip(block_e, 0, N_EXPERTS - 1)
    a_sorted = jnp.sort(e_flat * n_assign + jnp.arange(n_assign, dtype=jnp.int32)) % n_assign
    pos = (blk * tm_moe - pstarts[block_e])[:, None] + jnp.arange(tm_moe, dtype=jnp.int32)[None, :]
    src = jnp.clip(starts[block_e][:, None] + pos, 0, n_assign - 1)
    filler = (blk[:, None] * tm_moe + jnp.arange(tm_moe, dtype=jnp.int32)[None, :]) % n
    row_tok = jnp.where(pos < counts[block_e][:, None], a_sorted[src] // TOP_K, filler)
    gw = jnp.zeros((n, ROUTER_LANES), F32).at[:, :TOP_K].set(gate)
    return row_tok.reshape(n_rows), dest.reshape(n, TOP_K), gw, block_e, n_valid.reshape(1)


def _tile(n, pref):
    t = pref
    while n % t:
        t //= 2
    return t


def kernel(x, c, ctx, c_ctx, w_mod, b_mod, w_in, pool_w, pool_scale, hg_lb_logits, hg_norm_g, mla_q_norm_g, mla_w_uq, mla_kv_norm_g, mla_w_ukv, w_out, ln1_g, ln1_b, ln2_g, ln2_b, router_group_w, router_group_b, router_expert_w, router_expert_b, expert_w1, expert_w3, expert_w2):
    bsz, n_lat, d = x.shape
    n_ctx = ctx.shape[1]
    depth = w_mod.shape[0]
    alpha = (2 * depth) ** 0.25
    lat_rows = bsz * n_lat
    ctx_rows = bsz * n_ctx
    n_tok = lat_rows + ctx_rows

    tm = _tile(np.gcd(n_lat, ctx_rows), 512)
    tp = _tile(np.gcd(n_lat, n_ctx), 256)
    tb_lat = _tile(n_lat, 512)
    tb_ctx = _tile(n_ctx, 512)
    tq_lat = _tile(n_lat, 512)
    tkc = _tile(np.gcd(n_lat, n_ctx), 256)
    tq_ctx = _tile(n_ctx, 512)
    tm_moe = 256
    lat_tiles_per_batch = n_lat // tm
    mod_index = lambda i: jnp.minimum(i // lat_tiles_per_batch, bsz)

    c_rows = jnp.concatenate([jax.nn.silu(c), jax.nn.silu(c_ctx)[None, :],
                              jnp.zeros((-(bsz + 1) % 8, d), F32)], axis=0)
    mod_all = _mod_all(c_rows, w_mod, b_mod)
    lb_all = jnp.cumsum(jax.nn.softmax(hg_lb_logits.astype(F32), axis=0), axis=0)
    lb_all = lb_all - lb_all[0:1]
    log_lb = jnp.log(lb_all)
    log_1m_lb = jnp.log1p(-lb_all)
    w_aug = _prep_w_in(w_in)
    wq_p, wq_s, wk_p, wvt_p = _prep_mla(mla_w_uq, mla_w_ukv)
    w_bd = _block_diag(pool_w)
    w_out_b = w_out.astype(BF16)
    w_router = jnp.concatenate([router_group_w, router_expert_w,
                                jnp.zeros((depth, d, ROUTER_LANES - N_GROUPS - N_EXPERTS), F32)], axis=-1)
    gavg = jnp.kron(jnp.eye(HG_HEADS, dtype=F32), jnp.full((HG_DV, HG_DV), 1.0 / HG_DV, F32)).astype(BF16)
    cos_t, sin_t = _rope_tables(n_lat, n_ctx, bsz)
    zero_state = jnp.zeros((bsz, HG_HEADS, HG_DV, HG_DK), F32)

    xt = jnp.concatenate([x.reshape(lat_rows, d), ctx.reshape(ctx_rows, d)], axis=0)
    for l in range(depth):
        mod = mod_all[l, :bsz + 1].reshape(bsz + 1, 6, 1, d)
        sh1, sc1, g1, sh2, sc2, g2 = (mod[:, k] for k in range(6))

        gate_consts = jnp.stack([log_lb[l, 0], log_1m_lb[l, 0], log_lb[l, 1], log_1m_lb[l, 1]], axis=0)
        p_in, hq, gf, kf, gb, kb, hin, hg, cq, ckv, kr, krs = _in_proj(xt, sh1, sc1, gate_consts, w_aug, l, tm,
                                                                       mod_index)

        a_mix = _pool(p_in, w_bd[l], pool_scale[l][None, :], tp, lat_rows, n_lat, n_ctx)

        oc_f, s_f = _hgrn_scan(hq, gf, kf, hin, zero_state, reverse=False, base_row=lat_rows, n_seq=n_ctx, tb=tb_ctx)
        oc_b, s_b = _hgrn_scan(hq, gb, kb, hin, zero_state, reverse=True, base_row=lat_rows, n_seq=n_ctx, tb=tb_ctx)
        ol_f, _ = _hgrn_scan(hq, gf, kf, hin, s_f, reverse=False, base_row=0, n_seq=n_lat, tb=tb_lat)
        ol_b, _ = _hgrn_scan(hq, gb, kb, hin, s_b, reverse=True, base_row=0, n_seq=n_lat, tb=tb_lat)

        q_all, k_all, vt_all = _mla_proj(cq, ckv, kr, krs, cos_t, sin_t, mla_q_norm_g[l][None, :],
                                         mla_kv_norm_g[l][None, :], wq_p, wq_s, wk_p, wvt_p, l, tkc)
        c_lat = _attention(q_all, k_all, vt_all, bsz=bsz, n_q=n_lat, q_base=0, n_lat=n_lat, n_ctx=n_ctx,
                           ctx_base=lat_rows, tq=tq_lat, tkc=tkc, with_lat=True)
        c_ctx_o = _attention(q_all, k_all, vt_all, bsz=bsz, n_q=n_ctx, q_base=lat_rows, n_lat=n_lat, n_ctx=n_ctx,
                             ctx_base=lat_rows, tq=tq_ctx, tkc=tkc, with_lat=False)

        x1, h2, logits = _out_proj(a_mix, (ol_f, ol_b, c_lat), (oc_f, oc_b, c_ctx_o), hg, xt, g1, sh2, sc2,
                                   ln1_g[l][None, :],
                                   ln1_b[l][None, :], hg_norm_g[l][None, :], gavg, w_out_b, w_router, l,
                                   tm, mod_index, alpha)

        row_tok, dest, gw, block_e, n_valid = _route(logits, router_group_b[l], router_expert_b[l], tm_moe)
        xr = jnp.take(h2, row_tok, axis=0, mode="clip")
        y = _moe_ffn(block_e, n_valid, xr, expert_w1, expert_w3, expert_w2, l, tm_moe)
        y0 = jnp.take(y, dest[:, 0], axis=0, mode="clip")
        y1 = jnp.take(y, dest[:, 1], axis=0, mode="clip")
        xt = _combine(x1, y0, y1, gw, g2, ln2_g[l][None, :], ln2_b[l][None, :], tm, mod_index, alpha)
    return xt[:lat_rows].reshape(bsz, n_lat, d)
```

```python
import functools

import numpy as np
import jax
import jax.numpy as jnp
from jax import lax
from jax.experimental import pallas as pl
from jax.experimental.pallas import tpu as pltpu

F32 = jnp.float32
BF16 = jnp.bfloat16

D_MODEL = 1024
GRID_W = 64
POOL_WINDOWS = (2, 4, 8, 16)
POOL_DIM = 256
POOL_GROUP_DIM = 64
HG_HEADS = 4
HG_DK = 128
HG_DV = 64
HG_QK = HG_HEADS * HG_DK
HG_V = HG_HEADS * HG_DV
HG_CHUNK = 64
HG_SUB_LOG2 = 4
HG_SUB = 1 << HG_SUB_LOG2
MLA_HEADS = 8
MLA_NOPE = 64
MLA_ROPE = 32
MLA_V = 64
MLA_Q_LORA = 256
MLA_KV_LORA = 128
MLA_QK = MLA_NOPE + MLA_ROPE
MLA_SCALE = MLA_QK ** -0.5
LOG2_E = 1.4426950408889634
ROPE_BASE = 10000.0
HEAD_LANES = 128
N_GROUPS = 4
EXPERTS_PER_GROUP = 8
N_EXPERTS = N_GROUPS * EXPERTS_PER_GROUP
TOP_K = 2
D_EXPERT = 512
EPS = 1e-6
ROUTER_LANES = 128
COUNT_BLOCK = 128
OUT_ROW_SPLIT = 2
IN_SEGS = (("p_in", 256), ("hq", 512), ("hzf", 512), ("hzb", 512), ("hin", 256), ("hg", 256),
           ("cq", 256), ("ckv", 128), ("kr", 128), ("krs", 128))
IN_AUG = sum(w for _, w in IN_SEGS)
W_IN_MAIN = 2688

VMEM_LIMIT = 56 * 1024 * 1024


def _cparams(sem):
    return pltpu.CompilerParams(dimension_semantics=sem, vmem_limit_bytes=VMEM_LIMIT)


def _split_bf16(a):
    hi = a.astype(BF16)
    lo = (a - hi.astype(F32)).astype(BF16)
    return hi, lo


def _dot(a, b):
    return jnp.dot(a, b, preferred_element_type=F32)


def _dot_nt(a, b):
    return lax.dot_general(a, b, (((1,), (1,)), ((), ())), preferred_element_type=F32)


def _dot_tn(a, b):
    return lax.dot_general(a, b, (((0,), (0,)), ((), ())), preferred_element_type=F32)


def _dot3(a, b):
    a_hi, a_lo = _split_bf16(a)
    b_hi, b_lo = _split_bf16(b)
    return _dot(a_hi, b_hi) + _dot(a_hi, b_lo) + _dot(a_lo, b_hi)


def _dot_exact_rhs(a, b_bf16):
    a_hi = a.astype(BF16)
    r1 = a - a_hi.astype(F32)
    a_mid = r1.astype(BF16)
    a_lo = (r1 - a_mid.astype(F32)).astype(BF16)
    return _dot(a_hi, b_bf16) + _dot(a_mid, b_bf16) + _dot(a_lo, b_bf16)


def _layer_norm(x):
    mu = jnp.mean(x, axis=-1, keepdims=True)
    xc = x - mu
    var = jnp.mean(xc * xc, axis=-1, keepdims=True)
    return xc * lax.rsqrt(var + EPS)


def _sigmoid(x):
    return 1.0 / (1.0 + jnp.exp(-x))


def _silu(x):
    return x * _sigmoid(x)


def _mod_kernel(a_ref, w_ref, b_ref, o_ref):
    o_ref[0] = _dot3(a_ref[...], w_ref[0]) + b_ref[0]


def _mod_all(c_rows, w_mod, b_mod):
    depth, d, n6 = w_mod.shape
    tn = 1536
    rows = c_rows.shape[0]
    return pl.pallas_call(
        _mod_kernel,
        grid=(depth, n6 // tn),
        in_specs=[pl.BlockSpec((rows, d), lambda l, j: (0, 0)),
                  pl.BlockSpec((1, d, tn), lambda l, j: (l, 0, j)),
                  pl.BlockSpec((1, 1, tn), lambda l, j: (l, 0, j))],
        out_specs=pl.BlockSpec((1, rows, tn), lambda l, j: (l, 0, j)),
        out_shape=jax.ShapeDtypeStruct((depth, rows, n6), F32),
        name="adaln_mod",
        compiler_params=_cparams(("arbitrary", "arbitrary")),
    )(c_rows, w_mod, b_mod.reshape(depth, 1, n6))


def _forget_gates(z, la, lc):
    ls = jnp.minimum(z, 0.0) - jnp.log(1.0 + jnp.exp(-jnp.abs(z)))
    x2 = lc + ls
    g = jnp.maximum(la, x2) + jnp.log(1.0 + jnp.exp(-jnp.abs(la - x2)))
    k = jnp.exp(x2 - z)
    return g, k


IN_OUTS = (("p_in", 256), ("q", 512), ("gf", 512), ("kf", 512), ("gb", 512), ("kb", 512), ("hin", 256), ("hg", 256),
           ("cq", 256), ("ckv", 128), ("kr", 128), ("krs", 128))


def _inproj_kernel(x_ref, sh_ref, sc_ref, gc_ref, w_ref, *out_refs):
    outs = dict(zip((name for name, _ in IN_OUTS), out_refs))
    h = _layer_norm(x_ref[...]) * (1.0 + sc_ref[0]) + sh_ref[0]
    hb = h.astype(BF16)
    def finish(name, seg):
        if name == "hq":
            outs["q"][...] = _silu(seg)
        elif name == "hzf":
            outs["gf"][...], outs["kf"][...] = _forget_gates(seg, gc_ref[0:1, :], gc_ref[1:2, :])
        elif name == "hzb":
            outs["gb"][...], outs["kb"][...] = _forget_gates(seg, gc_ref[2:3, :], gc_ref[3:4, :])
        else:
            outs[name][...] = seg

    start = 0
    pending = None
    for name, width in IN_SEGS:
        seg = _dot(hb, w_ref[:, start:start + width])
        start += width
        if pending is not None:
            finish(*pending)
        pending = (name, seg)
    finish(*pending)


def _in_proj(x, sh, sc, gate_consts, w_aug, layer, tm, mod_index):
    n, d = x.shape
    row = lambda i: (i, 0)
    mod = lambda i: (mod_index(i), 0, 0)
    return pl.pallas_call(
        _inproj_kernel,
        grid=(n // tm,),
        in_specs=[pl.BlockSpec((tm, d), row),
                  pl.BlockSpec((1, 1, d), mod),
                  pl.BlockSpec((1, 1, d), mod),
                  pl.BlockSpec(gate_consts.shape, lambda i: (0, 0)),
                  pl.BlockSpec((None, d, IN_AUG), lambda i: (layer, 0, 0))],
        out_specs=[pl.BlockSpec((tm, w), row) for _, w in IN_OUTS],
        out_shape=[jax.ShapeDtypeStruct((n, w), F32) for _, w in IN_OUTS],
        name="in_proj",
        compiler_params=_cparams(("parallel",)),
    )(x, sh, sc, gate_consts, w_aug)


POOL_HALO = 8


def _pool_kernel(x_ref, prev_ref, next_ref, w_ref, scale_ref, o_ref, *, tp, lat_tiles, tpb_lat, tpb_ctx,
                 n_lat, n_ctx):
    i = pl.program_id(0)
    is_lat = i < lat_tiles
    tile_in_seq = jnp.where(is_lat, i % tpb_lat, (i - lat_tiles) % tpb_ctx)
    tiles_in_seq = jnp.where(is_lat, tpb_lat, tpb_ctx)
    n_seq = jnp.where(is_lat, n_lat, n_ctx)
    first = tile_in_seq == 0
    last = tile_in_seq == tiles_in_seq - 1
    x = x_ref[...]
    prev = jnp.where(first, 0.0, prev_ref[...])
    nxt = jnp.where(last, 0.0, next_ref[...])
    ext = jnp.concatenate([prev, x, nxt], axis=0)
    n_ext = tp + 2 * POOL_HALO

    def back(a, k):
        return pltpu.roll(a, k, 0)

    def fwd(a, k):
        return pltpu.roll(a, n_ext - k, 0)

    e2 = ext + back(ext, 1)
    e4 = back(e2, 1) + fwd(e2, 1)
    e8 = back(e4, 2) + fwd(e4, 2)
    e16 = back(e8, 4) + fwd(e8, 4)
    sl = slice(POOL_HALO, POOL_HALO + tp)
    lane = lax.broadcasted_iota(jnp.int32, (tp, POOL_DIM), 1)
    grp = lane >> 6
    wsum = jnp.where(grp == 0, e2[sl], jnp.where(grp == 1, e4[sl], jnp.where(grp == 2, e8[sl], e16[sl])))
    half = jnp.where(grp == 0, 1, jnp.where(grp == 1, 2, jnp.where(grp == 2, 4, 8)))
    t = tile_in_seq * tp + lax.broadcasted_iota(jnp.int32, (tp, POOL_DIM), 0)
    cnt = jnp.minimum(t + half, n_seq) - jnp.maximum(t - half, 0)
    pooled = wsum / cnt.astype(F32) - x
    o_ref[...] = _dot(pooled.astype(BF16), w_ref[...]) * scale_ref[...]


def _pool(p_in, w_bd, scale, tp, n_lat_rows, n_lat, n_ctx):
    n, c = p_in.shape
    hb = tp // POOL_HALO
    last_halo_block = n // POOL_HALO - 1
    kern = functools.partial(_pool_kernel, tp=tp, lat_tiles=n_lat_rows // tp, tpb_lat=n_lat // tp,
                             tpb_ctx=n_ctx // tp, n_lat=n_lat, n_ctx=n_ctx)
    return pl.pallas_call(
        kern,
        grid=(n // tp,),
        in_specs=[pl.BlockSpec((tp, c), lambda i: (i, 0)),
                  pl.BlockSpec((POOL_HALO, c), lambda i: (jnp.maximum(i * hb - 1, 0), 0)),
                  pl.BlockSpec((POOL_HALO, c), lambda i: (jnp.minimum((i + 1) * hb, last_halo_block), 0)),
                  pl.BlockSpec((c, c), lambda i: (0, 0)),
                  pl.BlockSpec((1, c), lambda i: (0, 0))],
        out_specs=pl.BlockSpec((tp, c), lambda i: (i, 0)),
        out_shape=jax.ShapeDtypeStruct((n, c), F32),
        name="pool",
        compiler_params=_cparams(("parallel",)),
    )(p_in, p_in, p_in, w_bd, scale)


def _hgrn_kernel(q_s, g_s, k_s, hin_ref, s0_ref, o_ref, sT_ref, st_s, *, reverse, nchunk):
    j = pl.program_id(1)

    @pl.when(j == 0)
    def _():
        st_s[...] = s0_ref[0]

    C = HG_CHUNK
    row = lax.broadcasted_iota(jnp.int32, (C, C), 0)
    col = lax.broadcasted_iota(jnp.int32, (C, C), 1)
    tri = ((row <= col) if reverse else (row >= col)).astype(BF16)
    rmod = lax.broadcasted_iota(jnp.int32, (C, 1), 0) & (HG_SUB - 1)
    nsub = C // HG_SUB
    row_sub = row >> HG_SUB_LOG2
    col_sub = col >> HG_SUB_LOG2

    def chunk(c, carry):
        cc = (nchunk - 1 - c) if reverse else c
        r0 = pl.multiple_of(cc * C, C)
        b_all = _dot_exact_rhs_lhs(tri, g_s[pl.ds(r0, C), :])
        q_all = q_s[pl.ds(r0, C), :]
        k_all = k_s[pl.ds(r0, C), :]
        v_all = hin_ref[pl.ds(r0, C), :]
        outs = []
        for h in range(HG_HEADS):
            ks = slice(h * HG_DK, (h + 1) * HG_DK)
            b, q, k = b_all[:, ks], q_all[:, ks], k_all[:, ks]
            v = v_all[:, h * HG_DV:(h + 1) * HG_DV]
            st = st_s[h]
            b_last = b[0:1] if reverse else b[C - 1:C]
            o = _dot_nt((q * jnp.exp(b)).astype(BF16), st.astype(BF16))
            a_rows = []
            for i in range(nsub):
                rs = slice(i * HG_SUB, (i + 1) * HG_SUB)
                if (reverse and i == nsub - 1) or (not reverse and i == 0):
                    a_rows.append(jnp.zeros((HG_SUB, C), F32))
                    continue
                m = b[(i + 1) * HG_SUB - 1:(i + 1) * HG_SUB] if reverse else b[i * HG_SUB:i * HG_SUB + 1]
                qi = q[rs] * jnp.exp(b[rs] - m)
                ksc = k * jnp.exp(m - b)
                a_rows.append(_dot_nt(qi.astype(BF16), ksc.astype(BF16)))
            a = jnp.concatenate(a_rows, axis=0)
            if reverse:
                a = jnp.where(col_sub > row_sub, a, 0.0)
            else:
                a = jnp.where(col_sub < row_sub, a, 0.0)
            o = o + _dot(a.astype(BF16), v.astype(BF16))
            for delta in range(HG_SUB):
                if delta == 0:
                    w = jnp.sum(q * k, axis=-1, keepdims=True)
                    o = o + w * v
                    continue
                shift = (C - delta) if reverse else delta
                kd = pltpu.roll(k, shift, 0)
                bd = pltpu.roll(b, shift, 0)
                vd = pltpu.roll(v, shift, 0)
                e = jnp.exp(b - bd)
                w = jnp.sum(q * kd * e, axis=-1, keepdims=True)
                valid = (rmod + delta < HG_SUB) if reverse else (rmod >= delta)
                o = o + jnp.where(valid, w, 0.0) * vd
            outs.append(o)
            kdec = k * jnp.exp(b_last - b)
            st_s[h] = st * jnp.exp(b_last) + _dot_tn(v.astype(BF16), kdec.astype(BF16))
        o_ref[pl.ds(r0, C), :] = jnp.concatenate(outs, axis=1)
        return carry

    lax.fori_loop(0, nchunk, chunk, 0)

    @pl.when(j == pl.num_programs(1) - 1)
    def _():
        sT_ref[0] = st_s[...]


def _dot_exact_rhs_lhs(tri_bf16, g):
    g_hi = g.astype(BF16)
    r1 = g - g_hi.astype(F32)
    g_mid = r1.astype(BF16)
    g_lo = (r1 - g_mid.astype(F32)).astype(BF16)
    return _dot(tri_bf16, g_hi) + _dot(tri_bf16, g_mid) + _dot(tri_bf16, g_lo)


def _hgrn_scan(q, g, k, hin, s0, *, reverse, base_row, n_seq, tb):
    bsz = s0.shape[0]
    nblk = n_seq // tb
    base_blk = base_row // tb

    def rows(b, j):
        jj = (nblk - 1 - j) if reverse else j
        return (base_blk + b * nblk + jj, 0)

    def orow(b, j):
        jj = (nblk - 1 - j) if reverse else j
        return (b * nblk + jj, 0)

    kern = functools.partial(_hgrn_kernel, reverse=reverse, nchunk=tb // HG_CHUNK)
    return pl.pallas_call(
        kern,
        grid=(bsz, nblk),
        in_specs=[pl.BlockSpec((tb, HG_QK), rows),
                  pl.BlockSpec((tb, HG_QK), rows),
                  pl.BlockSpec((tb, HG_QK), rows),
                  pl.BlockSpec((tb, HG_V), rows),
                  pl.BlockSpec((1, HG_HEADS, HG_DV, HG_DK), lambda b, j: (b, 0, 0, 0))],
        out_specs=[pl.BlockSpec((tb, HG_V), orow),
                   pl.BlockSpec((1, HG_HEADS, HG_DV, HG_DK), lambda b, j: (b, 0, 0, 0))],
        out_shape=[jax.ShapeDtypeStruct((bsz * n_seq, HG_V), F32),
                   jax.ShapeDtypeStruct((bsz, HG_HEADS, HG_DV, HG_DK), F32)],
        scratch_shapes=[pltpu.VMEM((HG_HEADS, HG_DV, HG_DK), F32)],
        name="hgrn_bwd" if reverse else "hgrn_fwd",
        compiler_params=_cparams(("parallel", "arbitrary")),
    )(q, g, k, hin, s0)


def _rms(x, g):
    return x * lax.rsqrt(jnp.mean(x * x, axis=-1, keepdims=True) + EPS) * g


def _mla_proj_kernel(cq_ref, ckv_ref, kr_ref, krs_ref, cos_ref, sin_ref, qg_ref, kvg_ref,
                     wq_ref, wqs_ref, wk_ref, wvt_ref, q_out, k_out, vt_out):
    cos = cos_ref[...]
    sin = sin_ref[...]
    cos_h = jnp.concatenate([cos] * MLA_HEADS, axis=1)
    sin_h = jnp.concatenate([sin] * MLA_HEADS, axis=1)
    xq = _rms(cq_ref[...], qg_ref[...]).astype(BF16)
    q = _dot(xq, wq_ref[...]) * cos_h + _dot(xq, wqs_ref[...]) * sin_h
    q_out[...] = (q * (MLA_SCALE * LOG2_E)).astype(BF16)
    xkv = _rms(ckv_ref[...], kvg_ref[...]).astype(BF16)
    k_rope = kr_ref[...] * cos + krs_ref[...] * sin
    k = _dot(xkv, wk_ref[...]) + jnp.concatenate([k_rope] * MLA_HEADS, axis=1)
    k_out[...] = k.astype(BF16)
    vt_out[0] = _dot_nt(wvt_ref[...], xkv).astype(BF16)


def _mla_proj(cq, ckv, kr, krs, cos_t, sin_t, qg, kvg, wq, wqs, wk, wvt, layer, tm):
    n = cq.shape[0]
    row = lambda i: (i, 0)
    full = lambda i: (0, 0)
    lay = lambda i: (layer, 0, 0)
    hq = MLA_HEADS * HEAD_LANES
    hv = MLA_HEADS * MLA_V
    return pl.pallas_call(
        _mla_proj_kernel,
        grid=(n // tm,),
        in_specs=[pl.BlockSpec((tm, MLA_Q_LORA), row), pl.BlockSpec((tm, MLA_KV_LORA), row),
                  pl.BlockSpec((tm, HEAD_LANES), row), pl.BlockSpec((tm, HEAD_LANES), row),
                  pl.BlockSpec((tm, HEAD_LANES), row), pl.BlockSpec((tm, HEAD_LANES), row),
                  pl.BlockSpec((1, MLA_Q_LORA), full), pl.BlockSpec((1, MLA_KV_LORA), full),
                  pl.BlockSpec((None, MLA_Q_LORA, hq), lay), pl.BlockSpec((None, MLA_Q_LORA, hq), lay),
                  pl.BlockSpec((None, MLA_KV_LORA, hq), lay), pl.BlockSpec((None, hv, MLA_KV_LORA), lay)],
        out_specs=[pl.BlockSpec((tm, hq), row), pl.BlockSpec((tm, hq), row),
                   pl.BlockSpec((1, hv, tm), lambda i: (i, 0, 0))],
        out_shape=[jax.ShapeDtypeStruct((n, hq), BF16), jax.ShapeDtypeStruct((n, hq), BF16),
                   jax.ShapeDtypeStruct((n // tm, hv, tm), BF16)],
        name="mla_proj",
        compiler_params=_cparams(("parallel",)),
    )(cq, ckv, kr, krs, cos_t, sin_t, qg, kvg, wq, wqs, wk, wvt)


ACC_ROWS = MLA_V + 16
LAT_UNROLL = 8
LOOKAHEAD = 2


def _attn_kernel(*refs, tkc, n_lat_chunks, n_ctx_chunks):
    if n_lat_chunks:
        q_ref, kl_ref, vl_ref, kc_ref, vc_ref, o_ref, m_s, acc_s, s_s = refs
    else:
        q_ref, kc_ref, vc_ref, o_ref, m_s, acc_s, s_s = refs
    m_s[...] = jnp.full(m_s.shape, -jnp.inf, F32)
    acc_s[...] = jnp.zeros(acc_s.shape, F32)
    ones = jnp.ones((ACC_ROWS - MLA_V, tkc), BF16)

    def scores(a, k_ref, r0):
        hs = slice(a * HEAD_LANES, (a + 1) * HEAD_LANES)
        return _dot_nt(k_ref[pl.ds(r0, tkc), hs], q_ref[:, hs])

    def accumulate(a, s_t, vt):
        m_old = m_s[a]
        m_new = jnp.maximum(m_old, jnp.max(s_t, axis=0, keepdims=True))
        alpha = jnp.exp2(m_old - m_new)
        p_t = jnp.exp2(s_t - m_new).astype(BF16)
        vta = jnp.concatenate([vt[a * MLA_V:(a + 1) * MLA_V], ones], axis=0)
        acc_s[a] = alpha * acc_s[a] + _dot(vta, p_t)
        m_s[a] = m_new

    def run(items, queue, after):
        upcoming = items[LOOKAHEAD:] + after
        queue = list(queue)
        for idx, (a, _, _, vt) in enumerate(items):
            if idx < len(upcoming):
                queue.append(scores(*upcoming[idx][:3]))
            accumulate(a, queue.pop(0), vt)
        return queue

    ahead_chunks = (LOOKAHEAD + 1) // 2
    unroll = int(np.gcd(LAT_UNROLL, n_lat_chunks)) if n_lat_chunks else 1

    def lat_items(c0, dynamic, count=None):
        items = []
        for u in range(unroll if count is None else count):
            cc = c0 + u
            r0 = pl.multiple_of(cc * tkc, tkc) if dynamic else cc * tkc
            vt = vl_ref[cc] if count is None else None
            items += [(0, kl_ref, r0, vt), (1, kl_ref, r0, vt)]
        return items

    ctx_items = []
    for c in range(n_ctx_chunks):
        ctx_items += [(0, kc_ref, c * tkc, vc_ref[c]), (1, kc_ref, c * tkc, vc_ref[c])]

    if n_lat_chunks:
        n_groups = n_lat_chunks // unroll
        for d, it in enumerate(lat_items(0, False, count=ahead_chunks)[:LOOKAHEAD]):
            s_s[d] = scores(*it[:3])

        def body(g, carry):
            c0 = g * unroll
            after = lat_items(c0 + unroll, True, count=ahead_chunks)[:LOOKAHEAD]
            queue = run(lat_items(c0, True), [s_s[d] for d in range(LOOKAHEAD)], after)
            for d in range(LOOKAHEAD):
                s_s[d] = queue[d]
            return carry

        lax.fori_loop(0, n_groups - 1, body, 0)
        run(lat_items((n_groups - 1) * unroll, False) + ctx_items, [s_s[d] for d in range(LOOKAHEAD)], [])
    else:
        run(ctx_items, [scores(*it[:3]) for it in ctx_items[:LOOKAHEAD]], [])
    outs = []
    for a in range(2):
        acc = acc_s[a]
        outs.append(acc[0:MLA_V] / acc[MLA_V:MLA_V + 1])
    o_ref[...] = jnp.concatenate(outs, axis=0).T.astype(o_ref.dtype)


def _attention(q, k, vt, *, bsz, n_q, q_base, n_lat, n_ctx, ctx_base, tq, tkc, with_lat):
    pairs = MLA_HEADS // 2
    qblocks = n_q // tq
    qb0 = q_base // tq
    cb0 = ctx_base // n_ctx
    kw = 2 * HEAD_LANES
    vw = 2 * MLA_V
    in_specs = [pl.BlockSpec((tq, kw), lambda b, p, i: (qb0 + b * qblocks + i, p))]
    args = [q]
    if with_lat:
        in_specs += [pl.BlockSpec((n_lat, kw), lambda b, p, i: (b, p)),
                     pl.BlockSpec((n_lat // tkc, vw, tkc), lambda b, p, i: (b, p, 0))]
        args += [k, vt]
    in_specs += [pl.BlockSpec((n_ctx, kw), lambda b, p, i: (cb0 + b, p)),
                 pl.BlockSpec((n_ctx // tkc, vw, tkc), lambda b, p, i: (cb0 + b, p, 0))]
    args += [k, vt]
    kern = functools.partial(_attn_kernel, tkc=tkc, n_lat_chunks=(n_lat // tkc) if with_lat else 0,
                             n_ctx_chunks=n_ctx // tkc)
    return pl.pallas_call(
        kern,
        grid=(bsz, pairs, qblocks),
        in_specs=in_specs,
        out_specs=pl.BlockSpec((tq, vw), lambda b, p, i: (b * qblocks + i, p)),
        out_shape=jax.ShapeDtypeStruct((bsz * n_q, MLA_HEADS * MLA_V), BF16),
        scratch_shapes=[pltpu.VMEM((2, 1, tq), F32), pltpu.VMEM((2, ACC_ROWS, tq), F32),
                        pltpu.VMEM((LOOKAHEAD, tkc, tq), F32)],
        name="attn_lat" if with_lat else "attn_ctx",
        compiler_params=_cparams(("parallel", "parallel", "arbitrary")),
    )(*args)


def _outproj_kernel(a_ref, ofl_ref, obl_ref, cl_ref, ofc_ref, obc_ref, cc_ref, hg_ref, x_ref, g1_ref, sh2_ref,
                    sc2_ref, lng_ref, lnb_ref, ng_ref, gavg_ref, w_ref, wr_ref, x1_out, h2_out, lg_out,
                    *, alpha, lat_tiles):
    is_ctx = pl.program_id(0) >= lat_tiles
    tm = x_ref.shape[0]
    half = tm // OUT_ROW_SPLIT
    for r in range(OUT_ROW_SPLIT):
        rs = slice(r * half, (r + 1) * half)
        o = jnp.where(is_ctx, ofc_ref[rs, :] + obc_ref[rs, :], ofl_ref[rs, :] + obl_ref[rs, :])
        c_att = jnp.where(is_ctx, cc_ref[rs, :], cl_ref[rs, :])
        o2_hi, o2_lo = _split_bf16(o * o)
        ms = _dot(o2_hi, gavg_ref[...]) + _dot(o2_lo, gavg_ref[...])
        b_lat = o * lax.rsqrt(ms + EPS) * ng_ref[...] * _silu(hg_ref[rs, :])
        m = (_dot(a_ref[rs, :].astype(BF16), w_ref[0:POOL_DIM, :])
             + _dot(b_lat.astype(BF16), w_ref[POOL_DIM:POOL_DIM + HG_V, :])
             + _dot(c_att, w_ref[POOL_DIM + HG_V:, :]))
        x1 = _layer_norm(alpha * x_ref[rs, :] + g1_ref[0] * m) * lng_ref[...] + lnb_ref[...]
        x1_out[rs, :] = x1
        h2 = _layer_norm(x1) * (1.0 + sc2_ref[0]) + sh2_ref[0]
        h2_out[rs, :] = h2
        lg_out[rs, :] = _dot3(h2, wr_ref[...])


def _out_proj(a, lat_parts, ctx_parts, hg, x, g1, sh2, sc2, ln_g, ln_b, norm_g, gavg, w_out, w_router, layer, tm,
              mod_index, alpha):
    n, d = x.shape
    lat_tiles = lat_parts[0].shape[0] // tm
    row = lambda i: (i, 0)
    lat_row = lambda i: (jnp.minimum(i, lat_tiles - 1), 0)
    ctx_row = lambda i: (jnp.maximum(i - lat_tiles, 0), 0)
    full = lambda i: (0, 0)
    lay = lambda i: (layer, 0, 0)
    mod = lambda i: (mod_index(i), 0, 0)
    hv = MLA_HEADS * MLA_V
    return pl.pallas_call(
        functools.partial(_outproj_kernel, alpha=alpha, lat_tiles=lat_tiles),
        grid=(n // tm,),
        in_specs=[pl.BlockSpec((tm, POOL_DIM), row),
                  pl.BlockSpec((tm, HG_V), lat_row), pl.BlockSpec((tm, HG_V), lat_row), pl.BlockSpec((tm, hv), lat_row),
                  pl.BlockSpec((tm, HG_V), ctx_row), pl.BlockSpec((tm, HG_V), ctx_row), pl.BlockSpec((tm, hv), ctx_row),
                  pl.BlockSpec((tm, HG_V), row),
                  pl.BlockSpec((tm, d), row),
                  pl.BlockSpec((1, 1, d), mod), pl.BlockSpec((1, 1, d), mod), pl.BlockSpec((1, 1, d), mod),
                  pl.BlockSpec((1, d), full), pl.BlockSpec((1, d), full), pl.BlockSpec((1, HG_V), full),
                  pl.BlockSpec((HG_V, HG_V), full), pl.BlockSpec((None, d, d), lay),
                  pl.BlockSpec((None, d, ROUTER_LANES), lay)],
        out_specs=[pl.BlockSpec((tm, d), row), pl.BlockSpec((tm, d), row), pl.BlockSpec((tm, ROUTER_LANES), row)],
        out_shape=[jax.ShapeDtypeStruct((n, d), F32), jax.ShapeDtypeStruct((n, d), F32),
                   jax.ShapeDtypeStruct((n, ROUTER_LANES), F32)],
        name="out_proj",
        compiler_params=_cparams(("parallel",)),
    )(a, *lat_parts, *ctx_parts, hg, x, g1, sh2, sc2, ln_g, ln_b, norm_g, gavg, w_out, w_router)


def _moe_kernel(be_ref, nv_ref, x_ref, w1_ref, w3_ref, w2_ref, y_ref, w1b, w3b, w2b):
    i = pl.program_id(0)

    @pl.when((i == 0) | (be_ref[i] != be_ref[jnp.maximum(i - 1, 0)]))
    def _():
        w1b[...] = w1_ref[0, 0].astype(BF16)
        w3b[...] = w3_ref[0, 0].astype(BF16)
        w2b[...] = w2_ref[0, 0].astype(BF16)

    @pl.when(i < nv_ref[0])
    def _():
        x = x_ref[...].astype(BF16)
        h1 = _dot(x, w1b[...])
        h3 = _dot(x, w3b[...])
        h = (_silu(h1) * h3).astype(BF16)
        y_ref[...] = _dot(h, w2b[...]).astype(y_ref.dtype)

    @pl.when(i >= nv_ref[0])
    def _():
        y_ref[...] = jnp.zeros(y_ref.shape, y_ref.dtype)


def _moe_ffn(block_e, n_valid, xr, w1, w3, w2, layer, tm):
    n_rows, d = xr.shape
    n_blocks = n_rows // tm
    grid_spec = pltpu.PrefetchScalarGridSpec(
        num_scalar_prefetch=2,
        grid=(n_blocks,),
        in_specs=[pl.BlockSpec((tm, d), lambda i, be, nv: (jnp.minimum(i, nv[0] - 1), 0)),
                  pl.BlockSpec((1, 1, d, D_EXPERT), lambda i, be, nv: (layer, be[i], 0, 0)),
                  pl.BlockSpec((1, 1, d, D_EXPERT), lambda i, be, nv: (layer, be[i], 0, 0)),
                  pl.BlockSpec((1, 1, D_EXPERT, d), lambda i, be, nv: (layer, be[i], 0, 0))],
        out_specs=pl.BlockSpec((tm, d), lambda i, be, nv: (i, 0)),
        scratch_shapes=[pltpu.VMEM((d, D_EXPERT), BF16), pltpu.VMEM((d, D_EXPERT), BF16),
                        pltpu.VMEM((D_EXPERT, d), BF16)],
    )
    return pl.pallas_call(
        _moe_kernel,
        grid_spec=grid_spec,
        out_shape=jax.ShapeDtypeStruct((n_rows, d), BF16),
        name="moe_ffn",
        compiler_params=_cparams(("arbitrary",)),
    )(block_e, n_valid, xr, w1, w3, w2)


def _combine_kernel(x_ref, y0_ref, y1_ref, gw_ref, g2_ref, lng_ref, lnb_ref, o_ref, *, alpha):
    gw = gw_ref[...]
    y = gw[:, 0:1] * y0_ref[...].astype(F32) + gw[:, 1:2] * y1_ref[...].astype(F32)
    o_ref[...] = _layer_norm(alpha * x_ref[...] + g2_ref[0] * y) * lng_ref[...] + lnb_ref[...]


def _combine(x, y0, y1, gw, g2, ln_g, ln_b, tm, mod_index, alpha):
    n, d = x.shape
    row = lambda i: (i, 0)
    full = lambda i: (0, 0)
    return pl.pallas_call(
        functools.partial(_combine_kernel, alpha=alpha),
        grid=(n // tm,),
        in_specs=[pl.BlockSpec((tm, d), row), pl.BlockSpec((tm, d), row), pl.BlockSpec((tm, d), row),
                  pl.BlockSpec((tm, ROUTER_LANES), row),
                  pl.BlockSpec((1, 1, d), lambda i: (mod_index(i), 0, 0)),
                  pl.BlockSpec((1, d), full), pl.BlockSpec((1, d), full)],
        out_specs=pl.BlockSpec((tm, d), row),
        out_shape=jax.ShapeDtypeStruct((n, d), F32),
        name="moe_combine",
        compiler_params=_cparams(("parallel",)),
    )(x, y0, y1, gw, g2, ln_g, ln_b)


def _rot_half_cols(w):
    lead = w.shape[:-1]
    wr = w.reshape(lead + (2, 2, MLA_ROPE // 4))
    return jnp.stack([-wr[..., 1, :], wr[..., 0, :]], axis=-2).reshape(w.shape)


def _rope_tables(n_lat, n_ctx, bsz):
    rows = n_lat // GRID_W
    row = jnp.repeat(jnp.arange(rows, dtype=F32), GRID_W)
    col = jnp.tile(jnp.arange(GRID_W, dtype=F32), rows)
    n_freq = MLA_ROPE // 4
    inv_freq = ROPE_BASE ** (-jnp.arange(n_freq, dtype=F32) / n_freq)
    ang = jnp.stack([row[:, None] * inv_freq, col[:, None] * inv_freq], axis=1)
    cos = jnp.broadcast_to(jnp.cos(ang)[:, :, None, :], (n_lat, 2, 2, n_freq)).reshape(n_lat, MLA_ROPE)
    sin = jnp.broadcast_to(jnp.sin(ang)[:, :, None, :], (n_lat, 2, 2, n_freq)).reshape(n_lat, MLA_ROPE)
    pad_l = MLA_NOPE
    pad_r = HEAD_LANES - MLA_QK
    cos_t = jnp.concatenate([jnp.ones((n_lat, pad_l), F32), cos, jnp.ones((n_lat, pad_r), F32)], axis=1)
    sin_t = jnp.concatenate([jnp.zeros((n_lat, pad_l), F32), sin, jnp.zeros((n_lat, pad_r), F32)], axis=1)
    cos_all = jnp.concatenate([jnp.tile(cos_t, (bsz, 1)), jnp.ones((bsz * n_ctx, HEAD_LANES), F32)], axis=0)
    sin_all = jnp.concatenate([jnp.tile(sin_t, (bsz, 1)), jnp.zeros((bsz * n_ctx, HEAD_LANES), F32)], axis=0)
    return cos_all, sin_all


def _prep_w_in(w_in):
    depth, d, _ = w_in.shape
    kr = w_in[:, :, W_IN_MAIN:]
    z_l = jnp.zeros((depth, d, MLA_NOPE), w_in.dtype)
    z_r = jnp.zeros((depth, d, HEAD_LANES - MLA_QK), w_in.dtype)
    return jnp.concatenate([w_in[:, :, :W_IN_MAIN], z_l, kr, z_r, z_l, _rot_half_cols(kr), z_r],
                           axis=-1).astype(BF16)


def _prep_mla(w_uq, w_ukv):
    depth = w_uq.shape[0]
    wq = w_uq.reshape(depth, MLA_Q_LORA, MLA_HEADS, MLA_QK)
    pad = jnp.zeros((depth, MLA_Q_LORA, MLA_HEADS, HEAD_LANES - MLA_QK), w_uq.dtype)
    wq_p = jnp.concatenate([wq, pad], axis=-1).reshape(depth, MLA_Q_LORA, MLA_HEADS * HEAD_LANES)
    wq_s = jnp.concatenate([jnp.zeros_like(wq[..., :MLA_NOPE]), _rot_half_cols(wq[..., MLA_NOPE:]), pad],
                           axis=-1).reshape(depth, MLA_Q_LORA, MLA_HEADS * HEAD_LANES)
    wkv = w_ukv.reshape(depth, MLA_KV_LORA, MLA_HEADS, MLA_NOPE + MLA_V)
    kpad = jnp.zeros((depth, MLA_KV_LORA, MLA_HEADS, HEAD_LANES - MLA_NOPE), w_ukv.dtype)
    wk = jnp.concatenate([wkv[..., :MLA_NOPE], kpad], axis=-1).reshape(depth, MLA_KV_LORA, MLA_HEADS * HEAD_LANES)
    wvt = jnp.swapaxes(wkv[..., MLA_NOPE:].reshape(depth, MLA_KV_LORA, MLA_HEADS * MLA_V), 1, 2)
    return wq_p.astype(BF16), wq_s.astype(BF16), wk.astype(BF16), wvt.astype(BF16)


def _block_diag(pool_w):
    depth, g, c, _ = pool_w.shape
    eye = jnp.eye(g, dtype=pool_w.dtype)
    return jnp.einsum("lgcd,gh->lgchd", pool_w, eye).reshape(depth, g * c, g * c).astype(BF16)


def _route(logits, b_rg, b_re, tm_moe):
    n = logits.shape[0]
    lg = logits[:, :N_GROUPS] + b_rg
    le = (logits[:, N_GROUPS:N_GROUPS + N_EXPERTS] + b_re).reshape(n, N_GROUPS, EXPERTS_PER_GROUP)
    pg = jax.nn.softmax(lg, axis=-1)
    g_sel = jnp.argmax(lg, axis=-1).astype(jnp.int32)
    g_hot = g_sel[:, None] == jnp.arange(N_GROUPS, dtype=jnp.int32)[None, :]
    w_g = jnp.sum(jnp.where(g_hot, pg, 0.0), axis=-1)
    le_sel = jnp.sum(jnp.where(g_hot[:, :, None], le, 0.0), axis=1)
    top_v, top_i = lax.top_k(le_sel, TOP_K)
    gate = jax.nn.softmax(top_v, axis=-1) * w_g[:, None]
    expert = (g_sel[:, None] * EXPERTS_PER_GROUP + top_i).astype(jnp.int32)

    n_assign = n * TOP_K
    e_flat = expert.reshape(-1)
    hot = e_flat[:, None] == jnp.arange(N_EXPERTS, dtype=jnp.int32)[None, :]
    nb = n_assign // COUNT_BLOCK
    tri = jnp.tril(jnp.ones((COUNT_BLOCK, COUNT_BLOCK), BF16))
    win = jnp.einsum("ts,bse->bte", tri, hot.astype(BF16).reshape(nb, COUNT_BLOCK, N_EXPERTS),
                     preferred_element_type=F32)
    bsum = win[:, -1, :]
    boff = jnp.cumsum(bsum, axis=0) - bsum
    csum = (win + boff[:, None, :]).reshape(n_assign, N_EXPERTS).astype(jnp.int32)
    counts = (boff[-1] + bsum[-1]).astype(jnp.int32)
    pcounts = (counts + tm_moe - 1) // tm_moe * tm_moe
    pends = jnp.cumsum(pcounts)
    pstarts = pends - pcounts
    starts = jnp.cumsum(counts) - counts
    dest = jnp.sum(jnp.where(hot, csum - 1 + pstarts[None, :], 0), axis=1)
    n_rows = (n_assign + N_EXPERTS * (tm_moe - 1) + tm_moe - 1) // tm_moe * tm_moe
    n_blocks = n_rows // tm_moe
    n_valid = (pends[-1] // tm_moe).astype(jnp.int32)
    blk = jnp.arange(n_blocks, dtype=jnp.int32)
    first_row = jnp.minimum(blk, n_valid - 1) * tm_moe
    block_e = jnp.sum((pends[None, :] <= first_row[:, None]).astype(jnp.int32), axis=1)
    block_e = jnp.clip(block_e, 0, N_EXPERTS - 1)
    a_sorted = jnp.sort(e_flat * n_assign + jnp.arange(n_assign, dtype=jnp.int32)) % n_assign
    pos = (blk * tm_moe - pstarts[block_e])[:, None] + jnp.arange(tm_moe, dtype=jnp.int32)[None, :]
    src = jnp.clip(starts[block_e][:, None] + pos, 0, n_assign - 1)
    filler = (blk[:, None] * tm_moe + jnp.arange(tm_moe, dtype=jnp.int32)[None, :]) % n
    row_tok = jnp.where(pos < counts[block_e][:, None], a_sorted[src] // TOP_K, filler)
    gw = jnp.zeros((n, ROUTER_LANES), F32).at[:, :TOP_K].set(gate)
    return row_tok.reshape(n_rows), dest.reshape(n, TOP_K), gw, block_e, n_valid.reshape(1)


def _tile(n, pref):
    t = pref
    while n % t:
        t //= 2
    return t


def kernel(x, c, ctx, c_ctx, w_mod, b_mod, w_in, pool_w, pool_scale, hg_lb_logits, hg_norm_g, mla_q_norm_g, mla_w_uq, mla_kv_norm_g, mla_w_ukv, w_out, ln1_g, ln1_b, ln2_g, ln2_b, router_group_w, router_group_b, router_expert_w, router_expert_b, expert_w1, expert_w3, expert_w2):
    bsz, n_lat, d = x.shape
    n_ctx = ctx.shape[1]
    depth = w_mod.shape[0]
    alpha = (2 * depth) ** 0.25
    lat_rows = bsz * n_lat
    ctx_rows = bsz * n_ctx
    n_tok = lat_rows + ctx_rows

    tm = _tile(np.gcd(n_lat, ctx_rows), 512)
    tp = _tile(np.gcd(n_lat, n_ctx), 256)
    tb_lat = _tile(n_lat, 512)
    tb_ctx = _tile(n_ctx, 512)
    tq_lat = _tile(n_lat, 512)
    tkc = _tile(np.gcd(n_lat, n_ctx), 256)
    tq_ctx = _tile(n_ctx, 512)
    tm_moe = 256
    lat_tiles_per_batch = n_lat // tm
    mod_index = lambda i: jnp.minimum(i // lat_tiles_per_batch, bsz)

    c_rows = jnp.concatenate([jax.nn.silu(c), jax.nn.silu(c_ctx)[None, :],
                              jnp.zeros((-(bsz + 1) % 8, d), F32)], axis=0)
    mod_all = _mod_all(c_rows, w_mod, b_mod)
    lb_all = jnp.cumsum(jax.nn.softmax(hg_lb_logits.astype(F32), axis=0), axis=0)
    lb_all = lb_all - lb_all[0:1]
    log_lb = jnp.log(lb_all)
    log_1m_lb = jnp.log1p(-lb_all)
    w_aug = _prep_w_in(w_in)
    wq_p, wq_s, wk_p, wvt_p = _prep_mla(mla_w_uq, mla_w_ukv)
    w_bd = _block_diag(pool_w)
    w_out_b = w_out.astype(BF16)
    w_router = jnp.concatenate([router_group_w, router_expert_w,
                                jnp.zeros((depth, d, ROUTER_LANES - N_GROUPS - N_EXPERTS), F32)], axis=-1)
    gavg = jnp.kron(jnp.eye(HG_HEADS, dtype=F32), jnp.full((HG_DV, HG_DV), 1.0 / HG_DV, F32)).astype(BF16)
    cos_t, sin_t = _rope_tables(n_lat, n_ctx, bsz)
    zero_state = jnp.zeros((bsz, HG_HEADS, HG_DV, HG_DK), F32)

    xt = jnp.concatenate([x.reshape(lat_rows, d), ctx.reshape(ctx_rows, d)], axis=0)
    for l in range(depth):
        mod = mod_all[l, :bsz + 1].reshape(bsz + 1, 6, 1, d)
        sh1, sc1, g1, sh2, sc2, g2 = (mod[:, k] for k in range(6))

        gate_consts = jnp.stack([log_lb[l, 0], log_1m_lb[l, 0], log_lb[l, 1], log_1m_lb[l, 1]], axis=0)
        p_in, hq, gf, kf, gb, kb, hin, hg, cq, ckv, kr, krs = _in_proj(xt, sh1, sc1, gate_consts, w_aug, l, tm,
                                                                       mod_index)

        a_mix = _pool(p_in, w_bd[l], pool_scale[l][None, :], tp, lat_rows, n_lat, n_ctx)

        oc_f, s_f = _hgrn_scan(hq, gf, kf, hin, zero_state, reverse=False, base_row=lat_rows, n_seq=n_ctx, tb=tb_ctx)
        oc_b, s_b = _hgrn_scan(hq, gb, kb, hin, zero_state, reverse=True, base_row=lat_rows, n_seq=n_ctx, tb=tb_ctx)
        ol_f, _ = _hgrn_scan(hq, gf, kf, hin, s_f, reverse=False, base_row=0, n_seq=n_lat, tb=tb_lat)
        ol_b, _ = _hgrn_scan(hq, gb, kb, hin, s_b, reverse=True, base_row=0, n_seq=n_lat, tb=tb_lat)

        q_all, k_all, vt_all = _mla_proj(cq, ckv, kr, krs, cos_t, sin_t, mla_q_norm_g[l][None, :],
                                         mla_kv_norm_g[l][None, :], wq_p, wq_s, wk_p, wvt_p, l, tkc)
        c_lat = _attention(q_all, k_all, vt_all, bsz=bsz, n_q=n_lat, q_base=0, n_lat=n_lat, n_ctx=n_ctx,
                           ctx_base=lat_rows, tq=tq_lat, tkc=tkc, with_lat=True)
        c_ctx_o = _attention(q_all, k_all, vt_all, bsz=bsz, n_q=n_ctx, q_base=lat_rows, n_lat=n_lat, n_ctx=n_ctx,
                             ctx_base=lat_rows, tq=tq_ctx, tkc=tkc, with_lat=False)

        x1, h2, logits = _out_proj(a_mix, (ol_f, ol_b, c_lat), (oc_f, oc_b, c_ctx_o), hg, xt, g1, sh2, sc2,
                                   ln1_g[l][None, :],
                                   ln1_b[l][None, :], hg_norm_g[l][None, :], gavg, w_out_b, w_router, l,
                                   tm, mod_index, alpha)

        row_tok, dest, gw, block_e, n_valid = _route(logits, router_group_b[l], router_expert_b[l], tm_moe)
        xr = jnp.take(h2, row_tok, axis=0, mode="clip")
        y = _moe_ffn(block_e, n_valid, xr, expert_w1, expert_w3, expert_w2, l, tm_moe)
        y0 = jnp.take(y, dest[:, 0], axis=0, mode="clip")
        y1 = jnp.take(y, dest[:, 1], axis=0, mode="clip")
        xt = _combine(x1, y0, y1, gw, g2, ln2_g[l][None, :], ln2_b[l][None, :], tm, mod_index, alpha)
    return xt[:lat_rows].reshape(bsz, n_lat, d)
```

```python
import functools

import numpy as np
import jax
import jax.numpy as jnp
from jax import lax
from jax.experimental import pallas as pl
from jax.experimental.pallas import tpu as pltpu

F32 = jnp.float32
BF16 = jnp.bfloat16

D_MODEL = 1024
GRID_W = 64
POOL_WINDOWS = (2, 4, 8, 16)
POOL_DIM = 256
POOL_GROUP_DIM = 64
HG_HEADS = 4
HG_DK = 128
HG_DV = 64
HG_QK = HG_HEADS * HG_DK
HG_V = HG_HEADS * HG_DV
HG_CHUNK = 64
HG_SUB_LOG2 = 4
HG_SUB = 1 << HG_SUB_LOG2
MLA_HEADS = 8
MLA_NOPE = 64
MLA_ROPE = 32
MLA_V = 64
MLA_Q_LORA = 256
MLA_KV_LORA = 128
MLA_QK = MLA_NOPE + MLA_ROPE
MLA_SCALE = MLA_QK ** -0.5
LOG2_E = 1.4426950408889634
ROPE_BASE = 10000.0
HEAD_LANES = 128
N_GROUPS = 4
EXPERTS_PER_GROUP = 8
N_EXPERTS = N_GROUPS * EXPERTS_PER_GROUP
TOP_K = 2
D_EXPERT = 512
EPS = 1e-6
ROUTER_LANES = 128
COUNT_BLOCK = 128
OUT_ROW_SPLIT = 2
IN_SEGS = (("p_in", 256), ("hq", 512), ("hzf", 512), ("hzb", 512), ("hin", 256), ("hg", 256),
           ("cq", 256), ("ckv", 128), ("kr", 128), ("krs", 128))
IN_AUG = sum(w for _, w in IN_SEGS)
W_IN_MAIN = 2688

VMEM_LIMIT = 56 * 1024 * 1024


def _cparams(sem):
    return pltpu.CompilerParams(dimension_semantics=sem, vmem_limit_bytes=VMEM_LIMIT)


def _split_bf16(a):
    hi = a.astype(BF16)
    lo = (a - hi.astype(F32)).astype(BF16)
    return hi, lo


def _dot(a, b):
    return jnp.dot(a, b, preferred_element_type=F32)


def _dot_nt(a, b):
    return lax.dot_general(a, b, (((1,), (1,)), ((), ())), preferred_element_type=F32)


def _dot_tn(a, b):
    return lax.dot_general(a, b, (((0,), (0,)), ((), ())), preferred_element_type=F32)


def _dot3(a, b):
    a_hi, a_lo = _split_bf16(a)
    b_hi, b_lo = _split_bf16(b)
    return _dot(a_hi, b_hi) + _dot(a_hi, b_lo) + _dot(a_lo, b_hi)


def _dot_exact_rhs(a, b_bf16):
    a_hi = a.astype(BF16)
    r1 = a - a_hi.astype(F32)
    a_mid = r1.astype(BF16)
    a_lo = (r1 - a_mid.astype(F32)).astype(BF16)
    return _dot(a_hi, b_bf16) + _dot(a_mid, b_bf16) + _dot(a_lo, b_bf16)


def _layer_norm(x):
    mu = jnp.mean(x, axis=-1, keepdims=True)
    xc = x - mu
    var = jnp.mean(xc * xc, axis=-1, keepdims=True)
    return xc * lax.rsqrt(var + EPS)


def _sigmoid(x):
    return 1.0 / (1.0 + jnp.exp(-x))


def _silu(x):
    return x * _sigmoid(x)


def _mod_kernel(a_ref, w_ref, b_ref, o_ref):
    o_ref[0] = _dot3(a_ref[...], w_ref[0]) + b_ref[0]


def _mod_all(c_rows, w_mod, b_mod):
    depth, d, n6 = w_mod.shape
    tn = 1536
    rows = c_rows.shape[0]
    return pl.pallas_call(
        _mod_kernel,
        grid=(depth, n6 // tn),
        in_specs=[pl.BlockSpec((rows, d), lambda l, j: (0, 0)),
                  pl.BlockSpec((1, d, tn), lambda l, j: (l, 0, j)),
                  pl.BlockSpec((1, 1, tn), lambda l, j: (l, 0, j))],
        out_specs=pl.BlockSpec((1, rows, tn), lambda l, j: (l, 0, j)),
        out_shape=jax.ShapeDtypeStruct((depth, rows, n6), F32),
        name="adaln_mod",
        compiler_params=_cparams(("arbitrary", "arbitrary")),
    )(c_rows, w_mod, b_mod.reshape(depth, 1, n6))


def _forget_gates(z, la, lc):
    ls = jnp.minimum(z, 0.0) - jnp.log(1.0 + jnp.exp(-jnp.abs(z)))
    x2 = lc + ls
    g = jnp.maximum(la, x2) + jnp.log(1.0 + jnp.exp(-jnp.abs(la - x2)))
    k = jnp.exp(x2 - z)
    return g, k


IN_OUTS = (("p_in", 256), ("q", 512), ("gf", 512), ("kf", 512), ("gb", 512), ("kb", 512), ("hin", 256), ("hg", 256),
           ("cq", 256), ("ckv", 128), ("kr", 128), ("krs", 128))


def _inproj_kernel(x_ref, sh_ref, sc_ref, gc_ref, w_ref, *out_refs):
    outs = dict(zip((name for name, _ in IN_OUTS), out_refs))
    h = _layer_norm(x_ref[...]) * (1.0 + sc_ref[0]) + sh_ref[0]
    hb = h.astype(BF16)
    def finish(name, seg):
        if name == "hq":
            outs["q"][...] = _silu(seg)
        elif name == "hzf":
            outs["gf"][...], outs["kf"][...] = _forget_gates(seg, gc_ref[0:1, :], gc_ref[1:2, :])
        elif name == "hzb":
            outs["gb"][...], outs["kb"][...] = _forget_gates(seg, gc_ref[2:3, :], gc_ref[3:4, :])
        else:
            outs[name][...] = seg

    start = 0
    pending = None
    for name, width in IN_SEGS:
        seg = _dot(hb, w_ref[:, start:start + width])
        start += width
        if pending is not None:
            finish(*pending)
        pending = (name, seg)
    finish(*pending)


def _in_proj(x, sh, sc, gate_consts, w_aug, layer, tm, mod_index):
    n, d = x.shape
    row = lambda i: (i, 0)
    mod = lambda i: (mod_index(i), 0, 0)
    return pl.pallas_call(
        _inproj_kernel,
        grid=(n // tm,),
        in_specs=[pl.BlockSpec((tm, d), row),
                  pl.BlockSpec((1, 1, d), mod),
                  pl.BlockSpec((1, 1, d), mod),
                  pl.BlockSpec(gate_consts.shape, lambda i: (0, 0)),
                  pl.BlockSpec((None, d, IN_AUG), lambda i: (layer, 0, 0))],
        out_specs=[pl.BlockSpec((tm, w), row) for _, w in IN_OUTS],
        out_shape=[jax.ShapeDtypeStruct((n, w), F32) for _, w in IN_OUTS],
        name="in_proj",
        compiler_params=_cparams(("parallel",)),
    )(x, sh, sc, gate_consts, w_aug)


POOL_HALO = 8


def _pool_kernel(x_ref, prev_ref, next_ref, w_ref, scale_ref, o_ref, *, tp, lat_tiles, tpb_lat, tpb_ctx,
                 n_lat, n_ctx):
    i = pl.program_id(0)
    is_lat = i < lat_tiles
    tile_in_seq = jnp.where(is_lat, i % tpb_lat, (i - lat_tiles) % tpb_ctx)
    tiles_in_seq = jnp.where(is_lat, tpb_lat, tpb_ctx)
    n_seq = jnp.where(is_lat, n_lat, n_ctx)
    first = tile_in_seq == 0
    last = tile_in_seq == tiles_in_seq - 1
    x = x_ref[...]
    prev = jnp.where(first, 0.0, prev_ref[...])
    nxt = jnp.where(last, 0.0, next_ref[...])
    ext = jnp.concatenate([prev, x, nxt], axis=0)
    n_ext = tp + 2 * POOL_HALO

    def back(a, k):
        return pltpu.roll(a, k, 0)

    def fwd(a, k):
        return pltpu.roll(a, n_ext - k, 0)

    e2 = ext + back(ext, 1)
    e4 = back(e2, 1) + fwd(e2, 1)
    e8 = back(e4, 2) + fwd(e4, 2)
    e16 = back(e8, 4) + fwd(e8, 4)
    sl = slice(POOL_HALO, POOL_HALO + tp)
    lane = lax.broadcasted_iota(jnp.int32, (tp, POOL_DIM), 1)
    grp = lane >> 6
    wsum = jnp.where(grp == 0, e2[sl], jnp.where(grp == 1, e4[sl], jnp.where(grp == 2, e8[sl], e16[sl])))
    half = jnp.where(grp == 0, 1, jnp.where(grp == 1, 2, jnp.where(grp == 2, 4, 8)))
    t = tile_in_seq * tp + lax.broadcasted_iota(jnp.int32, (tp, POOL_DIM), 0)
    cnt = jnp.minimum(t + half, n_seq) - jnp.maximum(t - half, 0)
    pooled = wsum / cnt.astype(F32) - x
    o_ref[...] = _dot(pooled.astype(BF16), w_ref[...]) * scale_ref[...]


def _pool(p_in, w_bd, scale, tp, n_lat_rows, n_lat, n_ctx):
    n, c = p_in.shape
    hb = tp // POOL_HALO
    last_halo_block = n // POOL_HALO - 1
    kern = functools.partial(_pool_kernel, tp=tp, lat_tiles=n_lat_rows // tp, tpb_lat=n_lat // tp,
                             tpb_ctx=n_ctx // tp, n_lat=n_lat, n_ctx=n_ctx)
    return pl.pallas_call(
        kern,
        grid=(n // tp,),
        in_specs=[pl.BlockSpec((tp, c), lambda i: (i, 0)),
                  pl.BlockSpec((POOL_HALO, c), lambda i: (jnp.maximum(i * hb - 1, 0), 0)),
                  pl.BlockSpec((POOL_HALO, c), lambda i: (jnp.minimum((i + 1) * hb, last_halo_block), 0)),
                  pl.BlockSpec((c, c), lambda i: (0, 0)),
                  pl.BlockSpec((1, c), lambda i: (0, 0))],
        out_specs=pl.BlockSpec((tp, c), lambda i: (i, 0)),
        out_shape=jax.ShapeDtypeStruct((n, c), F32),
        name="pool",
        compiler_params=_cparams(("parallel",)),
    )(p_in, p_in, p_in, w_bd, scale)


def _hgrn_kernel(q_s, g_s, k_s, hin_ref, s0_ref, o_ref, sT_ref, st_s, *, reverse, nchunk):
    j = pl.program_id(1)

    @pl.when(j == 0)
    def _():
        st_s[...] = s0_ref[0]

    C = HG_CHUNK
    row = lax.broadcasted_iota(jnp.int32, (C, C), 0)
    col = lax.broadcasted_iota(jnp.int32, (C, C), 1)
    tri = ((row <= col) if reverse else (row >= col)).astype(BF16)
    rmod = lax.broadcasted_iota(jnp.int32, (C, 1), 0) & (HG_SUB - 1)
    nsub = C // HG_SUB
    row_sub = row >> HG_SUB_LOG2
    col_sub = col >> HG_SUB_LOG2

    def chunk(c, carry):
        cc = (nchunk - 1 - c) if reverse else c
        r0 = pl.multiple_of(cc * C, C)
        b_all = _dot_exact_rhs_lhs(tri, g_s[pl.ds(r0, C), :])
        q_all = q_s[pl.ds(r0, C), :]
        k_all = k_s[pl.ds(r0, C), :]
        v_all = hin_ref[pl.ds(r0, C), :]
        outs = []
        for h in range(HG_HEADS):
            ks = slice(h * HG_DK, (h + 1) * HG_DK)
            b, q, k = b_all[:, ks], q_all[:, ks], k_all[:, ks]
            v = v_all[:, h * HG_DV:(h + 1) * HG_DV]
            st = st_s[h]
            b_last = b[0:1] if reverse else b[C - 1:C]
            o = _dot_nt((q * jnp.exp(b)).astype(BF16), st.astype(BF16))
            a_rows = []
            for i in range(nsub):
                rs = slice(i * HG_SUB, (i + 1) * HG_SUB)
                if (reverse and i == nsub - 1) or (not reverse and i == 0):
                    a_rows.append(jnp.zeros((HG_SUB, C), F32))
                    continue
                m = b[(i + 1) * HG_SUB - 1:(i + 1) * HG_SUB] if reverse else b[i * HG_SUB:i * HG_SUB + 1]
                qi = q[rs] * jnp.exp(b[rs] - m)
                ksc = k * jnp.exp(m - b)
                a_rows.append(_dot_nt(qi.astype(BF16), ksc.astype(BF16)))
            a = jnp.concatenate(a_rows, axis=0)
            if reverse:
                a = jnp.where(col_sub > row_sub, a, 0.0)
            else:
                a = jnp.where(col_sub < row_sub, a, 0.0)
            o = o + _dot(a.astype(BF16), v.astype(BF16))
            for delta in range(HG_SUB):
                if delta == 0:
                    w = jnp.sum(q * k, axis=-1, keepdims=True)
                    o = o + w * v
                    continue
                shift = (C - delta) if reverse else delta
                kd = pltpu.roll(k, shift, 0)
                bd = pltpu.roll(b, shift, 0)
                vd = pltpu.roll(v, shift, 0)
                e = jnp.exp(b - bd)
                w = jnp.sum(q * kd * e, axis=-1, keepdims=True)
                valid = (rmod + delta < HG_SUB) if reverse else (rmod >= delta)
                o = o + jnp.where(valid, w, 0.0) * vd
            outs.append(o)
            kdec = k * jnp.exp(b_last - b)
            st_s[h] = st * jnp.exp(b_last) + _dot_tn(v.astype(BF16), kdec.astype(BF16))
        o_ref[pl.ds(r0, C), :] = jnp.concatenate(outs, axis=1)
        return carry

    lax.fori_loop(0, nchunk, chunk, 0, unroll=2 if nchunk % 2 == 0 else 1)

    @pl.when(j == pl.num_programs(1) - 1)
    def _():
        sT_ref[0] = st_s[...]


def _dot_exact_rhs_lhs(tri_bf16, g):
    g_hi = g.astype(BF16)
    r1 = g - g_hi.astype(F32)
    g_mid = r1.astype(BF16)
    g_lo = (r1 - g_mid.astype(F32)).astype(BF16)
    return _dot(tri_bf16, g_hi) + _dot(tri_bf16, g_mid) + _dot(tri_bf16, g_lo)


def _hgrn_scan(q, g, k, hin, s0, *, reverse, base_row, n_seq, tb):
    bsz = s0.shape[0]
    nblk = n_seq // tb
    base_blk = base_row // tb

    def rows(b, j):
        jj = (nblk - 1 - j) if reverse else j
        return (base_blk + b * nblk + jj, 0)

    def orow(b, j):
        jj = (nblk - 1 - j) if reverse else j
        return (b * nblk + jj, 0)

    kern = functools.partial(_hgrn_kernel, reverse=reverse, nchunk=tb // HG_CHUNK)
    return pl.pallas_call(
        kern,
        grid=(bsz, nblk),
        in_specs=[pl.BlockSpec((tb, HG_QK), rows),
                  pl.BlockSpec((tb, HG_QK), rows),
                  pl.BlockSpec((tb, HG_QK), rows),
                  pl.BlockSpec((tb, HG_V), rows),
                  pl.BlockSpec((1, HG_HEADS, HG_DV, HG_DK), lambda b, j: (b, 0, 0, 0))],
        out_specs=[pl.BlockSpec((tb, HG_V), orow),
                   pl.BlockSpec((1, HG_HEADS, HG_DV, HG_DK), lambda b, j: (b, 0, 0, 0))],
        out_shape=[jax.ShapeDtypeStruct((bsz * n_seq, HG_V), F32),
                   jax.ShapeDtypeStruct((bsz, HG_HEADS, HG_DV, HG_DK), F32)],
        scratch_shapes=[pltpu.VMEM((HG_HEADS, HG_DV, HG_DK), F32)],
        name="hgrn_bwd" if reverse else "hgrn_fwd",
        compiler_params=_cparams(("parallel", "arbitrary")),
    )(q, g, k, hin, s0)


def _rms(x, g):
    return x * lax.rsqrt(jnp.mean(x * x, axis=-1, keepdims=True) + EPS) * g


def _mla_proj_kernel(cq_ref, ckv_ref, kr_ref, krs_ref, cos_ref, sin_ref, qg_ref, kvg_ref,
                     wq_ref, wqs_ref, wk_ref, wvt_ref, q_out, k_out, vt_out):
    cos = cos_ref[...]
    sin = sin_ref[...]
    cos_h = jnp.concatenate([cos] * MLA_HEADS, axis=1)
    sin_h = jnp.concatenate([sin] * MLA_HEADS, axis=1)
    xq = _rms(cq_ref[...], qg_ref[...]).astype(BF16)
    q = _dot(xq, wq_ref[...]) * cos_h + _dot(xq, wqs_ref[...]) * sin_h
    q_out[...] = (q * (MLA_SCALE * LOG2_E)).astype(BF16)
    xkv = _rms(ckv_ref[...], kvg_ref[...]).astype(BF16)
    k_rope = kr_ref[...] * cos + krs_ref[...] * sin
    k = _dot(xkv, wk_ref[...]) + jnp.concatenate([k_rope] * MLA_HEADS, axis=1)
    k_out[...] = k.astype(BF16)
    vt_out[0] = _dot_nt(wvt_ref[...], xkv).astype(BF16)


def _mla_proj(cq, ckv, kr, krs, cos_t, sin_t, qg, kvg, wq, wqs, wk, wvt, layer, tm):
    n = cq.shape[0]
    row = lambda i: (i, 0)
    full = lambda i: (0, 0)
    lay = lambda i: (layer, 0, 0)
    hq = MLA_HEADS * HEAD_LANES
    hv = MLA_HEADS * MLA_V
    return pl.pallas_call(
        _mla_proj_kernel,
        grid=(n // tm,),
        in_specs=[pl.BlockSpec((tm, MLA_Q_LORA), row), pl.BlockSpec((tm, MLA_KV_LORA), row),
                  pl.BlockSpec((tm, HEAD_LANES), row), pl.BlockSpec((tm, HEAD_LANES), row),
                  pl.BlockSpec((tm, HEAD_LANES), row), pl.BlockSpec((tm, HEAD_LANES), row),
                  pl.BlockSpec((1, MLA_Q_LORA), full), pl.BlockSpec((1, MLA_KV_LORA), full),
                  pl.BlockSpec((None, MLA_Q_LORA, hq), lay), pl.BlockSpec((None, MLA_Q_LORA, hq), lay),
                  pl.BlockSpec((None, MLA_KV_LORA, hq), lay), pl.BlockSpec((None, hv, MLA_KV_LORA), lay)],
        out_specs=[pl.BlockSpec((tm, hq), row), pl.BlockSpec((tm, hq), row),
                   pl.BlockSpec((1, hv, tm), lambda i: (i, 0, 0))],
        out_shape=[jax.ShapeDtypeStruct((n, hq), BF16), jax.ShapeDtypeStruct((n, hq), BF16),
                   jax.ShapeDtypeStruct((n // tm, hv, tm), BF16)],
        name="mla_proj",
        compiler_params=_cparams(("parallel",)),
    )(cq, ckv, kr, krs, cos_t, sin_t, qg, kvg, wq, wqs, wk, wvt)


ACC_ROWS = MLA_V + 16
LAT_UNROLL = 8
LOOKAHEAD = 2


def _attn_kernel(*refs, tkc, n_lat_chunks, n_ctx_chunks):
    if n_lat_chunks:
        q_ref, kl_ref, vl_ref, kc_ref, vc_ref, o_ref, m_s, acc_s, s_s = refs
    else:
        q_ref, kc_ref, vc_ref, o_ref, m_s, acc_s, s_s = refs
    m_s[...] = jnp.full(m_s.shape, -jnp.inf, F32)
    acc_s[...] = jnp.zeros(acc_s.shape, F32)
    ones = jnp.ones((ACC_ROWS - MLA_V, tkc), BF16)

    def scores(a, k_ref, r0):
        hs = slice(a * HEAD_LANES, (a + 1) * HEAD_LANES)
        return _dot_nt(k_ref[pl.ds(r0, tkc), hs], q_ref[:, hs])

    def accumulate(a, s_t, vt):
        m_old = m_s[a]
        m_new = jnp.maximum(m_old, jnp.max(s_t, axis=0, keepdims=True))
        alpha = jnp.exp2(m_old - m_new)
        p_t = jnp.exp2(s_t - m_new).astype(BF16)
        vta = jnp.concatenate([vt[a * MLA_V:(a + 1) * MLA_V], ones], axis=0)
        acc_s[a] = alpha * acc_s[a] + _dot(vta, p_t)
        m_s[a] = m_new

    def run(items, queue, after):
        upcoming = items[LOOKAHEAD:] + after
        queue = list(queue)
        for idx, (a, _, _, vt) in enumerate(items):
            if idx < len(upcoming):
                queue.append(scores(*upcoming[idx][:3]))
            accumulate(a, queue.pop(0), vt)
        return queue

    ahead_chunks = (LOOKAHEAD + 1) // 2
    unroll = int(np.gcd(LAT_UNROLL, n_lat_chunks)) if n_lat_chunks else 1

    def lat_items(c0, dynamic, count=None):
        items = []
        for u in range(unroll if count is None else count):
            cc = c0 + u
            r0 = pl.multiple_of(cc * tkc, tkc) if dynamic else cc * tkc
            vt = vl_ref[cc] if count is None else None
            items += [(0, kl_ref, r0, vt), (1, kl_ref, r0, vt)]
        return items

    ctx_items = []
    for c in range(n_ctx_chunks):
        ctx_items += [(0, kc_ref, c * tkc, vc_ref[c]), (1, kc_ref, c * tkc, vc_ref[c])]

    if n_lat_chunks:
        n_groups = n_lat_chunks // unroll
        for d, it in enumerate(lat_items(0, False, count=ahead_chunks)[:LOOKAHEAD]):
            s_s[d] = scores(*it[:3])

        def body(g, carry):
            c0 = g * unroll
            after = lat_items(c0 + unroll, True, count=ahead_chunks)[:LOOKAHEAD]
            queue = run(lat_items(c0, True), [s_s[d] for d in range(LOOKAHEAD)], after)
            for d in range(LOOKAHEAD):
                s_s[d] = queue[d]
            return carry

        lax.fori_loop(0, n_groups - 1, body, 0)
        run(lat_items((n_groups - 1) * unroll, False) + ctx_items, [s_s[d] for d in range(LOOKAHEAD)], [])
    else:
        run(ctx_items, [scores(*it[:3]) for it in ctx_items[:LOOKAHEAD]], [])
    outs = []
    for a in range(2):
        acc = acc_s[a]
        outs.append(acc[0:MLA_V] / acc[MLA_V:MLA_V + 1])
    o_ref[...] = jnp.concatenate(outs, axis=0).T.astype(o_ref.dtype)


def _attention(q, k, vt, *, bsz, n_q, q_base, n_lat, n_ctx, ctx_base, tq, tkc, with_lat):
    pairs = MLA_HEADS // 2
    qblocks = n_q // tq
    qb0 = q_base // tq
    cb0 = ctx_base // n_ctx
    kw = 2 * HEAD_LANES
    vw = 2 * MLA_V
    in_specs = [pl.BlockSpec((tq, kw), lambda b, p, i: (qb0 + b * qblocks + i, p))]
    args = [q]
    if with_lat:
        in_specs += [pl.BlockSpec((n_lat, kw), lambda b, p, i: (b, p)),
                     pl.BlockSpec((n_lat // tkc, vw, tkc), lambda b, p, i: (b, p, 0))]
        args += [k, vt]
    in_specs += [pl.BlockSpec((n_ctx, kw), lambda b, p, i: (cb0 + b, p)),
                 pl.BlockSpec((n_ctx // tkc, vw, tkc), lambda b, p, i: (cb0 + b, p, 0))]
    args += [k, vt]
    kern = functools.partial(_attn_kernel, tkc=tkc, n_lat_chunks=(n_lat // tkc) if with_lat else 0,
                             n_ctx_chunks=n_ctx // tkc)
    return pl.pallas_call(
        kern,
        grid=(bsz, pairs, qblocks),
        in_specs=in_specs,
        out_specs=pl.BlockSpec((tq, vw), lambda b, p, i: (b * qblocks + i, p)),
        out_shape=jax.ShapeDtypeStruct((bsz * n_q, MLA_HEADS * MLA_V), BF16),
        scratch_shapes=[pltpu.VMEM((2, 1, tq), F32), pltpu.VMEM((2, ACC_ROWS, tq), F32),
                        pltpu.VMEM((LOOKAHEAD, tkc, tq), F32)],
        name="attn_lat" if with_lat else "attn_ctx",
        compiler_params=_cparams(("parallel", "parallel", "arbitrary")),
    )(*args)


def _outproj_kernel(a_ref, ofl_ref, obl_ref, cl_ref, ofc_ref, obc_ref, cc_ref, hg_ref, x_ref, g1_ref, sh2_ref,
                    sc2_ref, lng_ref, lnb_ref, ng_ref, gavg_ref, w_ref, wr_ref, x1_out, h2_out, lg_out,
                    *, alpha, lat_tiles):
    is_ctx = pl.program_id(0) >= lat_tiles
    tm = x_ref.shape[0]
    half = tm // OUT_ROW_SPLIT
    for r in range(OUT_ROW_SPLIT):
        rs = slice(r * half, (r + 1) * half)
        o = jnp.where(is_ctx, ofc_ref[rs, :] + obc_ref[rs, :], ofl_ref[rs, :] + obl_ref[rs, :])
        c_att = jnp.where(is_ctx, cc_ref[rs, :], cl_ref[rs, :])
        o2_hi, o2_lo = _split_bf16(o * o)
        ms = _dot(o2_hi, gavg_ref[...]) + _dot(o2_lo, gavg_ref[...])
        b_lat = o * lax.rsqrt(ms + EPS) * ng_ref[...] * _silu(hg_ref[rs, :])
        m = (_dot(a_ref[rs, :].astype(BF16), w_ref[0:POOL_DIM, :])
             + _dot(b_lat.astype(BF16), w_ref[POOL_DIM:POOL_DIM + HG_V, :])
             + _dot(c_att, w_ref[POOL_DIM + HG_V:, :]))
        x1 = _layer_norm(alpha * x_ref[rs, :] + g1_ref[0] * m) * lng_ref[...] + lnb_ref[...]
        x1_out[rs, :] = x1
        h2 = _layer_norm(x1) * (1.0 + sc2_ref[0]) + sh2_ref[0]
        h2_out[rs, :] = h2
        lg_out[rs, :] = _dot3(h2, wr_ref[...])


def _out_proj(a, lat_parts, ctx_parts, hg, x, g1, sh2, sc2, ln_g, ln_b, norm_g, gavg, w_out, w_router, layer, tm,
              mod_index, alpha):
    n, d = x.shape
    lat_tiles = lat_parts[0].shape[0] // tm
    row = lambda i: (i, 0)
    lat_row = lambda i: (jnp.minimum(i, lat_tiles - 1), 0)
    ctx_row = lambda i: (jnp.maximum(i - lat_tiles, 0), 0)
    full = lambda i: (0, 0)
    lay = lambda i: (layer, 0, 0)
    mod = lambda i: (mod_index(i), 0, 0)
    hv = MLA_HEADS * MLA_V
    return pl.pallas_call(
        functools.partial(_outproj_kernel, alpha=alpha, lat_tiles=lat_tiles),
        grid=(n // tm,),
        in_specs=[pl.BlockSpec((tm, POOL_DIM), row),
                  pl.BlockSpec((tm, HG_V), lat_row), pl.BlockSpec((tm, HG_V), lat_row), pl.BlockSpec((tm, hv), lat_row),
                  pl.BlockSpec((tm, HG_V), ctx_row), pl.BlockSpec((tm, HG_V), ctx_row), pl.BlockSpec((tm, hv), ctx_row),
                  pl.BlockSpec((tm, HG_V), row),
                  pl.BlockSpec((tm, d), row),
                  pl.BlockSpec((1, 1, d), mod), pl.BlockSpec((1, 1, d), mod), pl.BlockSpec((1, 1, d), mod),
                  pl.BlockSpec((1, d), full), pl.BlockSpec((1, d), full), pl.BlockSpec((1, HG_V), full),
                  pl.BlockSpec((HG_V, HG_V), full), pl.BlockSpec((None, d, d), lay),
                  pl.BlockSpec((None, d, ROUTER_LANES), lay)],
        out_specs=[pl.BlockSpec((tm, d), row), pl.BlockSpec((tm, d), row), pl.BlockSpec((tm, ROUTER_LANES), row)],
        out_shape=[jax.ShapeDtypeStruct((n, d), F32), jax.ShapeDtypeStruct((n, d), F32),
                   jax.ShapeDtypeStruct((n, ROUTER_LANES), F32)],
        name="out_proj",
        compiler_params=_cparams(("parallel",)),
    )(a, *lat_parts, *ctx_parts, hg, x, g1, sh2, sc2, ln_g, ln_b, norm_g, gavg, w_out, w_router)


def _moe_kernel(be_ref, nv_ref, x_ref, w1_ref, w3_ref, w2_ref, y_ref, w1b, w3b, w2b):
    i = pl.program_id(0)

    @pl.when((i == 0) | (be_ref[i] != be_ref[jnp.maximum(i - 1, 0)]))
    def _():
        w1b[...] = w1_ref[0, 0].astype(BF16)
        w3b[...] = w3_ref[0, 0].astype(BF16)
        w2b[...] = w2_ref[0, 0].astype(BF16)

    @pl.when(i < nv_ref[0])
    def _():
        x = x_ref[...].astype(BF16)
        h1 = _dot(x, w1b[...])
        h3 = _dot(x, w3b[...])
        h = (_silu(h1) * h3).astype(BF16)
        y_ref[...] = _dot(h, w2b[...]).astype(y_ref.dtype)

    @pl.when(i >= nv_ref[0])
    def _():
        y_ref[...] = jnp.zeros(y_ref.shape, y_ref.dtype)


def _moe_ffn(block_e, n_valid, xr, w1, w3, w2, layer, tm):
    n_rows, d = xr.shape
    n_blocks = n_rows // tm
    grid_spec = pltpu.PrefetchScalarGridSpec(
        num_scalar_prefetch=2,
        grid=(n_blocks,),
        in_specs=[pl.BlockSpec((tm, d), lambda i, be, nv: (jnp.minimum(i, nv[0] - 1), 0)),
                  pl.BlockSpec((1, 1, d, D_EXPERT), lambda i, be, nv: (layer, be[i], 0, 0)),
                  pl.BlockSpec((1, 1, d, D_EXPERT), lambda i, be, nv: (layer, be[i], 0, 0)),
                  pl.BlockSpec((1, 1, D_EXPERT, d), lambda i, be, nv: (layer, be[i], 0, 0))],
        out_specs=pl.BlockSpec((tm, d), lambda i, be, nv: (i, 0)),
        scratch_shapes=[pltpu.VMEM((d, D_EXPERT), BF16), pltpu.VMEM((d, D_EXPERT), BF16),
                        pltpu.VMEM((D_EXPERT, d), BF16)],
    )
    return pl.pallas_call(
        _moe_kernel,
        grid_spec=grid_spec,
        out_shape=jax.ShapeDtypeStruct((n_rows, d), BF16),
        name="moe_ffn",
        compiler_params=_cparams(("arbitrary",)),
    )(block_e, n_valid, xr, w1, w3, w2)


def _combine_kernel(x_ref, y0_ref, y1_ref, gw_ref, g2_ref, lng_ref, lnb_ref, o_ref, *, alpha):
    gw = gw_ref[...]
    y = gw[:, 0:1] * y0_ref[...].astype(F32) + gw[:, 1:2] * y1_ref[...].astype(F32)
    o_ref[...] = _layer_norm(alpha * x_ref[...] + g2_ref[0] * y) * lng_ref[...] + lnb_ref[...]


def _combine(x, y0, y1, gw, g2, ln_g, ln_b, tm, mod_index, alpha):
    n, d = x.shape
    row = lambda i: (i, 0)
    full = lambda i: (0, 0)
    return pl.pallas_call(
        functools.partial(_combine_kernel, alpha=alpha),
        grid=(n // tm,),
        in_specs=[pl.BlockSpec((tm, d), row), pl.BlockSpec((tm, d), row), pl.BlockSpec((tm, d), row),
                  pl.BlockSpec((tm, ROUTER_LANES), row),
                  pl.BlockSpec((1, 1, d), lambda i: (mod_index(i), 0, 0)),
                  pl.BlockSpec((1, d), full), pl.BlockSpec((1, d), full)],
        out_specs=pl.BlockSpec((tm, d), row),
        out_shape=jax.ShapeDtypeStruct((n, d), F32),
        name="moe_combine",
        compiler_params=_cparams(("parallel",)),
    )(x, y0, y1, gw, g2, ln_g, ln_b)


def _rot_half_cols(w):
    lead = w.shape[:-1]
    wr = w.reshape(lead + (2, 2, MLA_ROPE // 4))
    return jnp.stack([-wr[..., 1, :], wr[..., 0, :]], axis=-2).reshape(w.shape)


def _rope_tables(n_lat, n_ctx, bsz):
    rows = n_lat // GRID_W
    row = jnp.repeat(jnp.arange(rows, dtype=F32), GRID_W)
    col = jnp.tile(jnp.arange(GRID_W, dtype=F32), rows)
    n_freq = MLA_ROPE // 4
    inv_freq = ROPE_BASE ** (-jnp.arange(n_freq, dtype=F32) / n_freq)
    ang = jnp.stack([row[:, None] * inv_freq, col[:, None] * inv_freq], axis=1)
    cos = jnp.broadcast_to(jnp.cos(ang)[:, :, None, :], (n_lat, 2, 2, n_freq)).reshape(n_lat, MLA_ROPE)
    sin = jnp.broadcast_to(jnp.sin(ang)[:, :, None, :], (n_lat, 2, 2, n_freq)).reshape(n_lat, MLA_ROPE)
    pad_l = MLA_NOPE
    pad_r = HEAD_LANES - MLA_QK
    cos_t = jnp.concatenate([jnp.ones((n_lat, pad_l), F32), cos, jnp.ones((n_lat, pad_r), F32)], axis=1)
    sin_t = jnp.concatenate([jnp.zeros((n_lat, pad_l), F32), sin, jnp.zeros((n_lat, pad_r), F32)], axis=1)
    cos_all = jnp.concatenate([jnp.tile(cos_t, (bsz, 1)), jnp.ones((bsz * n_ctx, HEAD_LANES), F32)], axis=0)
    sin_all = jnp.concatenate([jnp.tile(sin_t, (bsz, 1)), jnp.zeros((bsz * n_ctx, HEAD_LANES), F32)], axis=0)
    return cos_all, sin_all


def _prep_w_in(w_in):
    depth, d, _ = w_in.shape
    kr = w_in[:, :, W_IN_MAIN:]
    z_l = jnp.zeros((depth, d, MLA_NOPE), w_in.dtype)
    z_r = jnp.zeros((depth, d, HEAD_LANES - MLA_QK), w_in.dtype)
    return jnp.concatenate([w_in[:, :, :W_IN_MAIN], z_l, kr, z_r, z_l, _rot_half_cols(kr), z_r],
                           axis=-1).astype(BF16)


def _prep_mla(w_uq, w_ukv):
    depth = w_uq.shape[0]
    wq = w_uq.reshape(depth, MLA_Q_LORA, MLA_HEADS, MLA_QK)
    pad = jnp.zeros((depth, MLA_Q_LORA, MLA_HEADS, HEAD_LANES - MLA_QK), w_uq.dtype)
    wq_p = jnp.concatenate([wq, pad], axis=-1).reshape(depth, MLA_Q_LORA, MLA_HEADS * HEAD_LANES)
    wq_s = jnp.concatenate([jnp.zeros_like(wq[..., :MLA_NOPE]), _rot_half_cols(wq[..., MLA_NOPE:]), pad],
                           axis=-1).reshape(depth, MLA_Q_LORA, MLA_HEADS * HEAD_LANES)
    wkv = w_ukv.reshape(depth, MLA_KV_LORA, MLA_HEADS, MLA_NOPE + MLA_V)
    kpad = jnp.zeros((depth, MLA_KV_LORA, MLA_HEADS, HEAD_LANES - MLA_NOPE), w_ukv.dtype)
    wk = jnp.concatenate([wkv[..., :MLA_NOPE], kpad], axis=-1).reshape(depth, MLA_KV_LORA, MLA_HEADS * HEAD_LANES)
    wvt = jnp.swapaxes(wkv[..., MLA_NOPE:].reshape(depth, MLA_KV_LORA, MLA_HEADS * MLA_V), 1, 2)
    return wq_p.astype(BF16), wq_s.astype(BF16), wk.astype(BF16), wvt.astype(BF16)


def _block_diag(pool_w):
    depth, g, c, _ = pool_w.shape
    eye = jnp.eye(g, dtype=pool_w.dtype)
    return jnp.einsum("lgcd,gh->lgchd", pool_w, eye).reshape(depth, g * c, g * c).astype(BF16)


def _route(logits, b_rg, b_re, tm_moe):
    n = logits.shape[0]
    lg = logits[:, :N_GROUPS] + b_rg
    le = (logits[:, N_GROUPS:N_GROUPS + N_EXPERTS] + b_re).reshape(n, N_GROUPS, EXPERTS_PER_GROUP)
    pg = jax.nn.softmax(lg, axis=-1)
    g_sel = jnp.argmax(lg, axis=-1).astype(jnp.int32)
    g_hot = g_sel[:, None] == jnp.arange(N_GROUPS, dtype=jnp.int32)[None, :]
    w_g = jnp.sum(jnp.where(g_hot, pg, 0.0), axis=-1)
    le_sel = jnp.sum(jnp.where(g_hot[:, :, None], le, 0.0), axis=1)
    top_v, top_i = lax.top_k(le_sel, TOP_K)
    gate = jax.nn.softmax(top_v, axis=-1) * w_g[:, None]
    expert = (g_sel[:, None] * EXPERTS_PER_GROUP + top_i).astype(jnp.int32)

    n_assign = n * TOP_K
    e_flat = expert.reshape(-1)
    hot = e_flat[:, None] == jnp.arange(N_EXPERTS, dtype=jnp.int32)[None, :]
    nb = n_assign // COUNT_BLOCK
    tri = jnp.tril(jnp.ones((COUNT_BLOCK, COUNT_BLOCK), BF16))
    win = jnp.einsum("ts,bse->bte", tri, hot.astype(BF16).reshape(nb, COUNT_BLOCK, N_EXPERTS),
                     preferred_element_type=F32)
    bsum = win[:, -1, :]
    boff = jnp.cumsum(bsum, axis=0) - bsum
    csum = (win + boff[:, None, :]).reshape(n_assign, N_EXPERTS).astype(jnp.int32)
    counts = (boff[-1] + bsum[-1]).astype(jnp.int32)
    pcounts = (counts + tm_moe - 1) // tm_moe * tm_moe
    pends = jnp.cumsum(pcounts)
    pstarts = pends - pcounts
    starts = jnp.cumsum(counts) - counts
    dest = jnp.sum(jnp.where(hot, csum - 1 + pstarts[None, :], 0), axis=1)
    n_rows = (n_assign + N_EXPERTS * (tm_moe - 1) + tm_moe - 1) // tm_moe * tm_moe
    n_blocks = n_rows // tm_moe
    n_valid = (pends[-1] // tm_moe).astype(jnp.int32)
    blk = jnp.arange(n_blocks, dtype=jnp.int32)
    first_row = jnp.minimum(blk, n_valid - 1) * tm_moe
    block_e = jnp.sum((pends[None, :] <= first_row[:, None]).astype(jnp.int32), axis=1)
    block_e = jnp.clip(block_e, 0, N_EXPERTS - 1)
    a_sorted = jnp.sort(e_flat * n_assign + jnp.arange(n_assign, dtype=jnp.int32)) % n_assign
    pos = (blk * tm_moe - pstarts[block_e])[:, None] + jnp.arange(tm_moe, dtype=jnp.int32)[None, :]
    src = jnp.clip(starts[block_e][:, None] + pos, 0, n_assign - 1)
    filler = (blk[:, None] * tm_moe + jnp.arange(tm_moe, dtype=jnp.int32)[None, :]) % n
    row_tok = jnp.where(pos < counts[block_e][:, None], a_sorted[src] // TOP_K, filler)
    gw = jnp.zeros((n, ROUTER_LANES), F32).at[:, :TOP_K].set(gate)
    return row_tok.reshape(n_rows), dest.reshape(n, TOP_K), gw, block_e, n_valid.reshape(1)


def _tile(n, pref):
    t = pref
    while n % t:
        t //= 2
    return t


def kernel(x, c, ctx, c_ctx, w_mod, b_mod, w_in, pool_w, pool_scale, hg_lb_logits, hg_norm_g, mla_q_norm_g, mla_w_uq, mla_kv_norm_g, mla_w_ukv, w_out, ln1_g, ln1_b, ln2_g, ln2_b, router_group_w, router_group_b, router_expert_w, router_expert_b, expert_w1, expert_w3, expert_w2):
    bsz, n_lat, d = x.shape
    n_ctx = ctx.shape[1]
    depth = w_mod.shape[0]
    alpha = (2 * depth) ** 0.25
    lat_rows = bsz * n_lat
    ctx_rows = bsz * n_ctx
    n_tok = lat_rows + ctx_rows

    tm = _tile(np.gcd(n_lat, ctx_rows), 512)
    tp = _tile(np.gcd(n_lat, n_ctx), 256)
    tb_lat = _tile(n_lat, 512)
    tb_ctx = _tile(n_ctx, 512)
    tq_lat = _tile(n_lat, 512)
    tkc = _tile(np.gcd(n_lat, n_ctx), 256)
    tq_ctx = _tile(n_ctx, 512)
    tm_moe = 256
    lat_tiles_per_batch = n_lat // tm
    mod_index = lambda i: jnp.minimum(i // lat_tiles_per_batch, bsz)

    c_rows = jnp.concatenate([jax.nn.silu(c), jax.nn.silu(c_ctx)[None, :],
                              jnp.zeros((-(bsz + 1) % 8, d), F32)], axis=0)
    mod_all = _mod_all(c_rows, w_mod, b_mod)
    lb_all = jnp.cumsum(jax.nn.softmax(hg_lb_logits.astype(F32), axis=0), axis=0)
    lb_all = lb_all - lb_all[0:1]
    log_lb = jnp.log(lb_all)
    log_1m_lb = jnp.log1p(-lb_all)
    w_aug = _prep_w_in(w_in)
    wq_p, wq_s, wk_p, wvt_p = _prep_mla(mla_w_uq, mla_w_ukv)
    w_bd = _block_diag(pool_w)
    w_out_b = w_out.astype(BF16)
    w_router = jnp.concatenate([router_group_w, router_expert_w,
                                jnp.zeros((depth, d, ROUTER_LANES - N_GROUPS - N_EXPERTS), F32)], axis=-1)
    gavg = jnp.kron(jnp.eye(HG_HEADS, dtype=F32), jnp.full((HG_DV, HG_DV), 1.0 / HG_DV, F32)).astype(BF16)
    cos_t, sin_t = _rope_tables(n_lat, n_ctx, bsz)
    zero_state = jnp.zeros((bsz, HG_HEADS, HG_DV, HG_DK), F32)

    xt = jnp.concatenate([x.reshape(lat_rows, d), ctx.reshape(ctx_rows, d)], axis=0)
    for l in range(depth):
        mod = mod_all[l, :bsz + 1].reshape(bsz + 1, 6, 1, d)
        sh1, sc1, g1, sh2, sc2, g2 = (mod[:, k] for k in range(6))

        gate_consts = jnp.stack([log_lb[l, 0], log_1m_lb[l, 0], log_lb[l, 1], log_1m_lb[l, 1]], axis=0)
        p_in, hq, gf, kf, gb, kb, hin, hg, cq, ckv, kr, krs = _in_proj(xt, sh1, sc1, gate_consts, w_aug, l, tm,
                                                                       mod_index)

        a_mix = _pool(p_in, w_bd[l], pool_scale[l][None, :], tp, lat_rows, n_lat, n_ctx)

        oc_f, s_f = _hgrn_scan(hq, gf, kf, hin, zero_state, reverse=False, base_row=lat_rows, n_seq=n_ctx, tb=tb_ctx)
        oc_b, s_b = _hgrn_scan(hq, gb, kb, hin, zero_state, reverse=True, base_row=lat_rows, n_seq=n_ctx, tb=tb_ctx)
        ol_f, _ = _hgrn_scan(hq, gf, kf, hin, s_f, reverse=False, base_row=0, n_seq=n_lat, tb=tb_lat)
        ol_b, _ = _hgrn_scan(hq, gb, kb, hin, s_b, reverse=True, base_row=0, n_seq=n_lat, tb=tb_lat)

        q_all, k_all, vt_all = _mla_proj(cq, ckv, kr, krs, cos_t, sin_t, mla_q_norm_g[l][None, :],
                                         mla_kv_norm_g[l][None, :], wq_p, wq_s, wk_p, wvt_p, l, tkc)
        c_lat = _attention(q_all, k_all, vt_all, bsz=bsz, n_q=n_lat, q_base=0, n_lat=n_lat, n_ctx=n_ctx,
                           ctx_base=lat_rows, tq=tq_lat, tkc=tkc, with_lat=True)
        c_ctx_o = _attention(q_all, k_all, vt_all, bsz=bsz, n_q=n_ctx, q_base=lat_rows, n_lat=n_lat, n_ctx=n_ctx,
                             ctx_base=lat_rows, tq=tq_ctx, tkc=tkc, with_lat=False)

        x1, h2, logits = _out_proj(a_mix, (ol_f, ol_b, c_lat), (oc_f, oc_b, c_ctx_o), hg, xt, g1, sh2, sc2,
                                   ln1_g[l][None, :],
                                   ln1_b[l][None, :], hg_norm_g[l][None, :], gavg, w_out_b, w_router, l,
                                   tm, mod_index, alpha)

        row_tok, dest, gw, block_e, n_valid = _route(logits, router_group_b[l], router_expert_b[l], tm_moe)
        xr = jnp.take(h2, row_tok, axis=0, mode="clip")
        y = _moe_ffn(block_e, n_valid, xr, expert_w1, expert_w3, expert_w2, l, tm_moe)
        y0 = jnp.take(y, dest[:, 0], axis=0, mode="clip")
        y1 = jnp.take(y, dest[:, 1], axis=0, mode="clip")
        xt = _combine(x1, y0, y1, gw, g2, ln2_g[l][None, :], ln2_b[l][None, :], tm, mod_index, alpha)
    return xt[:lat_rows].reshape(bsz, n_lat, d)
```
